```python
import math
import jax, jax.numpy as jnp
from jax import lax
import numpy as np

D_MODEL = 4096
BATCH = 16
SEQ = 2048
DEPTH = 4

D_MIX = D_MODEL
ATT_WIDTH = D_MIX // 2
CONV_WIDTH = D_MIX - ATT_WIDTH

N_HEADS = 16
V_HEAD_DIM = ATT_WIDTH // N_HEADS
QK_NOPE_DIM = 128
QK_ROPE_DIM = 64
QK_HEAD_DIM = QK_NOPE_DIM + QK_ROPE_DIM
Q_LORA_RANK = 1024
KV_LORA_RANK = 512
ROPE_THETA = 10000.0
Q_BLOCK = 128

CONV_GROUPS = 16
CONV_KERNEL = 31

EPS = 1e-6

IN_SIZES = (Q_LORA_RANK, KV_LORA_RANK, QK_ROPE_DIM, ATT_WIDTH, 2 * CONV_WIDTH, CONV_WIDTH)
IN_COLS = sum(IN_SIZES)
IN_SPLITS = tuple(int(s) for s in np.cumsum(IN_SIZES)[:-1])

kernel_name = "hymba_mla_conformer_hybrid"


def rmsnorm(x, g):
    xf = x.astype(jnp.float32)
    y = xf * lax.rsqrt(jnp.mean(xf * xf, axis=-1, keepdims=True) + EPS)
    return (y * g.astype(jnp.float32)).astype(x.dtype)


def layernorm(x, g, b):
    xf = x.astype(jnp.float32)
    mu = jnp.mean(xf, axis=-1, keepdims=True)
    xc = xf - mu
    var = jnp.mean(xc * xc, axis=-1, keepdims=True)
    y = xc * lax.rsqrt(var + EPS)
    return (y * g.astype(jnp.float32) + b.astype(jnp.float32)).astype(x.dtype)


def rope_tables(positions, dtype):
    half = QK_ROPE_DIM // 2
    inv_freq = ROPE_THETA ** (-jnp.arange(half, dtype=jnp.float32) / half)
    ang = positions.astype(jnp.float32)[..., None] * inv_freq
    return jnp.cos(ang)[:, :, None, :].astype(dtype), jnp.sin(ang)[:, :, None, :].astype(dtype)


def apply_rope(x, cos, sin):
    x1, x2 = jnp.split(x, 2, axis=-1)
    return jnp.concatenate([x1 * cos - x2 * sin, x2 * cos + x1 * sin], axis=-1)


def causal_attention(q, k, v):
    B, S, H, Dq = q.shape
    Dv = v.shape[-1]
    nb = S // Q_BLOCK
    scale = 1.0 / math.sqrt(Dq)
    qb = q.reshape(B, nb, Q_BLOCK, H, Dq).transpose(1, 0, 2, 3, 4)
    kpos = jnp.arange(S)

    def one_block(args):
        i, qi = args
        s = jnp.einsum('bqhd,bkhd->bhqk', qi, k, preferred_element_type=jnp.float32) * scale
        qpos = i * Q_BLOCK + jnp.arange(Q_BLOCK)
        mask = kpos[None, :] <= qpos[:, None]
        s = jnp.where(mask[None, None], s, -jnp.inf)
        p = jax.nn.softmax(s, axis=-1)
        return jnp.einsum('bhqk,bkhd->bqhd', p.astype(v.dtype), v)

    out = lax.map(one_block, (jnp.arange(nb), qb))
    return out.transpose(1, 0, 2, 3, 4).reshape(B, S, H, Dv)


def causal_depthwise_conv(u, w, b):
    C = u.shape[-1]
    y = lax.conv_general_dilated(
        u, w[:, None, :].astype(u.dtype),
        window_strides=(1,), padding=((CONV_KERNEL - 1, 0),),
        dimension_numbers=('NWC', 'WIO', 'NWC'), feature_group_count=C)
    return y + b


def _fwd_setup_inputs(seed: int = 0) -> dict:
    key = jax.random.key(seed)
    ks = jax.random.split(key, 16)
    f32 = jnp.float32

    def nrm(k, shape, scale):
        return jax.random.normal(k, shape, f32) * scale

    def gain(k, shape):
        return 1.0 + 0.01 * jax.random.normal(k, shape, f32)

    x = jax.random.normal(ks[0], (BATCH, SEQ, D_MODEL), f32)
    offset = jax.random.randint(ks[1], (BATCH, 1), 0, 4096, dtype=jnp.int32)
    positions = (jnp.arange(SEQ, dtype=jnp.int32)[None, :] + offset).astype(jnp.int32)
    return {
        "x": x,
        "positions": positions,
        "ln_g": gain(ks[2], (DEPTH, D_MODEL)),
        "w_in": nrm(ks[3], (DEPTH, D_MODEL, IN_COLS), D_MODEL ** -0.5),
        "q_a_norm": gain(ks[4], (DEPTH, Q_LORA_RANK)),
        "w_q_up": nrm(ks[5], (DEPTH, Q_LORA_RANK, N_HEADS * QK_HEAD_DIM), Q_LORA_RANK ** -0.5),
        "kv_a_norm": gain(ks[6], (DEPTH, KV_LORA_RANK)),
        "w_kv_up": nrm(ks[7], (DEPTH, KV_LORA_RANK, N_HEADS * (QK_NOPE_DIM + V_HEAD_DIM)), KV_LORA_RANK ** -0.5),
        "q_norm": gain(ks[8], (DEPTH, QK_HEAD_DIM)),
        "k_norm": gain(ks[9], (DEPTH, QK_HEAD_DIM)),
        "w_dw": nrm(ks[10], (DEPTH, CONV_KERNEL, CONV_WIDTH), CONV_KERNEL ** -0.5),
        "b_dw": nrm(ks[11], (DEPTH, CONV_WIDTH), 0.01),
        "conv_ln_g": gain(ks[12], (DEPTH, CONV_WIDTH)),
        "conv_ln_b": nrm(ks[13], (DEPTH, CONV_WIDTH), 0.01),
        "w_out": nrm(ks[14], (DEPTH, D_MIX, D_MODEL), D_MIX ** -0.5),
    }


def _fwd_reference(x, positions, ln_g, w_in, q_a_norm, w_q_up, kv_a_norm, w_kv_up,
              q_norm, k_norm, w_dw, b_dw, conv_ln_g, conv_ln_b, w_out):
    B, S, _ = x.shape
    cos, sin = rope_tables(positions, x.dtype)
    for l in range(DEPTH):
        h = rmsnorm(x, ln_g[l])
        z = h @ w_in[l]
        q_c, kv_c, k_pe, g_att, u_conv, g_conv = jnp.split(z, IN_SPLITS, axis=-1)

        q = (rmsnorm(q_c, q_a_norm[l]) @ w_q_up[l]).reshape(B, S, N_HEADS, QK_HEAD_DIM)
        kv = (rmsnorm(kv_c, kv_a_norm[l]) @ w_kv_up[l]).reshape(B, S, N_HEADS, QK_NOPE_DIM + V_HEAD_DIM)
        k_nope, v = kv[..., :QK_NOPE_DIM], kv[..., QK_NOPE_DIM:]
        k_pe_h = jnp.broadcast_to(k_pe[:, :, None, :], (B, S, N_HEADS, QK_ROPE_DIM))
        k = jnp.concatenate([k_nope, k_pe_h], axis=-1)
        q = rmsnorm(q, q_norm[l])
        k = rmsnorm(k, k_norm[l])
        q = jnp.concatenate([q[..., :QK_NOPE_DIM], apply_rope(q[..., QK_NOPE_DIM:], cos, sin)], axis=-1)
        k = jnp.concatenate([k[..., :QK_NOPE_DIM], apply_rope(k[..., QK_NOPE_DIM:], cos, sin)], axis=-1)
        att = causal_attention(q, k, v).reshape(B, S, ATT_WIDTH)
        att = att * jax.nn.silu(g_att)

        a, b = jnp.split(u_conv, 2, axis=-1)
        u = a * jax.nn.sigmoid(b)
        c = causal_depthwise_conv(u, w_dw[l], b_dw[l])
        c = jax.nn.silu(layernorm(c, conv_ln_g[l], conv_ln_b[l]))
        c = c * jax.nn.silu(g_conv)

        y = jnp.concatenate([att, c], axis=-1) @ w_out[l]
        x = x + y
    return x


import jax as _jax
import jax.numpy as _jnp

TWIN_FORMAT = 'train_step'
FWD_PARAMS = ['x', 'positions', 'ln_g', 'w_in', 'q_a_norm', 'w_q_up', 'kv_a_norm', 'w_kv_up', 'q_norm', 'k_norm', 'w_dw', 'b_dw', 'conv_ln_g', 'conv_ln_b', 'w_out']
TWIN_WEIGHTS = ['ln_g', 'w_in', 'q_a_norm', 'w_q_up', 'kv_a_norm', 'w_kv_up', 'q_norm', 'k_norm', 'w_dw', 'b_dw', 'conv_ln_g', 'conv_ln_b', 'w_out']
TWIN_DIFF_INPUT = 'x'
TWIN_INPUTS = ['x', 'positions', 'ln_g', 'w_in', 'q_a_norm', 'w_q_up', 'kv_a_norm', 'w_kv_up', 'q_norm', 'k_norm', 'w_dw', 'b_dw', 'conv_ln_g', 'conv_ln_b', 'w_out', 'loss_target', 'm_ln_g', 'm_w_in', 'm_q_a_norm', 'm_w_q_up', 'm_kv_a_norm', 'm_w_kv_up', 'm_q_norm', 'm_k_norm', 'm_w_dw', 'm_b_dw', 'm_conv_ln_g', 'm_conv_ln_b', 'm_w_out', 'v_ln_g', 'v_w_in', 'v_q_a_norm', 'v_w_q_up', 'v_kv_a_norm', 'v_w_kv_up', 'v_q_norm', 'v_k_norm', 'v_w_dw', 'v_b_dw', 'v_conv_ln_g', 'v_conv_ln_b', 'v_w_out']
TWIN_OUTPUTS = ['loss', 'grad_x', 'grad_ln_g', 'grad_w_in', 'grad_q_a_norm', 'grad_w_q_up', 'grad_kv_a_norm', 'grad_w_kv_up', 'grad_q_norm', 'grad_k_norm', 'grad_w_dw', 'grad_b_dw', 'grad_conv_ln_g', 'grad_conv_ln_b', 'grad_w_out', 'delta_ln_g', 'delta_w_in', 'delta_q_a_norm', 'delta_w_q_up', 'delta_kv_a_norm', 'delta_w_kv_up', 'delta_q_norm', 'delta_k_norm', 'delta_w_dw', 'delta_b_dw', 'delta_conv_ln_g', 'delta_conv_ln_b', 'delta_w_out', 'new_m_ln_g', 'new_m_w_in', 'new_m_q_a_norm', 'new_m_w_q_up', 'new_m_kv_a_norm', 'new_m_w_kv_up', 'new_m_q_norm', 'new_m_k_norm', 'new_m_w_dw', 'new_m_b_dw', 'new_m_conv_ln_g', 'new_m_conv_ln_b', 'new_m_w_out', 'new_v_ln_g', 'new_v_w_in', 'new_v_q_a_norm', 'new_v_w_q_up', 'new_v_kv_a_norm', 'new_v_w_kv_up', 'new_v_q_norm', 'new_v_k_norm', 'new_v_w_dw', 'new_v_b_dw', 'new_v_conv_ln_g', 'new_v_conv_ln_b', 'new_v_w_out']
TWIN_LEAF_KINDS = {'loss': 'loss', 'grad_x': 'grad_x', 'grad_ln_g': 'grad_w', 'grad_w_in': 'grad_w', 'grad_q_a_norm': 'grad_w', 'grad_w_q_up': 'grad_w', 'grad_kv_a_norm': 'grad_w', 'grad_w_kv_up': 'grad_w', 'grad_q_norm': 'grad_w', 'grad_k_norm': 'grad_w', 'grad_w_dw': 'grad_w', 'grad_b_dw': 'grad_w', 'grad_conv_ln_g': 'grad_w', 'grad_conv_ln_b': 'grad_w', 'grad_w_out': 'grad_w', 'delta_ln_g': 'delta_w', 'delta_w_in': 'delta_w', 'delta_q_a_norm': 'delta_w', 'delta_w_q_up': 'delta_w', 'delta_kv_a_norm': 'delta_w', 'delta_w_kv_up': 'delta_w', 'delta_q_norm': 'delta_w', 'delta_k_norm': 'delta_w', 'delta_w_dw': 'delta_w', 'delta_b_dw': 'delta_w', 'delta_conv_ln_g': 'delta_w', 'delta_conv_ln_b': 'delta_w', 'delta_w_out': 'delta_w', 'new_m_ln_g': 'new_m', 'new_m_w_in': 'new_m', 'new_m_q_a_norm': 'new_m', 'new_m_w_q_up': 'new_m', 'new_m_kv_a_norm': 'new_m', 'new_m_w_kv_up': 'new_m', 'new_m_q_norm': 'new_m', 'new_m_k_norm': 'new_m', 'new_m_w_dw': 'new_m', 'new_m_b_dw': 'new_m', 'new_m_conv_ln_g': 'new_m', 'new_m_conv_ln_b': 'new_m', 'new_m_w_out': 'new_m', 'new_v_ln_g': 'new_v', 'new_v_w_in': 'new_v', 'new_v_q_a_norm': 'new_v', 'new_v_w_q_up': 'new_v', 'new_v_kv_a_norm': 'new_v', 'new_v_w_kv_up': 'new_v', 'new_v_q_norm': 'new_v', 'new_v_k_norm': 'new_v', 'new_v_w_dw': 'new_v', 'new_v_b_dw': 'new_v', 'new_v_conv_ln_g': 'new_v', 'new_v_conv_ln_b': 'new_v', 'new_v_w_out': 'new_v'}


def _forward(args):
    return _fwd_reference(*[args[k] for k in FWD_PARAMS])


def _output_shape():
    def fwd():
        inp = _fwd_setup_inputs(0)
        return _fwd_reference(*[inp[k] for k in FWD_PARAMS])
    out = _jax.eval_shape(fwd)
    return out.shape, out.dtype

N_MICROBATCH = 1
ADAM_LR = 0.001
ADAM_B1 = 0.9
ADAM_B2 = 0.999
ADAM_EPS = 1e-08
ADAM_WD = 0.01
ADAM_STEP = 10
PER_EXAMPLE_BATCH_AXIS = {'x': 0, 'positions': 0, 'loss_target': 0}
SHARED_INPUTS = []
_WEIGHT_DTYPES = {'ln_g': _jnp.float32, 'w_in': _jnp.float32, 'q_a_norm': _jnp.float32, 'w_q_up': _jnp.float32, 'kv_a_norm': _jnp.float32, 'w_kv_up': _jnp.float32, 'q_norm': _jnp.float32, 'k_norm': _jnp.float32, 'w_dw': _jnp.float32, 'b_dw': _jnp.float32, 'conv_ln_g': _jnp.float32, 'conv_ln_b': _jnp.float32, 'w_out': _jnp.float32}
MOMENT_SCALE = {'ln_g': 5.756194e-01, 'w_in': 3.057920e-02, 'q_a_norm': 1.447740e-02, 'w_q_up': 8.372428e-03, 'kv_a_norm': 1.464021e-01, 'w_kv_up': 1.209875e-02, 'q_norm': 2.020701e-01, 'k_norm': 2.021096e-01, 'w_dw': 4.221157e-02, 'b_dw': 2.884347e-01, 'conv_ln_g': 1.246020e+00, 'conv_ln_b': 8.042914e-01, 'w_out': 2.975180e-02}


def _to_microbatches(a, axis):
    t = _jnp.moveaxis(a, axis, 0)
    t = t.reshape((N_MICROBATCH, t.shape[0] // N_MICROBATCH) + t.shape[1:])
    return _jnp.moveaxis(t, 1, axis + 1)


def setup_inputs(seed: int = 0) -> dict:
    inp = _fwd_setup_inputs(seed)
    key = _jax.random.fold_in(_jax.random.key(seed), 7919)
    shape, _ = _output_shape()
    out = dict(inp)
    out["loss_target"] = _jax.random.normal(_jax.random.fold_in(key, 0), shape, _jnp.float32)
    for i, name in enumerate(TWIN_WEIGHTS):
        w = inp[name].astype(_jnp.float32)
        if MOMENT_SCALE is None:
            s = _jnp.sqrt(_jnp.mean(_jnp.square(w)) + 1e-30)
        else:
            s = MOMENT_SCALE[name]
        km, kv = _jax.random.split(_jax.random.fold_in(key, i + 1))
        out[name] = w
        out["m_" + name] = s * _jax.random.normal(km, w.shape, _jnp.float32)
        out["v_" + name] = (s * s) * _jax.random.uniform(kv, w.shape, _jnp.float32, 0.5, 1.5)
    if N_MICROBATCH > 1:
        for name, axis in PER_EXAMPLE_BATCH_AXIS.items():
            out[name] = _to_microbatches(out[name], axis)
    return {'x': out['x'], 'positions': out['positions'], 'ln_g': out['ln_g'], 'w_in': out['w_in'], 'q_a_norm': out['q_a_norm'], 'w_q_up': out['w_q_up'], 'kv_a_norm': out['kv_a_norm'], 'w_kv_up': out['w_kv_up'], 'q_norm': out['q_norm'], 'k_norm': out['k_norm'], 'w_dw': out['w_dw'], 'b_dw': out['b_dw'], 'conv_ln_g': out['conv_ln_g'], 'conv_ln_b': out['conv_ln_b'], 'w_out': out['w_out'], 'loss_target': out['loss_target'], 'm_ln_g': out['m_ln_g'], 'm_w_in': out['m_w_in'], 'm_q_a_norm': out['m_q_a_norm'], 'm_w_q_up': out['m_w_q_up'], 'm_kv_a_norm': out['m_kv_a_norm'], 'm_w_kv_up': out['m_w_kv_up'], 'm_q_norm': out['m_q_norm'], 'm_k_norm': out['m_k_norm'], 'm_w_dw': out['m_w_dw'], 'm_b_dw': out['m_b_dw'], 'm_conv_ln_g': out['m_conv_ln_g'], 'm_conv_ln_b': out['m_conv_ln_b'], 'm_w_out': out['m_w_out'], 'v_ln_g': out['v_ln_g'], 'v_w_in': out['v_w_in'], 'v_q_a_norm': out['v_q_a_norm'], 'v_w_q_up': out['v_w_q_up'], 'v_kv_a_norm': out['v_kv_a_norm'], 'v_w_kv_up': out['v_w_kv_up'], 'v_q_norm': out['v_q_norm'], 'v_k_norm': out['v_k_norm'], 'v_w_dw': out['v_w_dw'], 'v_b_dw': out['v_b_dw'], 'v_conv_ln_g': out['v_conv_ln_g'], 'v_conv_ln_b': out['v_conv_ln_b'], 'v_w_out': out['v_w_out']}


def _loss(weights, diff, rest, loss_target):
    with _jax.named_scope("forward"):
        args = {**rest, TWIN_DIFF_INPUT: diff, **{k: w.astype(_WEIGHT_DTYPES[k]) for k, w in weights.items()}}
        y = _forward(args)
    with _jax.named_scope("loss_head"):
        err = _jnp.square(y.astype(_jnp.float32) - loss_target)
        return 0.5 * _jnp.sum(_jnp.mean(err, axis=-1)) if err.ndim else 0.5 * err


def _adamw(w, g, m, v):
    m = ADAM_B1 * m + (1.0 - ADAM_B1) * g
    v = ADAM_B2 * v + (1.0 - ADAM_B2) * _jnp.square(g)
    m_hat = m / (1.0 - ADAM_B1 ** ADAM_STEP)
    v_hat = v / (1.0 - ADAM_B2 ** ADAM_STEP)
    delta = -ADAM_LR * (m_hat / (_jnp.sqrt(v_hat) + ADAM_EPS) + ADAM_WD * w)
    return delta, m, v


def reference(x, positions, ln_g, w_in, q_a_norm, w_q_up, kv_a_norm, w_kv_up, q_norm, k_norm, w_dw, b_dw, conv_ln_g, conv_ln_b, w_out, loss_target, m_ln_g, m_w_in, m_q_a_norm, m_w_q_up, m_kv_a_norm, m_w_kv_up, m_q_norm, m_k_norm, m_w_dw, m_b_dw, m_conv_ln_g, m_conv_ln_b, m_w_out, v_ln_g, v_w_in, v_q_a_norm, v_w_q_up, v_kv_a_norm, v_w_kv_up, v_q_norm, v_k_norm, v_w_dw, v_b_dw, v_conv_ln_g, v_conv_ln_b, v_w_out):
    given = dict(x=x, positions=positions, ln_g=ln_g, w_in=w_in, q_a_norm=q_a_norm, w_q_up=w_q_up, kv_a_norm=kv_a_norm, w_kv_up=w_kv_up, q_norm=q_norm, k_norm=k_norm, w_dw=w_dw, b_dw=b_dw, conv_ln_g=conv_ln_g, conv_ln_b=conv_ln_b, w_out=w_out, loss_target=loss_target, m_ln_g=m_ln_g, m_w_in=m_w_in, m_q_a_norm=m_q_a_norm, m_w_q_up=m_w_q_up, m_kv_a_norm=m_kv_a_norm, m_w_kv_up=m_w_kv_up, m_q_norm=m_q_norm, m_k_norm=m_k_norm, m_w_dw=m_w_dw, m_b_dw=m_b_dw, m_conv_ln_g=m_conv_ln_g, m_conv_ln_b=m_conv_ln_b, m_w_out=m_w_out, v_ln_g=v_ln_g, v_w_in=v_w_in, v_q_a_norm=v_q_a_norm, v_w_q_up=v_w_q_up, v_kv_a_norm=v_kv_a_norm, v_w_kv_up=v_w_kv_up, v_q_norm=v_q_norm, v_k_norm=v_k_norm, v_w_dw=v_w_dw, v_b_dw=v_b_dw, v_conv_ln_g=v_conv_ln_g, v_conv_ln_b=v_conv_ln_b, v_w_out=v_w_out)
    weights = {n: given[n] for n in TWIN_WEIGHTS}
    shared = {n: given[n] for n in SHARED_INPUTS}
    per_example = {n: given[n] for n in ['x', 'positions']}
    grad_fn = _jax.value_and_grad(_loss, argnums=(0, 1))

    def one_microbatch(ex, loss_target):
        ex = dict(ex)
        diff = ex.pop(TWIN_DIFF_INPUT)
        return grad_fn(weights, diff, {**shared, **ex}, loss_target)

    if N_MICROBATCH == 1:
        loss, (grad_w, grad_x) = one_microbatch(per_example, given["loss_target"])
    else:
        def body(carry, xs):
            loss_sum, grad_sum = carry
            l_k, (gw_k, gx_k) = one_microbatch(xs[0], xs[1])
            with _jax.named_scope("update"):
                return (loss_sum + l_k, _jax.tree.map(_jnp.add, grad_sum, gw_k)), gx_k

        init = (_jnp.zeros((), _jnp.float32), _jax.tree.map(_jnp.zeros_like, weights))
        (loss, grad_w), grad_x = _jax.lax.scan(body, init, (per_example, given["loss_target"]))
    with _jax.named_scope("update"):
        delta_w, new_m, new_v = {}, {}, {}
        for n in TWIN_WEIGHTS:
            delta_w[n], new_m[n], new_v[n] = _adamw(weights[n], grad_w[n], given["m_" + n], given["v_" + n])
    return (loss, grad_x, *[grad_w[n] for n in TWIN_WEIGHTS], *[delta_w[n] for n in TWIN_WEIGHTS],
            *[new_m[n] for n in TWIN_WEIGHTS], *[new_v[n] for n in TWIN_WEIGHTS])
```

```python
import functools
import math

import jax
import jax.numpy as jnp
from jax import lax
from jax.experimental import pallas as pl
from jax.experimental.pallas import tpu as pltpu

F32 = jnp.float32
BF16 = jnp.bfloat16
MESH = pl.DeviceIdType.MESH

N_DEV = 8
NOPE = 128
ROPE = 64
VDIM = 128
HEAD_PAD = 256
QK_DIM = NOPE + ROPE
CONV_K = 31
HALO = 32
EPS = 1e-6
ROPE_THETA = 10000.0
NEG = -1e30

ADAM_LR = 0.001
ADAM_B1 = 0.9
ADAM_B2 = 0.999
ADAM_EPS = 1e-08
ADAM_WD = 0.01
ADAM_STEP = 10

TOK_TILE = 256
CONV_TILE = 256
ATT_TQ = 512
ATT_TK = 512
MM_TM = 512
MM_TN = 1024
MM_TK = 512
ADAM_BLOCK_ELEMS = 128 * 1024
LANE_CHUNK = 256
VMEM_LIMIT = 56 * 1024 * 1024


def _pcall(body, **kw):
    return pl.pallas_call(body, **kw)


def _tile(dim, pref, mult):
    t = min(pref, dim)
    t -= t % mult
    while t >= mult:
        if dim % t == 0:
            return t
        t -= mult
    return dim


def _params(sem):
    return pltpu.CompilerParams(dimension_semantics=sem, vmem_limit_bytes=VMEM_LIMIT)


def _sigmoid(v):
    return 1.0 / (1.0 + jnp.exp(-v))


def _dsilu(v, sg):
    return sg * (1.0 + v * (1.0 - sg))


def _mm(a, b, *, name, trans_a=False, add=None, out_dtype=F32, tm=None, tn=None, tk=None):
    if trans_a:
        kdim, m = a.shape
    else:
        m, kdim = a.shape
    n = b.shape[1]
    assert b.shape[0] == kdim
    tm = _tile(m, tm or MM_TM, 128)
    tn = _tile(n, tn or MM_TN, 128)
    tk = _tile(kdim, tk or MM_TK, 128)
    nk = kdim // tk
    has_add = add is not None

    def body(*refs):
        if has_add:
            a_ref, b_ref, add_ref, o_ref, acc_ref = refs
        else:
            a_ref, b_ref, o_ref, acc_ref = refs
        k = pl.program_id(2)

        @pl.when(k == 0)
        def _():
            acc_ref[...] = jnp.zeros_like(acc_ref)

        if trans_a:
            acc_ref[...] += lax.dot_general(a_ref[...], b_ref[...], (((0,), (0,)), ((), ())),
                                            preferred_element_type=F32)
        else:
            acc_ref[...] += jnp.dot(a_ref[...], b_ref[...], preferred_element_type=F32)

        @pl.when(k == nk - 1)
        def _():
            r = acc_ref[...]
            if has_add:
                r = r + add_ref[...]
            o_ref[...] = r.astype(o_ref.dtype)

    if trans_a:
        a_spec = pl.BlockSpec((tk, tm), lambda i, j, k: (k, i))
    else:
        a_spec = pl.BlockSpec((tm, tk), lambda i, j, k: (i, k))
    in_specs = [a_spec, pl.BlockSpec((tk, tn), lambda i, j, k: (k, j))]
    args = [a, b]
    if has_add:
        in_specs.append(pl.BlockSpec((tm, tn), lambda i, j, k: (i, j)))
        args.append(add)
    return _pcall(
        body, name=name,
        grid=(m // tm, n // tn, nk),
        in_specs=in_specs,
        out_specs=pl.BlockSpec((tm, tn), lambda i, j, k: (i, j)),
        out_shape=jax.ShapeDtypeStruct((m, n), out_dtype),
        scratch_shapes=[pltpu.VMEM((tm, tn), F32)],
        compiler_params=_params(("parallel", "parallel", "arbitrary")),
    )(*args)


def _rms_fwd(x, g, *, name):
    t, d = x.shape
    tt = _tile(t, TOK_TILE, 16)

    def body(x_ref, g_ref, h_ref):
        xv = x_ref[...]
        r = lax.rsqrt(jnp.mean(xv * xv, axis=-1, keepdims=True) + EPS)
        h_ref[...] = (xv * r * g_ref[...]).astype(BF16)

    return _pcall(
        body, name=name, grid=(t // tt,),
        in_specs=[pl.BlockSpec((tt, d), lambda i: (i, 0)), pl.BlockSpec((1, d), lambda i: (0, 0))],
        out_specs=pl.BlockSpec((tt, d), lambda i: (i, 0)),
        out_shape=jax.ShapeDtypeStruct((t, d), BF16),
        compiler_params=_params(("parallel",)),
    )(x, g.reshape(1, d))


def _rms_bwd(x, g, dh, dres, *, name):
    t, d = x.shape
    tt = _tile(t, TOK_TILE, 16)

    def body(x_ref, g_ref, dh_ref, dres_ref, dx_ref, dxb_ref, dg_ref):
        xv = x_ref[...]
        r = lax.rsqrt(jnp.mean(xv * xv, axis=-1, keepdims=True) + EPS)
        dy = dh_ref[...]
        dyg = dy * g_ref[...]
        dot = jnp.sum(dyg * xv, axis=-1, keepdims=True) * (1.0 / d)
        dx = dres_ref[...] + r * dyg - xv * (r * r * r) * dot
        dx_ref[...] = dx
        dxb_ref[...] = dx.astype(BF16)

        @pl.when(pl.program_id(0) == 0)
        def _():
            dg_ref[...] = jnp.zeros_like(dg_ref)

        dg_ref[...] += jnp.sum(dy * xv * r, axis=0, keepdims=True)

    row = pl.BlockSpec((tt, d), lambda i: (i, 0))
    vec = pl.BlockSpec((1, d), lambda i: (0, 0))
    return _pcall(
        body, name=name, grid=(t // tt,),
        in_specs=[row, vec, row, row],
        out_specs=[row, row, vec],
        out_shape=[jax.ShapeDtypeStruct((t, d), F32), jax.ShapeDtypeStruct((t, d), BF16),
                   jax.ShapeDtypeStruct((1, d), F32)],
        compiler_params=_params(("arbitrary",)),
    )(x, g.reshape(1, d), dh, dres)


def _lat_fwd(z, gq, gkv, lay, *, name):
    t = z.shape[0]
    ql, kvl = lay["QL"], lay["KVL"]
    tt = _tile(t, TOK_TILE, 16)

    def body(q_ref, kv_ref, gq_ref, gkv_ref, qn_ref, kvn_ref):
        for src, g_ref, dst in ((q_ref, gq_ref, qn_ref), (kv_ref, gkv_ref, kvn_ref)):
            v = src[...]
            r = lax.rsqrt(jnp.mean(v * v, axis=-1, keepdims=True) + EPS)
            dst[...] = (v * r * g_ref[...]).astype(BF16)

    return _pcall(
        body, name=name, grid=(t // tt,),
        in_specs=[pl.BlockSpec((tt, ql), lambda i: (i, lay["QC"] // ql)),
                  pl.BlockSpec((tt, kvl), lambda i: (i, lay["KVC"] // kvl)),
                  pl.BlockSpec((1, ql), lambda i: (0, 0)), pl.BlockSpec((1, kvl), lambda i: (0, 0))],
        out_specs=[pl.BlockSpec((tt, ql), lambda i: (i, 0)), pl.BlockSpec((tt, kvl), lambda i: (i, 0))],
        out_shape=[jax.ShapeDtypeStruct((t, ql), BF16), jax.ShapeDtypeStruct((t, kvl), BF16)],
        compiler_params=_params(("parallel",)),
    )(z, z, gq.reshape(1, ql), gkv.reshape(1, kvl))


def _lat_bwd(z, gq, gkv, dqn, dkvn, lay, *, name):
    t = z.shape[0]
    ql, kvl = lay["QL"], lay["KVL"]
    tt = _tile(t, TOK_TILE, 16)

    def body(q_ref, kv_ref, gq_ref, gkv_ref, dqn_ref, dkvn_ref, dq_ref, dkv_ref, dgq_ref, dgkv_ref):
        first = pl.program_id(0) == 0
        for src, g_ref, dy_ref, dst, dg_ref in ((q_ref, gq_ref, dqn_ref, dq_ref, dgq_ref),
                                                (kv_ref, gkv_ref, dkvn_ref, dkv_ref, dgkv_ref)):
            v = src[...]
            n = v.shape[-1]
            r = lax.rsqrt(jnp.mean(v * v, axis=-1, keepdims=True) + EPS)
            dy = dy_ref[...]
            dyg = dy * g_ref[...]
            dot = jnp.sum(dyg * v, axis=-1, keepdims=True) * (1.0 / n)
            dst[...] = (r * dyg - v * (r * r * r) * dot).astype(BF16)

            @pl.when(first)
            def _():
                dg_ref[...] = jnp.zeros_like(dg_ref)

            dg_ref[...] += jnp.sum(dy * v * r, axis=0, keepdims=True)

    return _pcall(
        body, name=name, grid=(t // tt,),
        in_specs=[pl.BlockSpec((tt, ql), lambda i: (i, lay["QC"] // ql)),
                  pl.BlockSpec((tt, kvl), lambda i: (i, lay["KVC"] // kvl)),
                  pl.BlockSpec((1, ql), lambda i: (0, 0)), pl.BlockSpec((1, kvl), lambda i: (0, 0)),
                  pl.BlockSpec((tt, ql), lambda i: (i, 0)), pl.BlockSpec((tt, kvl), lambda i: (i, 0))],
        out_specs=[pl.BlockSpec((tt, ql), lambda i: (i, 0)), pl.BlockSpec((tt, kvl), lambda i: (i, 0)),
                   pl.BlockSpec((1, ql), lambda i: (0, 0)), pl.BlockSpec((1, kvl), lambda i: (0, 0))],
        out_shape=[jax.ShapeDtypeStruct((t, ql), BF16), jax.ShapeDtypeStruct((t, kvl), BF16),
                   jax.ShapeDtypeStruct((1, ql), F32), jax.ShapeDtypeStruct((1, kvl), F32)],
        compiler_params=_params(("arbitrary",)),
    )(z, z, gq.reshape(1, ql), gkv.reshape(1, kvl), dqn, dkvn)


def _rope(r, c_tab, sa_tab, sb_tab):
    return r * c_tab + pltpu.roll(r, 96, 1) * sa_tab + pltpu.roll(r, 32, 1) * sb_tab


def _rope_t(d, c_tab, sa_tab, sb_tab):
    return d * c_tab + pltpu.roll(d * sa_tab, 32, 1) + pltpu.roll(d * sb_tab, 96, 1)


def _heads_fwd(q_raw, kv_raw, z, tabs, gq, gk, lay, *, name):
    t = z.shape[0]
    h = lay["H"]
    tt = _tile(t, TOK_TILE, 16)
    scale = 1.0 / math.sqrt(QK_DIM)

    def body(q_ref, kv_ref, pe_ref, c_ref, sa_ref, sb_ref, gq_ref, gk_ref, qh_ref, kh_ref, vh_ref):
        c_tab, sa_tab, sb_tab = c_ref[...], sa_ref[...], sb_ref[...]
        q = q_ref[...]
        r = lax.rsqrt(jnp.sum(q * q, axis=-1, keepdims=True) * (1.0 / QK_DIM) + EPS)
        qn = q * r * gq_ref[...]
        qh_ref[...] = (jnp.concatenate([qn[:, :NOPE], _rope(qn[:, NOPE:], c_tab, sa_tab, sb_tab)], axis=1)
                       * scale).astype(BF16)
        kv = kv_ref[...]
        kn, pe = kv[:, :NOPE], pe_ref[...]
        ss = jnp.sum(kn * kn, axis=-1, keepdims=True) + jnp.sum(pe * pe, axis=-1, keepdims=True)
        rk = lax.rsqrt(ss * (1.0 / QK_DIM) + EPS)
        gk_v = gk_ref[...]
        kh_ref[...] = jnp.concatenate(
            [kn * rk * gk_v[:, :NOPE], _rope(pe * rk * gk_v[:, NOPE:], c_tab, sa_tab, sb_tab)], axis=1).astype(BF16)
        vh_ref[...] = kv[:, NOPE:].astype(BF16)

    head = pl.BlockSpec((tt, HEAD_PAD), lambda i, j: (i, j))
    tab = pl.BlockSpec((tt, 128), lambda i, j: (i, 0))
    gain = pl.BlockSpec((1, HEAD_PAD), lambda i, j: (0, 0))
    return _pcall(
        body, name=name, grid=(t // tt, h),
        in_specs=[head, head, pl.BlockSpec((tt, 128), lambda i, j: (i, lay["KPE"] // 128)), tab, tab, tab, gain, gain],
        out_specs=[pl.BlockSpec((None, tt, HEAD_PAD), lambda i, j: (j, i, 0)),
                   pl.BlockSpec((None, tt, HEAD_PAD), lambda i, j: (j, i, 0)),
                   pl.BlockSpec((None, tt, VDIM), lambda i, j: (j, i, 0))],
        out_shape=[jax.ShapeDtypeStruct((h, t, HEAD_PAD), BF16), jax.ShapeDtypeStruct((h, t, HEAD_PAD), BF16),
                   jax.ShapeDtypeStruct((h, t, VDIM), BF16)],
        compiler_params=_params(("parallel", "parallel")),
    )(q_raw, kv_raw, z, *tabs, gq, gk)


def _heads_bwd(q_raw, kv_raw, z, tabs, gq, gk, dqh, dkh, dvh, lay, *, name):
    t = z.shape[0]
    h = lay["H"]
    tt = _tile(t, TOK_TILE, 16)
    scale = 1.0 / math.sqrt(QK_DIM)

    def body(q_ref, kv_ref, pe_ref, c_ref, sa_ref, sb_ref, gq_ref, gk_ref, dqh_ref, dkh_ref, dvh_ref,
             dq_ref, dkv_ref, dpe_ref, dgq_ref, dgk_ref):
        i, j = pl.program_id(0), pl.program_id(1)
        c_tab, sa_tab, sb_tab = c_ref[...], sa_ref[...], sb_ref[...]

        @pl.when((i == 0) & (j == 0))
        def _():
            dgq_ref[...] = jnp.zeros_like(dgq_ref)
            dgk_ref[...] = jnp.zeros_like(dgk_ref)

        @pl.when(j == 0)
        def _():
            dpe_ref[...] = jnp.zeros_like(dpe_ref)

        def norm_bwd(v, g, dy):
            r = lax.rsqrt(jnp.sum(v * v, axis=-1, keepdims=True) * (1.0 / QK_DIM) + EPS)
            dyg = dy * g
            dot = jnp.sum(dyg * v, axis=-1, keepdims=True) * (1.0 / QK_DIM)
            return r * dyg - v * (r * r * r) * dot, jnp.sum(dy * v * r, axis=0, keepdims=True)

        dqo = dqh_ref[...] * scale
        dy = jnp.concatenate([dqo[:, :NOPE], _rope_t(dqo[:, NOPE:], c_tab, sa_tab, sb_tab)], axis=1)
        dq, dgq = norm_bwd(q_ref[...], gq_ref[...], dy)
        dq_ref[...] = dq.astype(BF16)
        dgq_ref[...] += dgq

        dko = dkh_ref[...]
        dy = jnp.concatenate([dko[:, :NOPE], _rope_t(dko[:, NOPE:], c_tab, sa_tab, sb_tab)], axis=1)
        kfull = jnp.concatenate([kv_ref[...][:, :NOPE], pe_ref[...]], axis=1)
        dk, dgk = norm_bwd(kfull, gk_ref[...], dy)
        dkv_ref[...] = jnp.concatenate([dk[:, :NOPE], dvh_ref[...]], axis=1).astype(BF16)
        dpe_ref[...] += dk[:, NOPE:]
        dgk_ref[...] += dgk

    head = pl.BlockSpec((tt, HEAD_PAD), lambda i, j: (i, j))
    tab = pl.BlockSpec((tt, 128), lambda i, j: (i, 0))
    gain = pl.BlockSpec((1, HEAD_PAD), lambda i, j: (0, 0))
    hm = pl.BlockSpec((None, tt, HEAD_PAD), lambda i, j: (j, i, 0))
    return _pcall(
        body, name=name, grid=(t // tt, h),
        in_specs=[head, head, pl.BlockSpec((tt, 128), lambda i, j: (i, lay["KPE"] // 128)), tab, tab, tab, gain, gain,
                  hm, hm, pl.BlockSpec((None, tt, VDIM), lambda i, j: (j, i, 0))],
        out_specs=[head, head, tab, gain, gain],
        out_shape=[jax.ShapeDtypeStruct((t, h * HEAD_PAD), BF16), jax.ShapeDtypeStruct((t, h * HEAD_PAD), BF16),
                   jax.ShapeDtypeStruct((t, 128), F32),
                   jax.ShapeDtypeStruct((1, HEAD_PAD), F32), jax.ShapeDtypeStruct((1, HEAD_PAD), F32)],
        compiler_params=_params(("arbitrary", "arbitrary")),
    )(q_raw, kv_raw, z, *tabs, gq, gk, dqh, dkh, dvh)


def _causal_mask(i, j, tq, tk):
    qpos = i * tq + lax.broadcasted_iota(jnp.int32, (tq, tk), 0)
    kpos = j * tk + lax.broadcasted_iota(jnp.int32, (tq, tk), 1)
    return kpos <= qpos


def _qk(q, k):
    return lax.dot_general(q, k, (((1,), (1,)), ((), ())), preferred_element_type=F32)


def _flash_fwd(qh, kh, vh, z, lay, nb, *, name):
    h, t, _ = qh.shape
    s = t // nb
    tq, tk = _tile(s, ATT_TQ, 128), _tile(s, ATT_TK, 128)
    nq, nk = s // tq, s // tk
    att_w = h * VDIM
    gblk = lay["GATT"] // VDIM

    def body(q_ref, k_ref, v_ref, g_ref, att_ref, mix_ref, lse_ref, m_ref, l_ref, acc_ref):
        i, j = pl.program_id(2), pl.program_id(3)

        @pl.when(j == 0)
        def _():
            m_ref[...] = jnp.full_like(m_ref, NEG)
            l_ref[...] = jnp.zeros_like(l_ref)
            acc_ref[...] = jnp.zeros_like(acc_ref)

        @pl.when(j * tk <= i * tq + tq - 1)
        def _():
            sc = jnp.where(_causal_mask(i, j, tq, tk), _qk(q_ref[...], k_ref[...]), NEG)
            m_old = m_ref[...]
            m_new = jnp.maximum(m_old, jnp.max(sc, axis=-1, keepdims=True))
            alpha = jnp.exp(m_old - m_new)
            p = jnp.exp(sc - m_new)
            l_ref[...] = alpha * l_ref[...] + jnp.sum(p, axis=-1, keepdims=True)
            acc_ref[...] = alpha * acc_ref[...] + jnp.dot(p.astype(BF16), v_ref[...], preferred_element_type=F32)
            m_ref[...] = m_new

        @pl.when(j == nk - 1)
        def _():
            o = acc_ref[...] / l_ref[...]
            att_ref[...] = o
            g = g_ref[...]
            mix_ref[...] = (o * (g * _sigmoid(g))).astype(BF16)
            lse_ref[...] = m_ref[...] + jnp.log(l_ref[...])

    def kv_idx(hh, b, i, j):
        return (hh, b * nk + jnp.minimum(j, (i * tq + tq - 1) // tk), 0)

    row = lambda hh, b, i, j: (b * nq + i, hh)
    return _pcall(
        body, name=name, grid=(h, nb, nq, nk),
        in_specs=[pl.BlockSpec((None, tq, HEAD_PAD), lambda hh, b, i, j: (hh, b * nq + i, 0)),
                  pl.BlockSpec((None, tk, HEAD_PAD), kv_idx),
                  pl.BlockSpec((None, tk, VDIM), kv_idx),
                  pl.BlockSpec((tq, VDIM), lambda hh, b, i, j: (b * nq + i, gblk + hh))],
        out_specs=[pl.BlockSpec((tq, VDIM), row), pl.BlockSpec((tq, VDIM), row),
                   pl.BlockSpec((None, tq, 1), lambda hh, b, i, j: (hh, b * nq + i, 0))],
        out_shape=[jax.ShapeDtypeStruct((t, att_w), F32), jax.ShapeDtypeStruct((t, 2 * att_w), BF16),
                   jax.ShapeDtypeStruct((h, t, 1), F32)],
        scratch_shapes=[pltpu.VMEM((tq, 1), F32), pltpu.VMEM((tq, 1), F32), pltpu.VMEM((tq, VDIM), F32)],
        compiler_params=_params(("parallel", "parallel", "parallel", "arbitrary")),
    )(qh, kh, vh, z)


def _gate_bwd(dmix, att, z, lay, *, name):
    t, att_w = att.shape
    h = att_w // VDIM
    tt = _tile(t, TOK_TILE, 16)
    gblk = lay["GATT"] // VDIM

    def body(dm_ref, o_ref, g_ref, do_ref, delta_ref, dg_ref):
        dm, o, g = dm_ref[...], o_ref[...], g_ref[...]
        sg = _sigmoid(g)
        do = dm * (g * sg)
        do_ref[...] = do.astype(BF16)
        delta_ref[...] = jnp.sum(do * o, axis=-1, keepdims=True)
        dg_ref[...] = (dm * o * _dsilu(g, sg)).astype(BF16)

    blk = pl.BlockSpec((tt, VDIM), lambda i, j: (i, j))
    return _pcall(
        body, name=name, grid=(t // tt, h),
        in_specs=[blk, blk, pl.BlockSpec((tt, VDIM), lambda i, j: (i, gblk + j))],
        out_specs=[blk, pl.BlockSpec((None, tt, 1), lambda i, j: (j, i, 0)), blk],
        out_shape=[jax.ShapeDtypeStruct((t, att_w), BF16), jax.ShapeDtypeStruct((h, t, 1), F32),
                   jax.ShapeDtypeStruct((t, att_w), BF16)],
        compiler_params=_params(("parallel", "parallel")),
    )(dmix, att, z)


def _flash_bwd_kv(qh, kh, vh, do, lse, delta, nb, *, name):
    h, t, _ = qh.shape
    s = t // nb
    tq, tk = _tile(s, ATT_TQ, 128), _tile(s, ATT_TK, 128)
    nq, nk = s // tq, s // tk

    def body(q_ref, k_ref, v_ref, do_ref, lse_ref, dl_ref, dk_ref, dv_ref, dk_acc, dv_acc):
        j, i = pl.program_id(2), pl.program_id(3)

        @pl.when(i == 0)
        def _():
            dk_acc[...] = jnp.zeros_like(dk_acc)
            dv_acc[...] = jnp.zeros_like(dv_acc)

        @pl.when(j * tk <= i * tq + tq - 1)
        def _():
            q, do_v = q_ref[...], do_ref[...]
            sc = jnp.where(_causal_mask(i, j, tq, tk), _qk(q, k_ref[...]), NEG)
            p = jnp.exp(sc - lse_ref[...])
            dv_acc[...] += lax.dot_general(p.astype(BF16), do_v, (((0,), (0,)), ((), ())),
                                           preferred_element_type=F32)
            dp = _qk(do_v, v_ref[...])
            ds = (p * (dp - dl_ref[...])).astype(BF16)
            dk_acc[...] += lax.dot_general(ds, q, (((0,), (0,)), ((), ())), preferred_element_type=F32)

        @pl.when(i == nq - 1)
        def _():
            dk_ref[...] = dk_acc[...]
            dv_ref[...] = dv_acc[...]

    def q_blk(b, i, j):
        return b * nq + jnp.maximum(i, (j * tk) // tq)

    kv = lambda hh, b, j, i: (hh, b * nk + j, 0)
    return _pcall(
        body, name=name, grid=(h, nb, nk, nq),
        in_specs=[pl.BlockSpec((None, tq, HEAD_PAD), lambda hh, b, j, i: (hh, q_blk(b, i, j), 0)),
                  pl.BlockSpec((None, tk, HEAD_PAD), kv),
                  pl.BlockSpec((None, tk, VDIM), kv),
                  pl.BlockSpec((tq, VDIM), lambda hh, b, j, i: (q_blk(b, i, j), hh)),
                  pl.BlockSpec((None, tq, 1), lambda hh, b, j, i: (hh, q_blk(b, i, j), 0)),
                  pl.BlockSpec((None, tq, 1), lambda hh, b, j, i: (hh, q_blk(b, i, j), 0))],
        out_specs=[pl.BlockSpec((None, tk, HEAD_PAD), kv), pl.BlockSpec((None, tk, VDIM), kv)],
        out_shape=[jax.ShapeDtypeStruct((h, t, HEAD_PAD), F32), jax.ShapeDtypeStruct((h, t, VDIM), F32)],
        scratch_shapes=[pltpu.VMEM((tk, HEAD_PAD), F32), pltpu.VMEM((tk, VDIM), F32)],
        compiler_params=_params(("parallel", "parallel", "parallel", "arbitrary")),
    )(qh, kh, vh, do, lse, delta)


def _flash_bwd_q(qh, kh, vh, do, lse, delta, nb, *, name):
    h, t, _ = qh.shape
    s = t // nb
    tq, tk = _tile(s, ATT_TQ, 128), _tile(s, ATT_TK, 128)
    nq, nk = s // tq, s // tk

    def body(q_ref, k_ref, v_ref, do_ref, lse_ref, dl_ref, dq_ref, dq_acc):
        i, j = pl.program_id(2), pl.program_id(3)

        @pl.when(j == 0)
        def _():
            dq_acc[...] = jnp.zeros_like(dq_acc)

        @pl.when(j * tk <= i * tq + tq - 1)
        def _():
            k = k_ref[...]
            sc = jnp.where(_causal_mask(i, j, tq, tk), _qk(q_ref[...], k), NEG)
            p = jnp.exp(sc - lse_ref[...])
            dp = _qk(do_ref[...], v_ref[...])
            ds = (p * (dp - dl_ref[...])).astype(BF16)
            dq_acc[...] += jnp.dot(ds, k, preferred_element_type=F32)

        @pl.when(j == nk - 1)
        def _():
            dq_ref[...] = dq_acc[...]

    def kv_idx(hh, b, i, j):
        return (hh, b * nk + jnp.minimum(j, (i * tq + tq - 1) // tk), 0)

    qrow = lambda hh, b, i, j: (hh, b * nq + i, 0)
    return _pcall(
        body, name=name, grid=(h, nb, nq, nk),
        in_specs=[pl.BlockSpec((None, tq, HEAD_PAD), qrow),
                  pl.BlockSpec((None, tk, HEAD_PAD), kv_idx),
                  pl.BlockSpec((None, tk, VDIM), kv_idx),
                  pl.BlockSpec((tq, VDIM), lambda hh, b, i, j: (b * nq + i, hh)),
                  pl.BlockSpec((None, tq, 1), qrow), pl.BlockSpec((None, tq, 1), qrow)],
        out_specs=pl.BlockSpec((None, tq, HEAD_PAD), qrow),
        out_shape=jax.ShapeDtypeStruct((h, t, HEAD_PAD), F32),
        scratch_shapes=[pltpu.VMEM((tq, HEAD_PAD), F32)],
        compiler_params=_params(("parallel", "parallel", "parallel", "arbitrary")),
    )(qh, kh, vh, do, lse, delta)


def _conv_fwd(z, mix, w_dw, b_dw, ln_g, ln_b, lay, nb, *, name):
    t = z.shape[0]
    cw = lay["CW"]
    s = t // nb
    tt = _tile(s, CONV_TILE, HALO)
    ns = s // tt
    hb = tt // HALO
    lc = _tile(cw, LANE_CHUNK, 128)

    def body(a_ref, b_ref, ap_ref, bp_ref, gc_ref, w_ref, bias_ref, lg_ref, lb_ref, mix_in, mix_ref, u_ref, c_ref, ext):
        del mix_in
        i = pl.program_id(1)
        u = a_ref[...] * _sigmoid(b_ref[...])
        u_ref[...] = u
        ext[0:HALO, :] = jnp.where(i > 0, ap_ref[...] * _sigmoid(bp_ref[...]), 0.0)
        ext[HALO:HALO + tt, :] = u
        for c0 in range(0, cw, lc):
            acc = jnp.zeros((tt, lc), F32) + bias_ref[:, c0:c0 + lc]
            for k in range(CONV_K):
                off = HALO - (CONV_K - 1) + k
                acc = acc + w_ref[k:k + 1, c0:c0 + lc] * ext[off:off + tt, c0:c0 + lc]
            c_ref[:, c0:c0 + lc] = acc
        c = c_ref[...]
        mu = jnp.mean(c, axis=-1, keepdims=True)
        xc = c - mu
        var = jnp.mean(xc * xc, axis=-1, keepdims=True)
        y = xc * lax.rsqrt(var + EPS) * lg_ref[...] + lb_ref[...]
        g = gc_ref[...]
        mix_ref[...] = (y * _sigmoid(y) * (g * _sigmoid(g))).astype(BF16)

    cur = lambda col: pl.BlockSpec((tt, cw), lambda b, i: (b * ns + i, col))
    prev = lambda col: pl.BlockSpec((HALO, cw), lambda b, i: (jnp.maximum((b * ns + i) * hb - 1, 0), col))
    vec = pl.BlockSpec((1, cw), lambda b, i: (0, 0))
    out_row = pl.BlockSpec((tt, cw), lambda b, i: (b * ns + i, 0))
    return _pcall(
        body, name=name, grid=(nb, ns),
        in_specs=[cur(lay["A"] // cw), cur(lay["B"] // cw), prev(lay["A"] // cw), prev(lay["B"] // cw),
                  cur(lay["GCONV"] // cw), pl.BlockSpec((HALO, cw), lambda b, i: (0, 0)), vec, vec, vec,
                  pl.BlockSpec(memory_space=pl.ANY)],
        out_specs=[pl.BlockSpec((tt, cw), lambda b, i: (b * ns + i, 1)), out_row, out_row],
        out_shape=[jax.ShapeDtypeStruct(mix.shape, BF16), jax.ShapeDtypeStruct((t, cw), F32),
                   jax.ShapeDtypeStruct((t, cw), F32)],
        scratch_shapes=[pltpu.VMEM((tt + HALO, cw), F32)],
        input_output_aliases={9: 0},
        compiler_params=_params(("parallel", "parallel")),
    )(z, z, z, z, z, w_dw, b_dw.reshape(1, cw), ln_g.reshape(1, cw), ln_b.reshape(1, cw), mix)


def _conv_bwd_ln(c_pre, z, dmix, ln_g, ln_b, lay, *, name):
    t, cw = c_pre.shape
    tt = _tile(t, TOK_TILE, 16)

    def body(c_ref, gc_ref, dm_ref, lg_ref, lb_ref, dc_ref, dgc_ref, dlg_ref, dlb_ref, dbias_ref):
        @pl.when(pl.program_id(0) == 0)
        def _():
            dlg_ref[...] = jnp.zeros_like(dlg_ref)
            dlb_ref[...] = jnp.zeros_like(dlb_ref)
            dbias_ref[...] = jnp.zeros_like(dbias_ref)

        c = c_ref[...]
        mu = jnp.mean(c, axis=-1, keepdims=True)
        xc = c - mu
        rstd = lax.rsqrt(jnp.mean(xc * xc, axis=-1, keepdims=True) + EPS)
        xhat = xc * rstd
        y = xhat * lg_ref[...] + lb_ref[...]
        sy = _sigmoid(y)
        g = gc_ref[...]
        sg = _sigmoid(g)
        dm = dm_ref[...].astype(F32)
        dgc_ref[...] = (dm * (y * sy) * _dsilu(g, sg)).astype(BF16)
        dy = dm * (g * sg) * _dsilu(y, sy)
        dlb_ref[...] += jnp.sum(dy, axis=0, keepdims=True)
        dlg_ref[...] += jnp.sum(dy * xhat, axis=0, keepdims=True)
        dxh = dy * lg_ref[...]
        dc = rstd * (dxh - jnp.mean(dxh, axis=-1, keepdims=True)
                     - xhat * jnp.mean(dxh * xhat, axis=-1, keepdims=True))
        dc_ref[...] = dc
        dbias_ref[...] += jnp.sum(dc, axis=0, keepdims=True)

    row = pl.BlockSpec((tt, cw), lambda i: (i, 0))
    vec = pl.BlockSpec((1, cw), lambda i: (0, 0))
    return _pcall(
        body, name=name, grid=(t // tt,),
        in_specs=[row, pl.BlockSpec((tt, cw), lambda i: (i, lay["GCONV"] // cw)),
                  pl.BlockSpec((tt, cw), lambda i: (i, 1)), vec, vec],
        out_specs=[row, row, vec, vec, vec],
        out_shape=[jax.ShapeDtypeStruct((t, cw), F32), jax.ShapeDtypeStruct((t, cw), BF16),
                   jax.ShapeDtypeStruct((1, cw), F32), jax.ShapeDtypeStruct((1, cw), F32),
                   jax.ShapeDtypeStruct((1, cw), F32)],
        compiler_params=_params(("arbitrary",)),
    )(c_pre, z, dmix, ln_g.reshape(1, cw), ln_b.reshape(1, cw))


def _conv_bwd_dw(dc, u, z, w_dw, lay, nb, *, name):
    t, cw = dc.shape
    s = t // nb
    tt = _tile(s, CONV_TILE, HALO)
    ns = s // tt
    hb = tt // HALO
    lc = _tile(cw, LANE_CHUNK, 128)

    def body(dc_ref, dcn_ref, u_ref, up_ref, a_ref, b_ref, w_ref, dab_ref, dw_ref, ext_dc, ext_u, du_ref):
        b_i, i = pl.program_id(0), pl.program_id(1)

        @pl.when((b_i == 0) & (i == 0))
        def _():
            dw_ref[...] = jnp.zeros_like(dw_ref)

        dc_v = dc_ref[...]
        ext_dc[0:tt, :] = dc_v
        ext_dc[tt:tt + HALO, :] = jnp.where(i < ns - 1, dcn_ref[...], 0.0)
        ext_u[0:HALO, :] = jnp.where(i > 0, up_ref[...], 0.0)
        ext_u[HALO:HALO + tt, :] = u_ref[...]
        for c0 in range(0, cw, lc):
            acc = jnp.zeros((tt, lc), F32)
            dcc = dc_v[:, c0:c0 + lc]
            for k in range(CONV_K):
                acc = acc + w_ref[k:k + 1, c0:c0 + lc] * ext_dc[CONV_K - 1 - k:CONV_K - 1 - k + tt, c0:c0 + lc]
                off = HALO - (CONV_K - 1) + k
                dw_ref[k:k + 1, c0:c0 + lc] += jnp.sum(dcc * ext_u[off:off + tt, c0:c0 + lc], axis=0, keepdims=True)
            du_ref[:, c0:c0 + lc] = acc
        du = du_ref[...]
        sb = _sigmoid(b_ref[...])
        dab_ref[:, 0:cw] = (du * sb).astype(BF16)
        dab_ref[:, cw:2 * cw] = (du * a_ref[...] * sb * (1.0 - sb)).astype(BF16)

    last = nb * ns * hb - 1
    row = pl.BlockSpec((tt, cw), lambda b, i: (b * ns + i, 0))
    return _pcall(
        body, name=name, grid=(nb, ns),
        in_specs=[row, pl.BlockSpec((HALO, cw), lambda b, i: (jnp.minimum((b * ns + i + 1) * hb, last), 0)),
                  row, pl.BlockSpec((HALO, cw), lambda b, i: (jnp.maximum((b * ns + i) * hb - 1, 0), 0)),
                  pl.BlockSpec((tt, cw), lambda b, i: (b * ns + i, lay["A"] // cw)),
                  pl.BlockSpec((tt, cw), lambda b, i: (b * ns + i, lay["B"] // cw)),
                  pl.BlockSpec((HALO, cw), lambda b, i: (0, 0))],
        out_specs=[pl.BlockSpec((tt, 2 * cw), lambda b, i: (b * ns + i, 0)),
                   pl.BlockSpec((HALO, cw), lambda b, i: (0, 0))],
        out_shape=[jax.ShapeDtypeStruct((t, 2 * cw), BF16), jax.ShapeDtypeStruct((HALO, cw), F32)],
        scratch_shapes=[pltpu.VMEM((tt + HALO, cw), F32), pltpu.VMEM((tt + HALO, cw), F32),
                        pltpu.VMEM((tt, cw), F32)],
        compiler_params=_params(("arbitrary", "arbitrary")),
    )(dc, dc, u, u, z, z, w_dw)


def _loss_head(y, target, *, name):
    t, d = y.shape
    tt = _tile(t, TOK_TILE, 16)

    def body(y_ref, t_ref, sse_ref, dy_ref, dyb_ref):
        @pl.when(pl.program_id(0) == 0)
        def _():
            sse_ref[...] = jnp.zeros_like(sse_ref)

        e = y_ref[...] - t_ref[...]
        sse_ref[...] += jnp.sum(e * e)
        dy = e * (1.0 / d)
        dy_ref[...] = dy
        dyb_ref[...] = dy.astype(BF16)

    row = pl.BlockSpec((tt, d), lambda i: (i, 0))
    return _pcall(
        body, name=name, grid=(t // tt,),
        in_specs=[row, row],
        out_specs=[pl.BlockSpec((8, 128), lambda i: (0, 0)), row, row],
        out_shape=[jax.ShapeDtypeStruct((8, 128), F32), jax.ShapeDtypeStruct((t, d), F32),
                   jax.ShapeDtypeStruct((t, d), BF16)],
        compiler_params=_params(("arbitrary",)),
    )(y, target)


def _adam(w, m, v, g_parts, *, name):
    rows, cols = w.shape
    tr = _tile(rows, max(16, ADAM_BLOCK_ELEMS // cols), 16)
    n = len(g_parts)

    def body(*refs):
        w_ref, m_ref, v_ref = refs[:3]
        g_refs = refs[3:3 + n]
        g_out, d_out, m_out, v_out = refs[3 + n:]
        g = g_refs[0][...].astype(F32)
        for r in g_refs[1:]:
            g = g + r[...].astype(F32)
        m_new = ADAM_B1 * m_ref[...] + (1.0 - ADAM_B1) * g
        v_new = ADAM_B2 * v_ref[...] + (1.0 - ADAM_B2) * (g * g)
        m_hat = m_new / (1.0 - ADAM_B1 ** ADAM_STEP)
        v_hat = v_new / (1.0 - ADAM_B2 ** ADAM_STEP)
        g_out[...] = g
        d_out[...] = -ADAM_LR * (m_hat / (jnp.sqrt(v_hat) + ADAM_EPS) + ADAM_WD * w_ref[...])
        m_out[...] = m_new
        v_out[...] = v_new

    blk = pl.BlockSpec((tr, cols), lambda i: (i, 0))
    g_specs, g_args = [], []
    for arr, lead in g_parts:
        g_args.append(arr)
        if lead is None:
            g_specs.append(blk)
        else:
            g_specs.append(pl.BlockSpec((None, tr, cols), functools.partial(lambda i, p: (p, i, 0), p=lead)))
    out = jax.ShapeDtypeStruct((rows, cols), F32)
    return _pcall(
        body, name=name, grid=(rows // tr,),
        in_specs=[blk, blk, blk] + g_specs,
        out_specs=[blk, blk, blk, blk],
        out_shape=[out, out, out, out],
        compiler_params=_params(("parallel",)),
    )(w, m, v, *g_args)


def _position():
    return lax.axis_index("x"), lax.axis_index("y"), lax.axis_index("c")


def _block_id(p):
    return 4 * p[0] + 2 * p[1] + p[2]


def _flip(p, mask):
    return tuple((1 - v) if (mask >> (2 - a)) & 1 else v for a, v in enumerate(p))


def _all_gather(xs, *, name):
    n = len(xs)

    def body(*refs):
        x_refs, o_refs = refs[:n], refs[n:2 * n]
        send_sems, recv_sems, local_sems = refs[2 * n:]
        x, y, c = _position()
        me, sibling = (x, y, c), (x, y, 1 - c)
        chips = [(1 - x, y), (x, 1 - y), (1 - x, 1 - y)]

        def copy(t, k, block, to, src=None):
            dst = o_refs[t].at[_block_id(block)]
            return pltpu.make_async_remote_copy(
                src_ref=dst if src is None else src, dst_ref=dst,
                send_sem=send_sems.at[t, k], recv_sem=recv_sems.at[t, k],
                device_id=to, device_id_type=MESH)

        mine = [pltpu.make_async_copy(x_refs[t], o_refs[t].at[_block_id(me)], local_sems.at[t]) for t in range(n)]
        for cp in mine:
            cp.start()
        started = []
        for t in range(n):
            first = [copy(t, 0, me, sibling, src=x_refs[t])]
            first += [copy(t, 1 + j, me, (*chip, c), src=x_refs[t]) for j, chip in enumerate(chips)]
            for cp in first:
                cp.start()
            started += first
        for j, chip in enumerate(chips):
            for t in range(n):
                copy(t, 1 + j, (*chip, c), me).wait_recv()
                fwd = copy(t, 4 + j, (*chip, c), sibling)
                fwd.start()
                started.append(fwd)
        for t in range(n):
            copy(t, 0, sibling, me).wait_recv()
            for j, chip in enumerate(chips):
                copy(t, 4 + j, (*chip, 1 - c), me).wait_recv()
        for cp in started:
            cp.wait_send()
        for cp in mine:
            cp.wait()

    any_spec = pl.BlockSpec(memory_space=pl.ANY)
    return _pcall(
        body, name=name,
        in_specs=[any_spec] * n, out_specs=[any_spec] * n,
        out_shape=[jax.ShapeDtypeStruct((N_DEV,) + a.shape, a.dtype) for a in xs],
        scratch_shapes=[pltpu.SemaphoreType.DMA((n, 7)), pltpu.SemaphoreType.DMA((n, 7)),
                        pltpu.SemaphoreType.DMA((n,))],
    )(*xs)


def _scatter_to_owners(xs, *, name):
    n = len(xs)

    def body(*refs):
        x_refs, o_refs = refs[:n], refs[n:2 * n]
        send_sems, recv_sems, local_sems = refs[2 * n:]
        me = _position()
        my_id = _block_id(me)
        mine = [pltpu.make_async_copy(x_refs[t].at[my_id], o_refs[t].at[my_id], local_sems.at[t]) for t in range(n)]
        for cp in mine:
            cp.start()
        sent = []
        for t in range(n):
            for mask in range(1, N_DEV):
                peer = _flip(me, mask)
                cp = pltpu.make_async_remote_copy(
                    src_ref=x_refs[t].at[_block_id(peer)], dst_ref=o_refs[t].at[my_id],
                    send_sem=send_sems.at[t, mask - 1], recv_sem=recv_sems.at[t, mask - 1],
                    device_id=peer, device_id_type=MESH)
                cp.start()
                sent.append(cp)
        for t in range(n):
            for mask in range(1, N_DEV):
                slot = o_refs[t].at[_block_id(_flip(me, mask))]
                pltpu.make_async_remote_copy(
                    src_ref=slot, dst_ref=slot, send_sem=send_sems.at[t, mask - 1],
                    recv_sem=recv_sems.at[t, mask - 1], device_id=me, device_id_type=MESH).wait_recv()
        for cp in sent:
            cp.wait_send()
        for cp in mine:
            cp.wait()

    any_spec = pl.BlockSpec(memory_space=pl.ANY)
    return _pcall(
        body, name=name,
        in_specs=[any_spec] * n, out_specs=[any_spec] * n,
        out_shape=[jax.ShapeDtypeStruct(a.shape, a.dtype) for a in xs],
        scratch_shapes=[pltpu.SemaphoreType.DMA((n, 7)), pltpu.SemaphoreType.DMA((n, 7)),
                        pltpu.SemaphoreType.DMA((n,))],
    )(*xs)


def _all_reduce_small(pack, *, name):
    rows = pack.shape[0]

    def body(p_ref, o_ref, gath, send_sems, recv_sems):
        me = _position()
        my_id = _block_id(me)
        gath[my_id] = p_ref[...]
        sent = []
        for mask in range(1, N_DEV):
            peer = _flip(me, mask)
            cp = pltpu.make_async_remote_copy(
                src_ref=p_ref, dst_ref=gath.at[my_id], send_sem=send_sems.at[mask - 1],
                recv_sem=recv_sems.at[mask - 1], device_id=peer, device_id_type=MESH)
            cp.start()
            sent.append(cp)
        for mask in range(1, N_DEV):
            slot = gath.at[_block_id(_flip(me, mask))]
            pltpu.make_async_remote_copy(
                src_ref=slot, dst_ref=slot, send_sem=send_sems.at[mask - 1], recv_sem=recv_sems.at[mask - 1],
                device_id=me, device_id_type=MESH).wait_recv()
        for cp in sent:
            cp.wait_send()
        total = gath[0]
        for s in range(1, N_DEV):
            total = total + gath[s]
        o_ref[...] = total

    vm = pl.BlockSpec(memory_space=pltpu.VMEM)
    return _pcall(
        body, name=name,
        in_specs=[vm], out_specs=vm,
        out_shape=jax.ShapeDtypeStruct(pack.shape, F32),
        scratch_shapes=[pltpu.VMEM((N_DEV, rows, 128), F32), pltpu.SemaphoreType.DMA((7,)),
                        pltpu.SemaphoreType.DMA((7,))],
        compiler_params=pltpu.CompilerParams(vmem_limit_bytes=VMEM_LIMIT),
    )(pack)


def _layout(d, ql, kvl):
    cw = d // 2
    att = d // 2
    lay = {"D": d, "CW": cw, "ATT": att, "H": att // VDIM, "QL": ql, "KVL": kvl}
    lay["A"], lay["B"], lay["GATT"], lay["GCONV"] = 0, cw, 2 * cw, 2 * cw + att
    lay["QC"] = lay["GCONV"] + cw
    lay["KVC"] = lay["QC"] + ql
    lay["KPE"] = lay["KVC"] + kvl
    used = lay["KPE"] + 128
    tn = min(MM_TN, 1024)
    lay["NP"] = -(-used // tn) * tn
    assert att == cw and lay["QC"] % ql == 0 and lay["KVC"] % kvl == 0 and lay["KPE"] % 128 == 0
    lay["o_kv"], lay["o_pe"] = ql, ql + kvl
    lay["o_ga"] = lay["o_pe"] + ROPE
    lay["o_u"] = lay["o_ga"] + att
    lay["o_gc"] = lay["o_u"] + 2 * cw
    lay["IN_COLS"] = lay["o_gc"] + cw
    return lay


def _ungather_cols(g):
    return jnp.transpose(g, (1, 0, 2)).reshape(g.shape[1], -1)


def _to_col_blocks(w):
    r, c = w.shape
    return jnp.transpose(w.reshape(r, N_DEV, c // N_DEV), (1, 0, 2))


def _pad_w_in(w, lay):
    d, cw, att, ql, kvl = lay["D"], lay["CW"], lay["ATT"], lay["QL"], lay["KVL"]
    parts = [w[:, lay["o_u"]:lay["o_u"] + 2 * cw], w[:, lay["o_ga"]:lay["o_ga"] + att],
             w[:, lay["o_gc"]:lay["o_gc"] + cw], w[:, :ql], w[:, lay["o_kv"]:lay["o_kv"] + kvl],
             w[:, lay["o_pe"]:lay["o_pe"] + ROPE],
             jnp.zeros((d, lay["NP"] - lay["KPE"] - ROPE), w.dtype)]
    return jnp.concatenate(parts, axis=1)


def _unpad_w_in(wp, lay):
    cw, att, ql, kvl = lay["CW"], lay["ATT"], lay["QL"], lay["KVL"]
    parts = [wp[:, lay["QC"]:lay["QC"] + ql], wp[:, lay["KVC"]:lay["KVC"] + kvl],
             wp[:, lay["KPE"]:lay["KPE"] + ROPE], wp[:, lay["GATT"]:lay["GATT"] + att],
             wp[:, :2 * cw], wp[:, lay["GCONV"]:lay["GCONV"] + cw]]
    return jnp.concatenate(parts, axis=1)


def _pad_heads(w, h):
    r = w.shape[0]
    return jnp.pad(w.reshape(r, h, QK_DIM), ((0, 0), (0, 0), (0, HEAD_PAD - QK_DIM))).reshape(r, h * HEAD_PAD)


def _unpad_heads(w, h):
    r = w.shape[0]
    return w.reshape(r, h, HEAD_PAD)[:, :, :QK_DIM].reshape(r, h * QK_DIM)


def _rope_tabs(positions):
    half = ROPE // 2
    inv_freq = ROPE_THETA ** (-jnp.arange(half, dtype=F32) / half)
    ang = positions.astype(F32).reshape(-1)[:, None] * inv_freq
    cos, sin = jnp.cos(ang), jnp.sin(ang)
    zero = jnp.zeros_like(cos)
    return (jnp.concatenate([cos, cos, zero, zero], axis=1),
            jnp.concatenate([-sin, zero, zero, zero], axis=1),
            jnp.concatenate([zero, sin, zero, zero], axis=1))


def _pack_rows(vecs):
    rows = []
    for v in vecs:
        flat = v.reshape(-1)
        pad = (-flat.shape[0]) % 1024
        rows.append(jnp.pad(flat, (0, pad)).reshape(-1, 128))
    return jnp.concatenate(rows, axis=0)


def _unpack_rows(pack, shapes):
    out, r0 = [], 0
    for shp in shapes:
        size = math.prod(shp)
        nrows = -(-size // 1024) * 8
        out.append(pack[r0:r0 + nrows].reshape(-1)[:size].reshape(shp))
        r0 += nrows
    return out


def kernel(x, positions, ln_g, w_in, q_a_norm, w_q_up, kv_a_norm, w_kv_up, q_norm, k_norm, w_dw, b_dw, conv_ln_g, conv_ln_b, w_out, loss_target, m_ln_g, m_w_in, m_q_a_norm, m_w_q_up, m_kv_a_norm, m_w_kv_up, m_q_norm, m_k_norm, m_w_dw, m_b_dw, m_conv_ln_g, m_conv_ln_b, m_w_out, v_ln_g, v_w_in, v_q_a_norm, v_w_q_up, v_kv_a_norm, v_w_kv_up, v_q_norm, v_k_norm, v_w_dw, v_b_dw, v_conv_ln_g, v_conv_ln_b, v_w_out):
    nb, seq, d = x.shape
    depth = ln_g.shape[0]
    lay = _layout(d, q_a_norm.shape[1], kv_a_norm.shape[1])
    h, cw, ql, kvl = lay["H"], lay["CW"], lay["QL"], lay["KVL"]
    t = nb * seq
    my_id = _block_id(_position())

    g_in, g_q, g_kv, g_out, g_dw = _all_gather(
        [w_in.astype(BF16), w_q_up.astype(BF16), w_kv_up.astype(BF16), w_out.astype(BF16), w_dw],
        name="gather_weights")
    w_in_p, w_in_t, w_q_p, w_q_t, w_kv_f, w_kv_t, w_out_f, w_out_t = [], [], [], [], [], [], [], []
    for l in range(depth):
        wp = _pad_w_in(_ungather_cols(g_in[:, l]), lay)
        wq = _pad_heads(_ungather_cols(g_q[:, l]), h)
        wkv = _ungather_cols(g_kv[:, l])
        wo = g_out[:, l].reshape(2 * cw, d)
        w_in_p.append(wp), w_in_t.append(wp.T)
        w_q_p.append(wq), w_q_t.append(wq.T)
        w_kv_f.append(wkv), w_kv_t.append(wkv.T)
        w_out_f.append(wo), w_out_t.append(wo.T)

    tabs = _rope_tabs(positions)
    gq_pad = jnp.pad(q_norm, ((0, 0), (0, HEAD_PAD - QK_DIM)))
    gk_pad = jnp.pad(k_norm, ((0, 0), (0, HEAD_PAD - QK_DIM)))
    w_dw_all = jnp.transpose(g_dw, (1, 2, 0, 3)).reshape(depth, CONV_K, cw)
    w_dw_all = jnp.pad(w_dw_all, ((0, 0), (0, HALO - CONV_K), (0, 0)))

    saved = []
    xs = x.reshape(t, d)
    for l in range(depth):
        hid = _rms_fwd(xs, ln_g[l], name=f"rms_fwd_{l}")
        z = _mm(hid, w_in_p[l], name=f"in_proj_{l}")
        qn, kvn = _lat_fwd(z, q_a_norm[l], kv_a_norm[l], lay, name=f"lat_fwd_{l}")
        q_raw = _mm(qn, w_q_p[l], name=f"q_up_{l}")
        kv_raw = _mm(kvn, w_kv_f[l], name=f"kv_up_{l}")
        qh, kh, vh = _heads_fwd(q_raw, kv_raw, z, tabs, gq_pad[l:l + 1], gk_pad[l:l + 1], lay, name=f"heads_fwd_{l}")
        att, mix, lse = _flash_fwd(qh, kh, vh, z, lay, nb, name=f"flash_fwd_{l}")
        mix, u, c_pre = _conv_fwd(z, mix, w_dw_all[l], b_dw[l], conv_ln_g[l], conv_ln_b[l], lay, nb,
                                  name=f"conv_fwd_{l}")
        x_next = _mm(mix, w_out_f[l], add=xs, name=f"out_proj_{l}")
        saved.append((xs, hid, z, qn, kvn, q_raw, kv_raw, qh, kh, vh, att, lse, mix, u, c_pre))
        xs = x_next

    sse, dx, dxb = _loss_head(xs, loss_target.reshape(t, d), name="loss_head")
    loss = lax.psum(sse[0, 0] * (0.5 / d), ("x", "y", "c"))

    small = {k: [] for k in ("ln_g", "q_a", "kv_a", "q_n", "k_n", "w_dw", "b_dw", "cln_g", "cln_b")}
    big = {k: [] for k in ("in", "q", "kv", "out")}
    for l in reversed(range(depth)):
        xs, hid, z, qn, kvn, q_raw, kv_raw, qh, kh, vh, att, lse, mix, u, c_pre = saved[l]
        dmix = _mm(dxb, w_out_t[l], name=f"d_mix_{l}")
        big["out"].append(_mm(mix, dxb, trans_a=True, out_dtype=BF16, name=f"dw_out_{l}"))
        dc, dgc, dlg, dlb, dbias = _conv_bwd_ln(c_pre, z, dmix, conv_ln_g[l], conv_ln_b[l], lay,
                                                name=f"conv_bwd_ln_{l}")
        dab, dwdw = _conv_bwd_dw(dc, u, z, w_dw_all[l], lay, nb, name=f"conv_bwd_dw_{l}")
        do, delta, dga = _gate_bwd(dmix, att, z, lay, name=f"gate_bwd_{l}")
        dkh, dvh = _flash_bwd_kv(qh, kh, vh, do, lse, delta, nb, name=f"flash_bwd_kv_{l}")
        dqh = _flash_bwd_q(qh, kh, vh, do, lse, delta, nb, name=f"flash_bwd_q_{l}")
        dq_raw, dkv_raw, dpe, dgq, dgk = _heads_bwd(q_raw, kv_raw, z, tabs, gq_pad[l:l + 1], gk_pad[l:l + 1],
                                                    dqh, dkh, dvh, lay, name=f"heads_bwd_{l}")
        dqn = _mm(dq_raw, w_q_t[l], name=f"d_qn_{l}")
        dkvn = _mm(dkv_raw, w_kv_t[l], name=f"d_kvn_{l}")
        big["q"].append(_mm(qn, dq_raw, trans_a=True, out_dtype=BF16, name=f"dw_q_{l}"))
        big["kv"].append(_mm(kvn, dkv_raw, trans_a=True, out_dtype=BF16, name=f"dw_kv_{l}"))
        dqc, dkvc, dgqa, dgkva = _lat_bwd(z, q_a_norm[l], kv_a_norm[l], dqn, dkvn, lay, name=f"lat_bwd_{l}")
        dz = jnp.concatenate([dab, dga, dgc, dqc, dkvc, dpe.astype(BF16),
                              jnp.zeros((t, lay["NP"] - lay["KPE"] - 128), BF16)], axis=1)
        dh = _mm(dz, w_in_t[l], name=f"d_hid_{l}")
        big["in"].append(_mm(hid, dz, trans_a=True, out_dtype=BF16, name=f"dw_in_{l}"))
        dx, dxb, dlng = _rms_bwd(xs, ln_g[l], dh, dx, name=f"rms_bwd_{l}")
        for key, val in (("ln_g", dlng), ("q_a", dgqa), ("kv_a", dgkva), ("q_n", dgq[:, :QK_DIM]),
                         ("k_n", dgk[:, :QK_DIM]), ("w_dw", dwdw[:CONV_K]), ("b_dw", dbias),
                         ("cln_g", dlg), ("cln_b", dlb)):
            small[key].append(val)
    grad_x = dx.reshape(nb, seq, d)
    for key in small:
        small[key] = jnp.stack(small[key][::-1])
    for key in big:
        big[key] = big[key][::-1]

    s_in = jnp.stack([_to_col_blocks(_unpad_w_in(g, lay)) for g in big["in"]], axis=1)
    s_q = jnp.stack([_to_col_blocks(_unpad_heads(g, h)) for g in big["q"]], axis=1)
    s_kv = jnp.stack([_to_col_blocks(g) for g in big["kv"]], axis=1)
    s_out = jnp.stack([g.reshape(N_DEV, (2 * cw) // N_DEV, d) for g in big["out"]], axis=1)
    r_in, r_q, r_kv, r_out = _scatter_to_owners([s_in, s_q, s_kv, s_out], name="scatter_weight_grads")

    small_names = ("ln_g", "q_a", "kv_a", "q_n", "k_n", "b_dw", "cln_g", "cln_b", "w_dw")
    small_shapes = [small[k].shape for k in small_names]
    summed = _unpack_rows(_all_reduce_small(_pack_rows([small[k] for k in small_names]), name="reduce_small_grads"),
                          small_shapes)
    sg = dict(zip(small_names, summed))
    g_w_dw = lax.dynamic_slice_in_dim(sg["w_dw"], my_id * (cw // N_DEV), cw // N_DEV, axis=2)

    def adam_big(w, m, v, recv, nm):
        shp = w.shape
        rows, cols = shp[0] * shp[1], shp[2]
        parts = [(recv.reshape(N_DEV, rows, cols), s) for s in range(N_DEV)]
        outs = _adam(w.reshape(rows, cols), m.reshape(rows, cols), v.reshape(rows, cols), parts, name=nm)
        return [o.reshape(shp) for o in outs]

    def adam_small(ws, ms, vs, gs, nm):
        shapes = [w.shape for w in ws]
        outs = _adam(_pack_rows(ws), _pack_rows(ms), _pack_rows(vs), [(_pack_rows(gs), None)], name=nm)
        return [_unpack_rows(o, shapes) for o in outs]

    res = {}
    res["w_in"] = adam_big(w_in, m_w_in, v_w_in, r_in, "adam_w_in")
    res["w_q_up"] = adam_big(w_q_up, m_w_q_up, v_w_q_up, r_q, "adam_w_q_up")
    res["w_kv_up"] = adam_big(w_kv_up, m_w_kv_up, v_w_kv_up, r_kv, "adam_w_kv_up")
    res["w_out"] = adam_big(w_out, m_w_out, v_w_out, r_out, "adam_w_out")
    names_s = ["ln_g", "q_a_norm", "kv_a_norm", "q_norm", "k_norm", "w_dw", "b_dw", "conv_ln_g", "conv_ln_b"]
    ws = [ln_g, q_a_norm, kv_a_norm, q_norm, k_norm, w_dw, b_dw, conv_ln_g, conv_ln_b]
    ms = [m_ln_g, m_q_a_norm, m_kv_a_norm, m_q_norm, m_k_norm, m_w_dw, m_b_dw, m_conv_ln_g, m_conv_ln_b]
    vs = [v_ln_g, v_q_a_norm, v_kv_a_norm, v_q_norm, v_k_norm, v_w_dw, v_b_dw, v_conv_ln_g, v_conv_ln_b]
    gs = [sg["ln_g"].reshape(ln_g.shape), sg["q_a"].reshape(q_a_norm.shape), sg["kv_a"].reshape(kv_a_norm.shape),
          sg["q_n"].reshape(q_norm.shape), sg["k_n"].reshape(k_norm.shape), g_w_dw,
          sg["b_dw"].reshape(b_dw.shape), sg["cln_g"].reshape(conv_ln_g.shape), sg["cln_b"].reshape(conv_ln_b.shape)]
    outs_s = adam_small(ws, ms, vs, gs, "adam_small")
    for idx, nm in enumerate(names_s):
        res[nm] = [outs_s[k][idx] for k in range(4)]

    order = ["ln_g", "w_in", "q_a_norm", "w_q_up", "kv_a_norm", "w_kv_up", "q_norm", "k_norm", "w_dw", "b_dw",
             "conv_ln_g", "conv_ln_b", "w_out"]
    return (loss, grad_x, *[res[nm][0] for nm in order], *[res[nm][1] for nm in order],
            *[res[nm][2] for nm in order], *[res[nm][3] for nm in order])
```

```python
import functools
import math

import jax
import jax.numpy as jnp
from jax import lax
from jax.experimental import pallas as pl
from jax.experimental.pallas import tpu as pltpu

F32 = jnp.float32
BF16 = jnp.bfloat16
MESH = pl.DeviceIdType.MESH

N_DEV = 8
NOPE = 128
ROPE = 64
VDIM = 128
HEAD_PAD = 256
QK_DIM = NOPE + ROPE
CONV_K = 31
HALO = 32
EPS = 1e-6
ROPE_THETA = 10000.0
NEG = -1e30

ADAM_LR = 0.001
ADAM_B1 = 0.9
ADAM_B2 = 0.999
ADAM_EPS = 1e-08
ADAM_WD = 0.01
ADAM_STEP = 10

TOK_TILE = 256
CONV_TILE = 256
ATT_TQ = 512
MM_TM = 1024
MM_TN = 512
MM_TK = 4096
MM_VMEM_BUDGET = 40 * 1024 * 1024
ADAM_BLOCK_ELEMS = 128 * 1024
LANE_CHUNK = 256
VMEM_LIMIT = 56 * 1024 * 1024


def _pcall(body, **kw):
    return pl.pallas_call(body, **kw)


def _tile(dim, pref, mult):
    t = min(pref, dim)
    t -= t % mult
    while t >= mult:
        if dim % t == 0:
            return t
        t -= mult
    return dim


def _params(sem):
    return pltpu.CompilerParams(dimension_semantics=sem, vmem_limit_bytes=VMEM_LIMIT)


def _sigmoid(v):
    return 1.0 / (1.0 + jnp.exp(-v))


def _dsilu(v, sg):
    return sg * (1.0 + v * (1.0 - sg))


def _mm_tiles(m, n, kdim, out_bytes, has_add):
    tk = _tile(kdim, MM_TK, 128)
    tn = _tile(n, MM_TN, 128)
    for pref in (MM_TM, MM_TM // 2, MM_TM // 4, 128):
        tm = _tile(m, pref, 128)
        need = 2 * 2 * (tm * tk + tk * tn) + 2 * tm * tn * out_bytes
        need += tm * tn * 4 * ((kdim > tk) + 2 * has_add + 1)
        if need <= MM_VMEM_BUDGET:
            break
    return tm, tn, tk


def _mm(a, b, *, name, trans_a=False, add=None, out_dtype=F32):
    if trans_a:
        kdim, m = a.shape
    else:
        m, kdim = a.shape
    n = b.shape[1]
    assert b.shape[0] == kdim
    has_add = add is not None
    tm, tn, tk = _mm_tiles(m, n, kdim, jnp.dtype(out_dtype).itemsize, has_add)
    nk = kdim // tk

    def product(a_ref, b_ref):
        if trans_a:
            return lax.dot_general(a_ref[...], b_ref[...], (((0,), (0,)), ((), ())), preferred_element_type=F32)
        return jnp.dot(a_ref[...], b_ref[...], preferred_element_type=F32)

    def body(*refs):
        a_ref, b_ref = refs[:2]
        add_ref = refs[2] if has_add else None
        o_ref = refs[2 + has_add]

        def finish(r):
            if has_add:
                r = r + add_ref[...]
            o_ref[...] = r.astype(o_ref.dtype)

        if nk == 1:
            finish(product(a_ref, b_ref))
            return
        acc_ref = refs[-1]
        k = pl.program_id(2)

        @pl.when(k == 0)
        def _():
            acc_ref[...] = product(a_ref, b_ref)

        @pl.when((k > 0) & (k < nk - 1))
        def _():
            acc_ref[...] += product(a_ref, b_ref)

        @pl.when(k == nk - 1)
        def _():
            finish(acc_ref[...] + product(a_ref, b_ref))

    if trans_a:
        a_spec = pl.BlockSpec((tk, tm), lambda i, j, k: (k, i))
    else:
        a_spec = pl.BlockSpec((tm, tk), lambda i, j, k: (i, k))
    in_specs = [a_spec, pl.BlockSpec((tk, tn), lambda i, j, k: (k, j))]
    args = [a, b]
    if has_add:
        in_specs.append(pl.BlockSpec((tm, tn), lambda i, j, k: (i, j)))
        args.append(add)
    return _pcall(
        body, name=name,
        grid=(m // tm, n // tn, nk),
        in_specs=in_specs,
        out_specs=pl.BlockSpec((tm, tn), lambda i, j, k: (i, j)),
        out_shape=jax.ShapeDtypeStruct((m, n), out_dtype),
        scratch_shapes=[pltpu.VMEM((tm, tn), F32)] if nk > 1 else [],
        compiler_params=_params(("parallel", "parallel", "arbitrary")),
    )(*args)


def _rms_fwd(x, g, *, name):
    t, d = x.shape
    tt = _tile(t, TOK_TILE, 16)

    def body(x_ref, g_ref, h_ref):
        xv = x_ref[...]
        r = lax.rsqrt(jnp.mean(xv * xv, axis=-1, keepdims=True) + EPS)
        h_ref[...] = (xv * r * g_ref[...]).astype(BF16)

    return _pcall(
        body, name=name, grid=(t // tt,),
        in_specs=[pl.BlockSpec((tt, d), lambda i: (i, 0)), pl.BlockSpec((1, d), lambda i: (0, 0))],
        out_specs=pl.BlockSpec((tt, d), lambda i: (i, 0)),
        out_shape=jax.ShapeDtypeStruct((t, d), BF16),
        compiler_params=_params(("parallel",)),
    )(x, g.reshape(1, d))


def _rms_bwd(x, g, dh, dres, *, name):
    t, d = x.shape
    tt = _tile(t, TOK_TILE, 16)

    def body(x_ref, g_ref, dh_ref, dres_ref, dx_ref, dxb_ref, dg_ref):
        xv = x_ref[...]
        r = lax.rsqrt(jnp.mean(xv * xv, axis=-1, keepdims=True) + EPS)
        dy = dh_ref[...]
        dyg = dy * g_ref[...]
        dot = jnp.sum(dyg * xv, axis=-1, keepdims=True) * (1.0 / d)
        dx = dres_ref[...] + r * dyg - xv * (r * r * r) * dot
        dx_ref[...] = dx
        dxb_ref[...] = dx.astype(BF16)

        @pl.when(pl.program_id(0) == 0)
        def _():
            dg_ref[...] = jnp.zeros_like(dg_ref)

        dg_ref[...] += jnp.sum(dy * xv * r, axis=0, keepdims=True)

    row = pl.BlockSpec((tt, d), lambda i: (i, 0))
    vec = pl.BlockSpec((1, d), lambda i: (0, 0))
    return _pcall(
        body, name=name, grid=(t // tt,),
        in_specs=[row, vec, row, row],
        out_specs=[row, row, vec],
        out_shape=[jax.ShapeDtypeStruct((t, d), F32), jax.ShapeDtypeStruct((t, d), BF16),
                   jax.ShapeDtypeStruct((1, d), F32)],
        compiler_params=_params(("arbitrary",)),
    )(x, g.reshape(1, d), dh, dres)


def _lat_fwd(z, gq, gkv, lay, *, name):
    t = z.shape[0]
    ql, kvl = lay["QL"], lay["KVL"]
    tt = _tile(t, TOK_TILE, 16)

    def body(q_ref, kv_ref, gq_ref, gkv_ref, qn_ref, kvn_ref):
        for src, g_ref, dst in ((q_ref, gq_ref, qn_ref), (kv_ref, gkv_ref, kvn_ref)):
            v = src[...]
            r = lax.rsqrt(jnp.mean(v * v, axis=-1, keepdims=True) + EPS)
            dst[...] = (v * r * g_ref[...]).astype(BF16)

    return _pcall(
        body, name=name, grid=(t // tt,),
        in_specs=[pl.BlockSpec((tt, ql), lambda i: (i, lay["QC"] // ql)),
                  pl.BlockSpec((tt, kvl), lambda i: (i, lay["KVC"] // kvl)),
                  pl.BlockSpec((1, ql), lambda i: (0, 0)), pl.BlockSpec((1, kvl), lambda i: (0, 0))],
        out_specs=[pl.BlockSpec((tt, ql), lambda i: (i, 0)), pl.BlockSpec((tt, kvl), lambda i: (i, 0))],
        out_shape=[jax.ShapeDtypeStruct((t, ql), BF16), jax.ShapeDtypeStruct((t, kvl), BF16)],
        compiler_params=_params(("parallel",)),
    )(z, z, gq.reshape(1, ql), gkv.reshape(1, kvl))


def _lat_bwd(z, gq, gkv, dqn, dkvn, dpe, dz, lay, *, name):
    t = z.shape[0]
    ql, kvl = lay["QL"], lay["KVL"]
    tail = lay["NP"] - lay["QC"]
    assert lay["QC"] % tail == 0
    tt = _tile(t, TOK_TILE, 16)

    def body(q_ref, kv_ref, gq_ref, gkv_ref, dqn_ref, dkvn_ref, dpe_ref, dz_in, tail_ref, dgq_ref, dgkv_ref):
        del dz_in
        first = pl.program_id(0) == 0
        for src, g_ref, dy_ref, c0, dg_ref in ((q_ref, gq_ref, dqn_ref, 0, dgq_ref),
                                               (kv_ref, gkv_ref, dkvn_ref, ql, dgkv_ref)):
            v = src[...]
            n = v.shape[-1]
            r = lax.rsqrt(jnp.mean(v * v, axis=-1, keepdims=True) + EPS)
            dy = dy_ref[...]
            dyg = dy * g_ref[...]
            dot = jnp.sum(dyg * v, axis=-1, keepdims=True) * (1.0 / n)
            tail_ref[:, c0:c0 + n] = (r * dyg - v * (r * r * r) * dot).astype(BF16)

            @pl.when(first)
            def _():
                dg_ref[...] = jnp.zeros_like(dg_ref)

            dg_ref[...] += jnp.sum(dy * v * r, axis=0, keepdims=True)
        tail_ref[:, ql + kvl:ql + kvl + 128] = dpe_ref[...].astype(BF16)
        tail_ref[:, ql + kvl + 128:tail] = jnp.zeros((tt, tail - ql - kvl - 128), BF16)

    return _pcall(
        body, name=name, grid=(t // tt,),
        in_specs=[pl.BlockSpec((tt, ql), lambda i: (i, lay["QC"] // ql)),
                  pl.BlockSpec((tt, kvl), lambda i: (i, lay["KVC"] // kvl)),
                  pl.BlockSpec((1, ql), lambda i: (0, 0)), pl.BlockSpec((1, kvl), lambda i: (0, 0)),
                  pl.BlockSpec((tt, ql), lambda i: (i, 0)), pl.BlockSpec((tt, kvl), lambda i: (i, 0)),
                  pl.BlockSpec((tt, 128), lambda i: (i, 0)), pl.BlockSpec(memory_space=pl.ANY)],
        out_specs=[pl.BlockSpec((tt, tail), lambda i: (i, lay["QC"] // tail)),
                   pl.BlockSpec((1, ql), lambda i: (0, 0)), pl.BlockSpec((1, kvl), lambda i: (0, 0))],
        out_shape=[jax.ShapeDtypeStruct(dz.shape, BF16),
                   jax.ShapeDtypeStruct((1, ql), F32), jax.ShapeDtypeStruct((1, kvl), F32)],
        input_output_aliases={7: 0},
        compiler_params=_params(("arbitrary",)),
    )(z, z, gq.reshape(1, ql), gkv.reshape(1, kvl), dqn, dkvn, dpe, dz)


def _rope(r, c_tab, sa_tab, sb_tab):
    return r * c_tab + pltpu.roll(r, 96, 1) * sa_tab + pltpu.roll(r, 32, 1) * sb_tab


def _rope_t(d, c_tab, sa_tab, sb_tab):
    return d * c_tab + pltpu.roll(d * sa_tab, 32, 1) + pltpu.roll(d * sb_tab, 96, 1)


def _heads_fwd(q_raw, kv_raw, z, tabs, gq, gk, lay, *, name):
    t = z.shape[0]
    h = lay["H"]
    tt = _tile(t, TOK_TILE, 16)
    scale = 1.0 / math.sqrt(QK_DIM)

    def body(q_ref, kv_ref, pe_ref, c_ref, sa_ref, sb_ref, gq_ref, gk_ref, qh_ref, kh_ref, vh_ref):
        c_tab, sa_tab, sb_tab = c_ref[...], sa_ref[...], sb_ref[...]
        q = q_ref[...]
        r = lax.rsqrt(jnp.sum(q * q, axis=-1, keepdims=True) * (1.0 / QK_DIM) + EPS)
        qn = q * r * gq_ref[...]
        qh_ref[...] = (jnp.concatenate([qn[:, :NOPE], _rope(qn[:, NOPE:], c_tab, sa_tab, sb_tab)], axis=1)
                       * scale).astype(BF16)
        kv = kv_ref[...]
        kn, pe = kv[:, :NOPE], pe_ref[...]
        ss = jnp.sum(kn * kn, axis=-1, keepdims=True) + jnp.sum(pe * pe, axis=-1, keepdims=True)
        rk = lax.rsqrt(ss * (1.0 / QK_DIM) + EPS)
        gk_v = gk_ref[...]
        kh_ref[...] = jnp.concatenate(
            [kn * rk * gk_v[:, :NOPE], _rope(pe * rk * gk_v[:, NOPE:], c_tab, sa_tab, sb_tab)], axis=1).astype(BF16)
        vh_ref[...] = kv[:, NOPE:].astype(BF16)

    head = pl.BlockSpec((tt, HEAD_PAD), lambda i, j: (i, j))
    tab = pl.BlockSpec((tt, 128), lambda i, j: (i, 0))
    gain = pl.BlockSpec((1, HEAD_PAD), lambda i, j: (0, 0))
    return _pcall(
        body, name=name, grid=(t // tt, h),
        in_specs=[head, head, pl.BlockSpec((tt, 128), lambda i, j: (i, lay["KPE"] // 128)), tab, tab, tab, gain, gain],
        out_specs=[pl.BlockSpec((None, tt, HEAD_PAD), lambda i, j: (j, i, 0)),
                   pl.BlockSpec((None, tt, HEAD_PAD), lambda i, j: (j, i, 0)),
                   pl.BlockSpec((None, tt, VDIM), lambda i, j: (j, i, 0))],
        out_shape=[jax.ShapeDtypeStruct((h, t, HEAD_PAD), BF16), jax.ShapeDtypeStruct((h, t, HEAD_PAD), BF16),
                   jax.ShapeDtypeStruct((h, t, VDIM), BF16)],
        compiler_params=_params(("parallel", "parallel")),
    )(q_raw, kv_raw, z, *tabs, gq, gk)


def _heads_bwd(q_raw, kv_raw, z, tabs, gq, gk, dqh, dkh, dvh, lay, *, name):
    t = z.shape[0]
    h = lay["H"]
    tt = _tile(t, TOK_TILE, 16)
    scale = 1.0 / math.sqrt(QK_DIM)

    def body(q_ref, kv_ref, pe_ref, c_ref, sa_ref, sb_ref, gq_ref, gk_ref, dqh_ref, dkh_ref, dvh_ref,
             dq_ref, dkv_ref, dpe_ref, dgq_ref, dgk_ref):
        i, j = pl.program_id(0), pl.program_id(1)
        c_tab, sa_tab, sb_tab = c_ref[...], sa_ref[...], sb_ref[...]

        @pl.when((i == 0) & (j == 0))
        def _():
            dgq_ref[...] = jnp.zeros_like(dgq_ref)
            dgk_ref[...] = jnp.zeros_like(dgk_ref)

        @pl.when(j == 0)
        def _():
            dpe_ref[...] = jnp.zeros_like(dpe_ref)

        def norm_bwd(v, g, dy):
            r = lax.rsqrt(jnp.sum(v * v, axis=-1, keepdims=True) * (1.0 / QK_DIM) + EPS)
            dyg = dy * g
            dot = jnp.sum(dyg * v, axis=-1, keepdims=True) * (1.0 / QK_DIM)
            return r * dyg - v * (r * r * r) * dot, jnp.sum(dy * v * r, axis=0, keepdims=True)

        dqo = dqh_ref[...] * scale
        dy = jnp.concatenate([dqo[:, :NOPE], _rope_t(dqo[:, NOPE:], c_tab, sa_tab, sb_tab)], axis=1)
        dq, dgq = norm_bwd(q_ref[...], gq_ref[...], dy)
        dq_ref[...] = dq.astype(BF16)
        dgq_ref[...] += dgq

        dko = dkh_ref[...]
        dy = jnp.concatenate([dko[:, :NOPE], _rope_t(dko[:, NOPE:], c_tab, sa_tab, sb_tab)], axis=1)
        kfull = jnp.concatenate([kv_ref[...][:, :NOPE], pe_ref[...]], axis=1)
        dk, dgk = norm_bwd(kfull, gk_ref[...], dy)
        dkv_ref[...] = jnp.concatenate([dk[:, :NOPE], dvh_ref[...]], axis=1).astype(BF16)
        dpe_ref[...] += dk[:, NOPE:]
        dgk_ref[...] += dgk

    head = pl.BlockSpec((tt, HEAD_PAD), lambda i, j: (i, j))
    tab = pl.BlockSpec((tt, 128), lambda i, j: (i, 0))
    gain = pl.BlockSpec((1, HEAD_PAD), lambda i, j: (0, 0))
    hm = pl.BlockSpec((None, tt, HEAD_PAD), lambda i, j: (j, i, 0))
    return _pcall(
        body, name=name, grid=(t // tt, h),
        in_specs=[head, head, pl.BlockSpec((tt, 128), lambda i, j: (i, lay["KPE"] // 128)), tab, tab, tab, gain, gain,
                  hm, hm, pl.BlockSpec((None, tt, VDIM), lambda i, j: (j, i, 0))],
        out_specs=[head, head, tab, gain, gain],
        out_shape=[jax.ShapeDtypeStruct((t, h * HEAD_PAD), BF16), jax.ShapeDtypeStruct((t, h * HEAD_PAD), BF16),
                   jax.ShapeDtypeStruct((t, 128), F32),
                   jax.ShapeDtypeStruct((1, HEAD_PAD), F32), jax.ShapeDtypeStruct((1, HEAD_PAD), F32)],
        compiler_params=_params(("arbitrary", "arbitrary")),
    )(q_raw, kv_raw, z, *tabs, gq, gk, dqh, dkh, dvh)


def _lower_triangle(n):
    return lax.broadcasted_iota(jnp.int32, (n, n), 1) <= lax.broadcasted_iota(jnp.int32, (n, n), 0)


def _qk(q, k):
    return lax.dot_general(q, k, (((1,), (1,)), ((), ())), preferred_element_type=F32)


def _flash_fwd(qh, kh, vh, z, lay, nb, *, name):
    h, t, _ = qh.shape
    s = t // nb
    tq = _tile(s, ATT_TQ, 128)
    nq = s // tq
    att_w = h * VDIM
    gblk = lay["GATT"] // VDIM

    def body(q_ref, k_ref, v_ref, g_ref, att_ref, mix_ref, lse_ref):
        i = pl.program_id(2)
        tri = _lower_triangle(tq)
        for blk in range(nq):
            @pl.when(i == blk)
            def _():
                q = q_ref[...]
                pre = blk * tq
                sd = jnp.where(tri, _qk(q, k_ref[pre:pre + tq, :]), NEG)
                m = jnp.max(sd, axis=-1, keepdims=True)
                if pre:
                    sp = _qk(q, k_ref[0:pre, :])
                    m = jnp.maximum(m, jnp.max(sp, axis=-1, keepdims=True))
                pd = jnp.exp(sd - m)
                l = jnp.sum(pd, axis=-1, keepdims=True)
                acc = jnp.dot(pd.astype(BF16), v_ref[pre:pre + tq, :], preferred_element_type=F32)
                if pre:
                    pp = jnp.exp(sp - m)
                    l = l + jnp.sum(pp, axis=-1, keepdims=True)
                    acc = acc + jnp.dot(pp.astype(BF16), v_ref[0:pre, :], preferred_element_type=F32)
                o = acc / l
                att_ref[...] = o
                g = g_ref[...]
                mix_ref[...] = (o * (g * _sigmoid(g))).astype(BF16)
                lse_ref[...] = m + jnp.log(l)

    row = lambda hh, b, i: (b * nq + i, hh)
    seq = lambda hh, b, i: (hh, b, 0)
    return _pcall(
        body, name=name, grid=(h, nb, nq),
        in_specs=[pl.BlockSpec((None, tq, HEAD_PAD), lambda hh, b, i: (hh, b * nq + i, 0)),
                  pl.BlockSpec((None, s, HEAD_PAD), seq),
                  pl.BlockSpec((None, s, VDIM), seq),
                  pl.BlockSpec((tq, VDIM), lambda hh, b, i: (b * nq + i, gblk + hh))],
        out_specs=[pl.BlockSpec((tq, VDIM), row), pl.BlockSpec((tq, VDIM), row),
                   pl.BlockSpec((None, tq, 1), lambda hh, b, i: (hh, b * nq + i, 0))],
        out_shape=[jax.ShapeDtypeStruct((t, att_w), F32), jax.ShapeDtypeStruct((t, 2 * att_w), BF16),
                   jax.ShapeDtypeStruct((h, t, 1), F32)],
        compiler_params=_params(("parallel", "parallel", "parallel")),
    )(qh, kh, vh, z)


def _gate_bwd(dmix, att, z, dz, lay, *, name):
    t, att_w = att.shape
    h = att_w // VDIM
    tt = _tile(t, TOK_TILE, 16)
    gblk = lay["GATT"] // VDIM

    def body(dm_ref, o_ref, g_ref, dz_in, do_ref, delta_ref, dg_ref):
        del dz_in
        dm, o, g = dm_ref[...], o_ref[...], g_ref[...]
        sg = _sigmoid(g)
        do = dm * (g * sg)
        do_ref[...] = do.astype(BF16)
        delta_ref[...] = jnp.sum(do * o, axis=-1, keepdims=True)
        dg_ref[...] = (dm * o * _dsilu(g, sg)).astype(BF16)

    blk = pl.BlockSpec((tt, VDIM), lambda i, j: (i, j))
    return _pcall(
        body, name=name, grid=(t // tt, h),
        in_specs=[blk, blk, pl.BlockSpec((tt, VDIM), lambda i, j: (i, gblk + j)), pl.BlockSpec(memory_space=pl.ANY)],
        out_specs=[blk, pl.BlockSpec((None, tt, 1), lambda i, j: (j, i, 0)),
                   pl.BlockSpec((tt, VDIM), lambda i, j: (i, gblk + j))],
        out_shape=[jax.ShapeDtypeStruct((t, att_w), BF16), jax.ShapeDtypeStruct((h, t, 1), F32),
                   jax.ShapeDtypeStruct(dz.shape, BF16)],
        input_output_aliases={3: 2},
        compiler_params=_params(("parallel", "parallel")),
    )(dmix, att, z, dz)


def _flash_bwd(qh, kh, vh, do, lse, delta, nb, *, name):
    h, t, _ = qh.shape
    s = t // nb
    tk = _tile(s, ATT_TQ, 128)
    nk = s // tk
    tn_dims = (((0,), (0,)), ((), ()))

    def body(q_ref, k_ref, v_ref, do_ref, lse_ref, dl_ref, dq_ref, dk_ref, dv_ref):
        j = pl.program_id(2)
        tri = _lower_triangle(tk)

        @pl.when(j == 0)
        def _():
            dq_ref[...] = jnp.zeros_like(dq_ref)

        def rows_against_block(r0, r1, masked):
            q, do_v = q_ref[r0:r1, :], do_ref[r0:r1, :]
            k = k_ref[...]
            sc = _qk(q, k)
            if masked:
                sc = jnp.where(tri, sc, NEG)
            p = jnp.exp(sc - lse_ref[r0:r1, :])
            dv = lax.dot_general(p.astype(BF16), do_v, tn_dims, preferred_element_type=F32)
            ds = (p * (_qk(do_v, v_ref[...]) - dl_ref[r0:r1, :])).astype(BF16)
            dq_ref[r0:r1, :] += jnp.dot(ds, k, preferred_element_type=F32)
            return lax.dot_general(ds, q, tn_dims, preferred_element_type=F32), dv

        for blk in range(nk):
            @pl.when(j == blk)
            def _():
                r0 = blk * tk
                dk, dv = rows_against_block(r0, r0 + tk, True)
                if r0 + tk < s:
                    dk2, dv2 = rows_against_block(r0 + tk, s, False)
                    dk, dv = dk + dk2, dv + dv2
                dk_ref[...] = dk
                dv_ref[...] = dv

    seq = lambda hh, b, j: (hh, b, 0)
    kv = lambda hh, b, j: (hh, b * nk + j, 0)
    return _pcall(
        body, name=name, grid=(h, nb, nk),
        in_specs=[pl.BlockSpec((None, s, HEAD_PAD), seq),
                  pl.BlockSpec((None, tk, HEAD_PAD), kv),
                  pl.BlockSpec((None, tk, VDIM), kv),
                  pl.BlockSpec((s, VDIM), lambda hh, b, j: (b, hh)),
                  pl.BlockSpec((None, s, 1), seq), pl.BlockSpec((None, s, 1), seq)],
        out_specs=[pl.BlockSpec((None, s, HEAD_PAD), seq), pl.BlockSpec((None, tk, HEAD_PAD), kv),
                   pl.BlockSpec((None, tk, VDIM), kv)],
        out_shape=[jax.ShapeDtypeStruct((h, t, HEAD_PAD), F32), jax.ShapeDtypeStruct((h, t, HEAD_PAD), F32),
                   jax.ShapeDtypeStruct((h, t, VDIM), F32)],
        compiler_params=_params(("parallel", "parallel", "arbitrary")),
    )(qh, kh, vh, do, lse, delta)


def _conv_fwd(z, mix, w_dw, b_dw, ln_g, ln_b, lay, nb, *, name):
    t = z.shape[0]
    cw = lay["CW"]
    s = t // nb
    tt = _tile(s, CONV_TILE, HALO)
    ns = s // tt
    hb = tt // HALO
    lc = _tile(cw, LANE_CHUNK, 128)

    def body(a_ref, b_ref, ap_ref, bp_ref, gc_ref, w_ref, bias_ref, lg_ref, lb_ref, mix_in, mix_ref, u_ref, c_ref, ext):
        del mix_in
        i = pl.program_id(1)
        u = a_ref[...] * _sigmoid(b_ref[...])
        u_ref[...] = u
        ext[0:HALO, :] = jnp.where(i > 0, ap_ref[...] * _sigmoid(bp_ref[...]), 0.0)
        ext[HALO:HALO + tt, :] = u
        for c0 in range(0, cw, lc):
            acc = jnp.zeros((tt, lc), F32) + bias_ref[:, c0:c0 + lc]
            for k in range(CONV_K):
                off = HALO - (CONV_K - 1) + k
                acc = acc + w_ref[k:k + 1, c0:c0 + lc] * ext[off:off + tt, c0:c0 + lc]
            c_ref[:, c0:c0 + lc] = acc
        c = c_ref[...]
        mu = jnp.mean(c, axis=-1, keepdims=True)
        xc = c - mu
        var = jnp.mean(xc * xc, axis=-1, keepdims=True)
        y = xc * lax.rsqrt(var + EPS) * lg_ref[...] + lb_ref[...]
        g = gc_ref[...]
        mix_ref[...] = (y * _sigmoid(y) * (g * _sigmoid(g))).astype(BF16)

    cur = lambda col: pl.BlockSpec((tt, cw), lambda b, i: (b * ns + i, col))
    prev = lambda col: pl.BlockSpec((HALO, cw), lambda b, i: (jnp.maximum((b * ns + i) * hb - 1, 0), col))
    vec = pl.BlockSpec((1, cw), lambda b, i: (0, 0))
    out_row = pl.BlockSpec((tt, cw), lambda b, i: (b * ns + i, 0))
    return _pcall(
        body, name=name, grid=(nb, ns),
        in_specs=[cur(lay["A"] // cw), cur(lay["B"] // cw), prev(lay["A"] // cw), prev(lay["B"] // cw),
                  cur(lay["GCONV"] // cw), pl.BlockSpec((HALO, cw), lambda b, i: (0, 0)), vec, vec, vec,
                  pl.BlockSpec(memory_space=pl.ANY)],
        out_specs=[pl.BlockSpec((tt, cw), lambda b, i: (b * ns + i, 1)), out_row, out_row],
        out_shape=[jax.ShapeDtypeStruct(mix.shape, BF16), jax.ShapeDtypeStruct((t, cw), F32),
                   jax.ShapeDtypeStruct((t, cw), F32)],
        scratch_shapes=[pltpu.VMEM((tt + HALO, cw), F32)],
        input_output_aliases={9: 0},
        compiler_params=_params(("parallel", "parallel")),
    )(z, z, z, z, z, w_dw, b_dw.reshape(1, cw), ln_g.reshape(1, cw), ln_b.reshape(1, cw), mix)


def _conv_bwd_ln(c_pre, z, dmix, ln_g, ln_b, lay, *, name):
    t, cw = c_pre.shape
    tt = _tile(t, TOK_TILE, 16)

    def body(c_ref, gc_ref, dm_ref, lg_ref, lb_ref, dc_ref, dgc_ref, dlg_ref, dlb_ref, dbias_ref):
        @pl.when(pl.program_id(0) == 0)
        def _():
            dlg_ref[...] = jnp.zeros_like(dlg_ref)
            dlb_ref[...] = jnp.zeros_like(dlb_ref)
            dbias_ref[...] = jnp.zeros_like(dbias_ref)

        c = c_ref[...]
        mu = jnp.mean(c, axis=-1, keepdims=True)
        xc = c - mu
        rstd = lax.rsqrt(jnp.mean(xc * xc, axis=-1, keepdims=True) + EPS)
        xhat = xc * rstd
        y = xhat * lg_ref[...] + lb_ref[...]
        sy = _sigmoid(y)
        g = gc_ref[...]
        sg = _sigmoid(g)
        dm = dm_ref[...].astype(F32)
        dgc_ref[...] = (dm * (y * sy) * _dsilu(g, sg)).astype(BF16)
        dy = dm * (g * sg) * _dsilu(y, sy)
        dlb_ref[...] += jnp.sum(dy, axis=0, keepdims=True)
        dlg_ref[...] += jnp.sum(dy * xhat, axis=0, keepdims=True)
        dxh = dy * lg_ref[...]
        dc = rstd * (dxh - jnp.mean(dxh, axis=-1, keepdims=True)
                     - xhat * jnp.mean(dxh * xhat, axis=-1, keepdims=True))
        dc_ref[...] = dc
        dbias_ref[...] += jnp.sum(dc, axis=0, keepdims=True)

    row = pl.BlockSpec((tt, cw), lambda i: (i, 0))
    vec = pl.BlockSpec((1, cw), lambda i: (0, 0))
    return _pcall(
        body, name=name, grid=(t // tt,),
        in_specs=[row, pl.BlockSpec((tt, cw), lambda i: (i, lay["GCONV"] // cw)),
                  pl.BlockSpec((tt, cw), lambda i: (i, 1)), vec, vec],
        out_specs=[row, pl.BlockSpec((tt, cw), lambda i: (i, lay["GCONV"] // cw)), vec, vec, vec],
        out_shape=[jax.ShapeDtypeStruct((t, cw), F32), jax.ShapeDtypeStruct((t, lay["NP"]), BF16),
                   jax.ShapeDtypeStruct((1, cw), F32), jax.ShapeDtypeStruct((1, cw), F32),
                   jax.ShapeDtypeStruct((1, cw), F32)],
        compiler_params=_params(("arbitrary",)),
    )(c_pre, z, dmix, ln_g.reshape(1, cw), ln_b.reshape(1, cw))


def _conv_bwd_dw(dc, u, z, w_dw, dz, lay, nb, *, name):
    t, cw = dc.shape
    s = t // nb
    tt = _tile(s, CONV_TILE, HALO)
    ns = s // tt
    hb = tt // HALO
    lc = _tile(cw, LANE_CHUNK, 128)

    def body(dc_ref, dcn_ref, u_ref, up_ref, a_ref, b_ref, w_ref, dz_in, dab_ref, dw_ref, ext_dc, ext_u, du_ref):
        del dz_in
        b_i, i = pl.program_id(0), pl.program_id(1)

        @pl.when((b_i == 0) & (i == 0))
        def _():
            dw_ref[...] = jnp.zeros_like(dw_ref)

        dc_v = dc_ref[...]
        ext_dc[0:tt, :] = dc_v
        ext_dc[tt:tt + HALO, :] = jnp.where(i < ns - 1, dcn_ref[...], 0.0)
        ext_u[0:HALO, :] = jnp.where(i > 0, up_ref[...], 0.0)
        ext_u[HALO:HALO + tt, :] = u_ref[...]
        for c0 in range(0, cw, lc):
            acc = jnp.zeros((tt, lc), F32)
            dcc = dc_v[:, c0:c0 + lc]
            for k in range(CONV_K):
                acc = acc + w_ref[k:k + 1, c0:c0 + lc] * ext_dc[CONV_K - 1 - k:CONV_K - 1 - k + tt, c0:c0 + lc]
                off = HALO - (CONV_K - 1) + k
                dw_ref[k:k + 1, c0:c0 + lc] += jnp.sum(dcc * ext_u[off:off + tt, c0:c0 + lc], axis=0, keepdims=True)
            du_ref[:, c0:c0 + lc] = acc
        du = du_ref[...]
        sb = _sigmoid(b_ref[...])
        dab_ref[:, 0:cw] = (du * sb).astype(BF16)
        dab_ref[:, cw:2 * cw] = (du * a_ref[...] * sb * (1.0 - sb)).astype(BF16)

    last = nb * ns * hb - 1
    row = pl.BlockSpec((tt, cw), lambda b, i: (b * ns + i, 0))
    return _pcall(
        body, name=name, grid=(nb, ns),
        in_specs=[row, pl.BlockSpec((HALO, cw), lambda b, i: (jnp.minimum((b * ns + i + 1) * hb, last), 0)),
                  row, pl.BlockSpec((HALO, cw), lambda b, i: (jnp.maximum((b * ns + i) * hb - 1, 0), 0)),
                  pl.BlockSpec((tt, cw), lambda b, i: (b * ns + i, lay["A"] // cw)),
                  pl.BlockSpec((tt, cw), lambda b, i: (b * ns + i, lay["B"] // cw)),
                  pl.BlockSpec((HALO, cw), lambda b, i: (0, 0)), pl.BlockSpec(memory_space=pl.ANY)],
        out_specs=[pl.BlockSpec((tt, 2 * cw), lambda b, i: (b * ns + i, 0)),
                   pl.BlockSpec((HALO, cw), lambda b, i: (0, 0))],
        out_shape=[jax.ShapeDtypeStruct(dz.shape, BF16), jax.ShapeDtypeStruct((HALO, cw), F32)],
        scratch_shapes=[pltpu.VMEM((tt + HALO, cw), F32), pltpu.VMEM((tt + HALO, cw), F32),
                        pltpu.VMEM((tt, cw), F32)],
        input_output_aliases={7: 0},
        compiler_params=_params(("arbitrary", "arbitrary")),
    )(dc, dc, u, u, z, z, w_dw, dz)


def _loss_head(y, target, *, name):
    t, d = y.shape
    tt = _tile(t, TOK_TILE, 16)

    def body(y_ref, t_ref, sse_ref, dy_ref, dyb_ref):
        @pl.when(pl.program_id(0) == 0)
        def _():
            sse_ref[...] = jnp.zeros_like(sse_ref)

        e = y_ref[...] - t_ref[...]
        sse_ref[...] += jnp.sum(e * e)
        dy = e * (1.0 / d)
        dy_ref[...] = dy
        dyb_ref[...] = dy.astype(BF16)

    row = pl.BlockSpec((tt, d), lambda i: (i, 0))
    return _pcall(
        body, name=name, grid=(t // tt,),
        in_specs=[row, row],
        out_specs=[pl.BlockSpec((8, 128), lambda i: (0, 0)), row, row],
        out_shape=[jax.ShapeDtypeStruct((8, 128), F32), jax.ShapeDtypeStruct((t, d), F32),
                   jax.ShapeDtypeStruct((t, d), BF16)],
        compiler_params=_params(("arbitrary",)),
    )(y, target)


def _adam(w, m, v, g_parts, *, name):
    rows, cols = w.shape
    tr = _tile(rows, max(16, ADAM_BLOCK_ELEMS // cols), 16)
    n = len(g_parts)

    def body(*refs):
        w_ref, m_ref, v_ref = refs[:3]
        g_refs = refs[3:3 + n]
        g_out, d_out, m_out, v_out = refs[3 + n:]
        g = g_refs[0][...].astype(F32)
        for r in g_refs[1:]:
            g = g + r[...].astype(F32)
        m_new = ADAM_B1 * m_ref[...] + (1.0 - ADAM_B1) * g
        v_new = ADAM_B2 * v_ref[...] + (1.0 - ADAM_B2) * (g * g)
        m_hat = m_new / (1.0 - ADAM_B1 ** ADAM_STEP)
        v_hat = v_new / (1.0 - ADAM_B2 ** ADAM_STEP)
        g_out[...] = g
        d_out[...] = -ADAM_LR * (m_hat / (jnp.sqrt(v_hat) + ADAM_EPS) + ADAM_WD * w_ref[...])
        m_out[...] = m_new
        v_out[...] = v_new

    blk = pl.BlockSpec((tr, cols), lambda i: (i, 0))
    g_specs, g_args = [], []
    for arr, lead in g_parts:
        g_args.append(arr)
        if lead is None:
            g_specs.append(blk)
        else:
            g_specs.append(pl.BlockSpec((None, tr, cols), functools.partial(lambda i, p: (p, i, 0), p=lead)))
    out = jax.ShapeDtypeStruct((rows, cols), F32)
    return _pcall(
        body, name=name, grid=(rows // tr,),
        in_specs=[blk, blk, blk] + g_specs,
        out_specs=[blk, blk, blk, blk],
        out_shape=[out, out, out, out],
        compiler_params=_params(("parallel",)),
    )(w, m, v, *g_args)


def _position():
    return lax.axis_index("x"), lax.axis_index("y"), lax.axis_index("c")


def _block_id(p):
    return 4 * p[0] + 2 * p[1] + p[2]


def _flip(p, mask):
    return tuple((1 - v) if (mask >> (2 - a)) & 1 else v for a, v in enumerate(p))


def _all_gather(xs, *, name):
    n = len(xs)

    def body(*refs):
        x_refs, o_refs = refs[:n], refs[n:2 * n]
        send_sems, recv_sems, local_sems = refs[2 * n:]
        x, y, c = _position()
        me, sibling = (x, y, c), (x, y, 1 - c)
        chips = [(1 - x, y), (x, 1 - y), (1 - x, 1 - y)]

        def copy(t, k, block, to, src=None):
            dst = o_refs[t].at[_block_id(block)]
            return pltpu.make_async_remote_copy(
                src_ref=dst if src is None else src, dst_ref=dst,
                send_sem=send_sems.at[t, k], recv_sem=recv_sems.at[t, k],
                device_id=to, device_id_type=MESH)

        mine = [pltpu.make_async_copy(x_refs[t], o_refs[t].at[_block_id(me)], local_sems.at[t]) for t in range(n)]
        for cp in mine:
            cp.start()
        started = []
        for t in range(n):
            first = [copy(t, 0, me, sibling, src=x_refs[t])]
            first += [copy(t, 1 + j, me, (*chip, c), src=x_refs[t]) for j, chip in enumerate(chips)]
            for cp in first:
                cp.start()
            started += first
        for j, chip in enumerate(chips):
            for t in range(n):
                copy(t, 1 + j, (*chip, c), me).wait_recv()
                fwd = copy(t, 4 + j, (*chip, c), sibling)
                fwd.start()
                started.append(fwd)
        for t in range(n):
            copy(t, 0, sibling, me).wait_recv()
            for j, chip in enumerate(chips):
                copy(t, 4 + j, (*chip, 1 - c), me).wait_recv()
        for cp in started:
            cp.wait_send()
        for cp in mine:
            cp.wait()

    any_spec = pl.BlockSpec(memory_space=pl.ANY)
    return _pcall(
        body, name=name,
        in_specs=[any_spec] * n, out_specs=[any_spec] * n,
        out_shape=[jax.ShapeDtypeStruct((N_DEV,) + a.shape, a.dtype) for a in xs],
        scratch_shapes=[pltpu.SemaphoreType.DMA((n, 7)), pltpu.SemaphoreType.DMA((n, 7)),
                        pltpu.SemaphoreType.DMA((n,))],
    )(*xs)


def _scatter_to_owners(xs, *, name):
    n = len(xs)

    def body(*refs):
        x_refs, o_refs = refs[:n], refs[n:2 * n]
        send_sems, recv_sems, local_sems = refs[2 * n:]
        me = _position()
        my_id = _block_id(me)
        mine = [pltpu.make_async_copy(x_refs[t].at[my_id], o_refs[t].at[my_id], local_sems.at[t]) for t in range(n)]
        for cp in mine:
            cp.start()
        sent = []
        for t in range(n):
            for mask in range(1, N_DEV):
                peer = _flip(me, mask)
                cp = pltpu.make_async_remote_copy(
                    src_ref=x_refs[t].at[_block_id(peer)], dst_ref=o_refs[t].at[my_id],
                    send_sem=send_sems.at[t, mask - 1], recv_sem=recv_sems.at[t, mask - 1],
                    device_id=peer, device_id_type=MESH)
                cp.start()
                sent.append(cp)
        for t in range(n):
            for mask in range(1, N_DEV):
                slot = o_refs[t].at[_block_id(_flip(me, mask))]
                pltpu.make_async_remote_copy(
                    src_ref=slot, dst_ref=slot, send_sem=send_sems.at[t, mask - 1],
                    recv_sem=recv_sems.at[t, mask - 1], device_id=me, device_id_type=MESH).wait_recv()
        for cp in sent:
            cp.wait_send()
        for cp in mine:
            cp.wait()

    any_spec = pl.BlockSpec(memory_space=pl.ANY)
    return _pcall(
        body, name=name,
        in_specs=[any_spec] * n, out_specs=[any_spec] * n,
        out_shape=[jax.ShapeDtypeStruct(a.shape, a.dtype) for a in xs],
        scratch_shapes=[pltpu.SemaphoreType.DMA((n, 7)), pltpu.SemaphoreType.DMA((n, 7)),
                        pltpu.SemaphoreType.DMA((n,))],
    )(*xs)


def _all_reduce_small(pack, *, name):
    rows = pack.shape[0]

    def body(p_ref, o_ref, gath, send_sems, recv_sems):
        me = _position()
        my_id = _block_id(me)
        gath[my_id] = p_ref[...]
        sent = []
        for mask in range(1, N_DEV):
            peer = _flip(me, mask)
            cp = pltpu.make_async_remote_copy(
                src_ref=p_ref, dst_ref=gath.at[my_id], send_sem=send_sems.at[mask - 1],
                recv_sem=recv_sems.at[mask - 1], device_id=peer, device_id_type=MESH)
            cp.start()
            sent.append(cp)
        for mask in range(1, N_DEV):
            slot = gath.at[_block_id(_flip(me, mask))]
            pltpu.make_async_remote_copy(
                src_ref=slot, dst_ref=slot, send_sem=send_sems.at[mask - 1], recv_sem=recv_sems.at[mask - 1],
                device_id=me, device_id_type=MESH).wait_recv()
        for cp in sent:
            cp.wait_send()
        total = gath[0]
        for s in range(1, N_DEV):
            total = total + gath[s]
        o_ref[...] = total

    vm = pl.BlockSpec(memory_space=pltpu.VMEM)
    return _pcall(
        body, name=name,
        in_specs=[vm], out_specs=vm,
        out_shape=jax.ShapeDtypeStruct(pack.shape, F32),
        scratch_shapes=[pltpu.VMEM((N_DEV, rows, 128), F32), pltpu.SemaphoreType.DMA((7,)),
                        pltpu.SemaphoreType.DMA((7,))],
        compiler_params=pltpu.CompilerParams(vmem_limit_bytes=VMEM_LIMIT),
    )(pack)


def _layout(d, ql, kvl):
    cw = d // 2
    att = d // 2
    lay = {"D": d, "CW": cw, "ATT": att, "H": att // VDIM, "QL": ql, "KVL": kvl}
    lay["A"], lay["B"], lay["GATT"], lay["GCONV"] = 0, cw, 2 * cw, 2 * cw + att
    lay["QC"] = lay["GCONV"] + cw
    lay["KVC"] = lay["QC"] + ql
    lay["KPE"] = lay["KVC"] + kvl
    used = lay["KPE"] + 128
    tn = min(MM_TN, 1024)
    lay["NP"] = -(-used // tn) * tn
    assert att == cw and lay["QC"] % ql == 0 and lay["KVC"] % kvl == 0 and lay["KPE"] % 128 == 0
    lay["o_kv"], lay["o_pe"] = ql, ql + kvl
    lay["o_ga"] = lay["o_pe"] + ROPE
    lay["o_u"] = lay["o_ga"] + att
    lay["o_gc"] = lay["o_u"] + 2 * cw
    lay["IN_COLS"] = lay["o_gc"] + cw
    return lay


def _ungather_cols(g):
    return jnp.transpose(g, (1, 0, 2)).reshape(g.shape[1], -1)


def _to_col_blocks(w):
    r, c = w.shape
    return jnp.transpose(w.reshape(r, N_DEV, c // N_DEV), (1, 0, 2))


def _pad_w_in(w, lay):
    d, cw, att, ql, kvl = lay["D"], lay["CW"], lay["ATT"], lay["QL"], lay["KVL"]
    parts = [w[:, lay["o_u"]:lay["o_u"] + 2 * cw], w[:, lay["o_ga"]:lay["o_ga"] + att],
             w[:, lay["o_gc"]:lay["o_gc"] + cw], w[:, :ql], w[:, lay["o_kv"]:lay["o_kv"] + kvl],
             w[:, lay["o_pe"]:lay["o_pe"] + ROPE],
             jnp.zeros((d, lay["NP"] - lay["KPE"] - ROPE), w.dtype)]
    return jnp.concatenate(parts, axis=1)


def _unpad_w_in(wp, lay):
    cw, att, ql, kvl = lay["CW"], lay["ATT"], lay["QL"], lay["KVL"]
    parts = [wp[:, lay["QC"]:lay["QC"] + ql], wp[:, lay["KVC"]:lay["KVC"] + kvl],
             wp[:, lay["KPE"]:lay["KPE"] + ROPE], wp[:, lay["GATT"]:lay["GATT"] + att],
             wp[:, :2 * cw], wp[:, lay["GCONV"]:lay["GCONV"] + cw]]
    return jnp.concatenate(parts, axis=1)


def _pad_heads(w, h):
    r = w.shape[0]
    return jnp.pad(w.reshape(r, h, QK_DIM), ((0, 0), (0, 0), (0, HEAD_PAD - QK_DIM))).reshape(r, h * HEAD_PAD)


def _unpad_heads(w, h):
    r = w.shape[0]
    return w.reshape(r, h, HEAD_PAD)[:, :, :QK_DIM].reshape(r, h * QK_DIM)


def _rope_tabs(positions):
    half = ROPE // 2
    inv_freq = ROPE_THETA ** (-jnp.arange(half, dtype=F32) / half)
    ang = positions.astype(F32).reshape(-1)[:, None] * inv_freq
    cos, sin = jnp.cos(ang), jnp.sin(ang)
    zero = jnp.zeros_like(cos)
    return (jnp.concatenate([cos, cos, zero, zero], axis=1),
            jnp.concatenate([-sin, zero, zero, zero], axis=1),
            jnp.concatenate([zero, sin, zero, zero], axis=1))


def _pack_rows(vecs):
    rows = []
    for v in vecs:
        flat = v.reshape(-1)
        pad = (-flat.shape[0]) % 1024
        rows.append(jnp.pad(flat, (0, pad)).reshape(-1, 128))
    return jnp.concatenate(rows, axis=0)


def _unpack_rows(pack, shapes):
    out, r0 = [], 0
    for shp in shapes:
        size = math.prod(shp)
        nrows = -(-size // 1024) * 8
        out.append(pack[r0:r0 + nrows].reshape(-1)[:size].reshape(shp))
        r0 += nrows
    return out


def kernel(x, positions, ln_g, w_in, q_a_norm, w_q_up, kv_a_norm, w_kv_up, q_norm, k_norm, w_dw, b_dw, conv_ln_g, conv_ln_b, w_out, loss_target, m_ln_g, m_w_in, m_q_a_norm, m_w_q_up, m_kv_a_norm, m_w_kv_up, m_q_norm, m_k_norm, m_w_dw, m_b_dw, m_conv_ln_g, m_conv_ln_b, m_w_out, v_ln_g, v_w_in, v_q_a_norm, v_w_q_up, v_kv_a_norm, v_w_kv_up, v_q_norm, v_k_norm, v_w_dw, v_b_dw, v_conv_ln_g, v_conv_ln_b, v_w_out):
    nb, seq, d = x.shape
    depth = ln_g.shape[0]
    lay = _layout(d, q_a_norm.shape[1], kv_a_norm.shape[1])
    h, cw, ql, kvl = lay["H"], lay["CW"], lay["QL"], lay["KVL"]
    t = nb * seq
    my_id = _block_id(_position())

    g_in, g_q, g_kv, g_out, g_dw = _all_gather(
        [w_in.astype(BF16), w_q_up.astype(BF16), w_kv_up.astype(BF16), w_out.astype(BF16), w_dw],
        name="gather_weights")
    w_in_p, w_in_t, w_q_p, w_q_t, w_kv_f, w_kv_t, w_out_f, w_out_t = [], [], [], [], [], [], [], []
    for l in range(depth):
        wp = _pad_w_in(_ungather_cols(g_in[:, l]), lay)
        wq = _pad_heads(_ungather_cols(g_q[:, l]), h)
        wkv = _ungather_cols(g_kv[:, l])
        wo = g_out[:, l].reshape(2 * cw, d)
        w_in_p.append(wp), w_in_t.append(wp.T)
        w_q_p.append(wq), w_q_t.append(wq.T)
        w_kv_f.append(wkv), w_kv_t.append(wkv.T)
        w_out_f.append(wo), w_out_t.append(wo.T)

    tabs = _rope_tabs(positions)
    gq_pad = jnp.pad(q_norm, ((0, 0), (0, HEAD_PAD - QK_DIM)))
    gk_pad = jnp.pad(k_norm, ((0, 0), (0, HEAD_PAD - QK_DIM)))
    w_dw_all = jnp.transpose(g_dw, (1, 2, 0, 3)).reshape(depth, CONV_K, cw)
    w_dw_all = jnp.pad(w_dw_all, ((0, 0), (0, HALO - CONV_K), (0, 0)))

    saved = []
    xs = x.reshape(t, d)
    for l in range(depth):
        hid = _rms_fwd(xs, ln_g[l], name=f"rms_fwd_{l}")
        z = _mm(hid, w_in_p[l], name=f"in_proj_{l}")
        qn, kvn = _lat_fwd(z, q_a_norm[l], kv_a_norm[l], lay, name=f"lat_fwd_{l}")
        q_raw = _mm(qn, w_q_p[l], name=f"q_up_{l}")
        kv_raw = _mm(kvn, w_kv_f[l], name=f"kv_up_{l}")
        qh, kh, vh = _heads_fwd(q_raw, kv_raw, z, tabs, gq_pad[l:l + 1], gk_pad[l:l + 1], lay, name=f"heads_fwd_{l}")
        att, mix, lse = _flash_fwd(qh, kh, vh, z, lay, nb, name=f"flash_fwd_{l}")
        mix, u, c_pre = _conv_fwd(z, mix, w_dw_all[l], b_dw[l], conv_ln_g[l], conv_ln_b[l], lay, nb,
                                  name=f"conv_fwd_{l}")
        x_next = _mm(mix, w_out_f[l], add=xs, name=f"out_proj_{l}")
        saved.append((xs, hid, z, qn, kvn, q_raw, kv_raw, qh, kh, vh, att, lse, mix, u, c_pre))
        xs = x_next

    sse, dx, dxb = _loss_head(xs, loss_target.reshape(t, d), name="loss_head")
    loss = lax.psum(sse[0, 0] * (0.5 / d), ("x", "y", "c"))

    small = {k: [] for k in ("ln_g", "q_a", "kv_a", "q_n", "k_n", "w_dw", "b_dw", "cln_g", "cln_b")}
    big = {k: [] for k in ("in", "q", "kv", "out")}
    for l in reversed(range(depth)):
        xs, hid, z, qn, kvn, q_raw, kv_raw, qh, kh, vh, att, lse, mix, u, c_pre = saved[l]
        dmix = _mm(dxb, w_out_t[l], name=f"d_mix_{l}")
        big["out"].append(_mm(mix, dxb, trans_a=True, out_dtype=BF16, name=f"dw_out_{l}"))
        dc, dz, dlg, dlb, dbias = _conv_bwd_ln(c_pre, z, dmix, conv_ln_g[l], conv_ln_b[l], lay,
                                               name=f"conv_bwd_ln_{l}")
        dz, dwdw = _conv_bwd_dw(dc, u, z, w_dw_all[l], dz, lay, nb, name=f"conv_bwd_dw_{l}")
        do, delta, dz = _gate_bwd(dmix, att, z, dz, lay, name=f"gate_bwd_{l}")
        dqh, dkh, dvh = _flash_bwd(qh, kh, vh, do, lse, delta, nb, name=f"flash_bwd_{l}")
        dq_raw, dkv_raw, dpe, dgq, dgk = _heads_bwd(q_raw, kv_raw, z, tabs, gq_pad[l:l + 1], gk_pad[l:l + 1],
                                                    dqh, dkh, dvh, lay, name=f"heads_bwd_{l}")
        dqn = _mm(dq_raw, w_q_t[l], name=f"d_qn_{l}")
        dkvn = _mm(dkv_raw, w_kv_t[l], name=f"d_kvn_{l}")
        big["q"].append(_mm(qn, dq_raw, trans_a=True, out_dtype=BF16, name=f"dw_q_{l}"))
        big["kv"].append(_mm(kvn, dkv_raw, trans_a=True, out_dtype=BF16, name=f"dw_kv_{l}"))
        dz, dgqa, dgkva = _lat_bwd(z, q_a_norm[l], kv_a_norm[l], dqn, dkvn, dpe, dz, lay, name=f"lat_bwd_{l}")
        dh =_mm(dz, w_in_t[l], name=f"d_hid_{l}")
        big["in"].append(_mm(hid, dz, trans_a=True, out_dtype=BF16, name=f"dw_in_{l}"))
        dx, dxb, dlng = _rms_bwd(xs, ln_g[l], dh, dx, name=f"rms_bwd_{l}")
        for key, val in (("ln_g", dlng), ("q_a", dgqa), ("kv_a", dgkva), ("q_n", dgq[:, :QK_DIM]),
                         ("k_n", dgk[:, :QK_DIM]), ("w_dw", dwdw[:CONV_K]), ("b_dw", dbias),
                         ("cln_g", dlg), ("cln_b", dlb)):
            small[key].append(val)
    grad_x = dx.reshape(nb, seq, d)
    for key in small:
        small[key] = jnp.stack(small[key][::-1])
    for key in big:
        big[key] = big[key][::-1]

    s_in = jnp.stack([_to_col_blocks(_unpad_w_in(g, lay)) for g in big["in"]], axis=1)
    s_q = jnp.stack([_to_col_blocks(_unpad_heads(g, h)) for g in big["q"]], axis=1)
    s_kv = jnp.stack([_to_col_blocks(g) for g in big["kv"]], axis=1)
    s_out = jnp.stack([g.reshape(N_DEV, (2 * cw) // N_DEV, d) for g in big["out"]], axis=1)
    r_in, r_q, r_kv, r_out = _scatter_to_owners([s_in, s_q, s_kv, s_out], name="scatter_weight_grads")

    small_names = ("ln_g", "q_a", "kv_a", "q_n", "k_n", "b_dw", "cln_g", "cln_b", "w_dw")
    small_shapes = [small[k].shape for k in small_names]
    summed = _unpack_rows(_all_reduce_small(_pack_rows([small[k] for k in small_names]), name="reduce_small_grads"),
                          small_shapes)
    sg = dict(zip(small_names, summed))
    g_w_dw = lax.dynamic_slice_in_dim(sg["w_dw"], my_id * (cw // N_DEV), cw // N_DEV, axis=2)

    def adam_big(w, m, v, recv, nm):
        shp = w.shape
        rows, cols = shp[0] * shp[1], shp[2]
        parts = [(recv.reshape(N_DEV, rows, cols), s) for s in range(N_DEV)]
        outs = _adam(w.reshape(rows, cols), m.reshape(rows, cols), v.reshape(rows, cols), parts, name=nm)
        return [o.reshape(shp) for o in outs]

    def adam_small(ws, ms, vs, gs, nm):
        shapes = [w.shape for w in ws]
        outs = _adam(_pack_rows(ws), _pack_rows(ms), _pack_rows(vs), [(_pack_rows(gs), None)], name=nm)
        return [_unpack_rows(o, shapes) for o in outs]

    res = {}
    res["w_in"] = adam_big(w_in, m_w_in, v_w_in, r_in, "adam_w_in")
    res["w_q_up"] = adam_big(w_q_up, m_w_q_up, v_w_q_up, r_q, "adam_w_q_up")
    res["w_kv_up"] = adam_big(w_kv_up, m_w_kv_up, v_w_kv_up, r_kv, "adam_w_kv_up")
    res["w_out"] = adam_big(w_out, m_w_out, v_w_out, r_out, "adam_w_out")
    names_s = ["ln_g", "q_a_norm", "kv_a_norm", "q_norm", "k_norm", "w_dw", "b_dw", "conv_ln_g", "conv_ln_b"]
    ws = [ln_g, q_a_norm, kv_a_norm, q_norm, k_norm, w_dw, b_dw, conv_ln_g, conv_ln_b]
    ms = [m_ln_g, m_q_a_norm, m_kv_a_norm, m_q_norm, m_k_norm, m_w_dw, m_b_dw, m_conv_ln_g, m_conv_ln_b]
    vs = [v_ln_g, v_q_a_norm, v_kv_a_norm, v_q_norm, v_k_norm, v_w_dw, v_b_dw, v_conv_ln_g, v_conv_ln_b]
    gs = [sg["ln_g"].reshape(ln_g.shape), sg["q_a"].reshape(q_a_norm.shape), sg["kv_a"].reshape(kv_a_norm.shape),
          sg["q_n"].reshape(q_norm.shape), sg["k_n"].reshape(k_norm.shape), g_w_dw,
          sg["b_dw"].reshape(b_dw.shape), sg["cln_g"].reshape(conv_ln_g.shape), sg["cln_b"].reshape(conv_ln_b.shape)]
    outs_s = adam_small(ws, ms, vs, gs, "adam_small")
    for idx, nm in enumerate(names_s):
        res[nm] = [outs_s[k][idx] for k in range(4)]

    order = ["ln_g", "w_in", "q_a_norm", "w_q_up", "kv_a_norm", "w_kv_up", "q_norm", "k_norm", "w_dw", "b_dw",
             "conv_ln_g", "conv_ln_b", "w_out"]
    return (loss, grad_x, *[res[nm][0] for nm in order], *[res[nm][1] for nm in order],
            *[res[nm][2] for nm in order], *[res[nm][3] for nm in order])
```

```python
import functools
import math

import jax
import jax.numpy as jnp
from jax import lax
from jax.experimental import pallas as pl
from jax.experimental.pallas import tpu as pltpu

F32 = jnp.float32
BF16 = jnp.bfloat16
MESH = pl.DeviceIdType.MESH

N_DEV = 8
NOPE = 128
ROPE = 64
VDIM = 128
HEAD_PAD = 256
QK_DIM = NOPE + ROPE
CONV_K = 31
HALO = 32
EPS = 1e-6
ROPE_THETA = 10000.0
NEG = -1e30

ADAM_LR = 0.001
ADAM_B1 = 0.9
ADAM_B2 = 0.999
ADAM_EPS = 1e-08
ADAM_WD = 0.01
ADAM_STEP = 10

TOK_TILE = 256
CONV_TILE = 256
ATT_TQ = 512
MM_TM = 1024
MM_TN = 512
MM_TK = 4096
MM_VMEM_BUDGET = 40 * 1024 * 1024
ADAM_BLOCK_ELEMS = 128 * 1024
LANE_CHUNK = 256
VMEM_LIMIT = 56 * 1024 * 1024


def _pcall(body, **kw):
    return pl.pallas_call(body, **kw)


def _tile(dim, pref, mult):
    t = min(pref, dim)
    t -= t % mult
    while t >= mult:
        if dim % t == 0:
            return t
        t -= mult
    return dim


def _params(sem):
    return pltpu.CompilerParams(dimension_semantics=sem, vmem_limit_bytes=VMEM_LIMIT)


def _sigmoid(v):
    return 1.0 / (1.0 + jnp.exp(-v))


def _dsilu(v, sg):
    return sg * (1.0 + v * (1.0 - sg))


def _mm_tiles(m, n, kdim, out_bytes, has_add):
    tk = _tile(kdim, MM_TK, 128)
    tn = _tile(n, MM_TN, 128)
    for pref in (MM_TM, MM_TM // 2, MM_TM // 4, 128):
        tm = _tile(m, pref, 128)
        need = 2 * 2 * (tm * tk + tk * tn) + 2 * tm * tn * out_bytes
        need += tm * tn * 4 * ((kdim > tk) + 2 * has_add + 1)
        if need <= MM_VMEM_BUDGET:
            break
    return tm, tn, tk


def _mm(a, b, *, name, trans_a=False, add=None, out_dtype=F32, dep=None):
    if trans_a:
        kdim, m = a.shape
    else:
        m, kdim = a.shape
    n = b.shape[1]
    assert b.shape[0] == kdim
    has_add = add is not None
    tm, tn, tk = _mm_tiles(m, n, kdim, jnp.dtype(out_dtype).itemsize, has_add)
    nk = kdim // tk

    def product(a_ref, b_ref):
        if trans_a:
            return lax.dot_general(a_ref[...], b_ref[...], (((0,), (0,)), ((), ())), preferred_element_type=F32)
        return jnp.dot(a_ref[...], b_ref[...], preferred_element_type=F32)

    def body(*refs):
        a_ref, b_ref = refs[:2]
        add_ref = refs[2] if has_add else None
        o_ref = refs[2 + has_add + (dep is not None)]

        def finish(r):
            if has_add:
                r = r + add_ref[...]
            o_ref[...] = r.astype(o_ref.dtype)

        if nk == 1:
            finish(product(a_ref, b_ref))
            return
        acc_ref = refs[-1]
        k = pl.program_id(2)

        @pl.when(k == 0)
        def _():
            acc_ref[...] = product(a_ref, b_ref)

        @pl.when((k > 0) & (k < nk - 1))
        def _():
            acc_ref[...] += product(a_ref, b_ref)

        @pl.when(k == nk - 1)
        def _():
            finish(acc_ref[...] + product(a_ref, b_ref))

    if trans_a:
        a_spec = pl.BlockSpec((tk, tm), lambda i, j, k: (k, i))
    else:
        a_spec = pl.BlockSpec((tm, tk), lambda i, j, k: (i, k))
    in_specs = [a_spec, pl.BlockSpec((tk, tn), lambda i, j, k: (k, j))]
    args = [a, b]
    if has_add:
        in_specs.append(pl.BlockSpec((tm, tn), lambda i, j, k: (i, j)))
        args.append(add)
    if dep is not None:
        in_specs.append(pl.BlockSpec(memory_space=pl.ANY))
        args.append(dep)
    return _pcall(
        body, name=name,
        grid=(m // tm, n // tn, nk),
        in_specs=in_specs,
        out_specs=pl.BlockSpec((tm, tn), lambda i, j, k: (i, j)),
        out_shape=jax.ShapeDtypeStruct((m, n), out_dtype),
        scratch_shapes=[pltpu.VMEM((tm, tn), F32)] if nk > 1 else [],
        compiler_params=_params(("parallel", "parallel", "arbitrary")),
    )(*args)


def _rms_fwd(x, g, *, name, dep=None):
    t, d = x.shape
    tt = _tile(t, TOK_TILE, 16)

    def body(x_ref, g_ref, *rest):
        h_ref = rest[-1]
        xv = x_ref[...]
        r = lax.rsqrt(jnp.mean(xv * xv, axis=-1, keepdims=True) + EPS)
        h_ref[...] = (xv * r * g_ref[...]).astype(BF16)

    deps = [] if dep is None else [dep]
    return _pcall(
        body, name=name, grid=(t // tt,),
        in_specs=[pl.BlockSpec((tt, d), lambda i: (i, 0)), pl.BlockSpec((1, d), lambda i: (0, 0))]
        + [pl.BlockSpec(memory_space=pl.ANY)] * len(deps),
        out_specs=pl.BlockSpec((tt, d), lambda i: (i, 0)),
        out_shape=jax.ShapeDtypeStruct((t, d), BF16),
        compiler_params=_params(("parallel",)),
    )(x, g.reshape(1, d), *deps)


def _rms_bwd(x, g, dh, dres, *, name):
    t, d = x.shape
    tt = _tile(t, TOK_TILE, 16)

    def body(x_ref, g_ref, dh_ref, dres_ref, dx_ref, dxb_ref, dg_ref):
        xv = x_ref[...]
        r = lax.rsqrt(jnp.mean(xv * xv, axis=-1, keepdims=True) + EPS)
        dy = dh_ref[...]
        dyg = dy * g_ref[...]
        dot = jnp.sum(dyg * xv, axis=-1, keepdims=True) * (1.0 / d)
        dx = dres_ref[...] + r * dyg - xv * (r * r * r) * dot
        dx_ref[...] = dx
        dxb_ref[...] = dx.astype(BF16)

        @pl.when(pl.program_id(0) == 0)
        def _():
            dg_ref[...] = jnp.zeros_like(dg_ref)

        dg_ref[...] += jnp.sum(dy * xv * r, axis=0, keepdims=True)

    row = pl.BlockSpec((tt, d), lambda i: (i, 0))
    vec = pl.BlockSpec((1, d), lambda i: (0, 0))
    return _pcall(
        body, name=name, grid=(t // tt,),
        in_specs=[row, vec, row, row],
        out_specs=[row, row, vec],
        out_shape=[jax.ShapeDtypeStruct((t, d), F32), jax.ShapeDtypeStruct((t, d), BF16),
                   jax.ShapeDtypeStruct((1, d), F32)],
        compiler_params=_params(("arbitrary",)),
    )(x, g.reshape(1, d), dh, dres)


def _lat_fwd(z, gq, gkv, lay, *, name):
    t = z.shape[0]
    ql, kvl = lay["QL"], lay["KVL"]
    tt = _tile(t, TOK_TILE, 16)

    def body(q_ref, kv_ref, gq_ref, gkv_ref, qn_ref, kvn_ref):
        for src, g_ref, dst in ((q_ref, gq_ref, qn_ref), (kv_ref, gkv_ref, kvn_ref)):
            v = src[...]
            r = lax.rsqrt(jnp.mean(v * v, axis=-1, keepdims=True) + EPS)
            dst[...] = (v * r * g_ref[...]).astype(BF16)

    return _pcall(
        body, name=name, grid=(t // tt,),
        in_specs=[pl.BlockSpec((tt, ql), lambda i: (i, lay["QC"] // ql)),
                  pl.BlockSpec((tt, kvl), lambda i: (i, lay["KVC"] // kvl)),
                  pl.BlockSpec((1, ql), lambda i: (0, 0)), pl.BlockSpec((1, kvl), lambda i: (0, 0))],
        out_specs=[pl.BlockSpec((tt, ql), lambda i: (i, 0)), pl.BlockSpec((tt, kvl), lambda i: (i, 0))],
        out_shape=[jax.ShapeDtypeStruct((t, ql), BF16), jax.ShapeDtypeStruct((t, kvl), BF16)],
        compiler_params=_params(("parallel",)),
    )(z, z, gq.reshape(1, ql), gkv.reshape(1, kvl))


def _lat_bwd(z, gq, gkv, dqn, dkvn, dpe, dz, lay, *, name):
    t = z.shape[0]
    ql, kvl = lay["QL"], lay["KVL"]
    tail = lay["NP"] - lay["QC"]
    assert lay["QC"] % tail == 0
    tt = _tile(t, TOK_TILE, 16)

    def body(q_ref, kv_ref, gq_ref, gkv_ref, dqn_ref, dkvn_ref, dpe_ref, dz_in, tail_ref, dgq_ref, dgkv_ref):
        del dz_in
        first = pl.program_id(0) == 0
        for src, g_ref, dy_ref, c0, dg_ref in ((q_ref, gq_ref, dqn_ref, 0, dgq_ref),
                                               (kv_ref, gkv_ref, dkvn_ref, ql, dgkv_ref)):
            v = src[...]
            n = v.shape[-1]
            r = lax.rsqrt(jnp.mean(v * v, axis=-1, keepdims=True) + EPS)
            dy = dy_ref[...]
            dyg = dy * g_ref[...]
            dot = jnp.sum(dyg * v, axis=-1, keepdims=True) * (1.0 / n)
            tail_ref[:, c0:c0 + n] = (r * dyg - v * (r * r * r) * dot).astype(BF16)

            @pl.when(first)
            def _():
                dg_ref[...] = jnp.zeros_like(dg_ref)

            dg_ref[...] += jnp.sum(dy * v * r, axis=0, keepdims=True)
        tail_ref[:, ql + kvl:ql + kvl + 128] = dpe_ref[...].astype(BF16)
        tail_ref[:, ql + kvl + 128:tail] = jnp.zeros((tt, tail - ql - kvl - 128), BF16)

    return _pcall(
        body, name=name, grid=(t // tt,),
        in_specs=[pl.BlockSpec((tt, ql), lambda i: (i, lay["QC"] // ql)),
                  pl.BlockSpec((tt, kvl), lambda i: (i, lay["KVC"] // kvl)),
                  pl.BlockSpec((1, ql), lambda i: (0, 0)), pl.BlockSpec((1, kvl), lambda i: (0, 0)),
                  pl.BlockSpec((tt, ql), lambda i: (i, 0)), pl.BlockSpec((tt, kvl), lambda i: (i, 0)),
                  pl.BlockSpec((tt, 128), lambda i: (i, 0)), pl.BlockSpec(memory_space=pl.ANY)],
        out_specs=[pl.BlockSpec((tt, tail), lambda i: (i, lay["QC"] // tail)),
                   pl.BlockSpec((1, ql), lambda i: (0, 0)), pl.BlockSpec((1, kvl), lambda i: (0, 0))],
        out_shape=[jax.ShapeDtypeStruct(dz.shape, BF16),
                   jax.ShapeDtypeStruct((1, ql), F32), jax.ShapeDtypeStruct((1, kvl), F32)],
        input_output_aliases={7: 0},
        compiler_params=_params(("arbitrary",)),
    )(z, z, gq.reshape(1, ql), gkv.reshape(1, kvl), dqn, dkvn, dpe, dz)


def _rope(r, c_tab, sa_tab, sb_tab):
    return r * c_tab + pltpu.roll(r, 96, 1) * sa_tab + pltpu.roll(r, 32, 1) * sb_tab


def _rope_t(d, c_tab, sa_tab, sb_tab):
    return d * c_tab + pltpu.roll(d * sa_tab, 32, 1) + pltpu.roll(d * sb_tab, 96, 1)


def _heads_fwd(q_raw, kv_raw, z, tabs, gq, gk, lay, *, name):
    t = z.shape[0]
    h = lay["H"]
    tt = _tile(t, TOK_TILE, 16)
    scale = 1.0 / math.sqrt(QK_DIM)

    def body(q_ref, kv_ref, pe_ref, c_ref, sa_ref, sb_ref, gq_ref, gk_ref, qh_ref, kh_ref, vh_ref):
        c_tab, sa_tab, sb_tab = c_ref[...], sa_ref[...], sb_ref[...]
        q = q_ref[...]
        r = lax.rsqrt(jnp.sum(q * q, axis=-1, keepdims=True) * (1.0 / QK_DIM) + EPS)
        qn = q * r * gq_ref[...]
        qh_ref[...] = (jnp.concatenate([qn[:, :NOPE], _rope(qn[:, NOPE:], c_tab, sa_tab, sb_tab)], axis=1)
                       * scale).astype(BF16)
        kv = kv_ref[...]
        kn, pe = kv[:, :NOPE], pe_ref[...]
        ss = jnp.sum(kn * kn, axis=-1, keepdims=True) + jnp.sum(pe * pe, axis=-1, keepdims=True)
        rk = lax.rsqrt(ss * (1.0 / QK_DIM) + EPS)
        gk_v = gk_ref[...]
        kh_ref[...] = jnp.concatenate(
            [kn * rk * gk_v[:, :NOPE], _rope(pe * rk * gk_v[:, NOPE:], c_tab, sa_tab, sb_tab)], axis=1).astype(BF16)
        vh_ref[...] = kv[:, NOPE:].astype(BF16)

    head = pl.BlockSpec((tt, HEAD_PAD), lambda i, j: (i, j))
    tab = pl.BlockSpec((tt, 128), lambda i, j: (i, 0))
    gain = pl.BlockSpec((1, HEAD_PAD), lambda i, j: (0, 0))
    return _pcall(
        body, name=name, grid=(t // tt, h),
        in_specs=[head, head, pl.BlockSpec((tt, 128), lambda i, j: (i, lay["KPE"] // 128)), tab, tab, tab, gain, gain],
        out_specs=[pl.BlockSpec((None, tt, HEAD_PAD), lambda i, j: (j, i, 0)),
                   pl.BlockSpec((None, tt, HEAD_PAD), lambda i, j: (j, i, 0)),
                   pl.BlockSpec((None, tt, VDIM), lambda i, j: (j, i, 0))],
        out_shape=[jax.ShapeDtypeStruct((h, t, HEAD_PAD), BF16), jax.ShapeDtypeStruct((h, t, HEAD_PAD), BF16),
                   jax.ShapeDtypeStruct((h, t, VDIM), BF16)],
        compiler_params=_params(("parallel", "parallel")),
    )(q_raw, kv_raw, z, *tabs, gq, gk)


def _heads_bwd(q_raw, kv_raw, z, tabs, gq, gk, dqh, dkh, dvh, lay, *, name):
    t = z.shape[0]
    h = lay["H"]
    tt = _tile(t, TOK_TILE, 16)
    scale = 1.0 / math.sqrt(QK_DIM)

    def body(q_ref, kv_ref, pe_ref, c_ref, sa_ref, sb_ref, gq_ref, gk_ref, dqh_ref, dkh_ref, dvh_ref,
             dq_ref, dkv_ref, dpe_ref, dgq_ref, dgk_ref):
        i, j = pl.program_id(0), pl.program_id(1)
        c_tab, sa_tab, sb_tab = c_ref[...], sa_ref[...], sb_ref[...]

        @pl.when((i == 0) & (j == 0))
        def _():
            dgq_ref[...] = jnp.zeros_like(dgq_ref)
            dgk_ref[...] = jnp.zeros_like(dgk_ref)

        @pl.when(j == 0)
        def _():
            dpe_ref[...] = jnp.zeros_like(dpe_ref)

        def norm_bwd(v, g, dy):
            r = lax.rsqrt(jnp.sum(v * v, axis=-1, keepdims=True) * (1.0 / QK_DIM) + EPS)
            dyg = dy * g
            dot = jnp.sum(dyg * v, axis=-1, keepdims=True) * (1.0 / QK_DIM)
            return r * dyg - v * (r * r * r) * dot, jnp.sum(dy * v * r, axis=0, keepdims=True)

        dqo = dqh_ref[...] * scale
        dy = jnp.concatenate([dqo[:, :NOPE], _rope_t(dqo[:, NOPE:], c_tab, sa_tab, sb_tab)], axis=1)
        dq, dgq = norm_bwd(q_ref[...], gq_ref[...], dy)
        dq_ref[...] = dq.astype(BF16)
        dgq_ref[...] += dgq

        dko = dkh_ref[...]
        dy = jnp.concatenate([dko[:, :NOPE], _rope_t(dko[:, NOPE:], c_tab, sa_tab, sb_tab)], axis=1)
        kfull = jnp.concatenate([kv_ref[...][:, :NOPE], pe_ref[...]], axis=1)
        dk, dgk = norm_bwd(kfull, gk_ref[...], dy)
        dkv_ref[...] = jnp.concatenate([dk[:, :NOPE], dvh_ref[...]], axis=1).astype(BF16)
        dpe_ref[...] += dk[:, NOPE:]
        dgk_ref[...] += dgk

    head = pl.BlockSpec((tt, HEAD_PAD), lambda i, j: (i, j))
    tab = pl.BlockSpec((tt, 128), lambda i, j: (i, 0))
    gain = pl.BlockSpec((1, HEAD_PAD), lambda i, j: (0, 0))
    hm = pl.BlockSpec((None, tt, HEAD_PAD), lambda i, j: (j, i, 0))
    return _pcall(
        body, name=name, grid=(t // tt, h),
        in_specs=[head, head, pl.BlockSpec((tt, 128), lambda i, j: (i, lay["KPE"] // 128)), tab, tab, tab, gain, gain,
                  hm, hm, pl.BlockSpec((None, tt, VDIM), lambda i, j: (j, i, 0))],
        out_specs=[head, head, tab, gain, gain],
        out_shape=[jax.ShapeDtypeStruct((t, h * HEAD_PAD), BF16), jax.ShapeDtypeStruct((t, h * HEAD_PAD), BF16),
                   jax.ShapeDtypeStruct((t, 128), F32),
                   jax.ShapeDtypeStruct((1, HEAD_PAD), F32), jax.ShapeDtypeStruct((1, HEAD_PAD), F32)],
        compiler_params=_params(("arbitrary", "arbitrary")),
    )(q_raw, kv_raw, z, *tabs, gq, gk, dqh, dkh, dvh)


def _lower_triangle(n):
    return lax.broadcasted_iota(jnp.int32, (n, n), 1) <= lax.broadcasted_iota(jnp.int32, (n, n), 0)


def _qk(q, k):
    return lax.dot_general(q, k, (((1,), (1,)), ((), ())), preferred_element_type=F32)


def _flash_fwd(qh, kh, vh, z, lay, nb, *, name):
    h, t, _ = qh.shape
    s = t // nb
    tq = _tile(s, ATT_TQ, 128)
    nq = s // tq
    att_w = h * VDIM
    gblk = lay["GATT"] // VDIM

    def body(q_ref, k_ref, v_ref, g_ref, att_ref, mix_ref, lse_ref):
        i = pl.program_id(2)
        tri = _lower_triangle(tq)
        for blk in range(nq):
            @pl.when(i == blk)
            def _():
                q = q_ref[...]
                pre = blk * tq
                sd = jnp.where(tri, _qk(q, k_ref[pre:pre + tq, :]), NEG)
                m = jnp.max(sd, axis=-1, keepdims=True)
                if pre:
                    sp = _qk(q, k_ref[0:pre, :])
                    m = jnp.maximum(m, jnp.max(sp, axis=-1, keepdims=True))
                pd = jnp.exp(sd - m)
                l = jnp.sum(pd, axis=-1, keepdims=True)
                acc = jnp.dot(pd.astype(BF16), v_ref[pre:pre + tq, :], preferred_element_type=F32)
                if pre:
                    pp = jnp.exp(sp - m)
                    l = l + jnp.sum(pp, axis=-1, keepdims=True)
                    acc = acc + jnp.dot(pp.astype(BF16), v_ref[0:pre, :], preferred_element_type=F32)
                o = acc / l
                att_ref[...] = o
                g = g_ref[...]
                mix_ref[...] = (o * (g * _sigmoid(g))).astype(BF16)
                lse_ref[...] = m + jnp.log(l)

    row = lambda hh, b, i: (b * nq + i, hh)
    seq = lambda hh, b, i: (hh, b, 0)
    return _pcall(
        body, name=name, grid=(h, nb, nq),
        in_specs=[pl.BlockSpec((None, tq, HEAD_PAD), lambda hh, b, i: (hh, b * nq + i, 0)),
                  pl.BlockSpec((None, s, HEAD_PAD), seq),
                  pl.BlockSpec((None, s, VDIM), seq),
                  pl.BlockSpec((tq, VDIM), lambda hh, b, i: (b * nq + i, gblk + hh))],
        out_specs=[pl.BlockSpec((tq, VDIM), row), pl.BlockSpec((tq, VDIM), row),
                   pl.BlockSpec((None, tq, 1), lambda hh, b, i: (hh, b * nq + i, 0))],
        out_shape=[jax.ShapeDtypeStruct((t, att_w), F32), jax.ShapeDtypeStruct((t, 2 * att_w), BF16),
                   jax.ShapeDtypeStruct((h, t, 1), F32)],
        compiler_params=_params(("parallel", "parallel", "parallel")),
    )(qh, kh, vh, z)


def _gate_bwd(dmix, att, z, dz, lay, *, name):
    t, att_w = att.shape
    h = att_w // VDIM
    tt = _tile(t, TOK_TILE, 16)
    gblk = lay["GATT"] // VDIM

    def body(dm_ref, o_ref, g_ref, dz_in, do_ref, delta_ref, dg_ref):
        del dz_in
        dm, o, g = dm_ref[...], o_ref[...], g_ref[...]
        sg = _sigmoid(g)
        do = dm * (g * sg)
        do_ref[...] = do.astype(BF16)
        delta_ref[...] = jnp.sum(do * o, axis=-1, keepdims=True)
        dg_ref[...] = (dm * o * _dsilu(g, sg)).astype(BF16)

    blk = pl.BlockSpec((tt, VDIM), lambda i, j: (i, j))
    return _pcall(
        body, name=name, grid=(t // tt, h),
        in_specs=[blk, blk, pl.BlockSpec((tt, VDIM), lambda i, j: (i, gblk + j)), pl.BlockSpec(memory_space=pl.ANY)],
        out_specs=[blk, pl.BlockSpec((None, tt, 1), lambda i, j: (j, i, 0)),
                   pl.BlockSpec((tt, VDIM), lambda i, j: (i, gblk + j))],
        out_shape=[jax.ShapeDtypeStruct((t, att_w), BF16), jax.ShapeDtypeStruct((h, t, 1), F32),
                   jax.ShapeDtypeStruct(dz.shape, BF16)],
        input_output_aliases={3: 2},
        compiler_params=_params(("parallel", "parallel")),
    )(dmix, att, z, dz)


def _flash_bwd(qh, kh, vh, do, lse, delta, nb, *, name):
    h, t, _ = qh.shape
    s = t // nb
    tk = _tile(s, ATT_TQ, 128)
    nk = s // tk
    tn_dims = (((0,), (0,)), ((), ()))

    def body(q_ref, k_ref, v_ref, do_ref, lse_ref, dl_ref, dq_ref, dk_ref, dv_ref):
        j = pl.program_id(2)
        tri = _lower_triangle(tk)

        @pl.when(j == 0)
        def _():
            dq_ref[...] = jnp.zeros_like(dq_ref)

        def rows_against_block(r0, r1, masked):
            q, do_v = q_ref[r0:r1, :], do_ref[r0:r1, :]
            k = k_ref[...]
            sc = _qk(q, k)
            if masked:
                sc = jnp.where(tri, sc, NEG)
            p = jnp.exp(sc - lse_ref[r0:r1, :])
            dv = lax.dot_general(p.astype(BF16), do_v, tn_dims, preferred_element_type=F32)
            ds = (p * (_qk(do_v, v_ref[...]) - dl_ref[r0:r1, :])).astype(BF16)
            dq_ref[r0:r1, :] += jnp.dot(ds, k, preferred_element_type=F32)
            return lax.dot_general(ds, q, tn_dims, preferred_element_type=F32), dv

        for blk in range(nk):
            @pl.when(j == blk)
            def _():
                r0 = blk * tk
                dk, dv = rows_against_block(r0, r0 + tk, True)
                if r0 + tk < s:
                    dk2, dv2 = rows_against_block(r0 + tk, s, False)
                    dk, dv = dk + dk2, dv + dv2
                dk_ref[...] = dk
                dv_ref[...] = dv

    seq = lambda hh, b, j: (hh, b, 0)
    kv = lambda hh, b, j: (hh, b * nk + j, 0)
    return _pcall(
        body, name=name, grid=(h, nb, nk),
        in_specs=[pl.BlockSpec((None, s, HEAD_PAD), seq),
                  pl.BlockSpec((None, tk, HEAD_PAD), kv),
                  pl.BlockSpec((None, tk, VDIM), kv),
                  pl.BlockSpec((s, VDIM), lambda hh, b, j: (b, hh)),
                  pl.BlockSpec((None, s, 1), seq), pl.BlockSpec((None, s, 1), seq)],
        out_specs=[pl.BlockSpec((None, s, HEAD_PAD), seq), pl.BlockSpec((None, tk, HEAD_PAD), kv),
                   pl.BlockSpec((None, tk, VDIM), kv)],
        out_shape=[jax.ShapeDtypeStruct((h, t, HEAD_PAD), F32), jax.ShapeDtypeStruct((h, t, HEAD_PAD), F32),
                   jax.ShapeDtypeStruct((h, t, VDIM), F32)],
        compiler_params=_params(("parallel", "parallel", "arbitrary")),
    )(qh, kh, vh, do, lse, delta)


def _conv_fwd(z, mix, w_dw, b_dw, ln_g, ln_b, lay, nb, *, name):
    t = z.shape[0]
    cw = lay["CW"]
    s = t // nb
    tt = _tile(s, CONV_TILE, HALO)
    ns = s // tt
    hb = tt // HALO
    lc = _tile(cw, LANE_CHUNK, 128)

    def body(a_ref, b_ref, ap_ref, bp_ref, gc_ref, w_ref, bias_ref, lg_ref, lb_ref, mix_in, mix_ref, u_ref, c_ref, ext):
        del mix_in
        i = pl.program_id(1)
        u = a_ref[...] * _sigmoid(b_ref[...])
        u_ref[...] = u
        ext[0:HALO, :] = jnp.where(i > 0, ap_ref[...] * _sigmoid(bp_ref[...]), 0.0)
        ext[HALO:HALO + tt, :] = u
        for c0 in range(0, cw, lc):
            acc = jnp.zeros((tt, lc), F32) + bias_ref[:, c0:c0 + lc]
            for k in range(CONV_K):
                off = HALO - (CONV_K - 1) + k
                acc = acc + w_ref[k:k + 1, c0:c0 + lc] * ext[off:off + tt, c0:c0 + lc]
            c_ref[:, c0:c0 + lc] = acc
        c = c_ref[...]
        mu = jnp.mean(c, axis=-1, keepdims=True)
        xc = c - mu
        var = jnp.mean(xc * xc, axis=-1, keepdims=True)
        y = xc * lax.rsqrt(var + EPS) * lg_ref[...] + lb_ref[...]
        g = gc_ref[...]
        mix_ref[...] = (y * _sigmoid(y) * (g * _sigmoid(g))).astype(BF16)

    cur = lambda col: pl.BlockSpec((tt, cw), lambda b, i: (b * ns + i, col))
    prev = lambda col: pl.BlockSpec((HALO, cw), lambda b, i: (jnp.maximum((b * ns + i) * hb - 1, 0), col))
    vec = pl.BlockSpec((1, cw), lambda b, i: (0, 0))
    out_row = pl.BlockSpec((tt, cw), lambda b, i: (b * ns + i, 0))
    return _pcall(
        body, name=name, grid=(nb, ns),
        in_specs=[cur(lay["A"] // cw), cur(lay["B"] // cw), prev(lay["A"] // cw), prev(lay["B"] // cw),
                  cur(lay["GCONV"] // cw), pl.BlockSpec((HALO, cw), lambda b, i: (0, 0)), vec, vec, vec,
                  pl.BlockSpec(memory_space=pl.ANY)],
        out_specs=[pl.BlockSpec((tt, cw), lambda b, i: (b * ns + i, 1)), out_row, out_row],
        out_shape=[jax.ShapeDtypeStruct(mix.shape, BF16), jax.ShapeDtypeStruct((t, cw), F32),
                   jax.ShapeDtypeStruct((t, cw), F32)],
        scratch_shapes=[pltpu.VMEM((tt + HALO, cw), F32)],
        input_output_aliases={9: 0},
        compiler_params=_params(("parallel", "parallel")),
    )(z, z, z, z, z, w_dw, b_dw.reshape(1, cw), ln_g.reshape(1, cw), ln_b.reshape(1, cw), mix)


def _conv_bwd_ln(c_pre, z, dmix, ln_g, ln_b, lay, *, name):
    t, cw = c_pre.shape
    tt = _tile(t, TOK_TILE, 16)

    def body(c_ref, gc_ref, dm_ref, lg_ref, lb_ref, dc_ref, dgc_ref, dlg_ref, dlb_ref, dbias_ref):
        @pl.when(pl.program_id(0) == 0)
        def _():
            dlg_ref[...] = jnp.zeros_like(dlg_ref)
            dlb_ref[...] = jnp.zeros_like(dlb_ref)
            dbias_ref[...] = jnp.zeros_like(dbias_ref)

        c = c_ref[...]
        mu = jnp.mean(c, axis=-1, keepdims=True)
        xc = c - mu
        rstd = lax.rsqrt(jnp.mean(xc * xc, axis=-1, keepdims=True) + EPS)
        xhat = xc * rstd
        y = xhat * lg_ref[...] + lb_ref[...]
        sy = _sigmoid(y)
        g = gc_ref[...]
        sg = _sigmoid(g)
        dm = dm_ref[...].astype(F32)
        dgc_ref[...] = (dm * (y * sy) * _dsilu(g, sg)).astype(BF16)
        dy = dm * (g * sg) * _dsilu(y, sy)
        dlb_ref[...] += jnp.sum(dy, axis=0, keepdims=True)
        dlg_ref[...] += jnp.sum(dy * xhat, axis=0, keepdims=True)
        dxh = dy * lg_ref[...]
        dc = rstd * (dxh - jnp.mean(dxh, axis=-1, keepdims=True)
                     - xhat * jnp.mean(dxh * xhat, axis=-1, keepdims=True))
        dc_ref[...] = dc
        dbias_ref[...] += jnp.sum(dc, axis=0, keepdims=True)

    row = pl.BlockSpec((tt, cw), lambda i: (i, 0))
    vec = pl.BlockSpec((1, cw), lambda i: (0, 0))
    return _pcall(
        body, name=name, grid=(t // tt,),
        in_specs=[row, pl.BlockSpec((tt, cw), lambda i: (i, lay["GCONV"] // cw)),
                  pl.BlockSpec((tt, cw), lambda i: (i, 1)), vec, vec],
        out_specs=[row, pl.BlockSpec((tt, cw), lambda i: (i, lay["GCONV"] // cw)), vec, vec, vec],
        out_shape=[jax.ShapeDtypeStruct((t, cw), F32), jax.ShapeDtypeStruct((t, lay["NP"]), BF16),
                   jax.ShapeDtypeStruct((1, cw), F32), jax.ShapeDtypeStruct((1, cw), F32),
                   jax.ShapeDtypeStruct((1, cw), F32)],
        compiler_params=_params(("arbitrary",)),
    )(c_pre, z, dmix, ln_g.reshape(1, cw), ln_b.reshape(1, cw))


def _conv_bwd_dw(dc, u, z, w_dw, dz, lay, nb, *, name):
    t, cw = dc.shape
    s = t // nb
    tt = _tile(s, CONV_TILE, HALO)
    ns = s // tt
    hb = tt // HALO
    lc = _tile(cw, LANE_CHUNK, 128)

    def body(dc_ref, dcn_ref, u_ref, up_ref, a_ref, b_ref, w_ref, dz_in, dab_ref, dw_ref, ext_dc, ext_u, du_ref):
        del dz_in
        b_i, i = pl.program_id(0), pl.program_id(1)

        @pl.when((b_i == 0) & (i == 0))
        def _():
            dw_ref[...] = jnp.zeros_like(dw_ref)

        dc_v = dc_ref[...]
        ext_dc[0:tt, :] = dc_v
        ext_dc[tt:tt + HALO, :] = jnp.where(i < ns - 1, dcn_ref[...], 0.0)
        ext_u[0:HALO, :] = jnp.where(i > 0, up_ref[...], 0.0)
        ext_u[HALO:HALO + tt, :] = u_ref[...]
        for c0 in range(0, cw, lc):
            acc = jnp.zeros((tt, lc), F32)
            dcc = dc_v[:, c0:c0 + lc]
            for k in range(CONV_K):
                acc = acc + w_ref[k:k + 1, c0:c0 + lc] * ext_dc[CONV_K - 1 - k:CONV_K - 1 - k + tt, c0:c0 + lc]
                off = HALO - (CONV_K - 1) + k
                dw_ref[k:k + 1, c0:c0 + lc] += jnp.sum(dcc * ext_u[off:off + tt, c0:c0 + lc], axis=0, keepdims=True)
            du_ref[:, c0:c0 + lc] = acc
        du = du_ref[...]
        sb = _sigmoid(b_ref[...])
        dab_ref[:, 0:cw] = (du * sb).astype(BF16)
        dab_ref[:, cw:2 * cw] = (du * a_ref[...] * sb * (1.0 - sb)).astype(BF16)

    last = nb * ns * hb - 1
    row = pl.BlockSpec((tt, cw), lambda b, i: (b * ns + i, 0))
    return _pcall(
        body, name=name, grid=(nb, ns),
        in_specs=[row, pl.BlockSpec((HALO, cw), lambda b, i: (jnp.minimum((b * ns + i + 1) * hb, last), 0)),
                  row, pl.BlockSpec((HALO, cw), lambda b, i: (jnp.maximum((b * ns + i) * hb - 1, 0), 0)),
                  pl.BlockSpec((tt, cw), lambda b, i: (b * ns + i, lay["A"] // cw)),
                  pl.BlockSpec((tt, cw), lambda b, i: (b * ns + i, lay["B"] // cw)),
                  pl.BlockSpec((HALO, cw), lambda b, i: (0, 0)), pl.BlockSpec(memory_space=pl.ANY)],
        out_specs=[pl.BlockSpec((tt, 2 * cw), lambda b, i: (b * ns + i, 0)),
                   pl.BlockSpec((HALO, cw), lambda b, i: (0, 0))],
        out_shape=[jax.ShapeDtypeStruct(dz.shape, BF16), jax.ShapeDtypeStruct((HALO, cw), F32)],
        scratch_shapes=[pltpu.VMEM((tt + HALO, cw), F32), pltpu.VMEM((tt + HALO, cw), F32),
                        pltpu.VMEM((tt, cw), F32)],
        input_output_aliases={7: 0},
        compiler_params=_params(("arbitrary", "arbitrary")),
    )(dc, dc, u, u, z, z, w_dw, dz)


def _loss_head(y, target, *, name):
    t, d = y.shape
    tt = _tile(t, TOK_TILE, 16)

    def body(y_ref, t_ref, sse_ref, dy_ref, dyb_ref):
        @pl.when(pl.program_id(0) == 0)
        def _():
            sse_ref[...] = jnp.zeros_like(sse_ref)

        e = y_ref[...] - t_ref[...]
        sse_ref[...] += jnp.sum(e * e)
        dy = e * (1.0 / d)
        dy_ref[...] = dy
        dyb_ref[...] = dy.astype(BF16)

    row = pl.BlockSpec((tt, d), lambda i: (i, 0))
    return _pcall(
        body, name=name, grid=(t // tt,),
        in_specs=[row, row],
        out_specs=[pl.BlockSpec((8, 128), lambda i: (0, 0)), row, row],
        out_shape=[jax.ShapeDtypeStruct((8, 128), F32), jax.ShapeDtypeStruct((t, d), F32),
                   jax.ShapeDtypeStruct((t, d), BF16)],
        compiler_params=_params(("arbitrary",)),
    )(y, target)


def _adam(w, m, v, g_parts, *, name, layer=0, layers=1, prev=None):
    rows, cols = w.shape
    slab = rows // layers
    tr = _tile(slab, max(16, ADAM_BLOCK_ELEMS // cols), 16)
    blk0 = layer * (slab // tr)
    n = len(g_parts)
    n_prev = 0 if prev is None else 4

    def body(*refs):
        w_ref, m_ref, v_ref = refs[:3]
        g_refs = refs[3:3 + n]
        g_out, d_out, m_out, v_out = refs[3 + n + n_prev:]
        g = g_refs[0][...].astype(F32)
        for r in g_refs[1:]:
            g = g + r[...].astype(F32)
        m_new = ADAM_B1 * m_ref[...] + (1.0 - ADAM_B1) * g
        v_new = ADAM_B2 * v_ref[...] + (1.0 - ADAM_B2) * (g * g)
        m_hat = m_new / (1.0 - ADAM_B1 ** ADAM_STEP)
        v_hat = v_new / (1.0 - ADAM_B2 ** ADAM_STEP)
        g_out[...] = g
        d_out[...] = -ADAM_LR * (m_hat / (jnp.sqrt(v_hat) + ADAM_EPS) + ADAM_WD * w_ref[...])
        m_out[...] = m_new
        v_out[...] = v_new

    blk = pl.BlockSpec((tr, cols), lambda i: (blk0 + i, 0))
    g_specs, g_args = [], []
    for arr, lead in g_parts:
        g_args.append(arr)
        if lead is None:
            g_specs.append(pl.BlockSpec((tr, cols), lambda i: (i, 0)))
        else:
            g_specs.append(pl.BlockSpec((None, tr, cols), functools.partial(lambda i, p: (p, i, 0), p=lead)))
    out = jax.ShapeDtypeStruct((rows, cols), F32)
    return _pcall(
        body, name=name, grid=(slab // tr,),
        in_specs=[blk, blk, blk] + g_specs + [pl.BlockSpec(memory_space=pl.ANY)] * n_prev,
        out_specs=[blk, blk, blk, blk],
        out_shape=[out, out, out, out],
        input_output_aliases={3 + n + k: k for k in range(n_prev)},
        compiler_params=_params(("parallel",)),
    )(w, m, v, *g_args, *(prev or ()))


def _position():
    return lax.axis_index("x"), lax.axis_index("y"), lax.axis_index("c")


def _block_id(p):
    return 4 * p[0] + 2 * p[1] + p[2]


def _flip(p, mask):
    return tuple((1 - v) if (mask >> (2 - a)) & 1 else v for a, v in enumerate(p))


def _all_gather(xs, *, name):
    n = len(xs)

    def body(*refs):
        x_refs, o_refs = refs[:n], refs[n:2 * n]
        send_sems, recv_sems, local_sems = refs[2 * n:]
        x, y, c = _position()
        me, sibling = (x, y, c), (x, y, 1 - c)
        chips = [(1 - x, y), (x, 1 - y), (1 - x, 1 - y)]

        def copy(t, k, block, to, src=None):
            dst = o_refs[t].at[_block_id(block)]
            return pltpu.make_async_remote_copy(
                src_ref=dst if src is None else src, dst_ref=dst,
                send_sem=send_sems.at[t, k], recv_sem=recv_sems.at[t, k],
                device_id=to, device_id_type=MESH)

        mine = [pltpu.make_async_copy(x_refs[t], o_refs[t].at[_block_id(me)], local_sems.at[t]) for t in range(n)]
        for cp in mine:
            cp.start()
        started = []
        for t in range(n):
            first = [copy(t, 0, me, sibling, src=x_refs[t])]
            first += [copy(t, 1 + j, me, (*chip, c), src=x_refs[t]) for j, chip in enumerate(chips)]
            for cp in first:
                cp.start()
            started += first
        for j, chip in enumerate(chips):
            for t in range(n):
                copy(t, 1 + j, (*chip, c), me).wait_recv()
                fwd = copy(t, 4 + j, (*chip, c), sibling)
                fwd.start()
                started.append(fwd)
        for t in range(n):
            copy(t, 0, sibling, me).wait_recv()
            for j, chip in enumerate(chips):
                copy(t, 4 + j, (*chip, 1 - c), me).wait_recv()
        for cp in started:
            cp.wait_send()
        for cp in mine:
            cp.wait()

    any_spec = pl.BlockSpec(memory_space=pl.ANY)
    return _pcall(
        body, name=name,
        in_specs=[any_spec] * n, out_specs=[any_spec] * n,
        out_shape=[jax.ShapeDtypeStruct((N_DEV,) + a.shape, a.dtype) for a in xs],
        scratch_shapes=[pltpu.SemaphoreType.DMA((n, 7)), pltpu.SemaphoreType.DMA((n, 7)),
                        pltpu.SemaphoreType.DMA((n,))],
    )(*xs)


def _pushed_copy(x_ref, land_ref, send_sems, recv_sems, t, mask, me, chunked, at_receiver):
    peer = _flip(me, mask)
    src = x_ref.at[_block_id(peer)] if chunked else x_ref
    slot = _block_id(peer) if at_receiver else _block_id(me)
    k = (N_DEV - 1) * t + mask - 1
    return pltpu.make_async_remote_copy(
        src_ref=src, dst_ref=land_ref.at[slot], send_sem=send_sems.at[k], recv_sem=recv_sems.at[k],
        device_id=peer, device_id_type=MESH)


def _push_start(xs, chunked, *, name, after=None):
    n = len(xs)
    lands = [lax.empty(a.shape if chunked else (N_DEV,) + a.shape, a.dtype) for a in xs]

    n_after = 0 if after is None else 1

    def body(*refs):
        x_refs, land_refs = refs[:n], refs[n:2 * n]
        send_sems, recv_sems = refs[2 * n + n_after], refs[2 * n + n_after + 1]
        token = refs[4 * n + n_after + 2]
        me = _position()
        for t in range(n):
            for mask in range(1, N_DEV):
                _pushed_copy(x_refs[t], land_refs[t], send_sems, recv_sems, t, mask, me, chunked, False).start()
        token[...] = jnp.zeros_like(token)

    hbm = pl.BlockSpec(memory_space=pltpu.HBM)
    sem = pl.BlockSpec(memory_space=pltpu.SEMAPHORE)
    outs = _pcall(
        body, name=name,
        in_specs=[hbm] * (2 * n) + [pl.BlockSpec(memory_space=pl.ANY)] * n_after,
        out_specs=[sem, sem] + [hbm] * (2 * n) + [pl.BlockSpec(memory_space=pltpu.VMEM)],
        out_shape=[pltpu.SemaphoreType.DMA(((N_DEV - 1) * n,)), pltpu.SemaphoreType.DMA(((N_DEV - 1) * n,))]
        + [pltpu.HBM(a.shape, a.dtype) for a in xs] + [pltpu.HBM(a.shape, a.dtype) for a in lands]
        + [jax.ShapeDtypeStruct((8, 128), F32)],
        input_output_aliases={i: 2 + i for i in range(2 * n)},
        compiler_params=pltpu.CompilerParams(has_side_effects=pltpu.SideEffectType.DATAFLOW_SIDE_EFFECTING),
    )(*[pltpu.with_memory_space_constraint(a, pltpu.HBM) for a in list(xs) + lands], *([after] * n_after))
    return outs[0], outs[1], outs[2:2 + n], outs[2 + n:2 + 2 * n], outs[2 + 2 * n]


def _push_wait(handle, after, chunked, *, name):
    send_sems, recv_sems, xs, lands, _ = handle
    n = len(xs)

    def body(*refs):
        x_refs, land_refs = refs[:n], refs[n:2 * n]
        send_sems, recv_sems = refs[2 * n], refs[2 * n + 1]
        me = _position()
        for t in range(n):
            for mask in range(1, N_DEV):
                _pushed_copy(x_refs[t], land_refs[t], send_sems, recv_sems, t, mask, me, chunked, False).wait_send()
                _pushed_copy(x_refs[t], land_refs[t], send_sems, recv_sems, t, mask, me, chunked, True).wait_recv()

    hbm = pl.BlockSpec(memory_space=pltpu.HBM)
    sem = pl.BlockSpec(memory_space=pltpu.SEMAPHORE)
    outs = _pcall(
        body, name=name,
        in_specs=[hbm] * (2 * n) + [sem, sem, pl.BlockSpec(memory_space=pl.ANY)],
        out_specs=[hbm] * (2 * n),
        out_shape=[pltpu.HBM(a.shape, a.dtype) for a in list(xs) + list(lands)],
        input_output_aliases={i: i for i in range(2 * n)},
        compiler_params=pltpu.CompilerParams(has_side_effects=pltpu.SideEffectType.DATAFLOW_SIDE_EFFECTING),
    )(*xs, *lands, send_sems, recv_sems, after)
    return outs[:n], outs[n:]


def _all_reduce_small(pack, *, name):
    rows = pack.shape[0]

    def body(p_ref, o_ref, gath, send_sems, recv_sems):
        me = _position()
        my_id = _block_id(me)
        gath[my_id] = p_ref[...]
        sent = []
        for mask in range(1, N_DEV):
            peer = _flip(me, mask)
            cp = pltpu.make_async_remote_copy(
                src_ref=p_ref, dst_ref=gath.at[my_id], send_sem=send_sems.at[mask - 1],
                recv_sem=recv_sems.at[mask - 1], device_id=peer, device_id_type=MESH)
            cp.start()
            sent.append(cp)
        for mask in range(1, N_DEV):
            slot = gath.at[_block_id(_flip(me, mask))]
            pltpu.make_async_remote_copy(
                src_ref=slot, dst_ref=slot, send_sem=send_sems.at[mask - 1], recv_sem=recv_sems.at[mask - 1],
                device_id=me, device_id_type=MESH).wait_recv()
        for cp in sent:
            cp.wait_send()
        total = gath[0]
        for s in range(1, N_DEV):
            total = total + gath[s]
        o_ref[...] = total

    vm = pl.BlockSpec(memory_space=pltpu.VMEM)
    return _pcall(
        body, name=name,
        in_specs=[vm], out_specs=vm,
        out_shape=jax.ShapeDtypeStruct(pack.shape, F32),
        scratch_shapes=[pltpu.VMEM((N_DEV, rows, 128), F32), pltpu.SemaphoreType.DMA((7,)),
                        pltpu.SemaphoreType.DMA((7,))],
        compiler_params=pltpu.CompilerParams(vmem_limit_bytes=VMEM_LIMIT),
    )(pack)


def _layout(d, ql, kvl):
    cw = d // 2
    att = d // 2
    lay = {"D": d, "CW": cw, "ATT": att, "H": att // VDIM, "QL": ql, "KVL": kvl}
    lay["A"], lay["B"], lay["GATT"], lay["GCONV"] = 0, cw, 2 * cw, 2 * cw + att
    lay["QC"] = lay["GCONV"] + cw
    lay["KVC"] = lay["QC"] + ql
    lay["KPE"] = lay["KVC"] + kvl
    used = lay["KPE"] + 128
    tn = min(MM_TN, 1024)
    lay["NP"] = -(-used // tn) * tn
    assert att == cw and lay["QC"] % ql == 0 and lay["KVC"] % kvl == 0 and lay["KPE"] % 128 == 0
    lay["o_kv"], lay["o_pe"] = ql, ql + kvl
    lay["o_ga"] = lay["o_pe"] + ROPE
    lay["o_u"] = lay["o_ga"] + att
    lay["o_gc"] = lay["o_u"] + 2 * cw
    lay["IN_COLS"] = lay["o_gc"] + cw
    return lay


def _ungather_cols(g):
    return jnp.transpose(g, (1, 0, 2)).reshape(g.shape[1], -1)


def _to_col_blocks(w):
    r, c = w.shape
    return jnp.transpose(w.reshape(r, N_DEV, c // N_DEV), (1, 0, 2))


def _pad_w_in(w, lay):
    d, cw, att, ql, kvl = lay["D"], lay["CW"], lay["ATT"], lay["QL"], lay["KVL"]
    parts = [w[:, lay["o_u"]:lay["o_u"] + 2 * cw], w[:, lay["o_ga"]:lay["o_ga"] + att],
             w[:, lay["o_gc"]:lay["o_gc"] + cw], w[:, :ql], w[:, lay["o_kv"]:lay["o_kv"] + kvl],
             w[:, lay["o_pe"]:lay["o_pe"] + ROPE],
             jnp.zeros((d, lay["NP"] - lay["KPE"] - ROPE), w.dtype)]
    return jnp.concatenate(parts, axis=1)


def _unpad_w_in(wp, lay):
    cw, att, ql, kvl = lay["CW"], lay["ATT"], lay["QL"], lay["KVL"]
    parts = [wp[:, lay["QC"]:lay["QC"] + ql], wp[:, lay["KVC"]:lay["KVC"] + kvl],
             wp[:, lay["KPE"]:lay["KPE"] + ROPE], wp[:, lay["GATT"]:lay["GATT"] + att],
             wp[:, :2 * cw], wp[:, lay["GCONV"]:lay["GCONV"] + cw]]
    return jnp.concatenate(parts, axis=1)


def _pad_heads(w, h):
    r = w.shape[0]
    return jnp.pad(w.reshape(r, h, QK_DIM), ((0, 0), (0, 0), (0, HEAD_PAD - QK_DIM))).reshape(r, h * HEAD_PAD)


def _unpad_heads(w, h):
    r = w.shape[0]
    return w.reshape(r, h, HEAD_PAD)[:, :, :QK_DIM].reshape(r, h * QK_DIM)


def _rope_tabs(positions):
    half = ROPE // 2
    inv_freq = ROPE_THETA ** (-jnp.arange(half, dtype=F32) / half)
    ang = positions.astype(F32).reshape(-1)[:, None] * inv_freq
    cos, sin = jnp.cos(ang), jnp.sin(ang)
    zero = jnp.zeros_like(cos)
    return (jnp.concatenate([cos, cos, zero, zero], axis=1),
            jnp.concatenate([-sin, zero, zero, zero], axis=1),
            jnp.concatenate([zero, sin, zero, zero], axis=1))


def _pack_rows(vecs):
    rows = []
    for v in vecs:
        flat = v.reshape(-1)
        pad = (-flat.shape[0]) % 1024
        rows.append(jnp.pad(flat, (0, pad)).reshape(-1, 128))
    return jnp.concatenate(rows, axis=0)


def _unpack_rows(pack, shapes):
    out, r0 = [], 0
    for shp in shapes:
        size = math.prod(shp)
        nrows = -(-size // 1024) * 8
        out.append(pack[r0:r0 + nrows].reshape(-1)[:size].reshape(shp))
        r0 += nrows
    return out


def kernel(x, positions, ln_g, w_in, q_a_norm, w_q_up, kv_a_norm, w_kv_up, q_norm, k_norm, w_dw, b_dw, conv_ln_g, conv_ln_b, w_out, loss_target, m_ln_g, m_w_in, m_q_a_norm, m_w_q_up, m_kv_a_norm, m_w_kv_up, m_q_norm, m_k_norm, m_w_dw, m_b_dw, m_conv_ln_g, m_conv_ln_b, m_w_out, v_ln_g, v_w_in, v_q_a_norm, v_w_q_up, v_kv_a_norm, v_w_kv_up, v_q_norm, v_k_norm, v_w_dw, v_b_dw, v_conv_ln_g, v_conv_ln_b, v_w_out):
    nb, seq, d = x.shape
    depth = ln_g.shape[0]
    lay = _layout(d, q_a_norm.shape[1], kv_a_norm.shape[1])
    h, cw, ql, kvl = lay["H"], lay["CW"], lay["QL"], lay["KVL"]
    t = nb * seq
    my_id = _block_id(_position())

    def shards(l):
        return [w_in[l].astype(BF16), w_q_up[l].astype(BF16), w_kv_up[l].astype(BF16), w_out[l].astype(BF16)]

    def fill_own(lands, own):
        return [lax.dynamic_update_index_in_dim(land, blk, my_id, 0) for land, blk in zip(lands, own)]

    def matmul_layouts(g_in, g_q, g_kv, g_out):
        wp = _pad_w_in(_ungather_cols(g_in), lay)
        wq = _pad_heads(_ungather_cols(g_q), h)
        wkv = _ungather_cols(g_kv)
        wo = g_out.reshape(2 * cw, d)
        return {"in": wp, "in_t": wp.T, "q": wq, "q_t": wq.T, "kv": wkv, "kv_t": wkv.T, "out": wo, "out_t": wo.T}

    *g0, g_dw = _all_gather(shards(0) + [w_dw], name="gather_layer_0")
    weights = [matmul_layouts(*g0)]
    gathers, after = {}, g0[0]
    for l in range(1, depth):
        gathers[l] = _push_start(shards(l), False, after=after, name=f"gather_start_{l}")
        after = gathers[l][4]
    fwd_dep = after if depth > 1 else None

    tabs = _rope_tabs(positions)
    gq_pad = jnp.pad(q_norm, ((0, 0), (0, HEAD_PAD - QK_DIM)))
    gk_pad = jnp.pad(k_norm, ((0, 0), (0, HEAD_PAD - QK_DIM)))
    w_dw_all = jnp.transpose(g_dw, (1, 2, 0, 3)).reshape(depth, CONV_K, cw)
    w_dw_all = jnp.pad(w_dw_all, ((0, 0), (0, HALO - CONV_K), (0, 0)))

    saved = []
    xs = x.reshape(t, d)
    for l in range(depth):
        wl = weights[l]
        hid = _rms_fwd(xs, ln_g[l], dep=fwd_dep if l == 0 else None, name=f"rms_fwd_{l}")
        z = _mm(hid, wl["in"], name=f"in_proj_{l}")
        qn, kvn = _lat_fwd(z, q_a_norm[l], kv_a_norm[l], lay, name=f"lat_fwd_{l}")
        q_raw = _mm(qn, wl["q"], name=f"q_up_{l}")
        kv_raw = _mm(kvn, wl["kv"], name=f"kv_up_{l}")
        qh, kh, vh = _heads_fwd(q_raw, kv_raw, z, tabs, gq_pad[l:l + 1], gk_pad[l:l + 1], lay, name=f"heads_fwd_{l}")
        att, mix, lse = _flash_fwd(qh, kh, vh, z, lay, nb, name=f"flash_fwd_{l}")
        mix, u, c_pre = _conv_fwd(z, mix, w_dw_all[l], b_dw[l], conv_ln_g[l], conv_ln_b[l], lay, nb,
                                  name=f"conv_fwd_{l}")
        x_next = _mm(mix, wl["out"], add=xs, name=f"out_proj_{l}")
        saved.append((xs, hid, z, qn, kvn, q_raw, kv_raw, qh, kh, vh, att, lse, mix, u, c_pre))
        xs = x_next
        if l + 1 < depth:
            own, lands = _push_wait(gathers[l + 1], xs, False, name=f"gather_wait_{l + 1}")
            weights.append(matmul_layouts(*fill_own(lands, own)))

    sse, dx, dxb = _loss_head(xs, loss_target.reshape(t, d), name="loss_head")
    loss = lax.psum(sse[0, 0] * (0.5 / d), ("x", "y", "c"))

    small = {k: [] for k in ("ln_g", "q_a", "kv_a", "q_n", "k_n", "w_dw", "b_dw", "cln_g", "cln_b")}
    scatters, bwd_dep = {}, None
    for l in reversed(range(depth)):
        xs, hid, z, qn, kvn, q_raw, kv_raw, qh, kh, vh, att, lse, mix, u, c_pre = saved[l]
        wl = weights[l]
        dmix = _mm(dxb, wl["out_t"], dep=bwd_dep, name=f"d_mix_{l}")
        dw_out = _mm(mix, dxb, trans_a=True, out_dtype=BF16, name=f"dw_out_{l}")
        dc, dz, dlg, dlb, dbias = _conv_bwd_ln(c_pre, z, dmix, conv_ln_g[l], conv_ln_b[l], lay,
                                               name=f"conv_bwd_ln_{l}")
        dz, dwdw = _conv_bwd_dw(dc, u, z, w_dw_all[l], dz, lay, nb, name=f"conv_bwd_dw_{l}")
        do, delta, dz = _gate_bwd(dmix, att, z, dz, lay, name=f"gate_bwd_{l}")
        dqh, dkh, dvh = _flash_bwd(qh, kh, vh, do, lse, delta, nb, name=f"flash_bwd_{l}")
        dq_raw, dkv_raw, dpe, dgq, dgk = _heads_bwd(q_raw, kv_raw, z, tabs, gq_pad[l:l + 1], gk_pad[l:l + 1],
                                                    dqh, dkh, dvh, lay, name=f"heads_bwd_{l}")
        dqn = _mm(dq_raw, wl["q_t"], name=f"d_qn_{l}")
        dkvn = _mm(dkv_raw, wl["kv_t"], name=f"d_kvn_{l}")
        dw_q = _mm(qn, dq_raw, trans_a=True, out_dtype=BF16, name=f"dw_q_{l}")
        dw_kv = _mm(kvn, dkv_raw, trans_a=True, out_dtype=BF16, name=f"dw_kv_{l}")
        dz, dgqa, dgkva = _lat_bwd(z, q_a_norm[l], kv_a_norm[l], dqn, dkvn, dpe, dz, lay, name=f"lat_bwd_{l}")
        dw_in = _mm(hid, dz, trans_a=True, out_dtype=BF16, name=f"dw_in_{l}")
        scatters[l] = _push_start(
            [_to_col_blocks(_unpad_w_in(dw_in, lay)), _to_col_blocks(_unpad_heads(dw_q, h)), _to_col_blocks(dw_kv),
             dw_out.reshape(N_DEV, (2 * cw) // N_DEV, d)], True, name=f"scatter_start_{l}")
        bwd_dep = scatters[l][4]
        dh = _mm(dz, wl["in_t"], dep=bwd_dep, name=f"d_hid_{l}")
        dx, dxb, dlng = _rms_bwd(xs, ln_g[l], dh, dx, name=f"rms_bwd_{l}")
        for key, val in (("ln_g", dlng), ("q_a", dgqa), ("kv_a", dgkva), ("q_n", dgq[:, :QK_DIM]),
                         ("k_n", dgk[:, :QK_DIM]), ("w_dw", dwdw[:CONV_K]), ("b_dw", dbias),
                         ("cln_g", dlg), ("cln_b", dlb)):
            small[key].append(val)
    grad_x = dx.reshape(nb, seq, d)
    for key in small:
        small[key] = jnp.stack(small[key][::-1])

    small_names = ("ln_g", "q_a", "kv_a", "q_n", "k_n", "b_dw", "cln_g", "cln_b", "w_dw")
    small_shapes = [small[k].shape for k in small_names]
    summed = _unpack_rows(_all_reduce_small(_pack_rows([small[k] for k in small_names]), name="reduce_small_grads"),
                          small_shapes)
    sg = dict(zip(small_names, summed))
    g_w_dw = lax.dynamic_slice_in_dim(sg["w_dw"], my_id * (cw // N_DEV), cw // N_DEV, axis=2)

    def adam_small(ws, ms, vs, gs, nm):
        shapes = [w.shape for w in ws]
        outs = _adam(_pack_rows(ws), _pack_rows(ms), _pack_rows(vs), [(_pack_rows(gs), None)], name=nm)
        return [_unpack_rows(o, shapes) for o in outs]

    big = [("w_in", w_in, m_w_in, v_w_in), ("w_q_up", w_q_up, m_w_q_up, v_w_q_up),
           ("w_kv_up", w_kv_up, m_w_kv_up, v_w_kv_up), ("w_out", w_out, m_w_out, v_w_out)]
    res, prev = {}, [None] * len(big)
    for l in reversed(range(depth)):
        own, lands = _push_wait(scatters[l], dx, True, name=f"scatter_wait_{l}")
        own = [lax.dynamic_index_in_dim(o, my_id, 0, keepdims=False) for o in own]
        for idx, ((nm, w, m, v), recv) in enumerate(zip(big, fill_own(lands, own))):
            rows, cols = w.shape[1], w.shape[2]
            flat = lambda a: a.reshape(depth * rows, cols)
            prev[idx] = _adam(flat(w), flat(m), flat(v), [(recv, s) for s in range(N_DEV)], layer=l, layers=depth,
                              prev=prev[idx], name=f"adam_{nm}_{l}")
    for idx, (nm, w, _, _) in enumerate(big):
        res[nm] = [o.reshape(w.shape) for o in prev[idx]]
    names_s = ["ln_g", "q_a_norm", "kv_a_norm", "q_norm", "k_norm", "w_dw", "b_dw", "conv_ln_g", "conv_ln_b"]
    ws = [ln_g, q_a_norm, kv_a_norm, q_norm, k_norm, w_dw, b_dw, conv_ln_g, conv_ln_b]
    ms = [m_ln_g, m_q_a_norm, m_kv_a_norm, m_q_norm, m_k_norm, m_w_dw, m_b_dw, m_conv_ln_g, m_conv_ln_b]
    vs = [v_ln_g, v_q_a_norm, v_kv_a_norm, v_q_norm, v_k_norm, v_w_dw, v_b_dw, v_conv_ln_g, v_conv_ln_b]
    gs = [sg["ln_g"].reshape(ln_g.shape), sg["q_a"].reshape(q_a_norm.shape), sg["kv_a"].reshape(kv_a_norm.shape),
          sg["q_n"].reshape(q_norm.shape), sg["k_n"].reshape(k_norm.shape), g_w_dw,
          sg["b_dw"].reshape(b_dw.shape), sg["cln_g"].reshape(conv_ln_g.shape), sg["cln_b"].reshape(conv_ln_b.shape)]
    outs_s = adam_small(ws, ms, vs, gs, "adam_small")
    for idx, nm in enumerate(names_s):
        res[nm] = [outs_s[k][idx] for k in range(4)]

    order = ["ln_g", "w_in", "q_a_norm", "w_q_up", "kv_a_norm", "w_kv_up", "q_norm", "k_norm", "w_dw", "b_dw",
             "conv_ln_g", "conv_ln_b", "w_out"]
    return (loss, grad_x, *[res[nm][0] for nm in order], *[res[nm][1] for nm in order],
            *[res[nm][2] for nm in order], *[res[nm][3] for nm in order])
```

```python
import functools
import math

import jax
import jax.numpy as jnp
from jax import lax
from jax.experimental import pallas as pl
from jax.experimental.pallas import tpu as pltpu

F32 = jnp.float32
BF16 = jnp.bfloat16
MESH = pl.DeviceIdType.MESH

N_DEV = 8
NOPE = 128
ROPE = 64
VDIM = 128
HEAD_PAD = 256
QK_DIM = NOPE + ROPE
CONV_K = 31
HALO = 32
EPS = 1e-6
ROPE_THETA = 10000.0
NEG = -1e30

ADAM_LR = 0.001
ADAM_B1 = 0.9
ADAM_B2 = 0.999
ADAM_EPS = 1e-08
ADAM_WD = 0.01
ADAM_STEP = 10

TOK_TILE = 256
CONV_TILE = 256
ATT_TQ = 512
HEAD_GROUP = 4
MM_TM = 1024
MM_TN = 512
MM_TK = 4096
MM_VMEM_BUDGET = 40 * 1024 * 1024
ADAM_BLOCK_ELEMS = 128 * 1024
LANE_CHUNK = 256
VMEM_LIMIT = 56 * 1024 * 1024


def _pcall(body, **kw):
    return pl.pallas_call(body, **kw)


def _tile(dim, pref, mult):
    t = min(pref, dim)
    t -= t % mult
    while t >= mult:
        if dim % t == 0:
            return t
        t -= mult
    return dim


def _params(sem):
    return pltpu.CompilerParams(dimension_semantics=sem, vmem_limit_bytes=VMEM_LIMIT)


def _sigmoid(v):
    return 1.0 / (1.0 + jnp.exp(-v))


def _dsilu(v, sg):
    return sg * (1.0 + v * (1.0 - sg))


def _mm_tiles(m, n, kdim, out_bytes, has_add):
    def need(tm, tn, tk):
        return (2 * 2 * (tm * tk + tk * tn) + 2 * tm * tn * out_bytes
                + tm * tn * 4 * ((kdim > tk) + 2 * has_add + 1))

    shapes = [(_tile(m, pm, 128), _tile(n, pn, 128))
              for pm, pn in ((MM_TM, MM_TN), (MM_TM // 2, MM_TN), (MM_TM // 2, MM_TN // 2), (MM_TM // 4, MM_TN // 2))]
    for tm, tn in shapes:
        if need(tm, tn, kdim) <= MM_VMEM_BUDGET:
            return tm, tn, kdim
    tk = _tile(kdim, MM_TK, 128)
    for tm, tn in shapes:
        if need(tm, tn, tk) <= MM_VMEM_BUDGET:
            break
    return tm, tn, tk


def _mm(a, b, *, name, trans_a=False, trans_b=False, add=None, out_dtype=F32, dep=None):
    assert not (trans_a and trans_b)
    if trans_a:
        kdim, m = a.shape
    else:
        m, kdim = a.shape
    n = b.shape[0] if trans_b else b.shape[1]
    assert b.shape[1 if trans_b else 0] == kdim
    has_add = add is not None
    tm, tn, tk = _mm_tiles(m, n, kdim, jnp.dtype(out_dtype).itemsize, has_add)
    nk = kdim // tk
    contract = (((0 if trans_a else 1,), (1 if trans_b else 0,)), ((), ()))

    def product(a_ref, b_ref):
        return lax.dot_general(a_ref[...], b_ref[...], contract, preferred_element_type=F32)

    def body(*refs):
        a_ref, b_ref = refs[:2]
        add_ref = refs[2] if has_add else None
        o_ref = refs[2 + has_add + (dep is not None)]

        def finish(r):
            if has_add:
                r = r + add_ref[...]
            o_ref[...] = r.astype(o_ref.dtype)

        if nk == 1:
            finish(product(a_ref, b_ref))
            return
        acc_ref = refs[-1]
        k = pl.program_id(2)

        @pl.when(k == 0)
        def _():
            acc_ref[...] = product(a_ref, b_ref)

        @pl.when((k > 0) & (k < nk - 1))
        def _():
            acc_ref[...] += product(a_ref, b_ref)

        @pl.when(k == nk - 1)
        def _():
            finish(acc_ref[...] + product(a_ref, b_ref))

    if trans_a:
        a_spec = pl.BlockSpec((tk, tm), lambda i, j, k: (k, i))
    else:
        a_spec = pl.BlockSpec((tm, tk), lambda i, j, k: (i, k))
    if trans_b:
        b_spec = pl.BlockSpec((tn, tk), lambda i, j, k: (j, k))
    else:
        b_spec = pl.BlockSpec((tk, tn), lambda i, j, k: (k, j))
    in_specs = [a_spec, b_spec]
    args = [a, b]
    if has_add:
        in_specs.append(pl.BlockSpec((tm, tn), lambda i, j, k: (i, j)))
        args.append(add)
    if dep is not None:
        in_specs.append(pl.BlockSpec(memory_space=pl.ANY))
        args.append(dep)
    return _pcall(
        body, name=name,
        grid=(m // tm, n // tn, nk),
        in_specs=in_specs,
        out_specs=pl.BlockSpec((tm, tn), lambda i, j, k: (i, j)),
        out_shape=jax.ShapeDtypeStruct((m, n), out_dtype),
        scratch_shapes=[pltpu.VMEM((tm, tn), F32)] if nk > 1 else [],
        compiler_params=_params(("parallel", "parallel", "arbitrary")),
    )(*args)


def _rms_fwd(x, g, *, name, dep=None):
    t, d = x.shape
    tt = _tile(t, TOK_TILE, 16)

    def body(x_ref, g_ref, *rest):
        h_ref = rest[-1]
        xv = x_ref[...]
        r = lax.rsqrt(jnp.mean(xv * xv, axis=-1, keepdims=True) + EPS)
        h_ref[...] = (xv * r * g_ref[...]).astype(BF16)

    deps = [] if dep is None else [dep]
    return _pcall(
        body, name=name, grid=(t // tt,),
        in_specs=[pl.BlockSpec((tt, d), lambda i: (i, 0)), pl.BlockSpec((1, d), lambda i: (0, 0))]
        + [pl.BlockSpec(memory_space=pl.ANY)] * len(deps),
        out_specs=pl.BlockSpec((tt, d), lambda i: (i, 0)),
        out_shape=jax.ShapeDtypeStruct((t, d), BF16),
        compiler_params=_params(("parallel",)),
    )(x, g.reshape(1, d), *deps)


def _rms_bwd(x, g, dh, dres, *, name):
    t, d = x.shape
    tt = _tile(t, TOK_TILE, 16)

    def body(x_ref, g_ref, dh_ref, dres_ref, dx_ref, dxb_ref, dg_ref):
        xv = x_ref[...]
        r = lax.rsqrt(jnp.mean(xv * xv, axis=-1, keepdims=True) + EPS)
        dy = dh_ref[...]
        dyg = dy * g_ref[...]
        dot = jnp.sum(dyg * xv, axis=-1, keepdims=True) * (1.0 / d)
        dx = dres_ref[...] + r * dyg - xv * (r * r * r) * dot
        dx_ref[...] = dx
        dxb_ref[...] = dx.astype(BF16)

        @pl.when(pl.program_id(0) == 0)
        def _():
            dg_ref[...] = jnp.zeros_like(dg_ref)

        dg_ref[...] += jnp.sum(dy * xv * r, axis=0, keepdims=True)

    row = pl.BlockSpec((tt, d), lambda i: (i, 0))
    vec = pl.BlockSpec((1, d), lambda i: (0, 0))
    return _pcall(
        body, name=name, grid=(t // tt,),
        in_specs=[row, vec, row, row],
        out_specs=[row, row, vec],
        out_shape=[jax.ShapeDtypeStruct((t, d), F32), jax.ShapeDtypeStruct((t, d), BF16),
                   jax.ShapeDtypeStruct((1, d), F32)],
        compiler_params=_params(("arbitrary",)),
    )(x, g.reshape(1, d), dh, dres)


def _lat_fwd(z, gq, gkv, lay, *, name):
    t = z.shape[0]
    ql, kvl = lay["QL"], lay["KVL"]
    tt = _tile(t, TOK_TILE, 16)

    def body(q_ref, kv_ref, gq_ref, gkv_ref, qn_ref, kvn_ref):
        for src, g_ref, dst in ((q_ref, gq_ref, qn_ref), (kv_ref, gkv_ref, kvn_ref)):
            v = src[...]
            r = lax.rsqrt(jnp.mean(v * v, axis=-1, keepdims=True) + EPS)
            dst[...] = (v * r * g_ref[...]).astype(BF16)

    return _pcall(
        body, name=name, grid=(t // tt,),
        in_specs=[pl.BlockSpec((tt, ql), lambda i: (i, lay["QC"] // ql)),
                  pl.BlockSpec((tt, kvl), lambda i: (i, lay["KVC"] // kvl)),
                  pl.BlockSpec((1, ql), lambda i: (0, 0)), pl.BlockSpec((1, kvl), lambda i: (0, 0))],
        out_specs=[pl.BlockSpec((tt, ql), lambda i: (i, 0)), pl.BlockSpec((tt, kvl), lambda i: (i, 0))],
        out_shape=[jax.ShapeDtypeStruct((t, ql), BF16), jax.ShapeDtypeStruct((t, kvl), BF16)],
        compiler_params=_params(("parallel",)),
    )(z, z, gq.reshape(1, ql), gkv.reshape(1, kvl))


def _lat_bwd(z, gq, gkv, dqn, dkvn, dpe, dz, lay, *, name):
    t = z.shape[0]
    ql, kvl = lay["QL"], lay["KVL"]
    tail = lay["NP"] - lay["QC"]
    assert lay["QC"] % tail == 0
    tt = _tile(t, TOK_TILE, 16)

    def body(q_ref, kv_ref, gq_ref, gkv_ref, dqn_ref, dkvn_ref, dpe_ref, dz_in, tail_ref, dgq_ref, dgkv_ref):
        del dz_in
        first = pl.program_id(0) == 0
        for src, g_ref, dy_ref, c0, dg_ref in ((q_ref, gq_ref, dqn_ref, 0, dgq_ref),
                                               (kv_ref, gkv_ref, dkvn_ref, ql, dgkv_ref)):
            v = src[...]
            n = v.shape[-1]
            r = lax.rsqrt(jnp.mean(v * v, axis=-1, keepdims=True) + EPS)
            dy = dy_ref[...]
            dyg = dy * g_ref[...]
            dot = jnp.sum(dyg * v, axis=-1, keepdims=True) * (1.0 / n)
            tail_ref[:, c0:c0 + n] = (r * dyg - v * (r * r * r) * dot).astype(BF16)

            @pl.when(first)
            def _():
                dg_ref[...] = jnp.zeros_like(dg_ref)

            dg_ref[...] += jnp.sum(dy * v * r, axis=0, keepdims=True)
        tail_ref[:, ql + kvl:ql + kvl + 128] = dpe_ref[...].astype(BF16)
        tail_ref[:, ql + kvl + 128:tail] = jnp.zeros((tt, tail - ql - kvl - 128), BF16)

    return _pcall(
        body, name=name, grid=(t // tt,),
        in_specs=[pl.BlockSpec((tt, ql), lambda i: (i, lay["QC"] // ql)),
                  pl.BlockSpec((tt, kvl), lambda i: (i, lay["KVC"] // kvl)),
                  pl.BlockSpec((1, ql), lambda i: (0, 0)), pl.BlockSpec((1, kvl), lambda i: (0, 0)),
                  pl.BlockSpec((tt, ql), lambda i: (i, 0)), pl.BlockSpec((tt, kvl), lambda i: (i, 0)),
                  pl.BlockSpec((tt, 128), lambda i: (i, 0)), pl.BlockSpec(memory_space=pl.ANY)],
        out_specs=[pl.BlockSpec((tt, tail), lambda i: (i, lay["QC"] // tail)),
                   pl.BlockSpec((1, ql), lambda i: (0, 0)), pl.BlockSpec((1, kvl), lambda i: (0, 0))],
        out_shape=[jax.ShapeDtypeStruct(dz.shape, BF16),
                   jax.ShapeDtypeStruct((1, ql), F32), jax.ShapeDtypeStruct((1, kvl), F32)],
        input_output_aliases={7: 0},
        compiler_params=_params(("arbitrary",)),
    )(z, z, gq.reshape(1, ql), gkv.reshape(1, kvl), dqn, dkvn, dpe, dz)


def _rope(r, c_tab, sa_tab, sb_tab):
    return r * c_tab + pltpu.roll(r, 96, 1) * sa_tab + pltpu.roll(r, 32, 1) * sb_tab


def _rope_t(d, c_tab, sa_tab, sb_tab):
    return d * c_tab + pltpu.roll(d * sa_tab, 32, 1) + pltpu.roll(d * sb_tab, 96, 1)


def _heads_fwd(q_raw, kv_raw, z, tabs, gq, gk, lay, *, name):
    t = z.shape[0]
    h = lay["H"]
    tt = _tile(t, TOK_TILE, 16)
    hg = _tile(h, HEAD_GROUP, 1)
    scale = 1.0 / math.sqrt(QK_DIM)

    def body(q_ref, kv_ref, pe_ref, c_ref, sa_ref, sb_ref, gq_ref, gk_ref, qh_ref, kh_ref, vh_ref):
        c_tab, sa_tab, sb_tab = c_ref[...], sa_ref[...], sb_ref[...]
        pe, gq_v, gk_v = pe_ref[...], gq_ref[...], gk_ref[...]
        ss_pe = jnp.sum(pe * pe, axis=-1, keepdims=True)
        for g in range(hg):
            q = q_ref[:, g * HEAD_PAD:(g + 1) * HEAD_PAD]
            r = lax.rsqrt(jnp.sum(q * q, axis=-1, keepdims=True) * (1.0 / QK_DIM) + EPS)
            qn = q * r * gq_v
            qh_ref[g] = (jnp.concatenate([qn[:, :NOPE], _rope(qn[:, NOPE:], c_tab, sa_tab, sb_tab)], axis=1)
                         * scale).astype(BF16)
            kv = kv_ref[:, g * HEAD_PAD:(g + 1) * HEAD_PAD]
            kn = kv[:, :NOPE]
            rk = lax.rsqrt((jnp.sum(kn * kn, axis=-1, keepdims=True) + ss_pe) * (1.0 / QK_DIM) + EPS)
            kh_ref[g] = jnp.concatenate(
                [kn * rk * gk_v[:, :NOPE], _rope(pe * rk * gk_v[:, NOPE:], c_tab, sa_tab, sb_tab)],
                axis=1).astype(BF16)
            vh_ref[g] = kv[:, NOPE:].astype(BF16)

    head = pl.BlockSpec((tt, hg * HEAD_PAD), lambda i, j: (i, j))
    tab = pl.BlockSpec((tt, 128), lambda i, j: (i, 0))
    gain = pl.BlockSpec((1, HEAD_PAD), lambda i, j: (0, 0))
    return _pcall(
        body, name=name, grid=(t // tt, h // hg),
        in_specs=[head, head, pl.BlockSpec((tt, 128), lambda i, j: (i, lay["KPE"] // 128)), tab, tab, tab, gain, gain],
        out_specs=[pl.BlockSpec((hg, tt, HEAD_PAD), lambda i, j: (j, i, 0)),
                   pl.BlockSpec((hg, tt, HEAD_PAD), lambda i, j: (j, i, 0)),
                   pl.BlockSpec((hg, tt, VDIM), lambda i, j: (j, i, 0))],
        out_shape=[jax.ShapeDtypeStruct((h, t, HEAD_PAD), BF16), jax.ShapeDtypeStruct((h, t, HEAD_PAD), BF16),
                   jax.ShapeDtypeStruct((h, t, VDIM), BF16)],
        compiler_params=_params(("parallel", "parallel")),
    )(q_raw, kv_raw, z, *tabs, gq, gk)


def _heads_bwd(q_raw, kv_raw, z, tabs, gq, gk, dqh, dkh, dvh, lay, *, name):
    t = z.shape[0]
    h = lay["H"]
    tt = _tile(t, TOK_TILE, 16)
    hg = _tile(h, HEAD_GROUP, 1)
    scale = 1.0 / math.sqrt(QK_DIM)

    def body(q_ref, kv_ref, pe_ref, c_ref, sa_ref, sb_ref, gq_ref, gk_ref, dqh_ref, dkh_ref, dvh_ref,
             dq_ref, dkv_ref, dpe_ref, dgq_ref, dgk_ref):
        i, j = pl.program_id(0), pl.program_id(1)
        c_tab, sa_tab, sb_tab = c_ref[...], sa_ref[...], sb_ref[...]

        @pl.when((i == 0) & (j == 0))
        def _():
            dgq_ref[...] = jnp.zeros_like(dgq_ref)
            dgk_ref[...] = jnp.zeros_like(dgk_ref)

        @pl.when(j == 0)
        def _():
            dpe_ref[...] = jnp.zeros_like(dpe_ref)

        def norm_bwd(v, g, dy):
            r = lax.rsqrt(jnp.sum(v * v, axis=-1, keepdims=True) * (1.0 / QK_DIM) + EPS)
            dyg = dy * g
            dot = jnp.sum(dyg * v, axis=-1, keepdims=True) * (1.0 / QK_DIM)
            return r * dyg - v * (r * r * r) * dot, jnp.sum(dy * v * r, axis=0, keepdims=True)

        pe, gq_v, gk_v = pe_ref[...], gq_ref[...], gk_ref[...]
        dpe, dgq, dgk = jnp.zeros_like(pe), jnp.zeros_like(gq_v), jnp.zeros_like(gk_v)
        for g in range(hg):
            cols = slice(g * HEAD_PAD, (g + 1) * HEAD_PAD)
            dqo = dqh_ref[g] * scale
            dy = jnp.concatenate([dqo[:, :NOPE], _rope_t(dqo[:, NOPE:], c_tab, sa_tab, sb_tab)], axis=1)
            dq, dg = norm_bwd(q_ref[:, cols], gq_v, dy)
            dq_ref[:, cols] = dq.astype(BF16)
            dgq = dgq + dg

            dko = dkh_ref[g]
            dy = jnp.concatenate([dko[:, :NOPE], _rope_t(dko[:, NOPE:], c_tab, sa_tab, sb_tab)], axis=1)
            kfull = jnp.concatenate([kv_ref[:, cols][:, :NOPE], pe], axis=1)
            dk, dg = norm_bwd(kfull, gk_v, dy)
            dkv_ref[:, cols] = jnp.concatenate([dk[:, :NOPE], dvh_ref[g]], axis=1).astype(BF16)
            dpe = dpe + dk[:, NOPE:]
            dgk = dgk + dg
        dpe_ref[...] += dpe
        dgq_ref[...] += dgq
        dgk_ref[...] += dgk

    head = pl.BlockSpec((tt, hg * HEAD_PAD), lambda i, j: (i, j))
    tab = pl.BlockSpec((tt, 128), lambda i, j: (i, 0))
    gain = pl.BlockSpec((1, HEAD_PAD), lambda i, j: (0, 0))
    hm = pl.BlockSpec((hg, tt, HEAD_PAD), lambda i, j: (j, i, 0))
    return _pcall(
        body, name=name, grid=(t // tt, h // hg),
        in_specs=[head, head, pl.BlockSpec((tt, 128), lambda i, j: (i, lay["KPE"] // 128)), tab, tab, tab, gain, gain,
                  hm, hm, pl.BlockSpec((hg, tt, VDIM), lambda i, j: (j, i, 0))],
        out_specs=[head, head, tab, gain, gain],
        out_shape=[jax.ShapeDtypeStruct((t, h * HEAD_PAD), BF16), jax.ShapeDtypeStruct((t, h * HEAD_PAD), BF16),
                   jax.ShapeDtypeStruct((t, 128), F32),
                   jax.ShapeDtypeStruct((1, HEAD_PAD), F32), jax.ShapeDtypeStruct((1, HEAD_PAD), F32)],
        compiler_params=_params(("arbitrary", "arbitrary")),
    )(q_raw, kv_raw, z, *tabs, gq, gk, dqh, dkh, dvh)


def _lower_triangle(n):
    return lax.broadcasted_iota(jnp.int32, (n, n), 1) <= lax.broadcasted_iota(jnp.int32, (n, n), 0)


def _qk(q, k):
    return lax.dot_general(q, k, (((1,), (1,)), ((), ())), preferred_element_type=F32)


def _flash_fwd(qh, kh, vh, z, lay, nb, *, name):
    h, t, _ = qh.shape
    s = t // nb
    tq = _tile(s, ATT_TQ, 128)
    nq = s // tq
    att_w = h * VDIM
    gblk = lay["GATT"] // VDIM

    def body(q_ref, k_ref, v_ref, g_ref, att_ref, mix_ref, lse_ref):
        i = pl.program_id(2)
        tri = _lower_triangle(tq)
        for blk in range(nq):
            @pl.when(i == blk)
            def _():
                q = q_ref[...]
                pre = blk * tq
                sd = jnp.where(tri, _qk(q, k_ref[pre:pre + tq, :]), NEG)
                m = jnp.max(sd, axis=-1, keepdims=True)
                if pre:
                    sp = _qk(q, k_ref[0:pre, :])
                    m = jnp.maximum(m, jnp.max(sp, axis=-1, keepdims=True))
                pd = jnp.exp(sd - m)
                l = jnp.sum(pd, axis=-1, keepdims=True)
                acc = jnp.dot(pd.astype(BF16), v_ref[pre:pre + tq, :], preferred_element_type=F32)
                if pre:
                    pp = jnp.exp(sp - m)
                    l = l + jnp.sum(pp, axis=-1, keepdims=True)
                    acc = acc + jnp.dot(pp.astype(BF16), v_ref[0:pre, :], preferred_element_type=F32)
                o = acc / l
                att_ref[...] = o
                g = g_ref[...]
                mix_ref[...] = (o * (g * _sigmoid(g))).astype(BF16)
                lse_ref[...] = m + jnp.log(l)

    row = lambda hh, b, i: (b * nq + i, hh)
    seq = lambda hh, b, i: (hh, b, 0)
    return _pcall(
        body, name=name, grid=(h, nb, nq),
        in_specs=[pl.BlockSpec((None, tq, HEAD_PAD), lambda hh, b, i: (hh, b * nq + i, 0)),
                  pl.BlockSpec((None, s, HEAD_PAD), seq),
                  pl.BlockSpec((None, s, VDIM), seq),
                  pl.BlockSpec((tq, VDIM), lambda hh, b, i: (b * nq + i, gblk + hh))],
        out_specs=[pl.BlockSpec((tq, VDIM), row), pl.BlockSpec((tq, VDIM), row),
                   pl.BlockSpec((None, tq, 1), lambda hh, b, i: (hh, b * nq + i, 0))],
        out_shape=[jax.ShapeDtypeStruct((t, att_w), F32), jax.ShapeDtypeStruct((t, 2 * att_w), BF16),
                   jax.ShapeDtypeStruct((h, t, 1), F32)],
        compiler_params=_params(("parallel", "parallel", "parallel")),
    )(qh, kh, vh, z)


def _gate_bwd(dmix, att, z, dz, lay, *, name):
    t, att_w = att.shape
    h = att_w // VDIM
    tt = _tile(t, TOK_TILE, 16)
    gblk = lay["GATT"] // att_w

    def body(dm_ref, o_ref, g_ref, dz_in, do_ref, delta_ref, dg_ref):
        del dz_in
        dm, o, g = dm_ref[...], o_ref[...], g_ref[...]
        sg = _sigmoid(g)
        do = dm * (g * sg)
        do_ref[...] = do.astype(BF16)
        prod = do * o
        for hh in range(h):
            delta_ref[hh] = jnp.sum(prod[:, hh * VDIM:(hh + 1) * VDIM], axis=-1, keepdims=True)
        dg_ref[...] = (dm * o * _dsilu(g, sg)).astype(BF16)

    blk = pl.BlockSpec((tt, att_w), lambda i: (i, 0))
    gate = pl.BlockSpec((tt, att_w), lambda i: (i, gblk))
    return _pcall(
        body, name=name, grid=(t // tt,),
        in_specs=[blk, blk, gate, pl.BlockSpec(memory_space=pl.ANY)],
        out_specs=[blk, pl.BlockSpec((h, tt, 1), lambda i: (0, i, 0)), gate],
        out_shape=[jax.ShapeDtypeStruct((t, att_w), BF16), jax.ShapeDtypeStruct((h, t, 1), F32),
                   jax.ShapeDtypeStruct(dz.shape, BF16)],
        input_output_aliases={3: 2},
        compiler_params=_params(("parallel",)),
    )(dmix, att, z, dz)


def _flash_bwd(qh, kh, vh, do, lse, delta, nb, *, name):
    h, t, _ = qh.shape
    s = t // nb
    tk = _tile(s, ATT_TQ, 128)
    nk = s // tk
    tn_dims = (((0,), (0,)), ((), ()))

    def body(q_ref, k_ref, v_ref, do_ref, lse_ref, dl_ref, dq_ref, dk_ref, dv_ref):
        j = pl.program_id(2)
        tri = _lower_triangle(tk)

        @pl.when(j == 0)
        def _():
            dq_ref[...] = jnp.zeros_like(dq_ref)

        def rows_against_block(r0, r1, masked):
            q, do_v = q_ref[r0:r1, :], do_ref[r0:r1, :]
            k = k_ref[...]
            sc = _qk(q, k)
            if masked:
                sc = jnp.where(tri, sc, NEG)
            p = jnp.exp(sc - lse_ref[r0:r1, :])
            dv = lax.dot_general(p.astype(BF16), do_v, tn_dims, preferred_element_type=F32)
            ds = (p * (_qk(do_v, v_ref[...]) - dl_ref[r0:r1, :])).astype(BF16)
            dq_ref[r0:r1, :] += jnp.dot(ds, k, preferred_element_type=F32)
            return lax.dot_general(ds, q, tn_dims, preferred_element_type=F32), dv

        for blk in range(nk):
            @pl.when(j == blk)
            def _():
                r0 = blk * tk
                dk, dv = rows_against_block(r0, r0 + tk, True)
                if r0 + tk < s:
                    dk2, dv2 = rows_against_block(r0 + tk, s, False)
                    dk, dv = dk + dk2, dv + dv2
                dk_ref[...] = dk
                dv_ref[...] = dv

    seq = lambda hh, b, j: (hh, b, 0)
    kv = lambda hh, b, j: (hh, b * nk + j, 0)
    return _pcall(
        body, name=name, grid=(h, nb, nk),
        in_specs=[pl.BlockSpec((None, s, HEAD_PAD), seq),
                  pl.BlockSpec((None, tk, HEAD_PAD), kv),
                  pl.BlockSpec((None, tk, VDIM), kv),
                  pl.BlockSpec((s, VDIM), lambda hh, b, j: (b, hh)),
                  pl.BlockSpec((None, s, 1), seq), pl.BlockSpec((None, s, 1), seq)],
        out_specs=[pl.BlockSpec((None, s, HEAD_PAD), seq), pl.BlockSpec((None, tk, HEAD_PAD), kv),
                   pl.BlockSpec((None, tk, VDIM), kv)],
        out_shape=[jax.ShapeDtypeStruct((h, t, HEAD_PAD), F32), jax.ShapeDtypeStruct((h, t, HEAD_PAD), F32),
                   jax.ShapeDtypeStruct((h, t, VDIM), F32)],
        compiler_params=_params(("parallel", "parallel", "arbitrary")),
    )(qh, kh, vh, do, lse, delta)


def _conv_fwd(z, mix, w_dw, b_dw, ln_g, ln_b, lay, nb, *, name):
    t = z.shape[0]
    cw = lay["CW"]
    s = t // nb
    tt = _tile(s, CONV_TILE, HALO)
    ns = s // tt
    hb = tt // HALO
    lc = _tile(cw, LANE_CHUNK, 128)

    def body(a_ref, b_ref, ap_ref, bp_ref, gc_ref, w_ref, bias_ref, lg_ref, lb_ref, mix_in, mix_ref, u_ref, c_ref, ext):
        del mix_in
        i = pl.program_id(1)
        u = a_ref[...] * _sigmoid(b_ref[...])
        u_ref[...] = u
        ext[0:HALO, :] = jnp.where(i > 0, ap_ref[...] * _sigmoid(bp_ref[...]), 0.0)
        ext[HALO:HALO + tt, :] = u
        for c0 in range(0, cw, lc):
            acc = jnp.zeros((tt, lc), F32) + bias_ref[:, c0:c0 + lc]
            for k in range(CONV_K):
                off = HALO - (CONV_K - 1) + k
                acc = acc + w_ref[k:k + 1, c0:c0 + lc] * ext[off:off + tt, c0:c0 + lc]
            c_ref[:, c0:c0 + lc] = acc
        c = c_ref[...]
        mu = jnp.mean(c, axis=-1, keepdims=True)
        xc = c - mu
        var = jnp.mean(xc * xc, axis=-1, keepdims=True)
        y = xc * lax.rsqrt(var + EPS) * lg_ref[...] + lb_ref[...]
        g = gc_ref[...]
        mix_ref[...] = (y * _sigmoid(y) * (g * _sigmoid(g))).astype(BF16)

    cur = lambda col: pl.BlockSpec((tt, cw), lambda b, i: (b * ns + i, col))
    prev = lambda col: pl.BlockSpec((HALO, cw), lambda b, i: (jnp.maximum((b * ns + i) * hb - 1, 0), col))
    vec = pl.BlockSpec((1, cw), lambda b, i: (0, 0))
    out_row = pl.BlockSpec((tt, cw), lambda b, i: (b * ns + i, 0))
    return _pcall(
        body, name=name, grid=(nb, ns),
        in_specs=[cur(lay["A"] // cw), cur(lay["B"] // cw), prev(lay["A"] // cw), prev(lay["B"] // cw),
                  cur(lay["GCONV"] // cw), pl.BlockSpec((HALO, cw), lambda b, i: (0, 0)), vec, vec, vec,
                  pl.BlockSpec(memory_space=pl.ANY)],
        out_specs=[pl.BlockSpec((tt, cw), lambda b, i: (b * ns + i, 1)), out_row, out_row],
        out_shape=[jax.ShapeDtypeStruct(mix.shape, BF16), jax.ShapeDtypeStruct((t, cw), F32),
                   jax.ShapeDtypeStruct((t, cw), F32)],
        scratch_shapes=[pltpu.VMEM((tt + HALO, cw), F32)],
        input_output_aliases={9: 0},
        compiler_params=_params(("parallel", "parallel")),
    )(z, z, z, z, z, w_dw, b_dw.reshape(1, cw), ln_g.reshape(1, cw), ln_b.reshape(1, cw), mix)


def _conv_bwd_ln(c_pre, z, dmix, ln_g, ln_b, lay, *, name):
    t, cw = c_pre.shape
    tt = _tile(t, TOK_TILE, 16)

    def body(c_ref, gc_ref, dm_ref, lg_ref, lb_ref, dc_ref, dgc_ref, dlg_ref, dlb_ref, dbias_ref):
        @pl.when(pl.program_id(0) == 0)
        def _():
            dlg_ref[...] = jnp.zeros_like(dlg_ref)
            dlb_ref[...] = jnp.zeros_like(dlb_ref)
            dbias_ref[...] = jnp.zeros_like(dbias_ref)

        c = c_ref[...]
        mu = jnp.mean(c, axis=-1, keepdims=True)
        xc = c - mu
        rstd = lax.rsqrt(jnp.mean(xc * xc, axis=-1, keepdims=True) + EPS)
        xhat = xc * rstd
        y = xhat * lg_ref[...] + lb_ref[...]
        sy = _sigmoid(y)
        g = gc_ref[...]
        sg = _sigmoid(g)
        dm = dm_ref[...].astype(F32)
        dgc_ref[...] = (dm * (y * sy) * _dsilu(g, sg)).astype(BF16)
        dy = dm * (g * sg) * _dsilu(y, sy)
        dlb_ref[...] += jnp.sum(dy, axis=0, keepdims=True)
        dlg_ref[...] += jnp.sum(dy * xhat, axis=0, keepdims=True)
        dxh = dy * lg_ref[...]
        dc = rstd * (dxh - jnp.mean(dxh, axis=-1, keepdims=True)
                     - xhat * jnp.mean(dxh * xhat, axis=-1, keepdims=True))
        dc_ref[...] = dc
        dbias_ref[...] += jnp.sum(dc, axis=0, keepdims=True)

    row = pl.BlockSpec((tt, cw), lambda i: (i, 0))
    vec = pl.BlockSpec((1, cw), lambda i: (0, 0))
    return _pcall(
        body, name=name, grid=(t // tt,),
        in_specs=[row, pl.BlockSpec((tt, cw), lambda i: (i, lay["GCONV"] // cw)),
                  pl.BlockSpec((tt, cw), lambda i: (i, 1)), vec, vec],
        out_specs=[row, pl.BlockSpec((tt, cw), lambda i: (i, lay["GCONV"] // cw)), vec, vec, vec],
        out_shape=[jax.ShapeDtypeStruct((t, cw), F32), jax.ShapeDtypeStruct((t, lay["NP"]), BF16),
                   jax.ShapeDtypeStruct((1, cw), F32), jax.ShapeDtypeStruct((1, cw), F32),
                   jax.ShapeDtypeStruct((1, cw), F32)],
        compiler_params=_params(("arbitrary",)),
    )(c_pre, z, dmix, ln_g.reshape(1, cw), ln_b.reshape(1, cw))


def _conv_bwd_dw(dc, u, z, w_dw, dz, lay, nb, *, name):
    t, cw = dc.shape
    s = t // nb
    tt = _tile(s, CONV_TILE, HALO)
    ns = s // tt
    hb = tt // HALO
    lc = _tile(cw, LANE_CHUNK, 128)

    def body(dc_ref, dcn_ref, u_ref, up_ref, a_ref, b_ref, w_ref, dz_in, dab_ref, dw_ref, ext_dc, ext_u, du_ref):
        del dz_in
        b_i, i = pl.program_id(0), pl.program_id(1)

        @pl.when((b_i == 0) & (i == 0))
        def _():
            dw_ref[...] = jnp.zeros_like(dw_ref)

        dc_v = dc_ref[...]
        ext_dc[0:tt, :] = dc_v
        ext_dc[tt:tt + HALO, :] = jnp.where(i < ns - 1, dcn_ref[...], 0.0)
        ext_u[0:HALO, :] = jnp.where(i > 0, up_ref[...], 0.0)
        ext_u[HALO:HALO + tt, :] = u_ref[...]
        for c0 in range(0, cw, lc):
            acc = jnp.zeros((tt, lc), F32)
            dcc = dc_v[:, c0:c0 + lc]
            for k in range(CONV_K):
                acc = acc + w_ref[k:k + 1, c0:c0 + lc] * ext_dc[CONV_K - 1 - k:CONV_K - 1 - k + tt, c0:c0 + lc]
                off = HALO - (CONV_K - 1) + k
                dw_ref[k:k + 1, c0:c0 + lc] += jnp.sum(dcc * ext_u[off:off + tt, c0:c0 + lc], axis=0, keepdims=True)
            du_ref[:, c0:c0 + lc] = acc
        du = du_ref[...]
        sb = _sigmoid(b_ref[...])
        dab_ref[:, 0:cw] = (du * sb).astype(BF16)
        dab_ref[:, cw:2 * cw] = (du * a_ref[...] * sb * (1.0 - sb)).astype(BF16)

    last = nb * ns * hb - 1
    row = pl.BlockSpec((tt, cw), lambda b, i: (b * ns + i, 0))
    return _pcall(
        body, name=name, grid=(nb, ns),
        in_specs=[row, pl.BlockSpec((HALO, cw), lambda b, i: (jnp.minimum((b * ns + i + 1) * hb, last), 0)),
                  row, pl.BlockSpec((HALO, cw), lambda b, i: (jnp.maximum((b * ns + i) * hb - 1, 0), 0)),
                  pl.BlockSpec((tt, cw), lambda b, i: (b * ns + i, lay["A"] // cw)),
                  pl.BlockSpec((tt, cw), lambda b, i: (b * ns + i, lay["B"] // cw)),
                  pl.BlockSpec((HALO, cw), lambda b, i: (0, 0)), pl.BlockSpec(memory_space=pl.ANY)],
        out_specs=[pl.BlockSpec((tt, 2 * cw), lambda b, i: (b * ns + i, 0)),
                   pl.BlockSpec((HALO, cw), lambda b, i: (0, 0))],
        out_shape=[jax.ShapeDtypeStruct(dz.shape, BF16), jax.ShapeDtypeStruct((HALO, cw), F32)],
        scratch_shapes=[pltpu.VMEM((tt + HALO, cw), F32), pltpu.VMEM((tt + HALO, cw), F32),
                        pltpu.VMEM((tt, cw), F32)],
        input_output_aliases={7: 0},
        compiler_params=_params(("arbitrary", "arbitrary")),
    )(dc, dc, u, u, z, z, w_dw, dz)


def _loss_head(y, target, *, name):
    t, d = y.shape
    tt = _tile(t, TOK_TILE, 16)

    def body(y_ref, t_ref, sse_ref, dy_ref, dyb_ref):
        @pl.when(pl.program_id(0) == 0)
        def _():
            sse_ref[...] = jnp.zeros_like(sse_ref)

        e = y_ref[...] - t_ref[...]
        sse_ref[...] += jnp.sum(e * e)
        dy = e * (1.0 / d)
        dy_ref[...] = dy
        dyb_ref[...] = dy.astype(BF16)

    row = pl.BlockSpec((tt, d), lambda i: (i, 0))
    return _pcall(
        body, name=name, grid=(t // tt,),
        in_specs=[row, row],
        out_specs=[pl.BlockSpec((8, 128), lambda i: (0, 0)), row, row],
        out_shape=[jax.ShapeDtypeStruct((8, 128), F32), jax.ShapeDtypeStruct((t, d), F32),
                   jax.ShapeDtypeStruct((t, d), BF16)],
        compiler_params=_params(("arbitrary",)),
    )(y, target)


def _adam(w, m, v, g_parts, *, name, layer=0, layers=1, prev=None):
    rows, cols = w.shape
    slab = rows // layers
    tr = _tile(slab, max(16, ADAM_BLOCK_ELEMS // cols), 16)
    blk0 = layer * (slab // tr)
    n = len(g_parts)
    n_prev = 0 if prev is None else 4

    def body(*refs):
        w_ref, m_ref, v_ref = refs[:3]
        g_refs = refs[3:3 + n]
        g_out, d_out, m_out, v_out = refs[3 + n + n_prev:]
        g = g_refs[0][...].astype(F32)
        for r in g_refs[1:]:
            g = g + r[...].astype(F32)
        m_new = ADAM_B1 * m_ref[...] + (1.0 - ADAM_B1) * g
        v_new = ADAM_B2 * v_ref[...] + (1.0 - ADAM_B2) * (g * g)
        m_hat = m_new / (1.0 - ADAM_B1 ** ADAM_STEP)
        v_hat = v_new / (1.0 - ADAM_B2 ** ADAM_STEP)
        g_out[...] = g
        d_out[...] = -ADAM_LR * (m_hat / (jnp.sqrt(v_hat) + ADAM_EPS) + ADAM_WD * w_ref[...])
        m_out[...] = m_new
        v_out[...] = v_new

    blk = pl.BlockSpec((tr, cols), lambda i: (blk0 + i, 0))
    g_specs, g_args = [], []
    for arr, lead in g_parts:
        g_args.append(arr)
        if lead is None:
            g_specs.append(pl.BlockSpec((tr, cols), lambda i: (i, 0)))
        else:
            g_specs.append(pl.BlockSpec((None, tr, cols), functools.partial(lambda i, p: (p, i, 0), p=lead)))
    out = jax.ShapeDtypeStruct((rows, cols), F32)
    return _pcall(
        body, name=name, grid=(slab // tr,),
        in_specs=[blk, blk, blk] + g_specs + [pl.BlockSpec(memory_space=pl.ANY)] * n_prev,
        out_specs=[blk, blk, blk, blk],
        out_shape=[out, out, out, out],
        input_output_aliases={3 + n + k: k for k in range(n_prev)},
        compiler_params=_params(("parallel",)),
    )(w, m, v, *g_args, *(prev or ()))


def _position():
    return lax.axis_index("x"), lax.axis_index("y"), lax.axis_index("c")


def _block_id(p):
    return 4 * p[0] + 2 * p[1] + p[2]


def _flip(p, mask):
    return tuple((1 - v) if (mask >> (2 - a)) & 1 else v for a, v in enumerate(p))


def _all_gather(xs, *, name):
    n = len(xs)

    def body(*refs):
        x_refs, o_refs = refs[:n], refs[n:2 * n]
        send_sems, recv_sems, local_sems = refs[2 * n:]
        x, y, c = _position()
        me, sibling = (x, y, c), (x, y, 1 - c)
        chips = [(1 - x, y), (x, 1 - y), (1 - x, 1 - y)]

        def copy(t, k, block, to, src=None):
            dst = o_refs[t].at[_block_id(block)]
            return pltpu.make_async_remote_copy(
                src_ref=dst if src is None else src, dst_ref=dst,
                send_sem=send_sems.at[t, k], recv_sem=recv_sems.at[t, k],
                device_id=to, device_id_type=MESH)

        mine = [pltpu.make_async_copy(x_refs[t], o_refs[t].at[_block_id(me)], local_sems.at[t]) for t in range(n)]
        for cp in mine:
            cp.start()
        started = []
        for t in range(n):
            first = [copy(t, 0, me, sibling, src=x_refs[t])]
            first += [copy(t, 1 + j, me, (*chip, c), src=x_refs[t]) for j, chip in enumerate(chips)]
            for cp in first:
                cp.start()
            started += first
        for j, chip in enumerate(chips):
            for t in range(n):
                copy(t, 1 + j, (*chip, c), me).wait_recv()
                fwd = copy(t, 4 + j, (*chip, c), sibling)
                fwd.start()
                started.append(fwd)
        for t in range(n):
            copy(t, 0, sibling, me).wait_recv()
            for j, chip in enumerate(chips):
                copy(t, 4 + j, (*chip, 1 - c), me).wait_recv()
        for cp in started:
            cp.wait_send()
        for cp in mine:
            cp.wait()

    any_spec = pl.BlockSpec(memory_space=pl.ANY)
    return _pcall(
        body, name=name,
        in_specs=[any_spec] * n, out_specs=[any_spec] * n,
        out_shape=[jax.ShapeDtypeStruct((N_DEV,) + a.shape, a.dtype) for a in xs],
        scratch_shapes=[pltpu.SemaphoreType.DMA((n, 7)), pltpu.SemaphoreType.DMA((n, 7)),
                        pltpu.SemaphoreType.DMA((n,))],
    )(*xs)


def _pushed_copy(x_ref, land_ref, send_sems, recv_sems, t, mask, me, chunked, at_receiver):
    peer = _flip(me, mask)
    src = x_ref.at[_block_id(peer)] if chunked else x_ref
    slot = _block_id(peer) if at_receiver else _block_id(me)
    k = (N_DEV - 1) * t + mask - 1
    return pltpu.make_async_remote_copy(
        src_ref=src, dst_ref=land_ref.at[slot], send_sem=send_sems.at[k], recv_sem=recv_sems.at[k],
        device_id=peer, device_id_type=MESH)


def _push_start(xs, chunked, *, name, after=None):
    n = len(xs)
    lands = [lax.empty(a.shape if chunked else (N_DEV,) + a.shape, a.dtype) for a in xs]

    n_after = 0 if after is None else 1

    def body(*refs):
        x_refs, land_refs = refs[:n], refs[n:2 * n]
        send_sems, recv_sems = refs[2 * n + n_after], refs[2 * n + n_after + 1]
        token = refs[4 * n + n_after + 2]
        me = _position()
        for t in range(n):
            for mask in range(1, N_DEV):
                _pushed_copy(x_refs[t], land_refs[t], send_sems, recv_sems, t, mask, me, chunked, False).start()
        token[...] = jnp.zeros_like(token)

    hbm = pl.BlockSpec(memory_space=pltpu.HBM)
    sem = pl.BlockSpec(memory_space=pltpu.SEMAPHORE)
    outs = _pcall(
        body, name=name,
        in_specs=[hbm] * (2 * n) + [pl.BlockSpec(memory_space=pl.ANY)] * n_after,
        out_specs=[sem, sem] + [hbm] * (2 * n) + [pl.BlockSpec(memory_space=pltpu.VMEM)],
        out_shape=[pltpu.SemaphoreType.DMA(((N_DEV - 1) * n,)), pltpu.SemaphoreType.DMA(((N_DEV - 1) * n,))]
        + [pltpu.HBM(a.shape, a.dtype) for a in xs] + [pltpu.HBM(a.shape, a.dtype) for a in lands]
        + [jax.ShapeDtypeStruct((8, 128), F32)],
        input_output_aliases={i: 2 + i for i in range(2 * n)},
        compiler_params=pltpu.CompilerParams(has_side_effects=pltpu.SideEffectType.DATAFLOW_SIDE_EFFECTING),
    )(*[pltpu.with_memory_space_constraint(a, pltpu.HBM) for a in list(xs) + lands], *([after] * n_after))
    return outs[0], outs[1], outs[2:2 + n], outs[2 + n:2 + 2 * n], outs[2 + 2 * n]


def _push_wait(handle, after, chunked, *, name):
    send_sems, recv_sems, xs, lands, _ = handle
    n = len(xs)

    def body(*refs):
        x_refs, land_refs = refs[:n], refs[n:2 * n]
        send_sems, recv_sems = refs[2 * n], refs[2 * n + 1]
        me = _position()
        for t in range(n):
            for mask in range(1, N_DEV):
                _pushed_copy(x_refs[t], land_refs[t], send_sems, recv_sems, t, mask, me, chunked, False).wait_send()
                _pushed_copy(x_refs[t], land_refs[t], send_sems, recv_sems, t, mask, me, chunked, True).wait_recv()

    hbm = pl.BlockSpec(memory_space=pltpu.HBM)
    sem = pl.BlockSpec(memory_space=pltpu.SEMAPHORE)
    outs = _pcall(
        body, name=name,
        in_specs=[hbm] * (2 * n) + [sem, sem, pl.BlockSpec(memory_space=pl.ANY)],
        out_specs=[hbm] * (2 * n),
        out_shape=[pltpu.HBM(a.shape, a.dtype) for a in list(xs) + list(lands)],
        input_output_aliases={i: i for i in range(2 * n)},
        compiler_params=pltpu.CompilerParams(has_side_effects=pltpu.SideEffectType.DATAFLOW_SIDE_EFFECTING),
    )(*xs, *lands, send_sems, recv_sems, after)
    return outs[:n], outs[n:]


def _all_reduce_small(pack, *, name):
    rows = pack.shape[0]

    def body(p_ref, o_ref, gath, send_sems, recv_sems):
        me = _position()
        my_id = _block_id(me)
        gath[my_id] = p_ref[...]
        sent = []
        for mask in range(1, N_DEV):
            peer = _flip(me, mask)
            cp = pltpu.make_async_remote_copy(
                src_ref=p_ref, dst_ref=gath.at[my_id], send_sem=send_sems.at[mask - 1],
                recv_sem=recv_sems.at[mask - 1], device_id=peer, device_id_type=MESH)
            cp.start()
            sent.append(cp)
        for mask in range(1, N_DEV):
            slot = gath.at[_block_id(_flip(me, mask))]
            pltpu.make_async_remote_copy(
                src_ref=slot, dst_ref=slot, send_sem=send_sems.at[mask - 1], recv_sem=recv_sems.at[mask - 1],
                device_id=me, device_id_type=MESH).wait_recv()
        for cp in sent:
            cp.wait_send()
        total = gath[0]
        for s in range(1, N_DEV):
            total = total + gath[s]
        o_ref[...] = total

    vm = pl.BlockSpec(memory_space=pltpu.VMEM)
    return _pcall(
        body, name=name,
        in_specs=[vm], out_specs=vm,
        out_shape=jax.ShapeDtypeStruct(pack.shape, F32),
        scratch_shapes=[pltpu.VMEM((N_DEV, rows, 128), F32), pltpu.SemaphoreType.DMA((7,)),
                        pltpu.SemaphoreType.DMA((7,))],
        compiler_params=pltpu.CompilerParams(vmem_limit_bytes=VMEM_LIMIT),
    )(pack)


def _layout(d, ql, kvl):
    cw = d // 2
    att = d // 2
    lay = {"D": d, "CW": cw, "ATT": att, "H": att // VDIM, "QL": ql, "KVL": kvl}
    lay["A"], lay["B"], lay["GATT"], lay["GCONV"] = 0, cw, 2 * cw, 2 * cw + att
    lay["QC"] = lay["GCONV"] + cw
    lay["KVC"] = lay["QC"] + ql
    lay["KPE"] = lay["KVC"] + kvl
    used = lay["KPE"] + 128
    tn = min(MM_TN, 1024)
    lay["NP"] = -(-used // tn) * tn
    assert att == cw and lay["QC"] % ql == 0 and lay["KVC"] % kvl == 0 and lay["KPE"] % 128 == 0
    lay["o_kv"], lay["o_pe"] = ql, ql + kvl
    lay["o_ga"] = lay["o_pe"] + ROPE
    lay["o_u"] = lay["o_ga"] + att
    lay["o_gc"] = lay["o_u"] + 2 * cw
    lay["IN_COLS"] = lay["o_gc"] + cw
    return lay


def _ungather_cols(g):
    return jnp.transpose(g, (1, 0, 2)).reshape(g.shape[1], -1)


def _to_col_blocks(w):
    r, c = w.shape
    return jnp.transpose(w.reshape(r, N_DEV, c // N_DEV), (1, 0, 2))


def _pad_w_in(w, lay):
    d, cw, att, ql, kvl = lay["D"], lay["CW"], lay["ATT"], lay["QL"], lay["KVL"]
    parts = [w[:, lay["o_u"]:lay["o_u"] + 2 * cw], w[:, lay["o_ga"]:lay["o_ga"] + att],
             w[:, lay["o_gc"]:lay["o_gc"] + cw], w[:, :ql], w[:, lay["o_kv"]:lay["o_kv"] + kvl],
             w[:, lay["o_pe"]:lay["o_pe"] + ROPE],
             jnp.zeros((d, lay["NP"] - lay["KPE"] - ROPE), w.dtype)]
    return jnp.concatenate(parts, axis=1)


def _unpad_w_in(wp, lay):
    cw, att, ql, kvl = lay["CW"], lay["ATT"], lay["QL"], lay["KVL"]
    parts = [wp[:, lay["QC"]:lay["QC"] + ql], wp[:, lay["KVC"]:lay["KVC"] + kvl],
             wp[:, lay["KPE"]:lay["KPE"] + ROPE], wp[:, lay["GATT"]:lay["GATT"] + att],
             wp[:, :2 * cw], wp[:, lay["GCONV"]:lay["GCONV"] + cw]]
    return jnp.concatenate(parts, axis=1)


def _pad_heads(w, h):
    r = w.shape[0]
    return jnp.pad(w.reshape(r, h, QK_DIM), ((0, 0), (0, 0), (0, HEAD_PAD - QK_DIM))).reshape(r, h * HEAD_PAD)


def _unpad_heads(w, h):
    r = w.shape[0]
    return w.reshape(r, h, HEAD_PAD)[:, :, :QK_DIM].reshape(r, h * QK_DIM)


def _rope_tabs(positions):
    half = ROPE // 2
    inv_freq = ROPE_THETA ** (-jnp.arange(half, dtype=F32) / half)
    ang = positions.astype(F32).reshape(-1)[:, None] * inv_freq
    cos, sin = jnp.cos(ang), jnp.sin(ang)
    zero = jnp.zeros_like(cos)
    return (jnp.concatenate([cos, cos, zero, zero], axis=1),
            jnp.concatenate([-sin, zero, zero, zero], axis=1),
            jnp.concatenate([zero, sin, zero, zero], axis=1))


def _pack_rows(vecs):
    rows = []
    for v in vecs:
        flat = v.reshape(-1)
        pad = (-flat.shape[0]) % 1024
        rows.append(jnp.pad(flat, (0, pad)).reshape(-1, 128))
    return jnp.concatenate(rows, axis=0)


def _unpack_rows(pack, shapes):
    out, r0 = [], 0
    for shp in shapes:
        size = math.prod(shp)
        nrows = -(-size // 1024) * 8
        out.append(pack[r0:r0 + nrows].reshape(-1)[:size].reshape(shp))
        r0 += nrows
    return out


def kernel(x, positions, ln_g, w_in, q_a_norm, w_q_up, kv_a_norm, w_kv_up, q_norm, k_norm, w_dw, b_dw, conv_ln_g, conv_ln_b, w_out, loss_target, m_ln_g, m_w_in, m_q_a_norm, m_w_q_up, m_kv_a_norm, m_w_kv_up, m_q_norm, m_k_norm, m_w_dw, m_b_dw, m_conv_ln_g, m_conv_ln_b, m_w_out, v_ln_g, v_w_in, v_q_a_norm, v_w_q_up, v_kv_a_norm, v_w_kv_up, v_q_norm, v_k_norm, v_w_dw, v_b_dw, v_conv_ln_g, v_conv_ln_b, v_w_out):
    nb, seq, d = x.shape
    depth = ln_g.shape[0]
    lay = _layout(d, q_a_norm.shape[1], kv_a_norm.shape[1])
    h, cw, ql, kvl = lay["H"], lay["CW"], lay["QL"], lay["KVL"]
    t = nb * seq
    my_id = _block_id(_position())

    def shards(l):
        return [w_in[l].astype(BF16), w_q_up[l].astype(BF16), w_kv_up[l].astype(BF16), w_out[l].astype(BF16)]

    def fill_own(lands, own):
        return [lax.dynamic_update_index_in_dim(land, blk, my_id, 0) for land, blk in zip(lands, own)]

    def layout_in(g_in):
        return {"in": _pad_w_in(_ungather_cols(g_in), lay)}

    def layout_rest(g_q, g_kv, g_out):
        return {"q": _pad_heads(_ungather_cols(g_q), h), "kv": _ungather_cols(g_kv), "out": g_out.reshape(2 * cw, d)}

    first = shards(0)
    g_in0, g_dw = _all_gather([first[0], w_dw], name="gather_w_in_0")
    gathers = {0: _push_start(first[1:], False, after=g_in0, name="gather_start_0")}
    for l in range(1, depth):
        gathers[l] = _push_start(shards(l), False, after=gathers[l - 1][4], name=f"gather_start_{l}")
    fwd_dep = gathers[depth - 1][4]
    weights = []

    tabs = _rope_tabs(positions)
    gq_pad = jnp.pad(q_norm, ((0, 0), (0, HEAD_PAD - QK_DIM)))
    gk_pad = jnp.pad(k_norm, ((0, 0), (0, HEAD_PAD - QK_DIM)))
    w_dw_all = jnp.transpose(g_dw, (1, 2, 0, 3)).reshape(depth, CONV_K, cw)
    w_dw_all = jnp.pad(w_dw_all, ((0, 0), (0, HALO - CONV_K), (0, 0)))

    saved = []
    xs = x.reshape(t, d)
    for l in range(depth):
        hid = _rms_fwd(xs, ln_g[l], dep=fwd_dep if l == 0 else None, name=f"rms_fwd_{l}")
        if l == 0:
            weights.append(layout_in(g_in0))
        z = _mm(hid, weights[l]["in"], name=f"in_proj_{l}")
        if l == 0:
            own, lands = _push_wait(gathers[0], z, False, name="gather_wait_0")
            weights[0].update(layout_rest(*fill_own(lands, own)))
        wl = weights[l]
        qn, kvn = _lat_fwd(z, q_a_norm[l], kv_a_norm[l], lay, name=f"lat_fwd_{l}")
        q_raw = _mm(qn, wl["q"], name=f"q_up_{l}")
        kv_raw = _mm(kvn, wl["kv"], name=f"kv_up_{l}")
        qh, kh, vh = _heads_fwd(q_raw, kv_raw, z, tabs, gq_pad[l:l + 1], gk_pad[l:l + 1], lay, name=f"heads_fwd_{l}")
        att, mix, lse = _flash_fwd(qh, kh, vh, z, lay, nb, name=f"flash_fwd_{l}")
        mix, u, c_pre = _conv_fwd(z, mix, w_dw_all[l], b_dw[l], conv_ln_g[l], conv_ln_b[l], lay, nb,
                                  name=f"conv_fwd_{l}")
        x_next = _mm(mix, wl["out"], add=xs, name=f"out_proj_{l}")
        saved.append((xs, hid, z, qn, kvn, q_raw, kv_raw, qh, kh, vh, att, lse, mix, u, c_pre))
        xs = x_next
        if l + 1 < depth:
            own, lands = _push_wait(gathers[l + 1], xs, False, name=f"gather_wait_{l + 1}")
            g_in, *g_rest = fill_own(lands, own)
            weights.append({**layout_in(g_in), **layout_rest(*g_rest)})

    sse, dx, dxb = _loss_head(xs, loss_target.reshape(t, d), name="loss_head")
    loss = lax.psum(sse[0, 0] * (0.5 / d), ("x", "y", "c"))

    small = {k: [] for k in ("ln_g", "q_a", "kv_a", "q_n", "k_n", "w_dw", "b_dw", "cln_g", "cln_b")}
    scatters, bwd_dep = {}, None
    for l in reversed(range(depth)):
        xs, hid, z, qn, kvn, q_raw, kv_raw, qh, kh, vh, att, lse, mix, u, c_pre = saved[l]
        wl = weights[l]
        dmix = _mm(dxb, wl["out"], trans_b=True, dep=bwd_dep, name=f"d_mix_{l}")
        dw_out = _mm(mix, dxb, trans_a=True, out_dtype=BF16, name=f"dw_out_{l}")
        dc, dz, dlg, dlb, dbias = _conv_bwd_ln(c_pre, z, dmix, conv_ln_g[l], conv_ln_b[l], lay,
                                               name=f"conv_bwd_ln_{l}")
        dz, dwdw = _conv_bwd_dw(dc, u, z, w_dw_all[l], dz, lay, nb, name=f"conv_bwd_dw_{l}")
        do, delta, dz = _gate_bwd(dmix, att, z, dz, lay, name=f"gate_bwd_{l}")
        dqh, dkh, dvh = _flash_bwd(qh, kh, vh, do, lse, delta, nb, name=f"flash_bwd_{l}")
        dq_raw, dkv_raw, dpe, dgq, dgk = _heads_bwd(q_raw, kv_raw, z, tabs, gq_pad[l:l + 1], gk_pad[l:l + 1],
                                                    dqh, dkh, dvh, lay, name=f"heads_bwd_{l}")
        dqn = _mm(dq_raw, wl["q"], trans_b=True, name=f"d_qn_{l}")
        dkvn = _mm(dkv_raw, wl["kv"], trans_b=True, name=f"d_kvn_{l}")
        dw_q = _mm(qn, dq_raw, trans_a=True, out_dtype=BF16, name=f"dw_q_{l}")
        dw_kv = _mm(kvn, dkv_raw, trans_a=True, out_dtype=BF16, name=f"dw_kv_{l}")
        dz, dgqa, dgkva = _lat_bwd(z, q_a_norm[l], kv_a_norm[l], dqn, dkvn, dpe, dz, lay, name=f"lat_bwd_{l}")
        dw_in = _mm(hid, dz, trans_a=True, out_dtype=BF16, name=f"dw_in_{l}")
        scatters[l] = _push_start(
            [_to_col_blocks(_unpad_w_in(dw_in, lay)), _to_col_blocks(_unpad_heads(dw_q, h)), _to_col_blocks(dw_kv),
             dw_out.reshape(N_DEV, (2 * cw) // N_DEV, d)], True, name=f"scatter_start_{l}")
        bwd_dep = scatters[l][4]
        dh = _mm(dz, wl["in"], trans_b=True, dep=bwd_dep, name=f"d_hid_{l}")
        dx, dxb, dlng = _rms_bwd(xs, ln_g[l], dh, dx, name=f"rms_bwd_{l}")
        for key, val in (("ln_g", dlng), ("q_a", dgqa), ("kv_a", dgkva), ("q_n", dgq[:, :QK_DIM]),
                         ("k_n", dgk[:, :QK_DIM]), ("w_dw", dwdw[:CONV_K]), ("b_dw", dbias),
                         ("cln_g", dlg), ("cln_b", dlb)):
            small[key].append(val)
    grad_x = dx.reshape(nb, seq, d)
    for key in small:
        small[key] = jnp.stack(small[key][::-1])

    small_names = ("ln_g", "q_a", "kv_a", "q_n", "k_n", "b_dw", "cln_g", "cln_b", "w_dw")
    small_shapes = [small[k].shape for k in small_names]
    summed = _unpack_rows(_all_reduce_small(_pack_rows([small[k] for k in small_names]), name="reduce_small_grads"),
                          small_shapes)
    sg = dict(zip(small_names, summed))
    g_w_dw = lax.dynamic_slice_in_dim(sg["w_dw"], my_id * (cw // N_DEV), cw // N_DEV, axis=2)

    def adam_small(ws, ms, vs, gs, nm):
        shapes = [w.shape for w in ws]
        outs = _adam(_pack_rows(ws), _pack_rows(ms), _pack_rows(vs), [(_pack_rows(gs), None)], name=nm)
        return [_unpack_rows(o, shapes) for o in outs]

    big = [("w_in", w_in, m_w_in, v_w_in), ("w_q_up", w_q_up, m_w_q_up, v_w_q_up),
           ("w_kv_up", w_kv_up, m_w_kv_up, v_w_kv_up), ("w_out", w_out, m_w_out, v_w_out)]
    res, prev = {}, [None] * len(big)
    for l in reversed(range(depth)):
        own, lands = _push_wait(scatters[l], dx, True, name=f"scatter_wait_{l}")
        own = [lax.dynamic_index_in_dim(o, my_id, 0, keepdims=False) for o in own]
        for idx, ((nm, w, m, v), recv) in enumerate(zip(big, fill_own(lands, own))):
            rows, cols = w.shape[1], w.shape[2]
            flat = lambda a: a.reshape(depth * rows, cols)
            prev[idx] = _adam(flat(w), flat(m), flat(v), [(recv, s) for s in range(N_DEV)], layer=l, layers=depth,
                              prev=prev[idx], name=f"adam_{nm}_{l}")
    for idx, (nm, w, _, _) in enumerate(big):
        res[nm] = [o.reshape(w.shape) for o in prev[idx]]
    names_s = ["ln_g", "q_a_norm", "kv_a_norm", "q_norm", "k_norm", "w_dw", "b_dw", "conv_ln_g", "conv_ln_b"]
    ws = [ln_g, q_a_norm, kv_a_norm, q_norm, k_norm, w_dw, b_dw, conv_ln_g, conv_ln_b]
    ms = [m_ln_g, m_q_a_norm, m_kv_a_norm, m_q_norm, m_k_norm, m_w_dw, m_b_dw, m_conv_ln_g, m_conv_ln_b]
    vs = [v_ln_g, v_q_a_norm, v_kv_a_norm, v_q_norm, v_k_norm, v_w_dw, v_b_dw, v_conv_ln_g, v_conv_ln_b]
    gs = [sg["ln_g"].reshape(ln_g.shape), sg["q_a"].reshape(q_a_norm.shape), sg["kv_a"].reshape(kv_a_norm.shape),
          sg["q_n"].reshape(q_norm.shape), sg["k_n"].reshape(k_norm.shape), g_w_dw,
          sg["b_dw"].reshape(b_dw.shape), sg["cln_g"].reshape(conv_ln_g.shape), sg["cln_b"].reshape(conv_ln_b.shape)]
    outs_s = adam_small(ws, ms, vs, gs, "adam_small")
    for idx, nm in enumerate(names_s):
        res[nm] = [outs_s[k][idx] for k in range(4)]

    order = ["ln_g", "w_in", "q_a_norm", "w_q_up", "kv_a_norm", "w_kv_up", "q_norm", "k_norm", "w_dw", "b_dw",
             "conv_ln_g", "conv_ln_b", "w_out"]
    return (loss, grad_x, *[res[nm][0] for nm in order], *[res[nm][1] for nm in order],
            *[res[nm][2] for nm in order], *[res[nm][3] for nm in order])
```

```python
import functools
import math

import jax
import jax.numpy as jnp
from jax import lax
from jax.experimental import pallas as pl
from jax.experimental.pallas import tpu as pltpu

F32 = jnp.float32
BF16 = jnp.bfloat16
MESH = pl.DeviceIdType.MESH

N_DEV = 8
NOPE = 128
ROPE = 64
VDIM = 128
HEAD_PAD = 256
QK_DIM = NOPE + ROPE
CONV_K = 31
HALO = 32
EPS = 1e-6
ROPE_THETA = 10000.0
NEG = -1e30

ADAM_LR = 0.001
ADAM_B1 = 0.9
ADAM_B2 = 0.999
ADAM_EPS = 1e-08
ADAM_WD = 0.01
ADAM_STEP = 10

TOK_TILE = 256
CONV_TILE = 256
ATT_TQ = 512
HEAD_GROUP = 4
MM_TM = 1024
MM_TN = 512
MM_TK = 4096
MM_VMEM_BUDGET = 46 * 1024 * 1024
ADAM_BLOCK_ELEMS = 128 * 1024
LANE_CHUNK = 256
VMEM_LIMIT = 56 * 1024 * 1024


def _pcall(body, **kw):
    return pl.pallas_call(body, **kw)


def _tile(dim, pref, mult):
    t = min(pref, dim)
    t -= t % mult
    while t >= mult:
        if dim % t == 0:
            return t
        t -= mult
    return dim


def _params(sem):
    return pltpu.CompilerParams(dimension_semantics=sem, vmem_limit_bytes=VMEM_LIMIT)


def _sigmoid(v):
    return 1.0 / (1.0 + jnp.exp(-v))


def _dsilu(v, sg):
    return sg * (1.0 + v * (1.0 - sg))


def _mm_tiles(m, n, kdim, out_bytes, has_add):
    def need(tm, tn, tk):
        return (2 * 2 * (tm * tk + tk * tn) + 2 * tm * tn * out_bytes
                + tm * tn * 4 * ((kdim > tk) + 2 * has_add + 1))

    shapes = [(_tile(m, pm, 128), _tile(n, pn, 128))
              for pm, pn in ((MM_TM, MM_TN), (MM_TM // 2, MM_TN), (MM_TM // 2, MM_TN // 2), (MM_TM // 4, MM_TN // 2))]
    for tm, tn in shapes:
        if need(tm, tn, kdim) <= MM_VMEM_BUDGET:
            return tm, tn, kdim
    tk = _tile(kdim, MM_TK, 128)
    for tm, tn in shapes:
        if need(tm, tn, tk) <= MM_VMEM_BUDGET:
            break
    return tm, tn, tk


def _mm(a, b, *, name, trans_a=False, trans_b=False, add=None, out_dtype=F32, dep=None):
    assert not (trans_a and trans_b)
    if trans_a:
        kdim, m = a.shape
    else:
        m, kdim = a.shape
    n = b.shape[0] if trans_b else b.shape[1]
    assert b.shape[1 if trans_b else 0] == kdim
    has_add = add is not None
    tm, tn, tk = _mm_tiles(m, n, kdim, jnp.dtype(out_dtype).itemsize, has_add)
    nk = kdim // tk
    contract = (((0 if trans_a else 1,), (1 if trans_b else 0,)), ((), ()))

    def product(a_ref, b_ref):
        return lax.dot_general(a_ref[...], b_ref[...], contract, preferred_element_type=F32)

    def body(*refs):
        a_ref, b_ref = refs[:2]
        add_ref = refs[2] if has_add else None
        o_ref = refs[2 + has_add + (dep is not None)]

        def finish(r):
            if has_add:
                r = r + add_ref[...]
            o_ref[...] = r.astype(o_ref.dtype)

        if nk == 1:
            finish(product(a_ref, b_ref))
            return
        acc_ref = refs[-1]
        k = pl.program_id(2)

        @pl.when(k == 0)
        def _():
            acc_ref[...] = product(a_ref, b_ref)

        @pl.when((k > 0) & (k < nk - 1))
        def _():
            acc_ref[...] += product(a_ref, b_ref)

        @pl.when(k == nk - 1)
        def _():
            finish(acc_ref[...] + product(a_ref, b_ref))

    if trans_a:
        a_spec = pl.BlockSpec((tk, tm), lambda i, j, k: (k, i))
    else:
        a_spec = pl.BlockSpec((tm, tk), lambda i, j, k: (i, k))
    if trans_b:
        b_spec = pl.BlockSpec((tn, tk), lambda i, j, k: (j, k))
    else:
        b_spec = pl.BlockSpec((tk, tn), lambda i, j, k: (k, j))
    in_specs = [a_spec, b_spec]
    args = [a, b]
    if has_add:
        in_specs.append(pl.BlockSpec((tm, tn), lambda i, j, k: (i, j)))
        args.append(add)
    if dep is not None:
        in_specs.append(pl.BlockSpec(memory_space=pl.ANY))
        args.append(dep)
    return _pcall(
        body, name=name,
        grid=(m // tm, n // tn, nk),
        in_specs=in_specs,
        out_specs=pl.BlockSpec((tm, tn), lambda i, j, k: (i, j)),
        out_shape=jax.ShapeDtypeStruct((m, n), out_dtype),
        scratch_shapes=[pltpu.VMEM((tm, tn), F32)] if nk > 1 else [],
        compiler_params=_params(("parallel", "parallel", "arbitrary")),
    )(*args)


def _rms_fwd(x, g, *, name, dep=None):
    t, d = x.shape
    tt = _tile(t, TOK_TILE, 16)

    def body(x_ref, g_ref, *rest):
        h_ref = rest[-1]
        xv = x_ref[...]
        r = lax.rsqrt(jnp.mean(xv * xv, axis=-1, keepdims=True) + EPS)
        h_ref[...] = (xv * r * g_ref[...]).astype(BF16)

    deps = [] if dep is None else [dep]
    return _pcall(
        body, name=name, grid=(t // tt,),
        in_specs=[pl.BlockSpec((tt, d), lambda i: (i, 0)), pl.BlockSpec((1, d), lambda i: (0, 0))]
        + [pl.BlockSpec(memory_space=pl.ANY)] * len(deps),
        out_specs=pl.BlockSpec((tt, d), lambda i: (i, 0)),
        out_shape=jax.ShapeDtypeStruct((t, d), BF16),
        compiler_params=_params(("parallel",)),
    )(x, g.reshape(1, d), *deps)


def _rms_bwd(x, g, dh, dres, *, name):
    t, d = x.shape
    tt = _tile(t, TOK_TILE, 16)

    def body(x_ref, g_ref, dh_ref, dres_ref, dx_ref, dxb_ref, dg_ref):
        xv = x_ref[...]
        r = lax.rsqrt(jnp.mean(xv * xv, axis=-1, keepdims=True) + EPS)
        dy = dh_ref[...]
        dyg = dy * g_ref[...]
        dot = jnp.sum(dyg * xv, axis=-1, keepdims=True) * (1.0 / d)
        dx = dres_ref[...] + r * dyg - xv * (r * r * r) * dot
        dx_ref[...] = dx
        dxb_ref[...] = dx.astype(BF16)

        @pl.when(pl.program_id(0) == 0)
        def _():
            dg_ref[...] = jnp.zeros_like(dg_ref)

        dg_ref[...] += jnp.sum(dy * xv * r, axis=0, keepdims=True)

    row = pl.BlockSpec((tt, d), lambda i: (i, 0))
    vec = pl.BlockSpec((1, d), lambda i: (0, 0))
    return _pcall(
        body, name=name, grid=(t // tt,),
        in_specs=[row, vec, row, row],
        out_specs=[row, row, vec],
        out_shape=[jax.ShapeDtypeStruct((t, d), F32), jax.ShapeDtypeStruct((t, d), BF16),
                   jax.ShapeDtypeStruct((1, d), F32)],
        compiler_params=_params(("arbitrary",)),
    )(x, g.reshape(1, d), dh, dres)


def _lat_fwd(z, gq, gkv, lay, *, name):
    t = z.shape[0]
    ql, kvl = lay["QL"], lay["KVL"]
    tt = _tile(t, TOK_TILE, 16)

    def body(q_ref, kv_ref, gq_ref, gkv_ref, qn_ref, kvn_ref):
        for src, g_ref, dst in ((q_ref, gq_ref, qn_ref), (kv_ref, gkv_ref, kvn_ref)):
            v = src[...]
            r = lax.rsqrt(jnp.mean(v * v, axis=-1, keepdims=True) + EPS)
            dst[...] = (v * r * g_ref[...]).astype(BF16)

    return _pcall(
        body, name=name, grid=(t // tt,),
        in_specs=[pl.BlockSpec((tt, ql), lambda i: (i, lay["QC"] // ql)),
                  pl.BlockSpec((tt, kvl), lambda i: (i, lay["KVC"] // kvl)),
                  pl.BlockSpec((1, ql), lambda i: (0, 0)), pl.BlockSpec((1, kvl), lambda i: (0, 0))],
        out_specs=[pl.BlockSpec((tt, ql), lambda i: (i, 0)), pl.BlockSpec((tt, kvl), lambda i: (i, 0))],
        out_shape=[jax.ShapeDtypeStruct((t, ql), BF16), jax.ShapeDtypeStruct((t, kvl), BF16)],
        compiler_params=_params(("parallel",)),
    )(z, z, gq.reshape(1, ql), gkv.reshape(1, kvl))


def _lat_bwd(z, gq, gkv, dqn, dkvn, dpe, dz, lay, *, name):
    t = z.shape[0]
    ql, kvl = lay["QL"], lay["KVL"]
    tail = lay["NP"] - lay["QC"]
    assert lay["QC"] % tail == 0
    tt = _tile(t, TOK_TILE, 16)

    def body(q_ref, kv_ref, gq_ref, gkv_ref, dqn_ref, dkvn_ref, dpe_ref, dz_in, tail_ref, dgq_ref, dgkv_ref):
        del dz_in
        first = pl.program_id(0) == 0
        for src, g_ref, dy_ref, c0, dg_ref in ((q_ref, gq_ref, dqn_ref, 0, dgq_ref),
                                               (kv_ref, gkv_ref, dkvn_ref, ql, dgkv_ref)):
            v = src[...]
            n = v.shape[-1]
            r = lax.rsqrt(jnp.mean(v * v, axis=-1, keepdims=True) + EPS)
            dy = dy_ref[...]
            dyg = dy * g_ref[...]
            dot = jnp.sum(dyg * v, axis=-1, keepdims=True) * (1.0 / n)
            tail_ref[:, c0:c0 + n] = (r * dyg - v * (r * r * r) * dot).astype(BF16)

            @pl.when(first)
            def _():
                dg_ref[...] = jnp.zeros_like(dg_ref)

            dg_ref[...] += jnp.sum(dy * v * r, axis=0, keepdims=True)
        tail_ref[:, ql + kvl:ql + kvl + 128] = dpe_ref[...].astype(BF16)
        tail_ref[:, ql + kvl + 128:tail] = jnp.zeros((tt, tail - ql - kvl - 128), BF16)

    return _pcall(
        body, name=name, grid=(t // tt,),
        in_specs=[pl.BlockSpec((tt, ql), lambda i: (i, lay["QC"] // ql)),
                  pl.BlockSpec((tt, kvl), lambda i: (i, lay["KVC"] // kvl)),
                  pl.BlockSpec((1, ql), lambda i: (0, 0)), pl.BlockSpec((1, kvl), lambda i: (0, 0)),
                  pl.BlockSpec((tt, ql), lambda i: (i, 0)), pl.BlockSpec((tt, kvl), lambda i: (i, 0)),
                  pl.BlockSpec((tt, 128), lambda i: (i, 0)), pl.BlockSpec(memory_space=pl.ANY)],
        out_specs=[pl.BlockSpec((tt, tail), lambda i: (i, lay["QC"] // tail)),
                   pl.BlockSpec((1, ql), lambda i: (0, 0)), pl.BlockSpec((1, kvl), lambda i: (0, 0))],
        out_shape=[jax.ShapeDtypeStruct(dz.shape, BF16),
                   jax.ShapeDtypeStruct((1, ql), F32), jax.ShapeDtypeStruct((1, kvl), F32)],
        input_output_aliases={7: 0},
        compiler_params=_params(("arbitrary",)),
    )(z, z, gq.reshape(1, ql), gkv.reshape(1, kvl), dqn, dkvn, dpe, dz)


def _rope(r, c_tab, sa_tab, sb_tab):
    return r * c_tab + pltpu.roll(r, 96, 1) * sa_tab + pltpu.roll(r, 32, 1) * sb_tab


def _rope_t(d, c_tab, sa_tab, sb_tab):
    return d * c_tab + pltpu.roll(d * sa_tab, 32, 1) + pltpu.roll(d * sb_tab, 96, 1)


def _heads_fwd(q_raw, kv_raw, z, tabs, gq, gk, lay, *, name):
    t = z.shape[0]
    h = lay["H"]
    tt = _tile(t, TOK_TILE, 16)
    hg = _tile(h, HEAD_GROUP, 1)
    scale = 1.0 / math.sqrt(QK_DIM)

    def body(q_ref, kv_ref, pe_ref, c_ref, sa_ref, sb_ref, gq_ref, gk_ref, qh_ref, kh_ref, vh_ref):
        c_tab, sa_tab, sb_tab = c_ref[...], sa_ref[...], sb_ref[...]
        pe, gq_v, gk_v = pe_ref[...], gq_ref[...], gk_ref[...]
        ss_pe = jnp.sum(pe * pe, axis=-1, keepdims=True)
        for g in range(hg):
            q = q_ref[:, g * HEAD_PAD:(g + 1) * HEAD_PAD]
            r = lax.rsqrt(jnp.sum(q * q, axis=-1, keepdims=True) * (1.0 / QK_DIM) + EPS)
            qn = q * r * gq_v
            qh_ref[g] = (jnp.concatenate([qn[:, :NOPE], _rope(qn[:, NOPE:], c_tab, sa_tab, sb_tab)], axis=1)
                         * scale).astype(BF16)
            kv = kv_ref[:, g * HEAD_PAD:(g + 1) * HEAD_PAD]
            kn = kv[:, :NOPE]
            rk = lax.rsqrt((jnp.sum(kn * kn, axis=-1, keepdims=True) + ss_pe) * (1.0 / QK_DIM) + EPS)
            kh_ref[g] = jnp.concatenate(
                [kn * rk * gk_v[:, :NOPE], _rope(pe * rk * gk_v[:, NOPE:], c_tab, sa_tab, sb_tab)],
                axis=1).astype(BF16)
            vh_ref[g] = kv[:, NOPE:].astype(BF16)

    head = pl.BlockSpec((tt, hg * HEAD_PAD), lambda i, j: (i, j))
    tab = pl.BlockSpec((tt, 128), lambda i, j: (i, 0))
    gain = pl.BlockSpec((1, HEAD_PAD), lambda i, j: (0, 0))
    return _pcall(
        body, name=name, grid=(t // tt, h // hg),
        in_specs=[head, head, pl.BlockSpec((tt, 128), lambda i, j: (i, lay["KPE"] // 128)), tab, tab, tab, gain, gain],
        out_specs=[pl.BlockSpec((hg, tt, HEAD_PAD), lambda i, j: (j, i, 0)),
                   pl.BlockSpec((hg, tt, HEAD_PAD), lambda i, j: (j, i, 0)),
                   pl.BlockSpec((hg, tt, VDIM), lambda i, j: (j, i, 0))],
        out_shape=[jax.ShapeDtypeStruct((h, t, HEAD_PAD), BF16), jax.ShapeDtypeStruct((h, t, HEAD_PAD), BF16),
                   jax.ShapeDtypeStruct((h, t, VDIM), BF16)],
        compiler_params=_params(("parallel", "parallel")),
    )(q_raw, kv_raw, z, *tabs, gq, gk)


def _heads_bwd(q_raw, kv_raw, z, tabs, gq, gk, dqh, dkh, dvh, lay, *, name):
    t = z.shape[0]
    h = lay["H"]
    tt = _tile(t, TOK_TILE, 16)
    hg = _tile(h, HEAD_GROUP, 1)
    scale = 1.0 / math.sqrt(QK_DIM)

    def body(q_ref, kv_ref, pe_ref, c_ref, sa_ref, sb_ref, gq_ref, gk_ref, dqh_ref, dkh_ref, dvh_ref,
             dq_ref, dkv_ref, dpe_ref, dgq_ref, dgk_ref):
        i, j = pl.program_id(0), pl.program_id(1)
        c_tab, sa_tab, sb_tab = c_ref[...], sa_ref[...], sb_ref[...]

        @pl.when((i == 0) & (j == 0))
        def _():
            dgq_ref[...] = jnp.zeros_like(dgq_ref)
            dgk_ref[...] = jnp.zeros_like(dgk_ref)

        @pl.when(j == 0)
        def _():
            dpe_ref[...] = jnp.zeros_like(dpe_ref)

        def norm_bwd(v, g, dy):
            r = lax.rsqrt(jnp.sum(v * v, axis=-1, keepdims=True) * (1.0 / QK_DIM) + EPS)
            dyg = dy * g
            dot = jnp.sum(dyg * v, axis=-1, keepdims=True) * (1.0 / QK_DIM)
            return r * dyg - v * (r * r * r) * dot, jnp.sum(dy * v * r, axis=0, keepdims=True)

        pe, gq_v, gk_v = pe_ref[...], gq_ref[...], gk_ref[...]
        dpe, dgq, dgk = jnp.zeros_like(pe), jnp.zeros_like(gq_v), jnp.zeros_like(gk_v)
        for g in range(hg):
            cols = slice(g * HEAD_PAD, (g + 1) * HEAD_PAD)
            dqo = dqh_ref[g] * scale
            dy = jnp.concatenate([dqo[:, :NOPE], _rope_t(dqo[:, NOPE:], c_tab, sa_tab, sb_tab)], axis=1)
            dq, dg = norm_bwd(q_ref[:, cols], gq_v, dy)
            dq_ref[:, cols] = dq.astype(BF16)
            dgq = dgq + dg

            dko = dkh_ref[g]
            dy = jnp.concatenate([dko[:, :NOPE], _rope_t(dko[:, NOPE:], c_tab, sa_tab, sb_tab)], axis=1)
            kfull = jnp.concatenate([kv_ref[:, cols][:, :NOPE], pe], axis=1)
            dk, dg = norm_bwd(kfull, gk_v, dy)
            dkv_ref[:, cols] = jnp.concatenate([dk[:, :NOPE], dvh_ref[g]], axis=1).astype(BF16)
            dpe = dpe + dk[:, NOPE:]
            dgk = dgk + dg
        dpe_ref[...] += dpe
        dgq_ref[...] += dgq
        dgk_ref[...] += dgk

    head = pl.BlockSpec((tt, hg * HEAD_PAD), lambda i, j: (i, j))
    tab = pl.BlockSpec((tt, 128), lambda i, j: (i, 0))
    gain = pl.BlockSpec((1, HEAD_PAD), lambda i, j: (0, 0))
    hm = pl.BlockSpec((hg, tt, HEAD_PAD), lambda i, j: (j, i, 0))
    return _pcall(
        body, name=name, grid=(t // tt, h // hg),
        in_specs=[head, head, pl.BlockSpec((tt, 128), lambda i, j: (i, lay["KPE"] // 128)), tab, tab, tab, gain, gain,
                  hm, hm, pl.BlockSpec((hg, tt, VDIM), lambda i, j: (j, i, 0))],
        out_specs=[head, head, tab, gain, gain],
        out_shape=[jax.ShapeDtypeStruct((t, h * HEAD_PAD), BF16), jax.ShapeDtypeStruct((t, h * HEAD_PAD), BF16),
                   jax.ShapeDtypeStruct((t, 128), F32),
                   jax.ShapeDtypeStruct((1, HEAD_PAD), F32), jax.ShapeDtypeStruct((1, HEAD_PAD), F32)],
        compiler_params=_params(("arbitrary", "arbitrary")),
    )(q_raw, kv_raw, z, *tabs, gq, gk, dqh, dkh, dvh)


def _lower_triangle(n):
    return lax.broadcasted_iota(jnp.int32, (n, n), 1) <= lax.broadcasted_iota(jnp.int32, (n, n), 0)


def _qk(q, k):
    return lax.dot_general(q, k, (((1,), (1,)), ((), ())), preferred_element_type=F32)


def _flash_fwd(qh, kh, vh, z, lay, nb, *, name):
    h, t, _ = qh.shape
    s = t // nb
    tq = _tile(s, ATT_TQ, 128)
    nq = s // tq
    att_w = h * VDIM
    gblk = lay["GATT"] // VDIM

    def body(q_ref, k_ref, v_ref, g_ref, att_ref, mix_ref, lse_ref):
        i = pl.program_id(2)
        tri = _lower_triangle(tq)
        for blk in range(nq):
            @pl.when(i == blk)
            def _():
                q = q_ref[...]
                pre = blk * tq
                sd = jnp.where(tri, _qk(q, k_ref[pre:pre + tq, :]), NEG)
                m = jnp.max(sd, axis=-1, keepdims=True)
                if pre:
                    sp = _qk(q, k_ref[0:pre, :])
                    m = jnp.maximum(m, jnp.max(sp, axis=-1, keepdims=True))
                pd = jnp.exp(sd - m)
                l = jnp.sum(pd, axis=-1, keepdims=True)
                acc = jnp.dot(pd.astype(BF16), v_ref[pre:pre + tq, :], preferred_element_type=F32)
                if pre:
                    pp = jnp.exp(sp - m)
                    l = l + jnp.sum(pp, axis=-1, keepdims=True)
                    acc = acc + jnp.dot(pp.astype(BF16), v_ref[0:pre, :], preferred_element_type=F32)
                o = acc / l
                att_ref[...] = o
                g = g_ref[...]
                mix_ref[...] = (o * (g * _sigmoid(g))).astype(BF16)
                lse_ref[...] = m + jnp.log(l)

    row = lambda hh, b, i: (b * nq + i, hh)
    seq = lambda hh, b, i: (hh, b, 0)
    return _pcall(
        body, name=name, grid=(h, nb, nq),
        in_specs=[pl.BlockSpec((None, tq, HEAD_PAD), lambda hh, b, i: (hh, b * nq + i, 0)),
                  pl.BlockSpec((None, s, HEAD_PAD), seq),
                  pl.BlockSpec((None, s, VDIM), seq),
                  pl.BlockSpec((tq, VDIM), lambda hh, b, i: (b * nq + i, gblk + hh))],
        out_specs=[pl.BlockSpec((tq, VDIM), row), pl.BlockSpec((tq, VDIM), row),
                   pl.BlockSpec((None, tq, 1), lambda hh, b, i: (hh, b * nq + i, 0))],
        out_shape=[jax.ShapeDtypeStruct((t, att_w), F32), jax.ShapeDtypeStruct((t, 2 * att_w), BF16),
                   jax.ShapeDtypeStruct((h, t, 1), F32)],
        compiler_params=_params(("parallel", "parallel", "parallel")),
    )(qh, kh, vh, z)


def _gate_bwd(dmix, att, z, dz, lay, *, name):
    t, att_w = att.shape
    h = att_w // VDIM
    tt = _tile(t, TOK_TILE, 16)
    gblk = lay["GATT"] // att_w

    def body(dm_ref, o_ref, g_ref, dz_in, do_ref, delta_ref, dg_ref):
        del dz_in
        dm, o, g = dm_ref[...], o_ref[...], g_ref[...]
        sg = _sigmoid(g)
        do = dm * (g * sg)
        do_ref[...] = do.astype(BF16)
        prod = do * o
        for hh in range(h):
            delta_ref[hh] = jnp.sum(prod[:, hh * VDIM:(hh + 1) * VDIM], axis=-1, keepdims=True)
        dg_ref[...] = (dm * o * _dsilu(g, sg)).astype(BF16)

    blk = pl.BlockSpec((tt, att_w), lambda i: (i, 0))
    gate = pl.BlockSpec((tt, att_w), lambda i: (i, gblk))
    return _pcall(
        body, name=name, grid=(t // tt,),
        in_specs=[blk, blk, gate, pl.BlockSpec(memory_space=pl.ANY)],
        out_specs=[blk, pl.BlockSpec((h, tt, 1), lambda i: (0, i, 0)), gate],
        out_shape=[jax.ShapeDtypeStruct((t, att_w), BF16), jax.ShapeDtypeStruct((h, t, 1), F32),
                   jax.ShapeDtypeStruct(dz.shape, BF16)],
        input_output_aliases={3: 2},
        compiler_params=_params(("parallel",)),
    )(dmix, att, z, dz)


def _flash_bwd(qh, kh, vh, do, lse, delta, nb, *, name):
    h, t, _ = qh.shape
    s = t // nb
    tk = _tile(s, ATT_TQ, 128)
    nk = s // tk
    tn_dims = (((0,), (0,)), ((), ()))

    def body(q_ref, k_ref, v_ref, do_ref, lse_ref, dl_ref, dq_ref, dk_ref, dv_ref):
        j = pl.program_id(2)
        tri = _lower_triangle(tk)

        @pl.when(j == 0)
        def _():
            dq_ref[...] = jnp.zeros_like(dq_ref)

        def rows_against_block(r0, r1, masked):
            q, do_v = q_ref[r0:r1, :], do_ref[r0:r1, :]
            k = k_ref[...]
            sc = _qk(q, k)
            if masked:
                sc = jnp.where(tri, sc, NEG)
            p = jnp.exp(sc - lse_ref[r0:r1, :])
            dv = lax.dot_general(p.astype(BF16), do_v, tn_dims, preferred_element_type=F32)
            ds = (p * (_qk(do_v, v_ref[...]) - dl_ref[r0:r1, :])).astype(BF16)
            dq_ref[r0:r1, :] += jnp.dot(ds, k, preferred_element_type=F32)
            return lax.dot_general(ds, q, tn_dims, preferred_element_type=F32), dv

        for blk in range(nk):
            @pl.when(j == blk)
            def _():
                r0 = blk * tk
                dk, dv = rows_against_block(r0, r0 + tk, True)
                if r0 + tk < s:
                    dk2, dv2 = rows_against_block(r0 + tk, s, False)
                    dk, dv = dk + dk2, dv + dv2
                dk_ref[...] = dk
                dv_ref[...] = dv

    seq = lambda hh, b, j: (hh, b, 0)
    kv = lambda hh, b, j: (hh, b * nk + j, 0)
    return _pcall(
        body, name=name, grid=(h, nb, nk),
        in_specs=[pl.BlockSpec((None, s, HEAD_PAD), seq),
                  pl.BlockSpec((None, tk, HEAD_PAD), kv),
                  pl.BlockSpec((None, tk, VDIM), kv),
                  pl.BlockSpec((s, VDIM), lambda hh, b, j: (b, hh)),
                  pl.BlockSpec((None, s, 1), seq), pl.BlockSpec((None, s, 1), seq)],
        out_specs=[pl.BlockSpec((None, s, HEAD_PAD), seq), pl.BlockSpec((None, tk, HEAD_PAD), kv),
                   pl.BlockSpec((None, tk, VDIM), kv)],
        out_shape=[jax.ShapeDtypeStruct((h, t, HEAD_PAD), F32), jax.ShapeDtypeStruct((h, t, HEAD_PAD), F32),
                   jax.ShapeDtypeStruct((h, t, VDIM), F32)],
        compiler_params=_params(("parallel", "parallel", "arbitrary")),
    )(qh, kh, vh, do, lse, delta)


SUBLANES = 8


def _stage_row_shifts(ext, sh, c0, lc, rows):
    for r in range(1, SUBLANES):
        sh[r - 1, 0:rows, :] = ext[r:r + rows, c0:c0 + lc]


def _row_window(ext, sh, c0, lc, off, n):
    r = off % SUBLANES
    if r == 0:
        return ext[off:off + n, c0:c0 + lc]
    return sh[r - 1, off - r:off - r + n, :]


def _conv_fwd(z, mix, w_dw, b_dw, ln_g, ln_b, lay, nb, *, name):
    t = z.shape[0]
    cw = lay["CW"]
    s = t // nb
    tt = _tile(s, CONV_TILE, HALO)
    ns = s // tt
    hb = tt // HALO
    lc = _tile(cw, LANE_CHUNK, 128)

    def body(a_ref, b_ref, ap_ref, bp_ref, gc_ref, w_ref, bias_ref, lg_ref, lb_ref, mix_in, mix_ref, u_ref, c_ref,
             ext, sh):
        del mix_in
        i = pl.program_id(1)
        u = a_ref[...] * _sigmoid(b_ref[...])
        u_ref[...] = u
        ext[0:HALO, :] = jnp.where(i > 0, ap_ref[...] * _sigmoid(bp_ref[...]), 0.0)
        ext[HALO:HALO + tt, :] = u
        for c0 in range(0, cw, lc):
            _stage_row_shifts(ext, sh, c0, lc, tt + HALO - SUBLANES)
            acc = jnp.zeros((tt, lc), F32) + bias_ref[:, c0:c0 + lc]
            for k in range(CONV_K):
                off = HALO - (CONV_K - 1) + k
                acc = acc + w_ref[k:k + 1, c0:c0 + lc] * _row_window(ext, sh, c0, lc, off, tt)
            c_ref[:, c0:c0 + lc] = acc
        c = c_ref[...]
        mu = jnp.mean(c, axis=-1, keepdims=True)
        xc = c - mu
        var = jnp.mean(xc * xc, axis=-1, keepdims=True)
        y = xc * lax.rsqrt(var + EPS) * lg_ref[...] + lb_ref[...]
        g = gc_ref[...]
        mix_ref[...] = (y * _sigmoid(y) * (g * _sigmoid(g))).astype(BF16)

    cur = lambda col: pl.BlockSpec((tt, cw), lambda b, i: (b * ns + i, col))
    prev = lambda col: pl.BlockSpec((HALO, cw), lambda b, i: (jnp.maximum((b * ns + i) * hb - 1, 0), col))
    vec = pl.BlockSpec((1, cw), lambda b, i: (0, 0))
    out_row = pl.BlockSpec((tt, cw), lambda b, i: (b * ns + i, 0))
    return _pcall(
        body, name=name, grid=(nb, ns),
        in_specs=[cur(lay["A"] // cw), cur(lay["B"] // cw), prev(lay["A"] // cw), prev(lay["B"] // cw),
                  cur(lay["GCONV"] // cw), pl.BlockSpec((HALO, cw), lambda b, i: (0, 0)), vec, vec, vec,
                  pl.BlockSpec(memory_space=pl.ANY)],
        out_specs=[pl.BlockSpec((tt, cw), lambda b, i: (b * ns + i, 1)), out_row, out_row],
        out_shape=[jax.ShapeDtypeStruct(mix.shape, BF16), jax.ShapeDtypeStruct((t, cw), F32),
                   jax.ShapeDtypeStruct((t, cw), F32)],
        scratch_shapes=[pltpu.VMEM((tt + HALO, cw), F32),
                        pltpu.VMEM((SUBLANES - 1, tt + HALO - SUBLANES, lc), F32)],
        input_output_aliases={9: 0},
        compiler_params=_params(("parallel", "parallel")),
    )(z, z, z, z, z, w_dw, b_dw.reshape(1, cw), ln_g.reshape(1, cw), ln_b.reshape(1, cw), mix)


def _conv_bwd_ln(c_pre, z, dmix, ln_g, ln_b, lay, *, name):
    t, cw = c_pre.shape
    tt = _tile(t, TOK_TILE, 16)

    def body(c_ref, gc_ref, dm_ref, lg_ref, lb_ref, dc_ref, dgc_ref, dlg_ref, dlb_ref, dbias_ref):
        @pl.when(pl.program_id(0) == 0)
        def _():
            dlg_ref[...] = jnp.zeros_like(dlg_ref)
            dlb_ref[...] = jnp.zeros_like(dlb_ref)
            dbias_ref[...] = jnp.zeros_like(dbias_ref)

        c = c_ref[...]
        mu = jnp.mean(c, axis=-1, keepdims=True)
        xc = c - mu
        rstd = lax.rsqrt(jnp.mean(xc * xc, axis=-1, keepdims=True) + EPS)
        xhat = xc * rstd
        y = xhat * lg_ref[...] + lb_ref[...]
        sy = _sigmoid(y)
        g = gc_ref[...]
        sg = _sigmoid(g)
        dm = dm_ref[...].astype(F32)
        dgc_ref[...] = (dm * (y * sy) * _dsilu(g, sg)).astype(BF16)
        dy = dm * (g * sg) * _dsilu(y, sy)
        dlb_ref[...] += jnp.sum(dy, axis=0, keepdims=True)
        dlg_ref[...] += jnp.sum(dy * xhat, axis=0, keepdims=True)
        dxh = dy * lg_ref[...]
        dc = rstd * (dxh - jnp.mean(dxh, axis=-1, keepdims=True)
                     - xhat * jnp.mean(dxh * xhat, axis=-1, keepdims=True))
        dc_ref[...] = dc
        dbias_ref[...] += jnp.sum(dc, axis=0, keepdims=True)

    row = pl.BlockSpec((tt, cw), lambda i: (i, 0))
    vec = pl.BlockSpec((1, cw), lambda i: (0, 0))
    return _pcall(
        body, name=name, grid=(t // tt,),
        in_specs=[row, pl.BlockSpec((tt, cw), lambda i: (i, lay["GCONV"] // cw)),
                  pl.BlockSpec((tt, cw), lambda i: (i, 1)), vec, vec],
        out_specs=[row, pl.BlockSpec((tt, cw), lambda i: (i, lay["GCONV"] // cw)), vec, vec, vec],
        out_shape=[jax.ShapeDtypeStruct((t, cw), F32), jax.ShapeDtypeStruct((t, lay["NP"]), BF16),
                   jax.ShapeDtypeStruct((1, cw), F32), jax.ShapeDtypeStruct((1, cw), F32),
                   jax.ShapeDtypeStruct((1, cw), F32)],
        compiler_params=_params(("arbitrary",)),
    )(c_pre, z, dmix, ln_g.reshape(1, cw), ln_b.reshape(1, cw))


def _conv_bwd_dw(dc, u, z, w_dw, dz, lay, nb, *, name):
    t, cw = dc.shape
    s = t // nb
    tt = _tile(s, CONV_TILE, HALO)
    ns = s // tt
    hb = tt // HALO
    lc = _tile(cw, LANE_CHUNK, 128)

    def body(dc_ref, dcn_ref, u_ref, up_ref, a_ref, b_ref, w_ref, dz_in, dab_ref, dw_ref, ext_dc, ext_u, du_ref,
             sh_dc, sh_u):
        del dz_in
        b_i, i = pl.program_id(0), pl.program_id(1)

        @pl.when((b_i == 0) & (i == 0))
        def _():
            dw_ref[...] = jnp.zeros_like(dw_ref)

        dc_v = dc_ref[...]
        ext_dc[0:tt, :] = dc_v
        ext_dc[tt:tt + HALO, :] = jnp.where(i < ns - 1, dcn_ref[...], 0.0)
        ext_u[0:HALO, :] = jnp.where(i > 0, up_ref[...], 0.0)
        ext_u[HALO:HALO + tt, :] = u_ref[...]
        for c0 in range(0, cw, lc):
            _stage_row_shifts(ext_dc, sh_dc, c0, lc, tt + HALO - SUBLANES)
            _stage_row_shifts(ext_u, sh_u, c0, lc, tt + HALO - SUBLANES)
            acc = jnp.zeros((tt, lc), F32)
            dcc = dc_v[:, c0:c0 + lc]
            for k in range(CONV_K):
                acc = acc + w_ref[k:k + 1, c0:c0 + lc] * _row_window(ext_dc, sh_dc, c0, lc, CONV_K - 1 - k, tt)
                off = HALO - (CONV_K - 1) + k
                dw_ref[k:k + 1, c0:c0 + lc] += jnp.sum(dcc * _row_window(ext_u, sh_u, c0, lc, off, tt),
                                                       axis=0, keepdims=True)
            du_ref[:, c0:c0 + lc] = acc
        du = du_ref[...]
        sb = _sigmoid(b_ref[...])
        dab_ref[:, 0:cw] = (du * sb).astype(BF16)
        dab_ref[:, cw:2 * cw] = (du * a_ref[...] * sb * (1.0 - sb)).astype(BF16)

    last = nb * ns * hb - 1
    row = pl.BlockSpec((tt, cw), lambda b, i: (b * ns + i, 0))
    return _pcall(
        body, name=name, grid=(nb, ns),
        in_specs=[row, pl.BlockSpec((HALO, cw), lambda b, i: (jnp.minimum((b * ns + i + 1) * hb, last), 0)),
                  row, pl.BlockSpec((HALO, cw), lambda b, i: (jnp.maximum((b * ns + i) * hb - 1, 0), 0)),
                  pl.BlockSpec((tt, cw), lambda b, i: (b * ns + i, lay["A"] // cw)),
                  pl.BlockSpec((tt, cw), lambda b, i: (b * ns + i, lay["B"] // cw)),
                  pl.BlockSpec((HALO, cw), lambda b, i: (0, 0)), pl.BlockSpec(memory_space=pl.ANY)],
        out_specs=[pl.BlockSpec((tt, 2 * cw), lambda b, i: (b * ns + i, 0)),
                   pl.BlockSpec((HALO, cw), lambda b, i: (0, 0))],
        out_shape=[jax.ShapeDtypeStruct(dz.shape, BF16), jax.ShapeDtypeStruct((HALO, cw), F32)],
        scratch_shapes=[pltpu.VMEM((tt + HALO, cw), F32), pltpu.VMEM((tt + HALO, cw), F32),
                        pltpu.VMEM((tt, cw), F32),
                        pltpu.VMEM((SUBLANES - 1, tt + HALO - SUBLANES, lc), F32),
                        pltpu.VMEM((SUBLANES - 1, tt + HALO - SUBLANES, lc), F32)],
        input_output_aliases={7: 0},
        compiler_params=_params(("arbitrary", "arbitrary")),
    )(dc, dc, u, u, z, z, w_dw, dz)


def _loss_head(y, target, *, name):
    t, d = y.shape
    tt = _tile(t, TOK_TILE, 16)

    def body(y_ref, t_ref, sse_ref, dy_ref, dyb_ref):
        @pl.when(pl.program_id(0) == 0)
        def _():
            sse_ref[...] = jnp.zeros_like(sse_ref)

        e = y_ref[...] - t_ref[...]
        sse_ref[...] += jnp.sum(e * e)
        dy = e * (1.0 / d)
        dy_ref[...] = dy
        dyb_ref[...] = dy.astype(BF16)

    row = pl.BlockSpec((tt, d), lambda i: (i, 0))
    return _pcall(
        body, name=name, grid=(t // tt,),
        in_specs=[row, row],
        out_specs=[pl.BlockSpec((8, 128), lambda i: (0, 0)), row, row],
        out_shape=[jax.ShapeDtypeStruct((8, 128), F32), jax.ShapeDtypeStruct((t, d), F32),
                   jax.ShapeDtypeStruct((t, d), BF16)],
        compiler_params=_params(("arbitrary",)),
    )(y, target)


def _adam(w, m, v, g_parts, *, name, layer=0, layers=1, prev=None):
    rows, cols = w.shape
    slab = rows // layers
    tr = _tile(slab, max(16, ADAM_BLOCK_ELEMS // cols), 16)
    blk0 = layer * (slab // tr)
    n = len(g_parts)
    n_prev = 0 if prev is None else 4

    def body(*refs):
        w_ref, m_ref, v_ref = refs[:3]
        g_refs = refs[3:3 + n]
        g_out, d_out, m_out, v_out = refs[3 + n + n_prev:]
        g = g_refs[0][...].astype(F32)
        for r in g_refs[1:]:
            g = g + r[...].astype(F32)
        m_new = ADAM_B1 * m_ref[...] + (1.0 - ADAM_B1) * g
        v_new = ADAM_B2 * v_ref[...] + (1.0 - ADAM_B2) * (g * g)
        m_hat = m_new / (1.0 - ADAM_B1 ** ADAM_STEP)
        v_hat = v_new / (1.0 - ADAM_B2 ** ADAM_STEP)
        g_out[...] = g
        d_out[...] = -ADAM_LR * (m_hat / (jnp.sqrt(v_hat) + ADAM_EPS) + ADAM_WD * w_ref[...])
        m_out[...] = m_new
        v_out[...] = v_new

    blk = pl.BlockSpec((tr, cols), lambda i: (blk0 + i, 0))
    g_specs, g_args = [], []
    for arr, lead in g_parts:
        g_args.append(arr)
        if lead is None:
            g_specs.append(pl.BlockSpec((tr, cols), lambda i: (i, 0)))
        else:
            g_specs.append(pl.BlockSpec((None, tr, cols), functools.partial(lambda i, p: (p, i, 0), p=lead)))
    out = jax.ShapeDtypeStruct((rows, cols), F32)
    return _pcall(
        body, name=name, grid=(slab // tr,),
        in_specs=[blk, blk, blk] + g_specs + [pl.BlockSpec(memory_space=pl.ANY)] * n_prev,
        out_specs=[blk, blk, blk, blk],
        out_shape=[out, out, out, out],
        input_output_aliases={3 + n + k: k for k in range(n_prev)},
        compiler_params=_params(("parallel",)),
    )(w, m, v, *g_args, *(prev or ()))


def _position():
    return lax.axis_index("x"), lax.axis_index("y"), lax.axis_index("c")


def _block_id(p):
    return 4 * p[0] + 2 * p[1] + p[2]


def _flip(p, mask):
    return tuple((1 - v) if (mask >> (2 - a)) & 1 else v for a, v in enumerate(p))


def _all_gather(xs, *, name):
    n = len(xs)

    def body(*refs):
        x_refs, o_refs = refs[:n], refs[n:2 * n]
        send_sems, recv_sems, local_sems = refs[2 * n:]
        x, y, c = _position()
        me, sibling = (x, y, c), (x, y, 1 - c)
        chips = [(1 - x, y), (x, 1 - y), (1 - x, 1 - y)]

        def copy(t, k, block, to, src=None):
            dst = o_refs[t].at[_block_id(block)]
            return pltpu.make_async_remote_copy(
                src_ref=dst if src is None else src, dst_ref=dst,
                send_sem=send_sems.at[t, k], recv_sem=recv_sems.at[t, k],
                device_id=to, device_id_type=MESH)

        mine = [pltpu.make_async_copy(x_refs[t], o_refs[t].at[_block_id(me)], local_sems.at[t]) for t in range(n)]
        for cp in mine:
            cp.start()
        started = []
        for t in range(n):
            first = [copy(t, 0, me, sibling, src=x_refs[t])]
            first += [copy(t, 1 + j, me, (*chip, c), src=x_refs[t]) for j, chip in enumerate(chips)]
            for cp in first:
                cp.start()
            started += first
        for j, chip in enumerate(chips):
            for t in range(n):
                copy(t, 1 + j, (*chip, c), me).wait_recv()
                fwd = copy(t, 4 + j, (*chip, c), sibling)
                fwd.start()
                started.append(fwd)
        for t in range(n):
            copy(t, 0, sibling, me).wait_recv()
            for j, chip in enumerate(chips):
                copy(t, 4 + j, (*chip, 1 - c), me).wait_recv()
        for cp in started:
            cp.wait_send()
        for cp in mine:
            cp.wait()

    any_spec = pl.BlockSpec(memory_space=pl.ANY)
    return _pcall(
        body, name=name,
        in_specs=[any_spec] * n, out_specs=[any_spec] * n,
        out_shape=[jax.ShapeDtypeStruct((N_DEV,) + a.shape, a.dtype) for a in xs],
        scratch_shapes=[pltpu.SemaphoreType.DMA((n, 7)), pltpu.SemaphoreType.DMA((n, 7)),
                        pltpu.SemaphoreType.DMA((n,))],
    )(*xs)


def _pushed_copy(x_ref, land_ref, send_sems, recv_sems, t, mask, me, chunked, at_receiver):
    peer = _flip(me, mask)
    src = x_ref.at[_block_id(peer)] if chunked else x_ref
    slot = _block_id(peer) if at_receiver else _block_id(me)
    k = (N_DEV - 1) * t + mask - 1
    return pltpu.make_async_remote_copy(
        src_ref=src, dst_ref=land_ref.at[slot], send_sem=send_sems.at[k], recv_sem=recv_sems.at[k],
        device_id=peer, device_id_type=MESH)


def _push_start(xs, chunked, *, name, after=None):
    n = len(xs)
    lands = [lax.empty(a.shape if chunked else (N_DEV,) + a.shape, a.dtype) for a in xs]

    n_after = 0 if after is None else 1

    def body(*refs):
        x_refs, land_refs = refs[:n], refs[n:2 * n]
        send_sems, recv_sems = refs[2 * n + n_after], refs[2 * n + n_after + 1]
        token = refs[4 * n + n_after + 2]
        me = _position()
        for t in range(n):
            for mask in range(1, N_DEV):
                _pushed_copy(x_refs[t], land_refs[t], send_sems, recv_sems, t, mask, me, chunked, False).start()
        token[...] = jnp.zeros_like(token)

    hbm = pl.BlockSpec(memory_space=pltpu.HBM)
    sem = pl.BlockSpec(memory_space=pltpu.SEMAPHORE)
    outs = _pcall(
        body, name=name,
        in_specs=[hbm] * (2 * n) + [pl.BlockSpec(memory_space=pl.ANY)] * n_after,
        out_specs=[sem, sem] + [hbm] * (2 * n) + [pl.BlockSpec(memory_space=pltpu.VMEM)],
        out_shape=[pltpu.SemaphoreType.DMA(((N_DEV - 1) * n,)), pltpu.SemaphoreType.DMA(((N_DEV - 1) * n,))]
        + [pltpu.HBM(a.shape, a.dtype) for a in xs] + [pltpu.HBM(a.shape, a.dtype) for a in lands]
        + [jax.ShapeDtypeStruct((8, 128), F32)],
        input_output_aliases={i: 2 + i for i in range(2 * n)},
        compiler_params=pltpu.CompilerParams(has_side_effects=pltpu.SideEffectType.DATAFLOW_SIDE_EFFECTING),
    )(*[pltpu.with_memory_space_constraint(a, pltpu.HBM) for a in list(xs) + lands], *([after] * n_after))
    return outs[0], outs[1], outs[2:2 + n], outs[2 + n:2 + 2 * n], outs[2 + 2 * n]


def _push_wait(handle, after, chunked, *, name):
    send_sems, recv_sems, xs, lands, _ = handle
    n = len(xs)

    def body(*refs):
        x_refs, land_refs = refs[:n], refs[n:2 * n]
        send_sems, recv_sems = refs[2 * n], refs[2 * n + 1]
        me = _position()
        for t in range(n):
            for mask in range(1, N_DEV):
                _pushed_copy(x_refs[t], land_refs[t], send_sems, recv_sems, t, mask, me, chunked, False).wait_send()
                _pushed_copy(x_refs[t], land_refs[t], send_sems, recv_sems, t, mask, me, chunked, True).wait_recv()

    hbm = pl.BlockSpec(memory_space=pltpu.HBM)
    sem = pl.BlockSpec(memory_space=pltpu.SEMAPHORE)
    outs = _pcall(
        body, name=name,
        in_specs=[hbm] * (2 * n) + [sem, sem, pl.BlockSpec(memory_space=pl.ANY)],
        out_specs=[hbm] * (2 * n),
        out_shape=[pltpu.HBM(a.shape, a.dtype) for a in list(xs) + list(lands)],
        input_output_aliases={i: i for i in range(2 * n)},
        compiler_params=pltpu.CompilerParams(has_side_effects=pltpu.SideEffectType.DATAFLOW_SIDE_EFFECTING),
    )(*xs, *lands, send_sems, recv_sems, after)
    return outs[:n], outs[n:]


def _all_reduce_small(pack, *, name):
    rows = pack.shape[0]

    def body(p_ref, o_ref, gath, send_sems, recv_sems):
        me = _position()
        my_id = _block_id(me)
        gath[my_id] = p_ref[...]
        sent = []
        for mask in range(1, N_DEV):
            peer = _flip(me, mask)
            cp = pltpu.make_async_remote_copy(
                src_ref=p_ref, dst_ref=gath.at[my_id], send_sem=send_sems.at[mask - 1],
                recv_sem=recv_sems.at[mask - 1], device_id=peer, device_id_type=MESH)
            cp.start()
            sent.append(cp)
        for mask in range(1, N_DEV):
            slot = gath.at[_block_id(_flip(me, mask))]
            pltpu.make_async_remote_copy(
                src_ref=slot, dst_ref=slot, send_sem=send_sems.at[mask - 1], recv_sem=recv_sems.at[mask - 1],
                device_id=me, device_id_type=MESH).wait_recv()
        for cp in sent:
            cp.wait_send()
        total = gath[0]
        for s in range(1, N_DEV):
            total = total + gath[s]
        o_ref[...] = total

    vm = pl.BlockSpec(memory_space=pltpu.VMEM)
    return _pcall(
        body, name=name,
        in_specs=[vm], out_specs=vm,
        out_shape=jax.ShapeDtypeStruct(pack.shape, F32),
        scratch_shapes=[pltpu.VMEM((N_DEV, rows, 128), F32), pltpu.SemaphoreType.DMA((7,)),
                        pltpu.SemaphoreType.DMA((7,))],
        compiler_params=pltpu.CompilerParams(vmem_limit_bytes=VMEM_LIMIT),
    )(pack)


def _layout(d, ql, kvl):
    cw = d // 2
    att = d // 2
    lay = {"D": d, "CW": cw, "ATT": att, "H": att // VDIM, "QL": ql, "KVL": kvl}
    lay["A"], lay["B"], lay["GATT"], lay["GCONV"] = 0, cw, 2 * cw, 2 * cw + att
    lay["QC"] = lay["GCONV"] + cw
    lay["KVC"] = lay["QC"] + ql
    lay["KPE"] = lay["KVC"] + kvl
    used = lay["KPE"] + 128
    tn = min(MM_TN, 1024)
    lay["NP"] = -(-used // tn) * tn
    assert att == cw and lay["QC"] % ql == 0 and lay["KVC"] % kvl == 0 and lay["KPE"] % 128 == 0
    lay["o_kv"], lay["o_pe"] = ql, ql + kvl
    lay["o_ga"] = lay["o_pe"] + ROPE
    lay["o_u"] = lay["o_ga"] + att
    lay["o_gc"] = lay["o_u"] + 2 * cw
    lay["IN_COLS"] = lay["o_gc"] + cw
    return lay


def _ungather_cols(g):
    return jnp.transpose(g, (1, 0, 2)).reshape(g.shape[1], -1)


def _to_col_blocks(w):
    r, c = w.shape
    return jnp.transpose(w.reshape(r, N_DEV, c // N_DEV), (1, 0, 2))


def _pad_w_in(w, lay):
    d, cw, att, ql, kvl = lay["D"], lay["CW"], lay["ATT"], lay["QL"], lay["KVL"]
    parts = [w[:, lay["o_u"]:lay["o_u"] + 2 * cw], w[:, lay["o_ga"]:lay["o_ga"] + att],
             w[:, lay["o_gc"]:lay["o_gc"] + cw], w[:, :ql], w[:, lay["o_kv"]:lay["o_kv"] + kvl],
             w[:, lay["o_pe"]:lay["o_pe"] + ROPE],
             jnp.zeros((d, lay["NP"] - lay["KPE"] - ROPE), w.dtype)]
    return jnp.concatenate(parts, axis=1)


def _unpad_w_in(wp, lay):
    cw, att, ql, kvl = lay["CW"], lay["ATT"], lay["QL"], lay["KVL"]
    parts = [wp[:, lay["QC"]:lay["QC"] + ql], wp[:, lay["KVC"]:lay["KVC"] + kvl],
             wp[:, lay["KPE"]:lay["KPE"] + ROPE], wp[:, lay["GATT"]:lay["GATT"] + att],
             wp[:, :2 * cw], wp[:, lay["GCONV"]:lay["GCONV"] + cw]]
    return jnp.concatenate(parts, axis=1)


def _pad_heads(w, h):
    r = w.shape[0]
    return jnp.pad(w.reshape(r, h, QK_DIM), ((0, 0), (0, 0), (0, HEAD_PAD - QK_DIM))).reshape(r, h * HEAD_PAD)


def _unpad_heads(w, h):
    r = w.shape[0]
    return w.reshape(r, h, HEAD_PAD)[:, :, :QK_DIM].reshape(r, h * QK_DIM)


def _rope_tabs(positions):
    half = ROPE // 2
    inv_freq = ROPE_THETA ** (-jnp.arange(half, dtype=F32) / half)
    ang = positions.astype(F32).reshape(-1)[:, None] * inv_freq
    cos, sin = jnp.cos(ang), jnp.sin(ang)
    zero = jnp.zeros_like(cos)
    return (jnp.concatenate([cos, cos, zero, zero], axis=1),
            jnp.concatenate([-sin, zero, zero, zero], axis=1),
            jnp.concatenate([zero, sin, zero, zero], axis=1))


def _pack_rows(vecs):
    rows = []
    for v in vecs:
        flat = v.reshape(-1)
        pad = (-flat.shape[0]) % 1024
        rows.append(jnp.pad(flat, (0, pad)).reshape(-1, 128))
    return jnp.concatenate(rows, axis=0)


def _unpack_rows(pack, shapes):
    out, r0 = [], 0
    for shp in shapes:
        size = math.prod(shp)
        nrows = -(-size // 1024) * 8
        out.append(pack[r0:r0 + nrows].reshape(-1)[:size].reshape(shp))
        r0 += nrows
    return out


def kernel(x, positions, ln_g, w_in, q_a_norm, w_q_up, kv_a_norm, w_kv_up, q_norm, k_norm, w_dw, b_dw, conv_ln_g, conv_ln_b, w_out, loss_target, m_ln_g, m_w_in, m_q_a_norm, m_w_q_up, m_kv_a_norm, m_w_kv_up, m_q_norm, m_k_norm, m_w_dw, m_b_dw, m_conv_ln_g, m_conv_ln_b, m_w_out, v_ln_g, v_w_in, v_q_a_norm, v_w_q_up, v_kv_a_norm, v_w_kv_up, v_q_norm, v_k_norm, v_w_dw, v_b_dw, v_conv_ln_g, v_conv_ln_b, v_w_out):
    nb, seq, d = x.shape
    depth = ln_g.shape[0]
    lay = _layout(d, q_a_norm.shape[1], kv_a_norm.shape[1])
    h, cw, ql, kvl = lay["H"], lay["CW"], lay["QL"], lay["KVL"]
    t = nb * seq
    my_id = _block_id(_position())

    def shards(l):
        return [w_in[l].astype(BF16), w_q_up[l].astype(BF16), w_kv_up[l].astype(BF16), w_out[l].astype(BF16)]

    def fill_own(lands, own):
        return [lax.dynamic_update_index_in_dim(land, blk, my_id, 0) for land, blk in zip(lands, own)]

    def layout_in(g_in):
        return {"in": _pad_w_in(_ungather_cols(g_in), lay)}

    def layout_rest(g_q, g_kv, g_out):
        return {"q": _pad_heads(_ungather_cols(g_q), h), "kv": _ungather_cols(g_kv), "out": g_out.reshape(2 * cw, d)}

    first = shards(0)
    g_in0, g_dw = _all_gather([first[0], w_dw], name="gather_w_in_0")
    gathers = {0: _push_start(first[1:], False, after=g_in0, name="gather_start_0")}
    for l in range(1, depth):
        gathers[l] = _push_start(shards(l), False, after=gathers[l - 1][4], name=f"gather_start_{l}")
    fwd_dep = gathers[depth - 1][4]
    weights = []

    tabs = _rope_tabs(positions)
    gq_pad = jnp.pad(q_norm, ((0, 0), (0, HEAD_PAD - QK_DIM)))
    gk_pad = jnp.pad(k_norm, ((0, 0), (0, HEAD_PAD - QK_DIM)))
    w_dw_all = jnp.transpose(g_dw, (1, 2, 0, 3)).reshape(depth, CONV_K, cw)
    w_dw_all = jnp.pad(w_dw_all, ((0, 0), (0, HALO - CONV_K), (0, 0)))

    saved = []
    xs = x.reshape(t, d)
    for l in range(depth):
        hid = _rms_fwd(xs, ln_g[l], dep=fwd_dep if l == 0 else None, name=f"rms_fwd_{l}")
        if l == 0:
            weights.append(layout_in(g_in0))
        z = _mm(hid, weights[l]["in"], name=f"in_proj_{l}")
        if l == 0:
            own, lands = _push_wait(gathers[0], z, False, name="gather_wait_0")
            weights[0].update(layout_rest(*fill_own(lands, own)))
        wl = weights[l]
        qn, kvn = _lat_fwd(z, q_a_norm[l], kv_a_norm[l], lay, name=f"lat_fwd_{l}")
        q_raw = _mm(qn, wl["q"], name=f"q_up_{l}")
        kv_raw = _mm(kvn, wl["kv"], name=f"kv_up_{l}")
        qh, kh, vh = _heads_fwd(q_raw, kv_raw, z, tabs, gq_pad[l:l + 1], gk_pad[l:l + 1], lay, name=f"heads_fwd_{l}")
        att, mix, lse = _flash_fwd(qh, kh, vh, z, lay, nb, name=f"flash_fwd_{l}")
        mix, u, c_pre = _conv_fwd(z, mix, w_dw_all[l], b_dw[l], conv_ln_g[l], conv_ln_b[l], lay, nb,
                                  name=f"conv_fwd_{l}")
        x_next = _mm(mix, wl["out"], add=xs, name=f"out_proj_{l}")
        saved.append((xs, hid, z, qn, kvn, q_raw, kv_raw, qh, kh, vh, att, lse, mix, u, c_pre))
        xs = x_next
        if l + 1 < depth:
            own, lands = _push_wait(gathers[l + 1], xs, False, name=f"gather_wait_{l + 1}")
            g_in, *g_rest = fill_own(lands, own)
            weights.append({**layout_in(g_in), **layout_rest(*g_rest)})

    sse, dx, dxb = _loss_head(xs, loss_target.reshape(t, d), name="loss_head")
    loss = lax.psum(sse[0, 0] * (0.5 / d), ("x", "y", "c"))

    small = {k: [] for k in ("ln_g", "q_a", "kv_a", "q_n", "k_n", "w_dw", "b_dw", "cln_g", "cln_b")}
    scatters, bwd_dep = {}, None
    for l in reversed(range(depth)):
        xs, hid, z, qn, kvn, q_raw, kv_raw, qh, kh, vh, att, lse, mix, u, c_pre = saved[l]
        wl = weights[l]
        dmix = _mm(dxb, wl["out"], trans_b=True, dep=bwd_dep, name=f"d_mix_{l}")
        dw_out = _mm(mix, dxb, trans_a=True, out_dtype=BF16, name=f"dw_out_{l}")
        dc, dz, dlg, dlb, dbias = _conv_bwd_ln(c_pre, z, dmix, conv_ln_g[l], conv_ln_b[l], lay,
                                               name=f"conv_bwd_ln_{l}")
        dz, dwdw = _conv_bwd_dw(dc, u, z, w_dw_all[l], dz, lay, nb, name=f"conv_bwd_dw_{l}")
        do, delta, dz = _gate_bwd(dmix, att, z, dz, lay, name=f"gate_bwd_{l}")
        dqh, dkh, dvh = _flash_bwd(qh, kh, vh, do, lse, delta, nb, name=f"flash_bwd_{l}")
        dq_raw, dkv_raw, dpe, dgq, dgk = _heads_bwd(q_raw, kv_raw, z, tabs, gq_pad[l:l + 1], gk_pad[l:l + 1],
                                                    dqh, dkh, dvh, lay, name=f"heads_bwd_{l}")
        dqn = _mm(dq_raw, wl["q"], trans_b=True, name=f"d_qn_{l}")
        dkvn = _mm(dkv_raw, wl["kv"], trans_b=True, name=f"d_kvn_{l}")
        dw_q = _mm(qn, dq_raw, trans_a=True, out_dtype=BF16, name=f"dw_q_{l}")
        dw_kv = _mm(kvn, dkv_raw, trans_a=True, out_dtype=BF16, name=f"dw_kv_{l}")
        dz, dgqa, dgkva = _lat_bwd(z, q_a_norm[l], kv_a_norm[l], dqn, dkvn, dpe, dz, lay, name=f"lat_bwd_{l}")
        dw_in = _mm(hid, dz, trans_a=True, out_dtype=BF16, name=f"dw_in_{l}")
        scatters[l] = _push_start(
            [_to_col_blocks(_unpad_w_in(dw_in, lay)), _to_col_blocks(_unpad_heads(dw_q, h)), _to_col_blocks(dw_kv),
             dw_out.reshape(N_DEV, (2 * cw) // N_DEV, d)], True, name=f"scatter_start_{l}")
        bwd_dep = scatters[l][4]
        dh = _mm(dz, wl["in"], trans_b=True, dep=bwd_dep, name=f"d_hid_{l}")
        dx, dxb, dlng = _rms_bwd(xs, ln_g[l], dh, dx, name=f"rms_bwd_{l}")
        for key, val in (("ln_g", dlng), ("q_a", dgqa), ("kv_a", dgkva), ("q_n", dgq[:, :QK_DIM]),
                         ("k_n", dgk[:, :QK_DIM]), ("w_dw", dwdw[:CONV_K]), ("b_dw", dbias),
                         ("cln_g", dlg), ("cln_b", dlb)):
            small[key].append(val)
    grad_x = dx.reshape(nb, seq, d)
    for key in small:
        small[key] = jnp.stack(small[key][::-1])

    small_names = ("ln_g", "q_a", "kv_a", "q_n", "k_n", "b_dw", "cln_g", "cln_b", "w_dw")
    small_shapes = [small[k].shape for k in small_names]
    summed = _unpack_rows(_all_reduce_small(_pack_rows([small[k] for k in small_names]), name="reduce_small_grads"),
                          small_shapes)
    sg = dict(zip(small_names, summed))
    g_w_dw = lax.dynamic_slice_in_dim(sg["w_dw"], my_id * (cw // N_DEV), cw // N_DEV, axis=2)

    def adam_small(ws, ms, vs, gs, nm):
        shapes = [w.shape for w in ws]
        outs = _adam(_pack_rows(ws), _pack_rows(ms), _pack_rows(vs), [(_pack_rows(gs), None)], name=nm)
        return [_unpack_rows(o, shapes) for o in outs]

    big = [("w_in", w_in, m_w_in, v_w_in), ("w_q_up", w_q_up, m_w_q_up, v_w_q_up),
           ("w_kv_up", w_kv_up, m_w_kv_up, v_w_kv_up), ("w_out", w_out, m_w_out, v_w_out)]
    res, prev = {}, [None] * len(big)
    for l in reversed(range(depth)):
        own, lands = _push_wait(scatters[l], dx, True, name=f"scatter_wait_{l}")
        own = [lax.dynamic_index_in_dim(o, my_id, 0, keepdims=False) for o in own]
        for idx, ((nm, w, m, v), recv) in enumerate(zip(big, fill_own(lands, own))):
            rows, cols = w.shape[1], w.shape[2]
            flat = lambda a: a.reshape(depth * rows, cols)
            prev[idx] = _adam(flat(w), flat(m), flat(v), [(recv, s) for s in range(N_DEV)], layer=l, layers=depth,
                              prev=prev[idx], name=f"adam_{nm}_{l}")
    for idx, (nm, w, _, _) in enumerate(big):
        res[nm] = [o.reshape(w.shape) for o in prev[idx]]
    names_s = ["ln_g", "q_a_norm", "kv_a_norm", "q_norm", "k_norm", "w_dw", "b_dw", "conv_ln_g", "conv_ln_b"]
    ws = [ln_g, q_a_norm, kv_a_norm, q_norm, k_norm, w_dw, b_dw, conv_ln_g, conv_ln_b]
    ms = [m_ln_g, m_q_a_norm, m_kv_a_norm, m_q_norm, m_k_norm, m_w_dw, m_b_dw, m_conv_ln_g, m_conv_ln_b]
    vs = [v_ln_g, v_q_a_norm, v_kv_a_norm, v_q_norm, v_k_norm, v_w_dw, v_b_dw, v_conv_ln_g, v_conv_ln_b]
    gs = [sg["ln_g"].reshape(ln_g.shape), sg["q_a"].reshape(q_a_norm.shape), sg["kv_a"].reshape(kv_a_norm.shape),
          sg["q_n"].reshape(q_norm.shape), sg["k_n"].reshape(k_norm.shape), g_w_dw,
          sg["b_dw"].reshape(b_dw.shape), sg["cln_g"].reshape(conv_ln_g.shape), sg["cln_b"].reshape(conv_ln_b.shape)]
    outs_s = adam_small(ws, ms, vs, gs, "adam_small")
    for idx, nm in enumerate(names_s):
        res[nm] = [outs_s[k][idx] for k in range(4)]

    order = ["ln_g", "w_in", "q_a_norm", "w_q_up", "kv_a_norm", "w_kv_up", "q_norm", "k_norm", "w_dw", "b_dw",
             "conv_ln_g", "conv_ln_b", "w_out"]
    return (loss, grad_x, *[res[nm][0] for nm in order], *[res[nm][1] for nm in order],
            *[res[nm][2] for nm in order], *[res[nm][3] for nm in order])
```

```python
import functools
import math

import jax
import jax.numpy as jnp
from jax import lax
from jax.experimental import pallas as pl
from jax.experimental.pallas import tpu as pltpu

F32 = jnp.float32
BF16 = jnp.bfloat16
MESH = pl.DeviceIdType.MESH

N_DEV = 8
NOPE = 128
ROPE = 64
VDIM = 128
HEAD_PAD = 256
QK_DIM = NOPE + ROPE
CONV_K = 31
HALO = 32
EPS = 1e-6
ROPE_THETA = 10000.0
NEG = -1e30

ADAM_LR = 0.001
ADAM_B1 = 0.9
ADAM_B2 = 0.999
ADAM_EPS = 1e-08
ADAM_WD = 0.01
ADAM_STEP = 10

TOK_TILE = 256
CONV_TILE = 256
ATT_TQ = 512
HEAD_GROUP = 4
MM_TM = 1024
MM_TN = 512
MM_TK = 4096
MM_VMEM_BUDGET = 46 * 1024 * 1024
ADAM_BLOCK_ELEMS = 128 * 1024
LANE_CHUNK = 256
VMEM_LIMIT = 56 * 1024 * 1024


def _pcall(body, **kw):
    return pl.pallas_call(body, **kw)


def _tile(dim, pref, mult):
    t = min(pref, dim)
    t -= t % mult
    while t >= mult:
        if dim % t == 0:
            return t
        t -= mult
    return dim


def _params(sem):
    return pltpu.CompilerParams(dimension_semantics=sem, vmem_limit_bytes=VMEM_LIMIT)


def _sigmoid(v):
    return 1.0 / (1.0 + jnp.exp(-v))


def _dsilu(v, sg):
    return sg * (1.0 + v * (1.0 - sg))


def _mm_tiles(m, n, kdim, out_bytes, has_add):
    def need(tm, tn, tk):
        return (2 * 2 * (tm * tk + tk * tn) + 2 * tm * tn * out_bytes
                + tm * tn * 4 * ((kdim > tk) + 2 * has_add + 1))

    shapes = [(_tile(m, pm, 128), _tile(n, pn, 128))
              for pm, pn in ((MM_TM, MM_TN), (MM_TM // 2, MM_TN), (MM_TM // 2, MM_TN // 2), (MM_TM // 4, MM_TN // 2))]
    for tm, tn in shapes:
        if need(tm, tn, kdim) <= MM_VMEM_BUDGET:
            return tm, tn, kdim
    tk = _tile(kdim, MM_TK, 128)
    for tm, tn in shapes:
        if need(tm, tn, tk) <= MM_VMEM_BUDGET:
            break
    return tm, tn, tk


def _mm(a, b, *, name, trans_a=False, trans_b=False, add=None, out_dtype=F32, dep=None):
    assert not (trans_a and trans_b)
    if trans_a:
        kdim, m = a.shape
    else:
        m, kdim = a.shape
    n = b.shape[0] if trans_b else b.shape[1]
    assert b.shape[1 if trans_b else 0] == kdim
    has_add = add is not None
    tm, tn, tk = _mm_tiles(m, n, kdim, jnp.dtype(out_dtype).itemsize, has_add)
    nk = kdim // tk
    contract = (((0 if trans_a else 1,), (1 if trans_b else 0,)), ((), ()))

    def product(a_ref, b_ref):
        return lax.dot_general(a_ref[...], b_ref[...], contract, preferred_element_type=F32)

    def body(*refs):
        a_ref, b_ref = refs[:2]
        add_ref = refs[2] if has_add else None
        o_ref = refs[2 + has_add + (dep is not None)]

        def finish(r):
            if has_add:
                r = r + add_ref[...]
            o_ref[...] = r.astype(o_ref.dtype)

        if nk == 1:
            finish(product(a_ref, b_ref))
            return
        acc_ref = refs[-1]
        k = pl.program_id(2)

        @pl.when(k == 0)
        def _():
            acc_ref[...] = product(a_ref, b_ref)

        @pl.when((k > 0) & (k < nk - 1))
        def _():
            acc_ref[...] += product(a_ref, b_ref)

        @pl.when(k == nk - 1)
        def _():
            finish(acc_ref[...] + product(a_ref, b_ref))

    if trans_a:
        a_spec = pl.BlockSpec((tk, tm), lambda i, j, k: (k, i))
    else:
        a_spec = pl.BlockSpec((tm, tk), lambda i, j, k: (i, k))
    if trans_b:
        b_spec = pl.BlockSpec((tn, tk), lambda i, j, k: (j, k))
    else:
        b_spec = pl.BlockSpec((tk, tn), lambda i, j, k: (k, j))
    in_specs = [a_spec, b_spec]
    args = [a, b]
    if has_add:
        in_specs.append(pl.BlockSpec((tm, tn), lambda i, j, k: (i, j)))
        args.append(add)
    if dep is not None:
        in_specs.append(pl.BlockSpec(memory_space=pl.ANY))
        args.append(dep)
    return _pcall(
        body, name=name,
        grid=(m // tm, n // tn, nk),
        in_specs=in_specs,
        out_specs=pl.BlockSpec((tm, tn), lambda i, j, k: (i, j)),
        out_shape=jax.ShapeDtypeStruct((m, n), out_dtype),
        scratch_shapes=[pltpu.VMEM((tm, tn), F32)] if nk > 1 else [],
        compiler_params=_params(("parallel", "parallel", "arbitrary")),
    )(*args)


def _rms_fwd(x, g, *, name, dep=None):
    t, d = x.shape
    tt = _tile(t, TOK_TILE, 16)

    def body(x_ref, g_ref, *rest):
        h_ref = rest[-1]
        xv = x_ref[...]
        r = lax.rsqrt(jnp.mean(xv * xv, axis=-1, keepdims=True) + EPS)
        h_ref[...] = (xv * r * g_ref[...]).astype(BF16)

    deps = [] if dep is None else [dep]
    return _pcall(
        body, name=name, grid=(t // tt,),
        in_specs=[pl.BlockSpec((tt, d), lambda i: (i, 0)), pl.BlockSpec((1, d), lambda i: (0, 0))]
        + [pl.BlockSpec(memory_space=pl.ANY)] * len(deps),
        out_specs=pl.BlockSpec((tt, d), lambda i: (i, 0)),
        out_shape=jax.ShapeDtypeStruct((t, d), BF16),
        compiler_params=_params(("parallel",)),
    )(x, g.reshape(1, d), *deps)


def _rms_bwd(x, g, dh, dres, *, name):
    t, d = x.shape
    tt = _tile(t, TOK_TILE, 16)

    def body(x_ref, g_ref, dh_ref, dres_ref, dx_ref, dxb_ref, dg_ref):
        xv = x_ref[...]
        r = lax.rsqrt(jnp.mean(xv * xv, axis=-1, keepdims=True) + EPS)
        dy = dh_ref[...]
        dyg = dy * g_ref[...]
        dot = jnp.sum(dyg * xv, axis=-1, keepdims=True) * (1.0 / d)
        dx = dres_ref[...] + r * dyg - xv * (r * r * r) * dot
        dx_ref[...] = dx
        dxb_ref[...] = dx.astype(BF16)

        @pl.when(pl.program_id(0) == 0)
        def _():
            dg_ref[...] = jnp.zeros_like(dg_ref)

        dg_ref[...] += jnp.sum(dy * xv * r, axis=0, keepdims=True)

    row = pl.BlockSpec((tt, d), lambda i: (i, 0))
    vec = pl.BlockSpec((1, d), lambda i: (0, 0))
    return _pcall(
        body, name=name, grid=(t // tt,),
        in_specs=[row, vec, row, row],
        out_specs=[row, row, vec],
        out_shape=[jax.ShapeDtypeStruct((t, d), F32), jax.ShapeDtypeStruct((t, d), BF16),
                   jax.ShapeDtypeStruct((1, d), F32)],
        compiler_params=_params(("arbitrary",)),
    )(x, g.reshape(1, d), dh, dres)


def _lat_fwd(z, gq, gkv, lay, *, name):
    t = z.shape[0]
    ql, kvl = lay["QL"], lay["KVL"]
    tt = _tile(t, TOK_TILE, 16)

    def body(q_ref, kv_ref, gq_ref, gkv_ref, qn_ref, kvn_ref):
        for src, g_ref, dst in ((q_ref, gq_ref, qn_ref), (kv_ref, gkv_ref, kvn_ref)):
            v = src[...].astype(F32)
            r = lax.rsqrt(jnp.mean(v * v, axis=-1, keepdims=True) + EPS)
            dst[...] = (v * r * g_ref[...]).astype(BF16)

    return _pcall(
        body, name=name, grid=(t // tt,),
        in_specs=[pl.BlockSpec((tt, ql), lambda i: (i, lay["QC"] // ql)),
                  pl.BlockSpec((tt, kvl), lambda i: (i, lay["KVC"] // kvl)),
                  pl.BlockSpec((1, ql), lambda i: (0, 0)), pl.BlockSpec((1, kvl), lambda i: (0, 0))],
        out_specs=[pl.BlockSpec((tt, ql), lambda i: (i, 0)), pl.BlockSpec((tt, kvl), lambda i: (i, 0))],
        out_shape=[jax.ShapeDtypeStruct((t, ql), BF16), jax.ShapeDtypeStruct((t, kvl), BF16)],
        compiler_params=_params(("parallel",)),
    )(z, z, gq.reshape(1, ql), gkv.reshape(1, kvl))


def _lat_bwd(z, gq, gkv, dqn, dkvn, dpe, dz, lay, *, name):
    t = z.shape[0]
    ql, kvl = lay["QL"], lay["KVL"]
    tail = lay["NP"] - lay["QC"]
    assert lay["QC"] % tail == 0
    tt = _tile(t, TOK_TILE, 16)

    def body(q_ref, kv_ref, gq_ref, gkv_ref, dqn_ref, dkvn_ref, dpe_ref, dz_in, tail_ref, dgq_ref, dgkv_ref):
        del dz_in
        first = pl.program_id(0) == 0
        for src, g_ref, dy_ref, c0, dg_ref in ((q_ref, gq_ref, dqn_ref, 0, dgq_ref),
                                               (kv_ref, gkv_ref, dkvn_ref, ql, dgkv_ref)):
            v = src[...].astype(F32)
            n = v.shape[-1]
            r = lax.rsqrt(jnp.mean(v * v, axis=-1, keepdims=True) + EPS)
            dy = dy_ref[...]
            dyg = dy * g_ref[...]
            dot = jnp.sum(dyg * v, axis=-1, keepdims=True) * (1.0 / n)
            tail_ref[:, c0:c0 + n] = (r * dyg - v * (r * r * r) * dot).astype(BF16)

            @pl.when(first)
            def _():
                dg_ref[...] = jnp.zeros_like(dg_ref)

            dg_ref[...] += jnp.sum(dy * v * r, axis=0, keepdims=True)
        tail_ref[:, ql + kvl:ql + kvl + 128] = dpe_ref[...].astype(BF16)
        tail_ref[:, ql + kvl + 128:tail] = jnp.zeros((tt, tail - ql - kvl - 128), BF16)

    return _pcall(
        body, name=name, grid=(t // tt,),
        in_specs=[pl.BlockSpec((tt, ql), lambda i: (i, lay["QC"] // ql)),
                  pl.BlockSpec((tt, kvl), lambda i: (i, lay["KVC"] // kvl)),
                  pl.BlockSpec((1, ql), lambda i: (0, 0)), pl.BlockSpec((1, kvl), lambda i: (0, 0)),
                  pl.BlockSpec((tt, ql), lambda i: (i, 0)), pl.BlockSpec((tt, kvl), lambda i: (i, 0)),
                  pl.BlockSpec((tt, 128), lambda i: (i, 0)), pl.BlockSpec(memory_space=pl.ANY)],
        out_specs=[pl.BlockSpec((tt, tail), lambda i: (i, lay["QC"] // tail)),
                   pl.BlockSpec((1, ql), lambda i: (0, 0)), pl.BlockSpec((1, kvl), lambda i: (0, 0))],
        out_shape=[jax.ShapeDtypeStruct(dz.shape, BF16),
                   jax.ShapeDtypeStruct((1, ql), F32), jax.ShapeDtypeStruct((1, kvl), F32)],
        input_output_aliases={7: 0},
        compiler_params=_params(("arbitrary",)),
    )(z, z, gq.reshape(1, ql), gkv.reshape(1, kvl), dqn, dkvn, dpe, dz)


def _rope(r, c_tab, sa_tab, sb_tab):
    return r * c_tab + pltpu.roll(r, 96, 1) * sa_tab + pltpu.roll(r, 32, 1) * sb_tab


def _rope_t(d, c_tab, sa_tab, sb_tab):
    return d * c_tab + pltpu.roll(d * sa_tab, 32, 1) + pltpu.roll(d * sb_tab, 96, 1)


def _heads_fwd(q_raw, kv_raw, z, tabs, gq, gk, lay, *, name):
    t = z.shape[0]
    h = lay["H"]
    tt = _tile(t, TOK_TILE, 16)
    hg = _tile(h, HEAD_GROUP, 1)
    scale = 1.0 / math.sqrt(QK_DIM)

    def body(q_ref, kv_ref, pe_ref, c_ref, sa_ref, sb_ref, gq_ref, gk_ref, qh_ref, kh_ref, vh_ref):
        c_tab, sa_tab, sb_tab = c_ref[...], sa_ref[...], sb_ref[...]
        pe, gq_v, gk_v = pe_ref[...].astype(F32), gq_ref[...], gk_ref[...]
        ss_pe = jnp.sum(pe * pe, axis=-1, keepdims=True)
        for g in range(hg):
            q = q_ref[:, g * HEAD_PAD:(g + 1) * HEAD_PAD].astype(F32)
            r = lax.rsqrt(jnp.sum(q * q, axis=-1, keepdims=True) * (1.0 / QK_DIM) + EPS)
            qn = q * r * gq_v
            qh_ref[g] = (jnp.concatenate([qn[:, :NOPE], _rope(qn[:, NOPE:], c_tab, sa_tab, sb_tab)], axis=1)
                         * scale).astype(BF16)
            kv = kv_ref[:, g * HEAD_PAD:(g + 1) * HEAD_PAD].astype(F32)
            kn = kv[:, :NOPE]
            rk = lax.rsqrt((jnp.sum(kn * kn, axis=-1, keepdims=True) + ss_pe) * (1.0 / QK_DIM) + EPS)
            kh_ref[g] = jnp.concatenate(
                [kn * rk * gk_v[:, :NOPE], _rope(pe * rk * gk_v[:, NOPE:], c_tab, sa_tab, sb_tab)],
                axis=1).astype(BF16)
            vh_ref[g] = kv[:, NOPE:].astype(BF16)

    head = pl.BlockSpec((tt, hg * HEAD_PAD), lambda i, j: (i, j))
    tab = pl.BlockSpec((tt, 128), lambda i, j: (i, 0))
    gain = pl.BlockSpec((1, HEAD_PAD), lambda i, j: (0, 0))
    return _pcall(
        body, name=name, grid=(t // tt, h // hg),
        in_specs=[head, head, pl.BlockSpec((tt, 128), lambda i, j: (i, lay["KPE"] // 128)), tab, tab, tab, gain, gain],
        out_specs=[pl.BlockSpec((hg, tt, HEAD_PAD), lambda i, j: (j, i, 0)),
                   pl.BlockSpec((hg, tt, HEAD_PAD), lambda i, j: (j, i, 0)),
                   pl.BlockSpec((hg, tt, VDIM), lambda i, j: (j, i, 0))],
        out_shape=[jax.ShapeDtypeStruct((h, t, HEAD_PAD), BF16), jax.ShapeDtypeStruct((h, t, HEAD_PAD), BF16),
                   jax.ShapeDtypeStruct((h, t, VDIM), BF16)],
        compiler_params=_params(("parallel", "parallel")),
    )(q_raw, kv_raw, z, *tabs, gq, gk)


def _heads_bwd(q_raw, kv_raw, z, tabs, gq, gk, dqh, dkh, dvh, lay, *, name):
    t = z.shape[0]
    h = lay["H"]
    tt = _tile(t, TOK_TILE, 16)
    hg = _tile(h, HEAD_GROUP, 1)
    scale = 1.0 / math.sqrt(QK_DIM)

    def body(q_ref, kv_ref, pe_ref, c_ref, sa_ref, sb_ref, gq_ref, gk_ref, dqh_ref, dkh_ref, dvh_ref,
             dq_ref, dkv_ref, dpe_ref, dgq_ref, dgk_ref):
        i, j = pl.program_id(0), pl.program_id(1)
        c_tab, sa_tab, sb_tab = c_ref[...], sa_ref[...], sb_ref[...]

        @pl.when((i == 0) & (j == 0))
        def _():
            dgq_ref[...] = jnp.zeros_like(dgq_ref)
            dgk_ref[...] = jnp.zeros_like(dgk_ref)

        @pl.when(j == 0)
        def _():
            dpe_ref[...] = jnp.zeros_like(dpe_ref)

        def norm_bwd(v, g, dy):
            r = lax.rsqrt(jnp.sum(v * v, axis=-1, keepdims=True) * (1.0 / QK_DIM) + EPS)
            dyg = dy * g
            dot = jnp.sum(dyg * v, axis=-1, keepdims=True) * (1.0 / QK_DIM)
            return r * dyg - v * (r * r * r) * dot, jnp.sum(dy * v * r, axis=0, keepdims=True)

        pe, gq_v, gk_v = pe_ref[...].astype(F32), gq_ref[...], gk_ref[...]
        dpe, dgq, dgk = jnp.zeros_like(pe), jnp.zeros_like(gq_v), jnp.zeros_like(gk_v)
        for g in range(hg):
            cols = slice(g * HEAD_PAD, (g + 1) * HEAD_PAD)
            dqo = dqh_ref[g].astype(F32) * scale
            dy = jnp.concatenate([dqo[:, :NOPE], _rope_t(dqo[:, NOPE:], c_tab, sa_tab, sb_tab)], axis=1)
            dq, dg = norm_bwd(q_ref[:, cols].astype(F32), gq_v, dy)
            dq_ref[:, cols] = dq.astype(BF16)
            dgq = dgq + dg

            dko = dkh_ref[g].astype(F32)
            dy = jnp.concatenate([dko[:, :NOPE], _rope_t(dko[:, NOPE:], c_tab, sa_tab, sb_tab)], axis=1)
            kfull = jnp.concatenate([kv_ref[:, cols].astype(F32)[:, :NOPE], pe], axis=1)
            dk, dg = norm_bwd(kfull, gk_v, dy)
            dkv_ref[:, cols] = jnp.concatenate([dk[:, :NOPE].astype(BF16), dvh_ref[g]], axis=1)
            dpe = dpe + dk[:, NOPE:]
            dgk = dgk + dg
        dpe_ref[...] += dpe
        dgq_ref[...] += dgq
        dgk_ref[...] += dgk

    head = pl.BlockSpec((tt, hg * HEAD_PAD), lambda i, j: (i, j))
    tab = pl.BlockSpec((tt, 128), lambda i, j: (i, 0))
    gain = pl.BlockSpec((1, HEAD_PAD), lambda i, j: (0, 0))
    hm = pl.BlockSpec((hg, tt, HEAD_PAD), lambda i, j: (j, i, 0))
    return _pcall(
        body, name=name, grid=(t // tt, h // hg),
        in_specs=[head, head, pl.BlockSpec((tt, 128), lambda i, j: (i, lay["KPE"] // 128)), tab, tab, tab, gain, gain,
                  hm, hm, pl.BlockSpec((hg, tt, VDIM), lambda i, j: (j, i, 0))],
        out_specs=[head, head, tab, gain, gain],
        out_shape=[jax.ShapeDtypeStruct((t, h * HEAD_PAD), BF16), jax.ShapeDtypeStruct((t, h * HEAD_PAD), BF16),
                   jax.ShapeDtypeStruct((t, 128), F32),
                   jax.ShapeDtypeStruct((1, HEAD_PAD), F32), jax.ShapeDtypeStruct((1, HEAD_PAD), F32)],
        compiler_params=_params(("arbitrary", "arbitrary")),
    )(q_raw, kv_raw, z, *tabs, gq, gk, dqh, dkh, dvh)


def _lower_triangle(n):
    return lax.broadcasted_iota(jnp.int32, (n, n), 1) <= lax.broadcasted_iota(jnp.int32, (n, n), 0)


def _qk(q, k):
    return lax.dot_general(q, k, (((1,), (1,)), ((), ())), preferred_element_type=F32)


def _flash_fwd(qh, kh, vh, z, lay, nb, *, name):
    h, t, _ = qh.shape
    s = t // nb
    tq = _tile(s, ATT_TQ, 128)
    nq = s // tq
    att_w = h * VDIM
    gblk = lay["GATT"] // VDIM

    def body(q_ref, k_ref, v_ref, g_ref, att_ref, mix_ref, lse_ref):
        i = pl.program_id(2)
        tri = _lower_triangle(tq)
        for blk in range(nq):
            @pl.when(i == blk)
            def _():
                q = q_ref[...]
                pre = blk * tq
                sd = jnp.where(tri, _qk(q, k_ref[pre:pre + tq, :]), NEG)
                m = jnp.max(sd, axis=-1, keepdims=True)
                if pre:
                    sp = _qk(q, k_ref[0:pre, :])
                    m = jnp.maximum(m, jnp.max(sp, axis=-1, keepdims=True))
                pd = jnp.exp(sd - m)
                l = jnp.sum(pd, axis=-1, keepdims=True)
                acc = jnp.dot(pd.astype(BF16), v_ref[pre:pre + tq, :], preferred_element_type=F32)
                if pre:
                    pp = jnp.exp(sp - m)
                    l = l + jnp.sum(pp, axis=-1, keepdims=True)
                    acc = acc + jnp.dot(pp.astype(BF16), v_ref[0:pre, :], preferred_element_type=F32)
                o = acc / l
                att_ref[...] = o
                g = g_ref[...].astype(F32)
                mix_ref[...] = (o * (g * _sigmoid(g))).astype(BF16)
                lse_ref[...] = m + jnp.log(l)

    row = lambda hh, b, i: (b * nq + i, hh)
    seq = lambda hh, b, i: (hh, b, 0)
    return _pcall(
        body, name=name, grid=(h, nb, nq),
        in_specs=[pl.BlockSpec((None, tq, HEAD_PAD), lambda hh, b, i: (hh, b * nq + i, 0)),
                  pl.BlockSpec((None, s, HEAD_PAD), seq),
                  pl.BlockSpec((None, s, VDIM), seq),
                  pl.BlockSpec((tq, VDIM), lambda hh, b, i: (b * nq + i, gblk + hh))],
        out_specs=[pl.BlockSpec((tq, VDIM), row), pl.BlockSpec((tq, VDIM), row),
                   pl.BlockSpec((None, tq, 1), lambda hh, b, i: (hh, b * nq + i, 0))],
        out_shape=[jax.ShapeDtypeStruct((t, att_w), F32), jax.ShapeDtypeStruct((t, 2 * att_w), BF16),
                   jax.ShapeDtypeStruct((h, t, 1), F32)],
        compiler_params=_params(("parallel", "parallel", "parallel")),
    )(qh, kh, vh, z)


def _gate_bwd(dmix, att, z, dz, lay, *, name):
    t, att_w = att.shape
    h = att_w // VDIM
    tt = _tile(t, TOK_TILE, 16)
    gblk = lay["GATT"] // att_w

    def body(dm_ref, o_ref, g_ref, dz_in, do_ref, delta_ref, dg_ref):
        del dz_in
        dm, o, g = dm_ref[...], o_ref[...], g_ref[...].astype(F32)
        sg = _sigmoid(g)
        do = dm * (g * sg)
        do_ref[...] = do.astype(BF16)
        prod = do * o
        for hh in range(h):
            delta_ref[hh] = jnp.sum(prod[:, hh * VDIM:(hh + 1) * VDIM], axis=-1, keepdims=True)
        dg_ref[...] = (dm * o * _dsilu(g, sg)).astype(BF16)

    blk = pl.BlockSpec((tt, att_w), lambda i: (i, 0))
    gate = pl.BlockSpec((tt, att_w), lambda i: (i, gblk))
    return _pcall(
        body, name=name, grid=(t // tt,),
        in_specs=[blk, blk, gate, pl.BlockSpec(memory_space=pl.ANY)],
        out_specs=[blk, pl.BlockSpec((h, tt, 1), lambda i: (0, i, 0)), gate],
        out_shape=[jax.ShapeDtypeStruct((t, att_w), BF16), jax.ShapeDtypeStruct((h, t, 1), F32),
                   jax.ShapeDtypeStruct(dz.shape, BF16)],
        input_output_aliases={3: 2},
        compiler_params=_params(("parallel",)),
    )(dmix, att, z, dz)


def _flash_bwd(qh, kh, vh, do, lse, delta, nb, *, name):
    h, t, _ = qh.shape
    s = t // nb
    tk = _tile(s, ATT_TQ, 128)
    nk = s // tk
    tn_dims = (((0,), (0,)), ((), ()))

    def body(q_ref, k_ref, v_ref, do_ref, lse_ref, dl_ref, dq_out, dk_ref, dv_ref, dq_ref):
        j = pl.program_id(2)
        tri = _lower_triangle(tk)

        @pl.when(j == 0)
        def _():
            dq_ref[...] = jnp.zeros_like(dq_ref)

        def rows_against_block(r0, r1, masked):
            q, do_v = q_ref[r0:r1, :], do_ref[r0:r1, :]
            k = k_ref[...]
            sc = _qk(q, k)
            if masked:
                sc = jnp.where(tri, sc, NEG)
            p = jnp.exp(sc - lse_ref[r0:r1, :])
            dv = lax.dot_general(p.astype(BF16), do_v, tn_dims, preferred_element_type=F32)
            ds = (p * (_qk(do_v, v_ref[...]) - dl_ref[r0:r1, :])).astype(BF16)
            dq_ref[r0:r1, :] += jnp.dot(ds, k, preferred_element_type=F32)
            return lax.dot_general(ds, q, tn_dims, preferred_element_type=F32), dv

        for blk in range(nk):
            @pl.when(j == blk)
            def _():
                r0 = blk * tk
                dk, dv = rows_against_block(r0, r0 + tk, True)
                if r0 + tk < s:
                    dk2, dv2 = rows_against_block(r0 + tk, s, False)
                    dk, dv = dk + dk2, dv + dv2
                dk_ref[...] = dk.astype(BF16)
                dv_ref[...] = dv.astype(BF16)

        @pl.when(j == nk - 1)
        def _():
            dq_out[...] = dq_ref[...].astype(BF16)

    seq = lambda hh, b, j: (hh, b, 0)
    kv = lambda hh, b, j: (hh, b * nk + j, 0)
    return _pcall(
        body, name=name, grid=(h, nb, nk),
        in_specs=[pl.BlockSpec((None, s, HEAD_PAD), seq),
                  pl.BlockSpec((None, tk, HEAD_PAD), kv),
                  pl.BlockSpec((None, tk, VDIM), kv),
                  pl.BlockSpec((s, VDIM), lambda hh, b, j: (b, hh)),
                  pl.BlockSpec((None, s, 1), seq), pl.BlockSpec((None, s, 1), seq)],
        out_specs=[pl.BlockSpec((None, s, HEAD_PAD), seq), pl.BlockSpec((None, tk, HEAD_PAD), kv),
                   pl.BlockSpec((None, tk, VDIM), kv)],
        out_shape=[jax.ShapeDtypeStruct((h, t, HEAD_PAD), BF16), jax.ShapeDtypeStruct((h, t, HEAD_PAD), BF16),
                   jax.ShapeDtypeStruct((h, t, VDIM), BF16)],
        scratch_shapes=[pltpu.VMEM((s, HEAD_PAD), F32)],
        compiler_params=_params(("parallel", "parallel", "arbitrary")),
    )(qh, kh, vh, do, lse, delta)


SUBLANES = 8


def _stage_row_shifts(ext, sh, c0, lc, rows):
    for r in range(1, SUBLANES):
        sh[r - 1, 0:rows, :] = ext[r:r + rows, c0:c0 + lc]


def _row_window(ext, sh, c0, lc, off, n):
    r = off % SUBLANES
    if r == 0:
        return ext[off:off + n, c0:c0 + lc]
    return sh[r - 1, off - r:off - r + n, :]


def _conv_fwd(z, mix, w_dw, b_dw, ln_g, ln_b, lay, nb, *, name):
    t = z.shape[0]
    cw = lay["CW"]
    s = t // nb
    tt = _tile(s, CONV_TILE, HALO)
    ns = s // tt
    hb = tt // HALO
    lc = _tile(cw, LANE_CHUNK, 128)

    def body(a_ref, b_ref, ap_ref, bp_ref, gc_ref, w_ref, bias_ref, lg_ref, lb_ref, mix_in, mix_ref, u_ref, c_ref,
             ext, sh):
        del mix_in
        i = pl.program_id(1)
        u = a_ref[...].astype(F32) * _sigmoid(b_ref[...].astype(F32))
        u_ref[...] = u
        ext[0:HALO, :] = jnp.where(i > 0, ap_ref[...].astype(F32) * _sigmoid(bp_ref[...].astype(F32)), 0.0)
        ext[HALO:HALO + tt, :] = u
        for c0 in range(0, cw, lc):
            _stage_row_shifts(ext, sh, c0, lc, tt + HALO - SUBLANES)
            acc = jnp.zeros((tt, lc), F32) + bias_ref[:, c0:c0 + lc]
            for k in range(CONV_K):
                off = HALO - (CONV_K - 1) + k
                acc = acc + w_ref[k:k + 1, c0:c0 + lc] * _row_window(ext, sh, c0, lc, off, tt)
            c_ref[:, c0:c0 + lc] = acc
        c = c_ref[...]
        mu = jnp.mean(c, axis=-1, keepdims=True)
        xc = c - mu
        var = jnp.mean(xc * xc, axis=-1, keepdims=True)
        y = xc * lax.rsqrt(var + EPS) * lg_ref[...] + lb_ref[...]
        g = gc_ref[...].astype(F32)
        mix_ref[...] = (y * _sigmoid(y) * (g * _sigmoid(g))).astype(BF16)

    cur = lambda col: pl.BlockSpec((tt, cw), lambda b, i: (b * ns + i, col))
    prev = lambda col: pl.BlockSpec((HALO, cw), lambda b, i: (jnp.maximum((b * ns + i) * hb - 1, 0), col))
    vec = pl.BlockSpec((1, cw), lambda b, i: (0, 0))
    out_row = pl.BlockSpec((tt, cw), lambda b, i: (b * ns + i, 0))
    return _pcall(
        body, name=name, grid=(nb, ns),
        in_specs=[cur(lay["A"] // cw), cur(lay["B"] // cw), prev(lay["A"] // cw), prev(lay["B"] // cw),
                  cur(lay["GCONV"] // cw), pl.BlockSpec((HALO, cw), lambda b, i: (0, 0)), vec, vec, vec,
                  pl.BlockSpec(memory_space=pl.ANY)],
        out_specs=[pl.BlockSpec((tt, cw), lambda b, i: (b * ns + i, 1)), out_row, out_row],
        out_shape=[jax.ShapeDtypeStruct(mix.shape, BF16), jax.ShapeDtypeStruct((t, cw), F32),
                   jax.ShapeDtypeStruct((t, cw), F32)],
        scratch_shapes=[pltpu.VMEM((tt + HALO, cw), F32),
                        pltpu.VMEM((SUBLANES - 1, tt + HALO - SUBLANES, lc), F32)],
        input_output_aliases={9: 0},
        compiler_params=_params(("parallel", "parallel")),
    )(z, z, z, z, z, w_dw, b_dw.reshape(1, cw), ln_g.reshape(1, cw), ln_b.reshape(1, cw), mix)


def _conv_bwd_ln(c_pre, z, dmix, ln_g, ln_b, lay, *, name):
    t, cw = c_pre.shape
    tt = _tile(t, TOK_TILE, 16)

    def body(c_ref, gc_ref, dm_ref, lg_ref, lb_ref, dc_ref, dgc_ref, dlg_ref, dlb_ref, dbias_ref):
        @pl.when(pl.program_id(0) == 0)
        def _():
            dlg_ref[...] = jnp.zeros_like(dlg_ref)
            dlb_ref[...] = jnp.zeros_like(dlb_ref)
            dbias_ref[...] = jnp.zeros_like(dbias_ref)

        c = c_ref[...]
        mu = jnp.mean(c, axis=-1, keepdims=True)
        xc = c - mu
        rstd = lax.rsqrt(jnp.mean(xc * xc, axis=-1, keepdims=True) + EPS)
        xhat = xc * rstd
        y = xhat * lg_ref[...] + lb_ref[...]
        sy = _sigmoid(y)
        g = gc_ref[...].astype(F32)
        sg = _sigmoid(g)
        dm = dm_ref[...].astype(F32)
        dgc_ref[...] = (dm * (y * sy) * _dsilu(g, sg)).astype(BF16)
        dy = dm * (g * sg) * _dsilu(y, sy)
        dlb_ref[...] += jnp.sum(dy, axis=0, keepdims=True)
        dlg_ref[...] += jnp.sum(dy * xhat, axis=0, keepdims=True)
        dxh = dy * lg_ref[...]
        dc = rstd * (dxh - jnp.mean(dxh, axis=-1, keepdims=True)
                     - xhat * jnp.mean(dxh * xhat, axis=-1, keepdims=True))
        dc_ref[...] = dc
        dbias_ref[...] += jnp.sum(dc, axis=0, keepdims=True)

    row = pl.BlockSpec((tt, cw), lambda i: (i, 0))
    vec = pl.BlockSpec((1, cw), lambda i: (0, 0))
    return _pcall(
        body, name=name, grid=(t // tt,),
        in_specs=[row, pl.BlockSpec((tt, cw), lambda i: (i, lay["GCONV"] // cw)),
                  pl.BlockSpec((tt, cw), lambda i: (i, 1)), vec, vec],
        out_specs=[row, pl.BlockSpec((tt, cw), lambda i: (i, lay["GCONV"] // cw)), vec, vec, vec],
        out_shape=[jax.ShapeDtypeStruct((t, cw), F32), jax.ShapeDtypeStruct((t, lay["NP"]), BF16),
                   jax.ShapeDtypeStruct((1, cw), F32), jax.ShapeDtypeStruct((1, cw), F32),
                   jax.ShapeDtypeStruct((1, cw), F32)],
        compiler_params=_params(("arbitrary",)),
    )(c_pre, z, dmix, ln_g.reshape(1, cw), ln_b.reshape(1, cw))


def _conv_bwd_dw(dc, u, z, w_dw, dz, lay, nb, *, name):
    t, cw = dc.shape
    s = t // nb
    tt = _tile(s, CONV_TILE, HALO)
    ns = s // tt
    hb = tt // HALO
    lc = _tile(cw, LANE_CHUNK, 128)

    def body(dc_ref, dcn_ref, u_ref, up_ref, a_ref, b_ref, w_ref, dz_in, dab_ref, dw_ref, ext_dc, ext_u, du_ref,
             sh_dc, sh_u):
        del dz_in
        b_i, i = pl.program_id(0), pl.program_id(1)

        @pl.when((b_i == 0) & (i == 0))
        def _():
            dw_ref[...] = jnp.zeros_like(dw_ref)

        dc_v = dc_ref[...]
        ext_dc[0:tt, :] = dc_v
        ext_dc[tt:tt + HALO, :] = jnp.where(i < ns - 1, dcn_ref[...], 0.0)
        ext_u[0:HALO, :] = jnp.where(i > 0, up_ref[...], 0.0)
        ext_u[HALO:HALO + tt, :] = u_ref[...]
        for c0 in range(0, cw, lc):
            _stage_row_shifts(ext_dc, sh_dc, c0, lc, tt + HALO - SUBLANES)
            _stage_row_shifts(ext_u, sh_u, c0, lc, tt + HALO - SUBLANES)
            acc = jnp.zeros((tt, lc), F32)
            dcc = dc_v[:, c0:c0 + lc]
            for k in range(CONV_K):
                acc = acc + w_ref[k:k + 1, c0:c0 + lc] * _row_window(ext_dc, sh_dc, c0, lc, CONV_K - 1 - k, tt)
                off = HALO - (CONV_K - 1) + k
                dw_ref[k:k + 1, c0:c0 + lc] += jnp.sum(dcc * _row_window(ext_u, sh_u, c0, lc, off, tt),
                                                       axis=0, keepdims=True)
            du_ref[:, c0:c0 + lc] = acc
        du = du_ref[...]
        sb = _sigmoid(b_ref[...].astype(F32))
        dab_ref[:, 0:cw] = (du * sb).astype(BF16)
        dab_ref[:, cw:2 * cw] = (du * a_ref[...].astype(F32) * sb * (1.0 - sb)).astype(BF16)

    last = nb * ns * hb - 1
    row = pl.BlockSpec((tt, cw), lambda b, i: (b * ns + i, 0))
    return _pcall(
        body, name=name, grid=(nb, ns),
        in_specs=[row, pl.BlockSpec((HALO, cw), lambda b, i: (jnp.minimum((b * ns + i + 1) * hb, last), 0)),
                  row, pl.BlockSpec((HALO, cw), lambda b, i: (jnp.maximum((b * ns + i) * hb - 1, 0), 0)),
                  pl.BlockSpec((tt, cw), lambda b, i: (b * ns + i, lay["A"] // cw)),
                  pl.BlockSpec((tt, cw), lambda b, i: (b * ns + i, lay["B"] // cw)),
                  pl.BlockSpec((HALO, cw), lambda b, i: (0, 0)), pl.BlockSpec(memory_space=pl.ANY)],
        out_specs=[pl.BlockSpec((tt, 2 * cw), lambda b, i: (b * ns + i, 0)),
                   pl.BlockSpec((HALO, cw), lambda b, i: (0, 0))],
        out_shape=[jax.ShapeDtypeStruct(dz.shape, BF16), jax.ShapeDtypeStruct((HALO, cw), F32)],
        scratch_shapes=[pltpu.VMEM((tt + HALO, cw), F32), pltpu.VMEM((tt + HALO, cw), F32),
                        pltpu.VMEM((tt, cw), F32),
                        pltpu.VMEM((SUBLANES - 1, tt + HALO - SUBLANES, lc), F32),
                        pltpu.VMEM((SUBLANES - 1, tt + HALO - SUBLANES, lc), F32)],
        input_output_aliases={7: 0},
        compiler_params=_params(("arbitrary", "arbitrary")),
    )(dc, dc, u, u, z, z, w_dw, dz)


def _loss_head(y, target, *, name):
    t, d = y.shape
    tt = _tile(t, TOK_TILE, 16)

    def body(y_ref, t_ref, sse_ref, dy_ref, dyb_ref):
        @pl.when(pl.program_id(0) == 0)
        def _():
            sse_ref[...] = jnp.zeros_like(sse_ref)

        e = y_ref[...] - t_ref[...]
        sse_ref[...] += jnp.sum(e * e)
        dy = e * (1.0 / d)
        dy_ref[...] = dy
        dyb_ref[...] = dy.astype(BF16)

    row = pl.BlockSpec((tt, d), lambda i: (i, 0))
    return _pcall(
        body, name=name, grid=(t // tt,),
        in_specs=[row, row],
        out_specs=[pl.BlockSpec((8, 128), lambda i: (0, 0)), row, row],
        out_shape=[jax.ShapeDtypeStruct((8, 128), F32), jax.ShapeDtypeStruct((t, d), F32),
                   jax.ShapeDtypeStruct((t, d), BF16)],
        compiler_params=_params(("arbitrary",)),
    )(y, target)


def _adam(w, m, v, g_parts, *, name, layer=0, layers=1, prev=None):
    rows, cols = w.shape
    slab = rows // layers
    tr = _tile(slab, max(16, ADAM_BLOCK_ELEMS // cols), 16)
    blk0 = layer * (slab // tr)
    n = len(g_parts)
    n_prev = 0 if prev is None else 4

    def body(*refs):
        w_ref, m_ref, v_ref = refs[:3]
        g_refs = refs[3:3 + n]
        g_out, d_out, m_out, v_out = refs[3 + n + n_prev:]
        g = g_refs[0][...].astype(F32)
        for r in g_refs[1:]:
            g = g + r[...].astype(F32)
        m_new = ADAM_B1 * m_ref[...] + (1.0 - ADAM_B1) * g
        v_new = ADAM_B2 * v_ref[...] + (1.0 - ADAM_B2) * (g * g)
        m_hat = m_new / (1.0 - ADAM_B1 ** ADAM_STEP)
        v_hat = v_new / (1.0 - ADAM_B2 ** ADAM_STEP)
        g_out[...] = g
        d_out[...] = -ADAM_LR * (m_hat / (jnp.sqrt(v_hat) + ADAM_EPS) + ADAM_WD * w_ref[...])
        m_out[...] = m_new
        v_out[...] = v_new

    blk = pl.BlockSpec((tr, cols), lambda i: (blk0 + i, 0))
    g_specs, g_args = [], []
    for arr, lead in g_parts:
        g_args.append(arr)
        if lead is None:
            g_specs.append(pl.BlockSpec((tr, cols), lambda i: (i, 0)))
        else:
            g_specs.append(pl.BlockSpec((None, tr, cols), functools.partial(lambda i, p: (p, i, 0), p=lead)))
    out = jax.ShapeDtypeStruct((rows, cols), F32)
    return _pcall(
        body, name=name, grid=(slab // tr,),
        in_specs=[blk, blk, blk] + g_specs + [pl.BlockSpec(memory_space=pl.ANY)] * n_prev,
        out_specs=[blk, blk, blk, blk],
        out_shape=[out, out, out, out],
        input_output_aliases={3 + n + k: k for k in range(n_prev)},
        compiler_params=_params(("parallel",)),
    )(w, m, v, *g_args, *(prev or ()))


def _position():
    return lax.axis_index("x"), lax.axis_index("y"), lax.axis_index("c")


def _block_id(p):
    return 4 * p[0] + 2 * p[1] + p[2]


def _flip(p, mask):
    return tuple((1 - v) if (mask >> (2 - a)) & 1 else v for a, v in enumerate(p))


def _all_gather(xs, *, name):
    n = len(xs)

    def body(*refs):
        x_refs, o_refs = refs[:n], refs[n:2 * n]
        send_sems, recv_sems, local_sems = refs[2 * n:]
        x, y, c = _position()
        me, sibling = (x, y, c), (x, y, 1 - c)
        chips = [(1 - x, y), (x, 1 - y), (1 - x, 1 - y)]

        def copy(t, k, block, to, src=None):
            dst = o_refs[t].at[_block_id(block)]
            return pltpu.make_async_remote_copy(
                src_ref=dst if src is None else src, dst_ref=dst,
                send_sem=send_sems.at[t, k], recv_sem=recv_sems.at[t, k],
                device_id=to, device_id_type=MESH)

        mine = [pltpu.make_async_copy(x_refs[t], o_refs[t].at[_block_id(me)], local_sems.at[t]) for t in range(n)]
        for cp in mine:
            cp.start()
        started = []
        for t in range(n):
            first = [copy(t, 0, me, sibling, src=x_refs[t])]
            first += [copy(t, 1 + j, me, (*chip, c), src=x_refs[t]) for j, chip in enumerate(chips)]
            for cp in first:
                cp.start()
            started += first
        for j, chip in enumerate(chips):
            for t in range(n):
                copy(t, 1 + j, (*chip, c), me).wait_recv()
                fwd = copy(t, 4 + j, (*chip, c), sibling)
                fwd.start()
                started.append(fwd)
        for t in range(n):
            copy(t, 0, sibling, me).wait_recv()
            for j, chip in enumerate(chips):
                copy(t, 4 + j, (*chip, 1 - c), me).wait_recv()
        for cp in started:
            cp.wait_send()
        for cp in mine:
            cp.wait()

    any_spec = pl.BlockSpec(memory_space=pl.ANY)
    return _pcall(
        body, name=name,
        in_specs=[any_spec] * n, out_specs=[any_spec] * n,
        out_shape=[jax.ShapeDtypeStruct((N_DEV,) + a.shape, a.dtype) for a in xs],
        scratch_shapes=[pltpu.SemaphoreType.DMA((n, 7)), pltpu.SemaphoreType.DMA((n, 7)),
                        pltpu.SemaphoreType.DMA((n,))],
    )(*xs)


def _pushed_copy(x_ref, land_ref, send_sems, recv_sems, t, mask, me, chunked, at_receiver):
    peer = _flip(me, mask)
    src = x_ref.at[_block_id(peer)] if chunked else x_ref
    slot = _block_id(peer) if at_receiver else _block_id(me)
    k = (N_DEV - 1) * t + mask - 1
    return pltpu.make_async_remote_copy(
        src_ref=src, dst_ref=land_ref.at[slot], send_sem=send_sems.at[k], recv_sem=recv_sems.at[k],
        device_id=peer, device_id_type=MESH)


def _push_start(xs, chunked, *, name, after=None):
    n = len(xs)
    lands = [lax.empty(a.shape if chunked else (N_DEV,) + a.shape, a.dtype) for a in xs]

    n_after = 0 if after is None else 1

    def body(*refs):
        x_refs, land_refs = refs[:n], refs[n:2 * n]
        send_sems, recv_sems = refs[2 * n + n_after], refs[2 * n + n_after + 1]
        token = refs[4 * n + n_after + 2]
        me = _position()
        for t in range(n):
            for mask in range(1, N_DEV):
                _pushed_copy(x_refs[t], land_refs[t], send_sems, recv_sems, t, mask, me, chunked, False).start()
        token[...] = jnp.zeros_like(token)

    hbm = pl.BlockSpec(memory_space=pltpu.HBM)
    sem = pl.BlockSpec(memory_space=pltpu.SEMAPHORE)
    outs = _pcall(
        body, name=name,
        in_specs=[hbm] * (2 * n) + [pl.BlockSpec(memory_space=pl.ANY)] * n_after,
        out_specs=[sem, sem] + [hbm] * (2 * n) + [pl.BlockSpec(memory_space=pltpu.VMEM)],
        out_shape=[pltpu.SemaphoreType.DMA(((N_DEV - 1) * n,)), pltpu.SemaphoreType.DMA(((N_DEV - 1) * n,))]
        + [pltpu.HBM(a.shape, a.dtype) for a in xs] + [pltpu.HBM(a.shape, a.dtype) for a in lands]
        + [jax.ShapeDtypeStruct((8, 128), F32)],
        input_output_aliases={i: 2 + i for i in range(2 * n)},
        compiler_params=pltpu.CompilerParams(has_side_effects=pltpu.SideEffectType.DATAFLOW_SIDE_EFFECTING),
    )(*[pltpu.with_memory_space_constraint(a, pltpu.HBM) for a in list(xs) + lands], *([after] * n_after))
    return outs[0], outs[1], outs[2:2 + n], outs[2 + n:2 + 2 * n], outs[2 + 2 * n]


def _push_wait(handle, after, chunked, *, name):
    send_sems, recv_sems, xs, lands, _ = handle
    n = len(xs)

    def body(*refs):
        x_refs, land_refs = refs[:n], refs[n:2 * n]
        send_sems, recv_sems = refs[2 * n], refs[2 * n + 1]
        me = _position()
        for t in range(n):
            for mask in range(1, N_DEV):
                _pushed_copy(x_refs[t], land_refs[t], send_sems, recv_sems, t, mask, me, chunked, False).wait_send()
                _pushed_copy(x_refs[t], land_refs[t], send_sems, recv_sems, t, mask, me, chunked, True).wait_recv()

    hbm = pl.BlockSpec(memory_space=pltpu.HBM)
    sem = pl.BlockSpec(memory_space=pltpu.SEMAPHORE)
    outs = _pcall(
        body, name=name,
        in_specs=[hbm] * (2 * n) + [sem, sem, pl.BlockSpec(memory_space=pl.ANY)],
        out_specs=[hbm] * (2 * n),
        out_shape=[pltpu.HBM(a.shape, a.dtype) for a in list(xs) + list(lands)],
        input_output_aliases={i: i for i in range(2 * n)},
        compiler_params=pltpu.CompilerParams(has_side_effects=pltpu.SideEffectType.DATAFLOW_SIDE_EFFECTING),
    )(*xs, *lands, send_sems, recv_sems, after)
    return outs[:n], outs[n:]


def _all_reduce_small(pack, *, name):
    rows = pack.shape[0]

    def body(p_ref, o_ref, gath, send_sems, recv_sems):
        me = _position()
        my_id = _block_id(me)
        gath[my_id] = p_ref[...]
        sent = []
        for mask in range(1, N_DEV):
            peer = _flip(me, mask)
            cp = pltpu.make_async_remote_copy(
                src_ref=p_ref, dst_ref=gath.at[my_id], send_sem=send_sems.at[mask - 1],
                recv_sem=recv_sems.at[mask - 1], device_id=peer, device_id_type=MESH)
            cp.start()
            sent.append(cp)
        for mask in range(1, N_DEV):
            slot = gath.at[_block_id(_flip(me, mask))]
            pltpu.make_async_remote_copy(
                src_ref=slot, dst_ref=slot, send_sem=send_sems.at[mask - 1], recv_sem=recv_sems.at[mask - 1],
                device_id=me, device_id_type=MESH).wait_recv()
        for cp in sent:
            cp.wait_send()
        total = gath[0]
        for s in range(1, N_DEV):
            total = total + gath[s]
        o_ref[...] = total

    vm = pl.BlockSpec(memory_space=pltpu.VMEM)
    return _pcall(
        body, name=name,
        in_specs=[vm], out_specs=vm,
        out_shape=jax.ShapeDtypeStruct(pack.shape, F32),
        scratch_shapes=[pltpu.VMEM((N_DEV, rows, 128), F32), pltpu.SemaphoreType.DMA((7,)),
                        pltpu.SemaphoreType.DMA((7,))],
        compiler_params=pltpu.CompilerParams(vmem_limit_bytes=VMEM_LIMIT),
    )(pack)


def _layout(d, ql, kvl):
    cw = d // 2
    att = d // 2
    lay = {"D": d, "CW": cw, "ATT": att, "H": att // VDIM, "QL": ql, "KVL": kvl}
    lay["A"], lay["B"], lay["GATT"], lay["GCONV"] = 0, cw, 2 * cw, 2 * cw + att
    lay["QC"] = lay["GCONV"] + cw
    lay["KVC"] = lay["QC"] + ql
    lay["KPE"] = lay["KVC"] + kvl
    used = lay["KPE"] + 128
    tn = min(MM_TN, 1024)
    lay["NP"] = -(-used // tn) * tn
    assert att == cw and lay["QC"] % ql == 0 and lay["KVC"] % kvl == 0 and lay["KPE"] % 128 == 0
    lay["o_kv"], lay["o_pe"] = ql, ql + kvl
    lay["o_ga"] = lay["o_pe"] + ROPE
    lay["o_u"] = lay["o_ga"] + att
    lay["o_gc"] = lay["o_u"] + 2 * cw
    lay["IN_COLS"] = lay["o_gc"] + cw
    return lay


def _ungather_cols(g):
    return jnp.transpose(g, (1, 0, 2)).reshape(g.shape[1], -1)


def _to_col_blocks(w):
    r, c = w.shape
    return jnp.transpose(w.reshape(r, N_DEV, c // N_DEV), (1, 0, 2))


def _pad_w_in(w, lay):
    d, cw, att, ql, kvl = lay["D"], lay["CW"], lay["ATT"], lay["QL"], lay["KVL"]
    parts = [w[:, lay["o_u"]:lay["o_u"] + 2 * cw], w[:, lay["o_ga"]:lay["o_ga"] + att],
             w[:, lay["o_gc"]:lay["o_gc"] + cw], w[:, :ql], w[:, lay["o_kv"]:lay["o_kv"] + kvl],
             w[:, lay["o_pe"]:lay["o_pe"] + ROPE],
             jnp.zeros((d, lay["NP"] - lay["KPE"] - ROPE), w.dtype)]
    return jnp.concatenate(parts, axis=1)


def _unpad_w_in(wp, lay):
    cw, att, ql, kvl = lay["CW"], lay["ATT"], lay["QL"], lay["KVL"]
    parts = [wp[:, lay["QC"]:lay["QC"] + ql], wp[:, lay["KVC"]:lay["KVC"] + kvl],
             wp[:, lay["KPE"]:lay["KPE"] + ROPE], wp[:, lay["GATT"]:lay["GATT"] + att],
             wp[:, :2 * cw], wp[:, lay["GCONV"]:lay["GCONV"] + cw]]
    return jnp.concatenate(parts, axis=1)


def _pad_heads(w, h):
    r = w.shape[0]
    return jnp.pad(w.reshape(r, h, QK_DIM), ((0, 0), (0, 0), (0, HEAD_PAD - QK_DIM))).reshape(r, h * HEAD_PAD)


def _unpad_heads(w, h):
    r = w.shape[0]
    return w.reshape(r, h, HEAD_PAD)[:, :, :QK_DIM].reshape(r, h * QK_DIM)


def _rope_tabs(positions):
    half = ROPE // 2
    inv_freq = ROPE_THETA ** (-jnp.arange(half, dtype=F32) / half)
    ang = positions.astype(F32).reshape(-1)[:, None] * inv_freq
    cos, sin = jnp.cos(ang), jnp.sin(ang)
    zero = jnp.zeros_like(cos)
    return (jnp.concatenate([cos, cos, zero, zero], axis=1),
            jnp.concatenate([-sin, zero, zero, zero], axis=1),
            jnp.concatenate([zero, sin, zero, zero], axis=1))


def _pack_rows(vecs):
    rows = []
    for v in vecs:
        flat = v.reshape(-1)
        pad = (-flat.shape[0]) % 1024
        rows.append(jnp.pad(flat, (0, pad)).reshape(-1, 128))
    return jnp.concatenate(rows, axis=0)


def _unpack_rows(pack, shapes):
    out, r0 = [], 0
    for shp in shapes:
        size = math.prod(shp)
        nrows = -(-size // 1024) * 8
        out.append(pack[r0:r0 + nrows].reshape(-1)[:size].reshape(shp))
        r0 += nrows
    return out


def kernel(x, positions, ln_g, w_in, q_a_norm, w_q_up, kv_a_norm, w_kv_up, q_norm, k_norm, w_dw, b_dw, conv_ln_g, conv_ln_b, w_out, loss_target, m_ln_g, m_w_in, m_q_a_norm, m_w_q_up, m_kv_a_norm, m_w_kv_up, m_q_norm, m_k_norm, m_w_dw, m_b_dw, m_conv_ln_g, m_conv_ln_b, m_w_out, v_ln_g, v_w_in, v_q_a_norm, v_w_q_up, v_kv_a_norm, v_w_kv_up, v_q_norm, v_k_norm, v_w_dw, v_b_dw, v_conv_ln_g, v_conv_ln_b, v_w_out):
    nb, seq, d = x.shape
    depth = ln_g.shape[0]
    lay = _layout(d, q_a_norm.shape[1], kv_a_norm.shape[1])
    h, cw, ql, kvl = lay["H"], lay["CW"], lay["QL"], lay["KVL"]
    t = nb * seq
    my_id = _block_id(_position())

    def shards(l):
        return [w_in[l].astype(BF16), w_q_up[l].astype(BF16), w_kv_up[l].astype(BF16), w_out[l].astype(BF16)]

    def fill_own(lands, own):
        return [lax.dynamic_update_index_in_dim(land, blk, my_id, 0) for land, blk in zip(lands, own)]

    def layout_in(g_in):
        return {"in": _pad_w_in(_ungather_cols(g_in), lay)}

    def layout_rest(g_q, g_kv, g_out):
        return {"q": _pad_heads(_ungather_cols(g_q), h), "kv": _ungather_cols(g_kv), "out": g_out.reshape(2 * cw, d)}

    first = shards(0)
    g_in0, g_dw = _all_gather([first[0], w_dw], name="gather_w_in_0")
    gathers = {0: _push_start(first[1:], False, after=g_in0, name="gather_start_0")}
    for l in range(1, depth):
        gathers[l] = _push_start(shards(l), False, after=gathers[l - 1][4], name=f"gather_start_{l}")
    fwd_dep = gathers[depth - 1][4]
    weights = []

    tabs = _rope_tabs(positions)
    gq_pad = jnp.pad(q_norm, ((0, 0), (0, HEAD_PAD - QK_DIM)))
    gk_pad = jnp.pad(k_norm, ((0, 0), (0, HEAD_PAD - QK_DIM)))
    w_dw_all = jnp.transpose(g_dw, (1, 2, 0, 3)).reshape(depth, CONV_K, cw)
    w_dw_all = jnp.pad(w_dw_all, ((0, 0), (0, HALO - CONV_K), (0, 0)))

    saved = []
    xs = x.reshape(t, d)
    for l in range(depth):
        hid = _rms_fwd(xs, ln_g[l], dep=fwd_dep if l == 0 else None, name=f"rms_fwd_{l}")
        if l == 0:
            weights.append(layout_in(g_in0))
        z = _mm(hid, weights[l]["in"], out_dtype=BF16, name=f"in_proj_{l}")
        if l == 0:
            own, lands = _push_wait(gathers[0], z, False, name="gather_wait_0")
            weights[0].update(layout_rest(*fill_own(lands, own)))
        wl = weights[l]
        qn, kvn = _lat_fwd(z, q_a_norm[l], kv_a_norm[l], lay, name=f"lat_fwd_{l}")
        q_raw = _mm(qn, wl["q"], out_dtype=BF16, name=f"q_up_{l}")
        kv_raw = _mm(kvn, wl["kv"], out_dtype=BF16, name=f"kv_up_{l}")
        qh, kh, vh = _heads_fwd(q_raw, kv_raw, z, tabs, gq_pad[l:l + 1], gk_pad[l:l + 1], lay, name=f"heads_fwd_{l}")
        att, mix, lse = _flash_fwd(qh, kh, vh, z, lay, nb, name=f"flash_fwd_{l}")
        mix, u, c_pre = _conv_fwd(z, mix, w_dw_all[l], b_dw[l], conv_ln_g[l], conv_ln_b[l], lay, nb,
                                  name=f"conv_fwd_{l}")
        x_next = _mm(mix, wl["out"], add=xs, name=f"out_proj_{l}")
        saved.append((xs, hid, z, qn, kvn, q_raw, kv_raw, qh, kh, vh, att, lse, mix, u, c_pre))
        xs = x_next
        if l + 1 < depth:
            own, lands = _push_wait(gathers[l + 1], xs, False, name=f"gather_wait_{l + 1}")
            g_in, *g_rest = fill_own(lands, own)
            weights.append({**layout_in(g_in), **layout_rest(*g_rest)})

    sse, dx, dxb = _loss_head(xs, loss_target.reshape(t, d), name="loss_head")
    loss = lax.psum(sse[0, 0] * (0.5 / d), ("x", "y", "c"))

    small = {k: [] for k in ("ln_g", "q_a", "kv_a", "q_n", "k_n", "w_dw", "b_dw", "cln_g", "cln_b")}
    scatters, bwd_dep = {}, None
    for l in reversed(range(depth)):
        xs, hid, z, qn, kvn, q_raw, kv_raw, qh, kh, vh, att, lse, mix, u, c_pre = saved[l]
        wl = weights[l]
        dmix = _mm(dxb, wl["out"], trans_b=True, dep=bwd_dep, name=f"d_mix_{l}")
        dw_out = _mm(mix, dxb, trans_a=True, out_dtype=BF16, name=f"dw_out_{l}")
        dc, dz, dlg, dlb, dbias = _conv_bwd_ln(c_pre, z, dmix, conv_ln_g[l], conv_ln_b[l], lay,
                                               name=f"conv_bwd_ln_{l}")
        dz, dwdw = _conv_bwd_dw(dc, u, z, w_dw_all[l], dz, lay, nb, name=f"conv_bwd_dw_{l}")
        do, delta, dz = _gate_bwd(dmix, att, z, dz, lay, name=f"gate_bwd_{l}")
        dqh, dkh, dvh = _flash_bwd(qh, kh, vh, do, lse, delta, nb, name=f"flash_bwd_{l}")
        dq_raw, dkv_raw, dpe, dgq, dgk = _heads_bwd(q_raw, kv_raw, z, tabs, gq_pad[l:l + 1], gk_pad[l:l + 1],
                                                    dqh, dkh, dvh, lay, name=f"heads_bwd_{l}")
        dqn = _mm(dq_raw, wl["q"], trans_b=True, name=f"d_qn_{l}")
        dkvn = _mm(dkv_raw, wl["kv"], trans_b=True, name=f"d_kvn_{l}")
        dw_q = _mm(qn, dq_raw, trans_a=True, out_dtype=BF16, name=f"dw_q_{l}")
        dw_kv = _mm(kvn, dkv_raw, trans_a=True, out_dtype=BF16, name=f"dw_kv_{l}")
        early = _push_start(
            [_to_col_blocks(_unpad_heads(dw_q, h)), _to_col_blocks(dw_kv), dw_out.reshape(N_DEV, (2 * cw) // N_DEV, d)],
            True, name=f"scatter_start_a_{l}")
        dz, dgqa, dgkva = _lat_bwd(z, q_a_norm[l], kv_a_norm[l], dqn, dkvn, dpe, dz, lay, name=f"lat_bwd_{l}")
        dw_in = _mm(hid, dz, trans_a=True, out_dtype=BF16, dep=early[4], name=f"dw_in_{l}")
        late = _push_start([_to_col_blocks(_unpad_w_in(dw_in, lay))], True, name=f"scatter_start_b_{l}")
        scatters[l] = (late, early)
        bwd_dep = late[4]
        dh = _mm(dz, wl["in"], trans_b=True, dep=bwd_dep, name=f"d_hid_{l}")
        dx, dxb, dlng = _rms_bwd(xs, ln_g[l], dh, dx, name=f"rms_bwd_{l}")
        for key, val in (("ln_g", dlng), ("q_a", dgqa), ("kv_a", dgkva), ("q_n", dgq[:, :QK_DIM]),
                         ("k_n", dgk[:, :QK_DIM]), ("w_dw", dwdw[:CONV_K]), ("b_dw", dbias),
                         ("cln_g", dlg), ("cln_b", dlb)):
            small[key].append(val)
    grad_x = dx.reshape(nb, seq, d)
    for key in small:
        small[key] = jnp.stack(small[key][::-1])

    small_names = ("ln_g", "q_a", "kv_a", "q_n", "k_n", "b_dw", "cln_g", "cln_b", "w_dw")
    small_shapes = [small[k].shape for k in small_names]
    summed = _unpack_rows(_all_reduce_small(_pack_rows([small[k] for k in small_names]), name="reduce_small_grads"),
                          small_shapes)
    sg = dict(zip(small_names, summed))
    g_w_dw = lax.dynamic_slice_in_dim(sg["w_dw"], my_id * (cw // N_DEV), cw // N_DEV, axis=2)

    def adam_small(ws, ms, vs, gs, nm):
        shapes = [w.shape for w in ws]
        outs = _adam(_pack_rows(ws), _pack_rows(ms), _pack_rows(vs), [(_pack_rows(gs), None)], name=nm)
        return [_unpack_rows(o, shapes) for o in outs]

    big = [("w_in", w_in, m_w_in, v_w_in), ("w_q_up", w_q_up, m_w_q_up, v_w_q_up),
           ("w_kv_up", w_kv_up, m_w_kv_up, v_w_kv_up), ("w_out", w_out, m_w_out, v_w_out)]
    res, prev = {}, [None] * len(big)
    for l in reversed(range(depth)):
        own, lands = [], []
        for half, tag in zip(scatters[l], "ba"):
            sent, landed = _push_wait(half, dx, True, name=f"scatter_wait_{tag}_{l}")
            own += [lax.dynamic_index_in_dim(o, my_id, 0, keepdims=False) for o in sent]
            lands += landed
        for idx, ((nm, w, m, v), recv) in enumerate(zip(big, fill_own(lands, own))):
            rows, cols = w.shape[1], w.shape[2]
            flat = lambda a: a.reshape(depth * rows, cols)
            prev[idx] = _adam(flat(w), flat(m), flat(v), [(recv, s) for s in range(N_DEV)], layer=l, layers=depth,
                              prev=prev[idx], name=f"adam_{nm}_{l}")
    for idx, (nm, w, _, _) in enumerate(big):
        res[nm] = [o.reshape(w.shape) for o in prev[idx]]
    names_s = ["ln_g", "q_a_norm", "kv_a_norm", "q_norm", "k_norm", "w_dw", "b_dw", "conv_ln_g", "conv_ln_b"]
    ws = [ln_g, q_a_norm, kv_a_norm, q_norm, k_norm, w_dw, b_dw, conv_ln_g, conv_ln_b]
    ms = [m_ln_g, m_q_a_norm, m_kv_a_norm, m_q_norm, m_k_norm, m_w_dw, m_b_dw, m_conv_ln_g, m_conv_ln_b]
    vs = [v_ln_g, v_q_a_norm, v_kv_a_norm, v_q_norm, v_k_norm, v_w_dw, v_b_dw, v_conv_ln_g, v_conv_ln_b]
    gs = [sg["ln_g"].reshape(ln_g.shape), sg["q_a"].reshape(q_a_norm.shape), sg["kv_a"].reshape(kv_a_norm.shape),
          sg["q_n"].reshape(q_norm.shape), sg["k_n"].reshape(k_norm.shape), g_w_dw,
          sg["b_dw"].reshape(b_dw.shape), sg["cln_g"].reshape(conv_ln_g.shape), sg["cln_b"].reshape(conv_ln_b.shape)]
    outs_s = adam_small(ws, ms, vs, gs, "adam_small")
    for idx, nm in enumerate(names_s):
        res[nm] = [outs_s[k][idx] for k in range(4)]

    order = ["ln_g", "w_in", "q_a_norm", "w_q_up", "kv_a_norm", "w_kv_up", "q_norm", "k_norm", "w_dw", "b_dw",
             "conv_ln_g", "conv_ln_b", "w_out"]
    return (loss, grad_x, *[res[nm][0] for nm in order], *[res[nm][1] for nm in order],
            *[res[nm][2] for nm in order], *[res[nm][3] for nm in order])
```

```python
import functools
import math

import jax
import jax.numpy as jnp
from jax import lax
from jax.experimental import pallas as pl
from jax.experimental.pallas import tpu as pltpu

F32 = jnp.float32
BF16 = jnp.bfloat16
MESH = pl.DeviceIdType.MESH

N_DEV = 8
NOPE = 128
ROPE = 64
VDIM = 128
HEAD_PAD = 256
QK_DIM = NOPE + ROPE
CONV_K = 31
HALO = 32
EPS = 1e-6
ROPE_THETA = 10000.0
NEG = -1e30

ADAM_LR = 0.001
ADAM_B1 = 0.9
ADAM_B2 = 0.999
ADAM_EPS = 1e-08
ADAM_WD = 0.01
ADAM_STEP = 10

TOK_TILE = 256
CONV_TILE = 256
ATT_TQ = 512
HEAD_GROUP = 4
MM_TM = 1024
MM_TN = 512
MM_TK = 4096
MM_VMEM_BUDGET = 46 * 1024 * 1024
ADAM_BLOCK_ELEMS = 128 * 1024
LANE_CHUNK = 256
VMEM_LIMIT = 56 * 1024 * 1024


def _pcall(body, **kw):
    return pl.pallas_call(body, **kw)


def _tile(dim, pref, mult):
    t = min(pref, dim)
    t -= t % mult
    while t >= mult:
        if dim % t == 0:
            return t
        t -= mult
    return dim


def _params(sem):
    return pltpu.CompilerParams(dimension_semantics=sem, vmem_limit_bytes=VMEM_LIMIT)


def _sigmoid(v):
    return 1.0 / (1.0 + jnp.exp(-v))


def _dsilu(v, sg):
    return sg * (1.0 + v * (1.0 - sg))


def _mm_tiles(m, n, kdim, out_bytes, has_add):
    def need(tm, tn, tk):
        return (2 * 2 * (tm * tk + tk * tn) + 2 * tm * tn * out_bytes
                + tm * tn * 4 * ((kdim > tk) + 2 * has_add + 1))

    shapes = [(_tile(m, pm, 128), _tile(n, pn, 128))
              for pm, pn in ((MM_TM, MM_TN), (MM_TM // 2, MM_TN), (MM_TM // 2, MM_TN // 2), (MM_TM // 4, MM_TN // 2))]
    for tm, tn in shapes:
        if need(tm, tn, kdim) <= MM_VMEM_BUDGET:
            return tm, tn, kdim
    tk = _tile(kdim, MM_TK, 128)
    for tm, tn in shapes:
        if need(tm, tn, tk) <= MM_VMEM_BUDGET:
            break
    return tm, tn, tk


def _mm(a, b, *, name, trans_a=False, trans_b=False, add=None, out_dtype=F32, dep=None):
    assert not (trans_a and trans_b)
    if trans_a:
        kdim, m = a.shape
    else:
        m, kdim = a.shape
    n = b.shape[0] if trans_b else b.shape[1]
    assert b.shape[1 if trans_b else 0] == kdim
    has_add = add is not None
    tm, tn, tk = _mm_tiles(m, n, kdim, jnp.dtype(out_dtype).itemsize, has_add)
    nk = kdim // tk
    contract = (((0 if trans_a else 1,), (1 if trans_b else 0,)), ((), ()))

    def product(a_ref, b_ref):
        return lax.dot_general(a_ref[...], b_ref[...], contract, preferred_element_type=F32)

    def body(*refs):
        a_ref, b_ref = refs[:2]
        add_ref = refs[2] if has_add else None
        o_ref = refs[2 + has_add + (dep is not None)]

        def finish(r):
            if has_add:
                r = r + add_ref[...]
            o_ref[...] = r.astype(o_ref.dtype)

        if nk == 1:
            finish(product(a_ref, b_ref))
            return
        acc_ref = refs[-1]
        k = pl.program_id(2)

        @pl.when(k == 0)
        def _():
            acc_ref[...] = product(a_ref, b_ref)

        @pl.when((k > 0) & (k < nk - 1))
        def _():
            acc_ref[...] += product(a_ref, b_ref)

        @pl.when(k == nk - 1)
        def _():
            finish(acc_ref[...] + product(a_ref, b_ref))

    if trans_a:
        a_spec = pl.BlockSpec((tk, tm), lambda i, j, k: (k, i))
    else:
        a_spec = pl.BlockSpec((tm, tk), lambda i, j, k: (i, k))
    if trans_b:
        b_spec = pl.BlockSpec((tn, tk), lambda i, j, k: (j, k))
    else:
        b_spec = pl.BlockSpec((tk, tn), lambda i, j, k: (k, j))
    in_specs = [a_spec, b_spec]
    args = [a, b]
    if has_add:
        in_specs.append(pl.BlockSpec((tm, tn), lambda i, j, k: (i, j)))
        args.append(add)
    if dep is not None:
        in_specs.append(pl.BlockSpec(memory_space=pl.ANY))
        args.append(dep)
    return _pcall(
        body, name=name,
        grid=(m // tm, n // tn, nk),
        in_specs=in_specs,
        out_specs=pl.BlockSpec((tm, tn), lambda i, j, k: (i, j)),
        out_shape=jax.ShapeDtypeStruct((m, n), out_dtype),
        scratch_shapes=[pltpu.VMEM((tm, tn), F32)] if nk > 1 else [],
        compiler_params=_params(("parallel", "parallel", "arbitrary")),
    )(*args)


def _rms_fwd(x, g, *, name, dep=None):
    t, d = x.shape
    tt = _tile(t, TOK_TILE, 16)

    def body(x_ref, g_ref, *rest):
        h_ref = rest[-1]
        xv = x_ref[...]
        r = lax.rsqrt(jnp.mean(xv * xv, axis=-1, keepdims=True) + EPS)
        h_ref[...] = (xv * r * g_ref[...]).astype(BF16)

    deps = [] if dep is None else [dep]
    return _pcall(
        body, name=name, grid=(t // tt,),
        in_specs=[pl.BlockSpec((tt, d), lambda i: (i, 0)), pl.BlockSpec((1, d), lambda i: (0, 0))]
        + [pl.BlockSpec(memory_space=pl.ANY)] * len(deps),
        out_specs=pl.BlockSpec((tt, d), lambda i: (i, 0)),
        out_shape=jax.ShapeDtypeStruct((t, d), BF16),
        compiler_params=_params(("parallel",)),
    )(x, g.reshape(1, d), *deps)


def _rms_bwd(x, g, dh, dres, *, name):
    t, d = x.shape
    tt = _tile(t, TOK_TILE, 16)

    def body(x_ref, g_ref, dh_ref, dres_ref, dx_ref, dxb_ref, dg_ref):
        xv = x_ref[...]
        r = lax.rsqrt(jnp.mean(xv * xv, axis=-1, keepdims=True) + EPS)
        dy = dh_ref[...]
        dyg = dy * g_ref[...]
        dot = jnp.sum(dyg * xv, axis=-1, keepdims=True) * (1.0 / d)
        dx = dres_ref[...] + r * dyg - xv * (r * r * r) * dot
        dx_ref[...] = dx
        dxb_ref[...] = dx.astype(BF16)

        @pl.when(pl.program_id(0) == 0)
        def _():
            dg_ref[...] = jnp.zeros_like(dg_ref)

        dg_ref[...] += jnp.sum(dy * xv * r, axis=0, keepdims=True)

    row = pl.BlockSpec((tt, d), lambda i: (i, 0))
    vec = pl.BlockSpec((1, d), lambda i: (0, 0))
    return _pcall(
        body, name=name, grid=(t // tt,),
        in_specs=[row, vec, row, row],
        out_specs=[row, row, vec],
        out_shape=[jax.ShapeDtypeStruct((t, d), F32), jax.ShapeDtypeStruct((t, d), BF16),
                   jax.ShapeDtypeStruct((1, d), F32)],
        compiler_params=_params(("arbitrary",)),
    )(x, g.reshape(1, d), dh, dres)


def _lat_fwd(z, gq, gkv, lay, *, name):
    t = z.shape[0]
    ql, kvl = lay["QL"], lay["KVL"]
    tt = _tile(t, TOK_TILE, 16)

    def body(q_ref, kv_ref, gq_ref, gkv_ref, qn_ref, kvn_ref):
        for src, g_ref, dst in ((q_ref, gq_ref, qn_ref), (kv_ref, gkv_ref, kvn_ref)):
            v = src[...].astype(F32)
            r = lax.rsqrt(jnp.mean(v * v, axis=-1, keepdims=True) + EPS)
            dst[...] = (v * r * g_ref[...]).astype(BF16)

    return _pcall(
        body, name=name, grid=(t // tt,),
        in_specs=[pl.BlockSpec((tt, ql), lambda i: (i, lay["QC"] // ql)),
                  pl.BlockSpec((tt, kvl), lambda i: (i, lay["KVC"] // kvl)),
                  pl.BlockSpec((1, ql), lambda i: (0, 0)), pl.BlockSpec((1, kvl), lambda i: (0, 0))],
        out_specs=[pl.BlockSpec((tt, ql), lambda i: (i, 0)), pl.BlockSpec((tt, kvl), lambda i: (i, 0))],
        out_shape=[jax.ShapeDtypeStruct((t, ql), BF16), jax.ShapeDtypeStruct((t, kvl), BF16)],
        compiler_params=_params(("parallel",)),
    )(z, z, gq.reshape(1, ql), gkv.reshape(1, kvl))


def _lat_bwd(z, gq, gkv, dqn, dkvn, dpe, dz, lay, *, name):
    t = z.shape[0]
    ql, kvl = lay["QL"], lay["KVL"]
    tail = lay["NP"] - lay["QC"]
    assert lay["QC"] % tail == 0
    tt = _tile(t, TOK_TILE, 16)

    def body(q_ref, kv_ref, gq_ref, gkv_ref, dqn_ref, dkvn_ref, dpe_ref, dz_in, tail_ref, dgq_ref, dgkv_ref):
        del dz_in
        first = pl.program_id(0) == 0
        for src, g_ref, dy_ref, c0, dg_ref in ((q_ref, gq_ref, dqn_ref, 0, dgq_ref),
                                               (kv_ref, gkv_ref, dkvn_ref, ql, dgkv_ref)):
            v = src[...].astype(F32)
            n = v.shape[-1]
            r = lax.rsqrt(jnp.mean(v * v, axis=-1, keepdims=True) + EPS)
            dy = dy_ref[...]
            dyg = dy * g_ref[...]
            dot = jnp.sum(dyg * v, axis=-1, keepdims=True) * (1.0 / n)
            tail_ref[:, c0:c0 + n] = (r * dyg - v * (r * r * r) * dot).astype(BF16)

            @pl.when(first)
            def _():
                dg_ref[...] = jnp.zeros_like(dg_ref)

            dg_ref[...] += jnp.sum(dy * v * r, axis=0, keepdims=True)
        tail_ref[:, ql + kvl:ql + kvl + 128] = dpe_ref[...].astype(BF16)
        tail_ref[:, ql + kvl + 128:tail] = jnp.zeros((tt, tail - ql - kvl - 128), BF16)

    return _pcall(
        body, name=name, grid=(t // tt,),
        in_specs=[pl.BlockSpec((tt, ql), lambda i: (i, lay["QC"] // ql)),
                  pl.BlockSpec((tt, kvl), lambda i: (i, lay["KVC"] // kvl)),
                  pl.BlockSpec((1, ql), lambda i: (0, 0)), pl.BlockSpec((1, kvl), lambda i: (0, 0)),
                  pl.BlockSpec((tt, ql), lambda i: (i, 0)), pl.BlockSpec((tt, kvl), lambda i: (i, 0)),
                  pl.BlockSpec((tt, 128), lambda i: (i, 0)), pl.BlockSpec(memory_space=pl.ANY)],
        out_specs=[pl.BlockSpec((tt, tail), lambda i: (i, lay["QC"] // tail)),
                   pl.BlockSpec((1, ql), lambda i: (0, 0)), pl.BlockSpec((1, kvl), lambda i: (0, 0))],
        out_shape=[jax.ShapeDtypeStruct(dz.shape, BF16),
                   jax.ShapeDtypeStruct((1, ql), F32), jax.ShapeDtypeStruct((1, kvl), F32)],
        input_output_aliases={7: 0},
        compiler_params=_params(("arbitrary",)),
    )(z, z, gq.reshape(1, ql), gkv.reshape(1, kvl), dqn, dkvn, dpe, dz)


def _rope(r, c_tab, sa_tab, sb_tab):
    return r * c_tab + pltpu.roll(r, 96, 1) * sa_tab + pltpu.roll(r, 32, 1) * sb_tab


def _rope_t(d, c_tab, sa_tab, sb_tab):
    return d * c_tab + pltpu.roll(d * sa_tab, 32, 1) + pltpu.roll(d * sb_tab, 96, 1)


def _heads_fwd(q_raw, kv_raw, z, tabs, gq, gk, lay, *, name):
    t = z.shape[0]
    h = lay["H"]
    tt = _tile(t, TOK_TILE, 16)
    hg = _tile(h, HEAD_GROUP, 1)
    scale = 1.0 / math.sqrt(QK_DIM)

    def body(q_ref, kv_ref, pe_ref, c_ref, sa_ref, sb_ref, gq_ref, gk_ref, qh_ref, kh_ref, vh_ref):
        c_tab, sa_tab, sb_tab = c_ref[...], sa_ref[...], sb_ref[...]
        pe, gq_v, gk_v = pe_ref[...].astype(F32), gq_ref[...], gk_ref[...]
        ss_pe = jnp.sum(pe * pe, axis=-1, keepdims=True)
        for g in range(hg):
            q = q_ref[:, g * HEAD_PAD:(g + 1) * HEAD_PAD].astype(F32)
            r = lax.rsqrt(jnp.sum(q * q, axis=-1, keepdims=True) * (1.0 / QK_DIM) + EPS)
            qn = q * r * gq_v
            qh_ref[g] = (jnp.concatenate([qn[:, :NOPE], _rope(qn[:, NOPE:], c_tab, sa_tab, sb_tab)], axis=1)
                         * scale).astype(BF16)
            kv = kv_ref[:, g * HEAD_PAD:(g + 1) * HEAD_PAD].astype(F32)
            kn = kv[:, :NOPE]
            rk = lax.rsqrt((jnp.sum(kn * kn, axis=-1, keepdims=True) + ss_pe) * (1.0 / QK_DIM) + EPS)
            kh_ref[g] = jnp.concatenate(
                [kn * rk * gk_v[:, :NOPE], _rope(pe * rk * gk_v[:, NOPE:], c_tab, sa_tab, sb_tab)],
                axis=1).astype(BF16)
            vh_ref[g] = kv[:, NOPE:].astype(BF16)

    head = pl.BlockSpec((tt, hg * HEAD_PAD), lambda i, j: (i, j))
    tab = pl.BlockSpec((tt, 128), lambda i, j: (i, 0))
    gain = pl.BlockSpec((1, HEAD_PAD), lambda i, j: (0, 0))
    return _pcall(
        body, name=name, grid=(t // tt, h // hg),
        in_specs=[head, head, pl.BlockSpec((tt, 128), lambda i, j: (i, lay["KPE"] // 128)), tab, tab, tab, gain, gain],
        out_specs=[pl.BlockSpec((hg, tt, HEAD_PAD), lambda i, j: (j, i, 0)),
                   pl.BlockSpec((hg, tt, HEAD_PAD), lambda i, j: (j, i, 0)),
                   pl.BlockSpec((hg, tt, VDIM), lambda i, j: (j, i, 0))],
        out_shape=[jax.ShapeDtypeStruct((h, t, HEAD_PAD), BF16), jax.ShapeDtypeStruct((h, t, HEAD_PAD), BF16),
                   jax.ShapeDtypeStruct((h, t, VDIM), BF16)],
        compiler_params=_params(("parallel", "parallel")),
    )(q_raw, kv_raw, z, *tabs, gq, gk)


def _heads_bwd(q_raw, kv_raw, z, tabs, gq, gk, dqh, dkh, dvh, lay, *, name):
    t = z.shape[0]
    h = lay["H"]
    tt = _tile(t, TOK_TILE, 16)
    hg = _tile(h, HEAD_GROUP, 1)
    scale = 1.0 / math.sqrt(QK_DIM)

    def body(q_ref, kv_ref, pe_ref, c_ref, sa_ref, sb_ref, gq_ref, gk_ref, dqh_ref, dkh_ref, dvh_ref,
             dq_ref, dkv_ref, dpe_ref, dgq_ref, dgk_ref):
        i, j = pl.program_id(0), pl.program_id(1)
        c_tab, sa_tab, sb_tab = c_ref[...], sa_ref[...], sb_ref[...]

        @pl.when((i == 0) & (j == 0))
        def _():
            dgq_ref[...] = jnp.zeros_like(dgq_ref)
            dgk_ref[...] = jnp.zeros_like(dgk_ref)

        @pl.when(j == 0)
        def _():
            dpe_ref[...] = jnp.zeros_like(dpe_ref)

        def norm_bwd(v, g, dy):
            r = lax.rsqrt(jnp.sum(v * v, axis=-1, keepdims=True) * (1.0 / QK_DIM) + EPS)
            dyg = dy * g
            dot = jnp.sum(dyg * v, axis=-1, keepdims=True) * (1.0 / QK_DIM)
            return r * dyg - v * (r * r * r) * dot, jnp.sum(dy * v * r, axis=0, keepdims=True)

        pe, gq_v, gk_v = pe_ref[...].astype(F32), gq_ref[...], gk_ref[...]
        dpe, dgq, dgk = jnp.zeros_like(pe), jnp.zeros_like(gq_v), jnp.zeros_like(gk_v)
        for g in range(hg):
            cols = slice(g * HEAD_PAD, (g + 1) * HEAD_PAD)
            dqo = dqh_ref[g].astype(F32) * scale
            dy = jnp.concatenate([dqo[:, :NOPE], _rope_t(dqo[:, NOPE:], c_tab, sa_tab, sb_tab)], axis=1)
            dq, dg = norm_bwd(q_ref[:, cols].astype(F32), gq_v, dy)
            dq_ref[:, cols] = dq.astype(BF16)
            dgq = dgq + dg

            dko = dkh_ref[g].astype(F32)
            dy = jnp.concatenate([dko[:, :NOPE], _rope_t(dko[:, NOPE:], c_tab, sa_tab, sb_tab)], axis=1)
            kfull = jnp.concatenate([kv_ref[:, cols].astype(F32)[:, :NOPE], pe], axis=1)
            dk, dg = norm_bwd(kfull, gk_v, dy)
            dkv_ref[:, cols] = jnp.concatenate([dk[:, :NOPE].astype(BF16), dvh_ref[g]], axis=1)
            dpe = dpe + dk[:, NOPE:]
            dgk = dgk + dg
        dpe_ref[...] += dpe
        dgq_ref[...] += dgq
        dgk_ref[...] += dgk

    head = pl.BlockSpec((tt, hg * HEAD_PAD), lambda i, j: (i, j))
    tab = pl.BlockSpec((tt, 128), lambda i, j: (i, 0))
    gain = pl.BlockSpec((1, HEAD_PAD), lambda i, j: (0, 0))
    hm = pl.BlockSpec((hg, tt, HEAD_PAD), lambda i, j: (j, i, 0))
    return _pcall(
        body, name=name, grid=(t // tt, h // hg),
        in_specs=[head, head, pl.BlockSpec((tt, 128), lambda i, j: (i, lay["KPE"] // 128)), tab, tab, tab, gain, gain,
                  hm, hm, pl.BlockSpec((hg, tt, VDIM), lambda i, j: (j, i, 0))],
        out_specs=[head, head, tab, gain, gain],
        out_shape=[jax.ShapeDtypeStruct((t, h * HEAD_PAD), BF16), jax.ShapeDtypeStruct((t, h * HEAD_PAD), BF16),
                   jax.ShapeDtypeStruct((t, 128), F32),
                   jax.ShapeDtypeStruct((1, HEAD_PAD), F32), jax.ShapeDtypeStruct((1, HEAD_PAD), F32)],
        compiler_params=_params(("arbitrary", "arbitrary")),
    )(q_raw, kv_raw, z, *tabs, gq, gk, dqh, dkh, dvh)


def _lower_triangle(n):
    return lax.broadcasted_iota(jnp.int32, (n, n), 1) <= lax.broadcasted_iota(jnp.int32, (n, n), 0)


def _qk(q, k):
    return lax.dot_general(q, k, (((1,), (1,)), ((), ())), preferred_element_type=F32)


def _flash_fwd(qh, kh, vh, z, lay, nb, *, name):
    h, t, _ = qh.shape
    s = t // nb
    tq = _tile(s, ATT_TQ, 128)
    nq = s // tq
    att_w = h * VDIM
    gblk = lay["GATT"] // VDIM

    def body(q_ref, k_ref, v_ref, g_ref, att_ref, mix_ref, lse_ref):
        i = pl.program_id(2)
        tri = _lower_triangle(tq)
        for blk in range(nq):
            @pl.when(i == blk)
            def _():
                q = q_ref[...]
                pre = blk * tq
                sd = jnp.where(tri, _qk(q, k_ref[pre:pre + tq, :]), NEG)
                m = jnp.max(sd, axis=-1, keepdims=True)
                if pre:
                    sp = _qk(q, k_ref[0:pre, :])
                    m = jnp.maximum(m, jnp.max(sp, axis=-1, keepdims=True))
                pd = jnp.exp(sd - m)
                l = jnp.sum(pd, axis=-1, keepdims=True)
                acc = jnp.dot(pd.astype(BF16), v_ref[pre:pre + tq, :], preferred_element_type=F32)
                if pre:
                    pp = jnp.exp(sp - m)
                    l = l + jnp.sum(pp, axis=-1, keepdims=True)
                    acc = acc + jnp.dot(pp.astype(BF16), v_ref[0:pre, :], preferred_element_type=F32)
                o = acc / l
                att_ref[...] = o
                g = g_ref[...].astype(F32)
                mix_ref[...] = (o * (g * _sigmoid(g))).astype(BF16)
                lse_ref[...] = m + jnp.log(l)

    row = lambda hh, b, i: (b * nq + i, hh)
    seq = lambda hh, b, i: (hh, b, 0)
    return _pcall(
        body, name=name, grid=(h, nb, nq),
        in_specs=[pl.BlockSpec((None, tq, HEAD_PAD), lambda hh, b, i: (hh, b * nq + i, 0)),
                  pl.BlockSpec((None, s, HEAD_PAD), seq),
                  pl.BlockSpec((None, s, VDIM), seq),
                  pl.BlockSpec((tq, VDIM), lambda hh, b, i: (b * nq + i, gblk + hh))],
        out_specs=[pl.BlockSpec((tq, VDIM), row), pl.BlockSpec((tq, VDIM), row),
                   pl.BlockSpec((None, tq, 1), lambda hh, b, i: (hh, b * nq + i, 0))],
        out_shape=[jax.ShapeDtypeStruct((t, att_w), F32), jax.ShapeDtypeStruct((t, 2 * att_w), BF16),
                   jax.ShapeDtypeStruct((h, t, 1), F32)],
        compiler_params=_params(("parallel", "parallel", "parallel")),
    )(qh, kh, vh, z)


def _gate_bwd(dmix, att, z, dz, lay, *, name):
    t, att_w = att.shape
    h = att_w // VDIM
    tt = _tile(t, TOK_TILE, 16)
    gblk = lay["GATT"] // att_w

    def body(dm_ref, o_ref, g_ref, dz_in, do_ref, delta_ref, dg_ref):
        del dz_in
        dm, o, g = dm_ref[...], o_ref[...], g_ref[...].astype(F32)
        sg = _sigmoid(g)
        do = dm * (g * sg)
        do_ref[...] = do.astype(BF16)
        prod = do * o
        for hh in range(h):
            delta_ref[hh] = jnp.sum(prod[:, hh * VDIM:(hh + 1) * VDIM], axis=-1, keepdims=True)
        dg_ref[...] = (dm * o * _dsilu(g, sg)).astype(BF16)

    blk = pl.BlockSpec((tt, att_w), lambda i: (i, 0))
    gate = pl.BlockSpec((tt, att_w), lambda i: (i, gblk))
    return _pcall(
        body, name=name, grid=(t // tt,),
        in_specs=[blk, blk, gate, pl.BlockSpec(memory_space=pl.ANY)],
        out_specs=[blk, pl.BlockSpec((h, tt, 1), lambda i: (0, i, 0)), gate],
        out_shape=[jax.ShapeDtypeStruct((t, att_w), BF16), jax.ShapeDtypeStruct((h, t, 1), F32),
                   jax.ShapeDtypeStruct(dz.shape, BF16)],
        input_output_aliases={3: 2},
        compiler_params=_params(("parallel",)),
    )(dmix, att, z, dz)


def _flash_bwd(qh, kh, vh, do, lse, delta, nb, *, name):
    h, t, _ = qh.shape
    s = t // nb
    tk = _tile(s, ATT_TQ, 128)
    nk = s // tk
    tn_dims = (((0,), (0,)), ((), ()))

    def body(q_ref, k_ref, v_ref, do_ref, lse_ref, dl_ref, dq_out, dk_ref, dv_ref, dq_ref):
        j = pl.program_id(2)
        tri = _lower_triangle(tk)

        @pl.when(j == 0)
        def _():
            dq_ref[...] = jnp.zeros_like(dq_ref)

        def rows_against_block(r0, r1, masked):
            q, do_v = q_ref[r0:r1, :], do_ref[r0:r1, :]
            k = k_ref[...]
            sc = _qk(q, k)
            if masked:
                sc = jnp.where(tri, sc, NEG)
            p = jnp.exp(sc - lse_ref[r0:r1, :])
            dv = lax.dot_general(p.astype(BF16), do_v, tn_dims, preferred_element_type=F32)
            ds = (p * (_qk(do_v, v_ref[...]) - dl_ref[r0:r1, :])).astype(BF16)
            dq_ref[r0:r1, :] += jnp.dot(ds, k, preferred_element_type=F32)
            return lax.dot_general(ds, q, tn_dims, preferred_element_type=F32), dv

        for blk in range(nk):
            @pl.when(j == blk)
            def _():
                r0 = blk * tk
                dk, dv = rows_against_block(r0, r0 + tk, True)
                if r0 + tk < s:
                    dk2, dv2 = rows_against_block(r0 + tk, s, False)
                    dk, dv = dk + dk2, dv + dv2
                dk_ref[...] = dk.astype(BF16)
                dv_ref[...] = dv.astype(BF16)

        @pl.when(j == nk - 1)
        def _():
            dq_out[...] = dq_ref[...].astype(BF16)

    seq = lambda hh, b, j: (hh, b, 0)
    kv = lambda hh, b, j: (hh, b * nk + j, 0)
    return _pcall(
        body, name=name, grid=(h, nb, nk),
        in_specs=[pl.BlockSpec((None, s, HEAD_PAD), seq),
                  pl.BlockSpec((None, tk, HEAD_PAD), kv),
                  pl.BlockSpec((None, tk, VDIM), kv),
                  pl.BlockSpec((s, VDIM), lambda hh, b, j: (b, hh)),
                  pl.BlockSpec((None, s, 1), seq), pl.BlockSpec((None, s, 1), seq)],
        out_specs=[pl.BlockSpec((None, s, HEAD_PAD), seq), pl.BlockSpec((None, tk, HEAD_PAD), kv),
                   pl.BlockSpec((None, tk, VDIM), kv)],
        out_shape=[jax.ShapeDtypeStruct((h, t, HEAD_PAD), BF16), jax.ShapeDtypeStruct((h, t, HEAD_PAD), BF16),
                   jax.ShapeDtypeStruct((h, t, VDIM), BF16)],
        scratch_shapes=[pltpu.VMEM((s, HEAD_PAD), F32)],
        compiler_params=_params(("parallel", "parallel", "arbitrary")),
    )(qh, kh, vh, do, lse, delta)


SUBLANES = 8


def _stage_row_shifts(ext, sh, c0, lc, rows):
    for r in range(1, SUBLANES):
        sh[r - 1, 0:rows, :] = ext[r:r + rows, c0:c0 + lc]


def _row_window(ext, sh, c0, lc, off, n):
    r = off % SUBLANES
    if r == 0:
        return ext[off:off + n, c0:c0 + lc]
    return sh[r - 1, off - r:off - r + n, :]


def _conv_fwd(z, mix, w_dw, b_dw, ln_g, ln_b, lay, nb, *, name):
    t = z.shape[0]
    cw = lay["CW"]
    s = t // nb
    tt = _tile(s, CONV_TILE, HALO)
    ns = s // tt
    hb = tt // HALO
    lc = _tile(cw, LANE_CHUNK, 128)

    def body(a_ref, b_ref, ap_ref, bp_ref, gc_ref, w_ref, bias_ref, lg_ref, lb_ref, mix_in, mix_ref, u_ref, c_ref,
             ext, sh):
        del mix_in
        i = pl.program_id(1)
        u = a_ref[...].astype(F32) * _sigmoid(b_ref[...].astype(F32))
        u_ref[...] = u
        ext[0:HALO, :] = jnp.where(i > 0, ap_ref[...].astype(F32) * _sigmoid(bp_ref[...].astype(F32)), 0.0)
        ext[HALO:HALO + tt, :] = u
        for c0 in range(0, cw, lc):
            _stage_row_shifts(ext, sh, c0, lc, tt + HALO - SUBLANES)
            acc = jnp.zeros((tt, lc), F32) + bias_ref[:, c0:c0 + lc]
            for k in range(CONV_K):
                off = HALO - (CONV_K - 1) + k
                acc = acc + w_ref[k:k + 1, c0:c0 + lc] * _row_window(ext, sh, c0, lc, off, tt)
            c_ref[:, c0:c0 + lc] = acc
        c = c_ref[...]
        mu = jnp.mean(c, axis=-1, keepdims=True)
        xc = c - mu
        var = jnp.mean(xc * xc, axis=-1, keepdims=True)
        y = xc * lax.rsqrt(var + EPS) * lg_ref[...] + lb_ref[...]
        g = gc_ref[...].astype(F32)
        mix_ref[...] = (y * _sigmoid(y) * (g * _sigmoid(g))).astype(BF16)

    cur = lambda col: pl.BlockSpec((tt, cw), lambda b, i: (b * ns + i, col))
    prev = lambda col: pl.BlockSpec((HALO, cw), lambda b, i: (jnp.maximum((b * ns + i) * hb - 1, 0), col))
    vec = pl.BlockSpec((1, cw), lambda b, i: (0, 0))
    out_row = pl.BlockSpec((tt, cw), lambda b, i: (b * ns + i, 0))
    return _pcall(
        body, name=name, grid=(nb, ns),
        in_specs=[cur(lay["A"] // cw), cur(lay["B"] // cw), prev(lay["A"] // cw), prev(lay["B"] // cw),
                  cur(lay["GCONV"] // cw), pl.BlockSpec((HALO, cw), lambda b, i: (0, 0)), vec, vec, vec,
                  pl.BlockSpec(memory_space=pl.ANY)],
        out_specs=[pl.BlockSpec((tt, cw), lambda b, i: (b * ns + i, 1)), out_row, out_row],
        out_shape=[jax.ShapeDtypeStruct(mix.shape, BF16), jax.ShapeDtypeStruct((t, cw), F32),
                   jax.ShapeDtypeStruct((t, cw), F32)],
        scratch_shapes=[pltpu.VMEM((tt + HALO, cw), F32),
                        pltpu.VMEM((SUBLANES - 1, tt + HALO - SUBLANES, lc), F32)],
        input_output_aliases={9: 0},
        compiler_params=_params(("parallel", "parallel")),
    )(z, z, z, z, z, w_dw, b_dw.reshape(1, cw), ln_g.reshape(1, cw), ln_b.reshape(1, cw), mix)


def _conv_bwd_ln(c_pre, z, dmix, ln_g, ln_b, lay, *, name):
    t, cw = c_pre.shape
    tt = _tile(t, TOK_TILE, 16)

    def body(c_ref, gc_ref, dm_ref, lg_ref, lb_ref, dc_ref, dgc_ref, dlg_ref, dlb_ref, dbias_ref):
        @pl.when(pl.program_id(0) == 0)
        def _():
            dlg_ref[...] = jnp.zeros_like(dlg_ref)
            dlb_ref[...] = jnp.zeros_like(dlb_ref)
            dbias_ref[...] = jnp.zeros_like(dbias_ref)

        c = c_ref[...]
        mu = jnp.mean(c, axis=-1, keepdims=True)
        xc = c - mu
        rstd = lax.rsqrt(jnp.mean(xc * xc, axis=-1, keepdims=True) + EPS)
        xhat = xc * rstd
        y = xhat * lg_ref[...] + lb_ref[...]
        sy = _sigmoid(y)
        g = gc_ref[...].astype(F32)
        sg = _sigmoid(g)
        dm = dm_ref[...].astype(F32)
        dgc_ref[...] = (dm * (y * sy) * _dsilu(g, sg)).astype(BF16)
        dy = dm * (g * sg) * _dsilu(y, sy)
        dlb_ref[...] += jnp.sum(dy, axis=0, keepdims=True)
        dlg_ref[...] += jnp.sum(dy * xhat, axis=0, keepdims=True)
        dxh = dy * lg_ref[...]
        dc = rstd * (dxh - jnp.mean(dxh, axis=-1, keepdims=True)
                     - xhat * jnp.mean(dxh * xhat, axis=-1, keepdims=True))
        dc_ref[...] = dc
        dbias_ref[...] += jnp.sum(dc, axis=0, keepdims=True)

    row = pl.BlockSpec((tt, cw), lambda i: (i, 0))
    vec = pl.BlockSpec((1, cw), lambda i: (0, 0))
    return _pcall(
        body, name=name, grid=(t // tt,),
        in_specs=[row, pl.BlockSpec((tt, cw), lambda i: (i, lay["GCONV"] // cw)),
                  pl.BlockSpec((tt, cw), lambda i: (i, 1)), vec, vec],
        out_specs=[row, pl.BlockSpec((tt, cw), lambda i: (i, lay["GCONV"] // cw)), vec, vec, vec],
        out_shape=[jax.ShapeDtypeStruct((t, cw), F32), jax.ShapeDtypeStruct((t, lay["NP"]), BF16),
                   jax.ShapeDtypeStruct((1, cw), F32), jax.ShapeDtypeStruct((1, cw), F32),
                   jax.ShapeDtypeStruct((1, cw), F32)],
        compiler_params=_params(("arbitrary",)),
    )(c_pre, z, dmix, ln_g.reshape(1, cw), ln_b.reshape(1, cw))


def _conv_bwd_dw(dc, u, z, w_dw, dz, lay, nb, *, name):
    t, cw = dc.shape
    s = t // nb
    tt = _tile(s, CONV_TILE, HALO)
    ns = s // tt
    hb = tt // HALO
    lc = _tile(cw, LANE_CHUNK, 128)

    def body(dc_ref, dcn_ref, u_ref, up_ref, a_ref, b_ref, w_ref, dz_in, dab_ref, dw_ref, ext_dc, ext_u, du_ref,
             sh_dc, sh_u):
        del dz_in
        b_i, i = pl.program_id(0), pl.program_id(1)

        @pl.when((b_i == 0) & (i == 0))
        def _():
            dw_ref[...] = jnp.zeros_like(dw_ref)

        dc_v = dc_ref[...]
        ext_dc[0:tt, :] = dc_v
        ext_dc[tt:tt + HALO, :] = jnp.where(i < ns - 1, dcn_ref[...], 0.0)
        ext_u[0:HALO, :] = jnp.where(i > 0, up_ref[...], 0.0)
        ext_u[HALO:HALO + tt, :] = u_ref[...]
        for c0 in range(0, cw, lc):
            _stage_row_shifts(ext_dc, sh_dc, c0, lc, tt + HALO - SUBLANES)
            _stage_row_shifts(ext_u, sh_u, c0, lc, tt + HALO - SUBLANES)
            acc = jnp.zeros((tt, lc), F32)
            dcc = dc_v[:, c0:c0 + lc]
            for k in range(CONV_K):
                acc = acc + w_ref[k:k + 1, c0:c0 + lc] * _row_window(ext_dc, sh_dc, c0, lc, CONV_K - 1 - k, tt)
                off = HALO - (CONV_K - 1) + k
                dw_ref[k:k + 1, c0:c0 + lc] += jnp.sum(dcc * _row_window(ext_u, sh_u, c0, lc, off, tt),
                                                       axis=0, keepdims=True)
            du_ref[:, c0:c0 + lc] = acc
        du = du_ref[...]
        sb = _sigmoid(b_ref[...].astype(F32))
        dab_ref[:, 0:cw] = (du * sb).astype(BF16)
        dab_ref[:, cw:2 * cw] = (du * a_ref[...].astype(F32) * sb * (1.0 - sb)).astype(BF16)

    last = nb * ns * hb - 1
    row = pl.BlockSpec((tt, cw), lambda b, i: (b * ns + i, 0))
    return _pcall(
        body, name=name, grid=(nb, ns),
        in_specs=[row, pl.BlockSpec((HALO, cw), lambda b, i: (jnp.minimum((b * ns + i + 1) * hb, last), 0)),
                  row, pl.BlockSpec((HALO, cw), lambda b, i: (jnp.maximum((b * ns + i) * hb - 1, 0), 0)),
                  pl.BlockSpec((tt, cw), lambda b, i: (b * ns + i, lay["A"] // cw)),
                  pl.BlockSpec((tt, cw), lambda b, i: (b * ns + i, lay["B"] // cw)),
                  pl.BlockSpec((HALO, cw), lambda b, i: (0, 0)), pl.BlockSpec(memory_space=pl.ANY)],
        out_specs=[pl.BlockSpec((tt, 2 * cw), lambda b, i: (b * ns + i, 0)),
                   pl.BlockSpec((HALO, cw), lambda b, i: (0, 0))],
        out_shape=[jax.ShapeDtypeStruct(dz.shape, BF16), jax.ShapeDtypeStruct((HALO, cw), F32)],
        scratch_shapes=[pltpu.VMEM((tt + HALO, cw), F32), pltpu.VMEM((tt + HALO, cw), F32),
                        pltpu.VMEM((tt, cw), F32),
                        pltpu.VMEM((SUBLANES - 1, tt + HALO - SUBLANES, lc), F32),
                        pltpu.VMEM((SUBLANES - 1, tt + HALO - SUBLANES, lc), F32)],
        input_output_aliases={7: 0},
        compiler_params=_params(("arbitrary", "arbitrary")),
    )(dc, dc, u, u, z, z, w_dw, dz)


def _loss_head(y, target, *, name):
    t, d = y.shape
    tt = _tile(t, TOK_TILE, 16)

    def body(y_ref, t_ref, sse_ref, dy_ref, dyb_ref):
        @pl.when(pl.program_id(0) == 0)
        def _():
            sse_ref[...] = jnp.zeros_like(sse_ref)

        e = y_ref[...] - t_ref[...]
        sse_ref[...] += jnp.sum(e * e)
        dy = e * (1.0 / d)
        dy_ref[...] = dy
        dyb_ref[...] = dy.astype(BF16)

    row = pl.BlockSpec((tt, d), lambda i: (i, 0))
    return _pcall(
        body, name=name, grid=(t // tt,),
        in_specs=[row, row],
        out_specs=[pl.BlockSpec((8, 128), lambda i: (0, 0)), row, row],
        out_shape=[jax.ShapeDtypeStruct((8, 128), F32), jax.ShapeDtypeStruct((t, d), F32),
                   jax.ShapeDtypeStruct((t, d), BF16)],
        compiler_params=_params(("arbitrary",)),
    )(y, target)


def _adam(w, m, v, g_parts, *, name, layer=0, layers=1, prev=None):
    rows, cols = w.shape
    slab = rows // layers
    tr = _tile(slab, max(16, ADAM_BLOCK_ELEMS // cols), 16)
    blk0 = layer * (slab // tr)
    n = len(g_parts)
    n_prev = 0 if prev is None else 4

    def body(*refs):
        w_ref, m_ref, v_ref = refs[:3]
        g_refs = refs[3:3 + n]
        g_out, d_out, m_out, v_out = refs[3 + n + n_prev:]
        g = g_refs[0][...].astype(F32)
        for r in g_refs[1:]:
            g = g + r[...].astype(F32)
        m_new = ADAM_B1 * m_ref[...] + (1.0 - ADAM_B1) * g
        v_new = ADAM_B2 * v_ref[...] + (1.0 - ADAM_B2) * (g * g)
        m_hat = m_new / (1.0 - ADAM_B1 ** ADAM_STEP)
        v_hat = v_new / (1.0 - ADAM_B2 ** ADAM_STEP)
        g_out[...] = g
        d_out[...] = -ADAM_LR * (m_hat / (jnp.sqrt(v_hat) + ADAM_EPS) + ADAM_WD * w_ref[...])
        m_out[...] = m_new
        v_out[...] = v_new

    blk = pl.BlockSpec((tr, cols), lambda i: (blk0 + i, 0))
    g_specs, g_args = [], []
    for arr, lead in g_parts:
        g_args.append(arr)
        if lead is None:
            g_specs.append(pl.BlockSpec((tr, cols), lambda i: (i, 0)))
        else:
            g_specs.append(pl.BlockSpec((None, tr, cols), functools.partial(lambda i, p: (p, i, 0), p=lead)))
    out = jax.ShapeDtypeStruct((rows, cols), F32)
    return _pcall(
        body, name=name, grid=(slab // tr,),
        in_specs=[blk, blk, blk] + g_specs + [pl.BlockSpec(memory_space=pl.ANY)] * n_prev,
        out_specs=[blk, blk, blk, blk],
        out_shape=[out, out, out, out],
        input_output_aliases={3 + n + k: k for k in range(n_prev)},
        compiler_params=_params(("parallel",)),
    )(w, m, v, *g_args, *(prev or ()))


def _position():
    return lax.axis_index("x"), lax.axis_index("y"), lax.axis_index("c")


def _block_id(p):
    return 4 * p[0] + 2 * p[1] + p[2]


def _flip(p, mask):
    return tuple((1 - v) if (mask >> (2 - a)) & 1 else v for a, v in enumerate(p))


def _all_gather(xs, *, name):
    n = len(xs)

    def body(*refs):
        x_refs, o_refs = refs[:n], refs[n:2 * n]
        send_sems, recv_sems, local_sems = refs[2 * n:]
        x, y, c = _position()
        me, sibling = (x, y, c), (x, y, 1 - c)
        chips = [(1 - x, y), (x, 1 - y), (1 - x, 1 - y)]

        def copy(t, k, block, to, src=None):
            dst = o_refs[t].at[_block_id(block)]
            return pltpu.make_async_remote_copy(
                src_ref=dst if src is None else src, dst_ref=dst,
                send_sem=send_sems.at[t, k], recv_sem=recv_sems.at[t, k],
                device_id=to, device_id_type=MESH)

        mine = [pltpu.make_async_copy(x_refs[t], o_refs[t].at[_block_id(me)], local_sems.at[t]) for t in range(n)]
        for cp in mine:
            cp.start()
        started = []
        for t in range(n):
            first = [copy(t, 0, me, sibling, src=x_refs[t])]
            first += [copy(t, 1 + j, me, (*chip, c), src=x_refs[t]) for j, chip in enumerate(chips)]
            for cp in first:
                cp.start()
            started += first
        for j, chip in enumerate(chips):
            for t in range(n):
                copy(t, 1 + j, (*chip, c), me).wait_recv()
                fwd = copy(t, 4 + j, (*chip, c), sibling)
                fwd.start()
                started.append(fwd)
        for t in range(n):
            copy(t, 0, sibling, me).wait_recv()
            for j, chip in enumerate(chips):
                copy(t, 4 + j, (*chip, 1 - c), me).wait_recv()
        for cp in started:
            cp.wait_send()
        for cp in mine:
            cp.wait()

    any_spec = pl.BlockSpec(memory_space=pl.ANY)
    return _pcall(
        body, name=name,
        in_specs=[any_spec] * n, out_specs=[any_spec] * n,
        out_shape=[jax.ShapeDtypeStruct((N_DEV,) + a.shape, a.dtype) for a in xs],
        scratch_shapes=[pltpu.SemaphoreType.DMA((n, 7)), pltpu.SemaphoreType.DMA((n, 7)),
                        pltpu.SemaphoreType.DMA((n,))],
    )(*xs)


def _pushed_copy(x_ref, land_ref, send_sems, recv_sems, t, mask, me, chunked, at_receiver):
    peer = _flip(me, mask)
    src = x_ref.at[_block_id(peer)] if chunked else x_ref
    slot = _block_id(peer) if at_receiver else _block_id(me)
    k = (N_DEV - 1) * t + mask - 1
    return pltpu.make_async_remote_copy(
        src_ref=src, dst_ref=land_ref.at[slot], send_sem=send_sems.at[k], recv_sem=recv_sems.at[k],
        device_id=peer, device_id_type=MESH)


def _push_start(xs, chunked, *, name, after=None):
    n = len(xs)
    lands = [lax.empty(a.shape if chunked else (N_DEV,) + a.shape, a.dtype) for a in xs]

    n_after = 0 if after is None else 1

    def body(*refs):
        x_refs, land_refs = refs[:n], refs[n:2 * n]
        send_sems, recv_sems = refs[2 * n + n_after], refs[2 * n + n_after + 1]
        token = refs[4 * n + n_after + 2]
        me = _position()
        for t in range(n):
            for mask in range(1, N_DEV):
                _pushed_copy(x_refs[t], land_refs[t], send_sems, recv_sems, t, mask, me, chunked, False).start()
        token[...] = jnp.zeros_like(token)

    hbm = pl.BlockSpec(memory_space=pltpu.HBM)
    sem = pl.BlockSpec(memory_space=pltpu.SEMAPHORE)
    outs = _pcall(
        body, name=name,
        in_specs=[hbm] * (2 * n) + [pl.BlockSpec(memory_space=pl.ANY)] * n_after,
        out_specs=[sem, sem] + [hbm] * (2 * n) + [pl.BlockSpec(memory_space=pltpu.VMEM)],
        out_shape=[pltpu.SemaphoreType.DMA(((N_DEV - 1) * n,)), pltpu.SemaphoreType.DMA(((N_DEV - 1) * n,))]
        + [pltpu.HBM(a.shape, a.dtype) for a in xs] + [pltpu.HBM(a.shape, a.dtype) for a in lands]
        + [jax.ShapeDtypeStruct((8, 128), F32)],
        input_output_aliases={i: 2 + i for i in range(2 * n)},
        compiler_params=pltpu.CompilerParams(has_side_effects=pltpu.SideEffectType.DATAFLOW_SIDE_EFFECTING),
    )(*[pltpu.with_memory_space_constraint(a, pltpu.HBM) for a in list(xs) + lands], *([after] * n_after))
    return outs[0], outs[1], outs[2:2 + n], outs[2 + n:2 + 2 * n], outs[2 + 2 * n]


def _push_wait(handle, after, chunked, *, name):
    send_sems, recv_sems, xs, lands, _ = handle
    n = len(xs)

    def body(*refs):
        x_refs, land_refs = refs[:n], refs[n:2 * n]
        send_sems, recv_sems = refs[2 * n], refs[2 * n + 1]
        me = _position()
        for t in range(n):
            for mask in range(1, N_DEV):
                _pushed_copy(x_refs[t], land_refs[t], send_sems, recv_sems, t, mask, me, chunked, False).wait_send()
                _pushed_copy(x_refs[t], land_refs[t], send_sems, recv_sems, t, mask, me, chunked, True).wait_recv()

    hbm = pl.BlockSpec(memory_space=pltpu.HBM)
    sem = pl.BlockSpec(memory_space=pltpu.SEMAPHORE)
    outs = _pcall(
        body, name=name,
        in_specs=[hbm] * (2 * n) + [sem, sem, pl.BlockSpec(memory_space=pl.ANY)],
        out_specs=[hbm] * (2 * n),
        out_shape=[pltpu.HBM(a.shape, a.dtype) for a in list(xs) + list(lands)],
        input_output_aliases={i: i for i in range(2 * n)},
        compiler_params=pltpu.CompilerParams(has_side_effects=pltpu.SideEffectType.DATAFLOW_SIDE_EFFECTING),
    )(*xs, *lands, send_sems, recv_sems, after)
    return outs[:n], outs[n:]


def _all_reduce_small(pack, *, name):
    rows = pack.shape[0]

    def body(p_ref, o_ref, gath, send_sems, recv_sems):
        me = _position()
        my_id = _block_id(me)
        gath[my_id] = p_ref[...]
        sent = []
        for mask in range(1, N_DEV):
            peer = _flip(me, mask)
            cp = pltpu.make_async_remote_copy(
                src_ref=p_ref, dst_ref=gath.at[my_id], send_sem=send_sems.at[mask - 1],
                recv_sem=recv_sems.at[mask - 1], device_id=peer, device_id_type=MESH)
            cp.start()
            sent.append(cp)
        for mask in range(1, N_DEV):
            slot = gath.at[_block_id(_flip(me, mask))]
            pltpu.make_async_remote_copy(
                src_ref=slot, dst_ref=slot, send_sem=send_sems.at[mask - 1], recv_sem=recv_sems.at[mask - 1],
                device_id=me, device_id_type=MESH).wait_recv()
        for cp in sent:
            cp.wait_send()
        total = gath[0]
        for s in range(1, N_DEV):
            total = total + gath[s]
        o_ref[...] = total

    vm = pl.BlockSpec(memory_space=pltpu.VMEM)
    return _pcall(
        body, name=name,
        in_specs=[vm], out_specs=vm,
        out_shape=jax.ShapeDtypeStruct(pack.shape, F32),
        scratch_shapes=[pltpu.VMEM((N_DEV, rows, 128), F32), pltpu.SemaphoreType.DMA((7,)),
                        pltpu.SemaphoreType.DMA((7,))],
        compiler_params=pltpu.CompilerParams(vmem_limit_bytes=VMEM_LIMIT),
    )(pack)


def _layout(d, ql, kvl):
    cw = d // 2
    att = d // 2
    lay = {"D": d, "CW": cw, "ATT": att, "H": att // VDIM, "QL": ql, "KVL": kvl}
    lay["A"], lay["B"], lay["GATT"], lay["GCONV"] = 0, cw, 2 * cw, 2 * cw + att
    lay["QC"] = lay["GCONV"] + cw
    lay["KVC"] = lay["QC"] + ql
    lay["KPE"] = lay["KVC"] + kvl
    used = lay["KPE"] + 128
    tn = min(MM_TN, 1024)
    lay["NP"] = -(-used // tn) * tn
    assert att == cw and lay["QC"] % ql == 0 and lay["KVC"] % kvl == 0 and lay["KPE"] % 128 == 0
    lay["o_kv"], lay["o_pe"] = ql, ql + kvl
    lay["o_ga"] = lay["o_pe"] + ROPE
    lay["o_u"] = lay["o_ga"] + att
    lay["o_gc"] = lay["o_u"] + 2 * cw
    lay["IN_COLS"] = lay["o_gc"] + cw
    return lay


def _lane_pad(n):
    return -(-n // 128) * 128


def _assemble_w_in(g, shard, lay, *, name):
    _, d, padw = g.shape
    tr = _tile(d, 256, 16)
    sections = [(lay["A"], 2 * lay["CW"], lay["o_u"]), (lay["GATT"], lay["ATT"], lay["o_ga"]),
                (lay["GCONV"], lay["CW"], lay["o_gc"]), (lay["QC"], lay["QL"], 0),
                (lay["KVC"], lay["KVL"], lay["o_kv"]), (lay["KPE"], ROPE, lay["o_pe"])]

    def runs_of_tile(j):
        for start, width, orig in sections:
            if start <= j * 128 < start + width:
                todo, col, lane, out = min(128, start + width - j * 128), orig + j * 128 - start, 0, []
                while todo:
                    p, o = divmod(col, shard)
                    take = min(todo, shard - o)
                    first = (o // 128) * 128
                    win = 256 if first + 256 <= padw else 128
                    out.append((p, first, win, o - first, lane, take))
                    col, lane, todo = col + take, lane + take, todo - take
                return out
        return []

    def body(g_ref, o_ref):
        movers = {}

        def mover(win, off, lane, take):
            key = (win, off, lane, take)
            if key not in movers:
                row = lax.broadcasted_iota(jnp.int32, (win, 128), 0)
                col = lax.broadcasted_iota(jnp.int32, (win, 128), 1)
                hit = (row - off == col - lane) & (col >= lane) & (col < lane + take)
                movers[key] = jnp.where(hit, 1.0, 0.0).astype(BF16)
            return movers[key]

        for j in range(lay["NP"] // 128):
            tile = None
            for p, first, win, off, lane, take in runs_of_tile(j):
                part = jnp.dot(g_ref[p, :, first:first + win], mover(win, off, lane, take),
                               preferred_element_type=F32)
                tile = part if tile is None else tile + part
            if tile is None:
                tile = jnp.zeros((tr, 128), F32)
            o_ref[:, j * 128:(j + 1) * 128] = tile.astype(BF16)

    return _pcall(
        body, name=name, grid=(d // tr,),
        in_specs=[pl.BlockSpec((N_DEV, tr, padw), lambda i: (0, i, 0))],
        out_specs=pl.BlockSpec((tr, lay["NP"]), lambda i: (i, 0)),
        out_shape=jax.ShapeDtypeStruct((d, lay["NP"]), BF16),
        compiler_params=_params(("parallel",)),
    )(g)


def _ungather_cols(g):
    return jnp.transpose(g, (1, 0, 2)).reshape(g.shape[1], -1)


def _to_col_blocks(w):
    r, c = w.shape
    return jnp.transpose(w.reshape(r, N_DEV, c // N_DEV), (1, 0, 2))


def _pad_w_in(w, lay):
    d, cw, att, ql, kvl = lay["D"], lay["CW"], lay["ATT"], lay["QL"], lay["KVL"]
    parts = [w[:, lay["o_u"]:lay["o_u"] + 2 * cw], w[:, lay["o_ga"]:lay["o_ga"] + att],
             w[:, lay["o_gc"]:lay["o_gc"] + cw], w[:, :ql], w[:, lay["o_kv"]:lay["o_kv"] + kvl],
             w[:, lay["o_pe"]:lay["o_pe"] + ROPE],
             jnp.zeros((d, lay["NP"] - lay["KPE"] - ROPE), w.dtype)]
    return jnp.concatenate(parts, axis=1)


def _unpad_w_in(wp, lay):
    cw, att, ql, kvl = lay["CW"], lay["ATT"], lay["QL"], lay["KVL"]
    parts = [wp[:, lay["QC"]:lay["QC"] + ql], wp[:, lay["KVC"]:lay["KVC"] + kvl],
             wp[:, lay["KPE"]:lay["KPE"] + ROPE], wp[:, lay["GATT"]:lay["GATT"] + att],
             wp[:, :2 * cw], wp[:, lay["GCONV"]:lay["GCONV"] + cw]]
    return jnp.concatenate(parts, axis=1)


def _pad_heads(w, h):
    r = w.shape[0]
    return jnp.pad(w.reshape(r, h, QK_DIM), ((0, 0), (0, 0), (0, HEAD_PAD - QK_DIM))).reshape(r, h * HEAD_PAD)


def _unpad_heads(w, h):
    r = w.shape[0]
    return w.reshape(r, h, HEAD_PAD)[:, :, :QK_DIM].reshape(r, h * QK_DIM)


def _rope_tabs(positions):
    half = ROPE // 2
    inv_freq = ROPE_THETA ** (-jnp.arange(half, dtype=F32) / half)
    ang = positions.astype(F32).reshape(-1)[:, None] * inv_freq
    cos, sin = jnp.cos(ang), jnp.sin(ang)
    zero = jnp.zeros_like(cos)
    return (jnp.concatenate([cos, cos, zero, zero], axis=1),
            jnp.concatenate([-sin, zero, zero, zero], axis=1),
            jnp.concatenate([zero, sin, zero, zero], axis=1))


def _pack_rows(vecs):
    rows = []
    for v in vecs:
        flat = v.reshape(-1)
        pad = (-flat.shape[0]) % 1024
        rows.append(jnp.pad(flat, (0, pad)).reshape(-1, 128))
    return jnp.concatenate(rows, axis=0)


def _unpack_rows(pack, shapes):
    out, r0 = [], 0
    for shp in shapes:
        size = math.prod(shp)
        nrows = -(-size // 1024) * 8
        out.append(pack[r0:r0 + nrows].reshape(-1)[:size].reshape(shp))
        r0 += nrows
    return out


def kernel(x, positions, ln_g, w_in, q_a_norm, w_q_up, kv_a_norm, w_kv_up, q_norm, k_norm, w_dw, b_dw, conv_ln_g, conv_ln_b, w_out, loss_target, m_ln_g, m_w_in, m_q_a_norm, m_w_q_up, m_kv_a_norm, m_w_kv_up, m_q_norm, m_k_norm, m_w_dw, m_b_dw, m_conv_ln_g, m_conv_ln_b, m_w_out, v_ln_g, v_w_in, v_q_a_norm, v_w_q_up, v_kv_a_norm, v_w_kv_up, v_q_norm, v_k_norm, v_w_dw, v_b_dw, v_conv_ln_g, v_conv_ln_b, v_w_out):
    nb, seq, d = x.shape
    depth = ln_g.shape[0]
    lay = _layout(d, q_a_norm.shape[1], kv_a_norm.shape[1])
    h, cw, ql, kvl = lay["H"], lay["CW"], lay["QL"], lay["KVL"]
    t = nb * seq
    my_id = _block_id(_position())

    shard_in = w_in.shape[2]

    def shards(l):
        padded = jnp.pad(w_in[l].astype(BF16), ((0, 0), (0, _lane_pad(shard_in) - shard_in)))
        return [padded, w_q_up[l].astype(BF16), w_kv_up[l].astype(BF16), w_out[l].astype(BF16)]

    def fill_own(lands, own):
        return [lax.dynamic_update_index_in_dim(land, blk, my_id, 0) for land, blk in zip(lands, own)]

    def layout_in(g_in, l):
        return {"in": _assemble_w_in(g_in, shard_in, lay, name=f"assemble_w_in_{l}")}

    def layout_rest(g_q, g_kv, g_out):
        return {"q": _pad_heads(_ungather_cols(g_q), h), "kv": _ungather_cols(g_kv), "out": g_out.reshape(2 * cw, d)}

    first = shards(0)
    g_in0, g_dw = _all_gather([first[0], w_dw], name="gather_w_in_0")
    gathers = {0: _push_start(first[1:], False, after=g_in0, name="gather_start_0")}
    for l in range(1, depth):
        gathers[l] = _push_start(shards(l), False, after=gathers[l - 1][4], name=f"gather_start_{l}")
    fwd_dep = gathers[depth - 1][4]
    weights = []

    tabs = _rope_tabs(positions)
    gq_pad = jnp.pad(q_norm, ((0, 0), (0, HEAD_PAD - QK_DIM)))
    gk_pad = jnp.pad(k_norm, ((0, 0), (0, HEAD_PAD - QK_DIM)))
    w_dw_all = jnp.transpose(g_dw, (1, 2, 0, 3)).reshape(depth, CONV_K, cw)
    w_dw_all = jnp.pad(w_dw_all, ((0, 0), (0, HALO - CONV_K), (0, 0)))

    saved = []
    xs = x.reshape(t, d)
    for l in range(depth):
        hid = _rms_fwd(xs, ln_g[l], dep=fwd_dep if l == 0 else None, name=f"rms_fwd_{l}")
        if l == 0:
            weights.append(layout_in(g_in0, 0))
        z = _mm(hid, weights[l]["in"], out_dtype=BF16, name=f"in_proj_{l}")
        if l == 0:
            own, lands = _push_wait(gathers[0], z, False, name="gather_wait_0")
            weights[0].update(layout_rest(*fill_own(lands, own)))
        wl = weights[l]
        qn, kvn = _lat_fwd(z, q_a_norm[l], kv_a_norm[l], lay, name=f"lat_fwd_{l}")
        q_raw = _mm(qn, wl["q"], out_dtype=BF16, name=f"q_up_{l}")
        kv_raw = _mm(kvn, wl["kv"], out_dtype=BF16, name=f"kv_up_{l}")
        qh, kh, vh = _heads_fwd(q_raw, kv_raw, z, tabs, gq_pad[l:l + 1], gk_pad[l:l + 1], lay, name=f"heads_fwd_{l}")
        att, mix, lse = _flash_fwd(qh, kh, vh, z, lay, nb, name=f"flash_fwd_{l}")
        mix, u, c_pre = _conv_fwd(z, mix, w_dw_all[l], b_dw[l], conv_ln_g[l], conv_ln_b[l], lay, nb,
                                  name=f"conv_fwd_{l}")
        x_next = _mm(mix, wl["out"], add=xs, name=f"out_proj_{l}")
        saved.append((xs, hid, z, qn, kvn, q_raw, kv_raw, qh, kh, vh, att, lse, mix, u, c_pre))
        xs = x_next
        if l + 1 < depth:
            own, lands = _push_wait(gathers[l + 1], xs, False, name=f"gather_wait_{l + 1}")
            g_in, *g_rest = fill_own(lands, own)
            weights.append({**layout_in(g_in, l + 1), **layout_rest(*g_rest)})

    sse, dx, dxb = _loss_head(xs, loss_target.reshape(t, d), name="loss_head")
    loss = lax.psum(sse[0, 0] * (0.5 / d), ("x", "y", "c"))

    small = {k: [] for k in ("ln_g", "q_a", "kv_a", "q_n", "k_n", "w_dw", "b_dw", "cln_g", "cln_b")}
    scatters, bwd_dep = {}, None
    for l in reversed(range(depth)):
        xs, hid, z, qn, kvn, q_raw, kv_raw, qh, kh, vh, att, lse, mix, u, c_pre = saved[l]
        wl = weights[l]
        dmix = _mm(dxb, wl["out"], trans_b=True, dep=bwd_dep, name=f"d_mix_{l}")
        dw_out = _mm(mix, dxb, trans_a=True, out_dtype=BF16, name=f"dw_out_{l}")
        dc, dz, dlg, dlb, dbias = _conv_bwd_ln(c_pre, z, dmix, conv_ln_g[l], conv_ln_b[l], lay,
                                               name=f"conv_bwd_ln_{l}")
        dz, dwdw = _conv_bwd_dw(dc, u, z, w_dw_all[l], dz, lay, nb, name=f"conv_bwd_dw_{l}")
        do, delta, dz = _gate_bwd(dmix, att, z, dz, lay, name=f"gate_bwd_{l}")
        dqh, dkh, dvh = _flash_bwd(qh, kh, vh, do, lse, delta, nb, name=f"flash_bwd_{l}")
        dq_raw, dkv_raw, dpe, dgq, dgk = _heads_bwd(q_raw, kv_raw, z, tabs, gq_pad[l:l + 1], gk_pad[l:l + 1],
                                                    dqh, dkh, dvh, lay, name=f"heads_bwd_{l}")
        dqn = _mm(dq_raw, wl["q"], trans_b=True, name=f"d_qn_{l}")
        dkvn = _mm(dkv_raw, wl["kv"], trans_b=True, name=f"d_kvn_{l}")
        dw_q = _mm(qn, dq_raw, trans_a=True, out_dtype=BF16, name=f"dw_q_{l}")
        dw_kv = _mm(kvn, dkv_raw, trans_a=True, out_dtype=BF16, name=f"dw_kv_{l}")
        early = _push_start(
            [_to_col_blocks(_unpad_heads(dw_q, h)), _to_col_blocks(dw_kv), dw_out.reshape(N_DEV, (2 * cw) // N_DEV, d)],
            True, name=f"scatter_start_a_{l}")
        dz, dgqa, dgkva = _lat_bwd(z, q_a_norm[l], kv_a_norm[l], dqn, dkvn, dpe, dz, lay, name=f"lat_bwd_{l}")
        dw_in = _mm(hid, dz, trans_a=True, out_dtype=BF16, dep=early[4], name=f"dw_in_{l}")
        late = _push_start([_to_col_blocks(_unpad_w_in(dw_in, lay))], True, name=f"scatter_start_b_{l}")
        scatters[l] = (late, early)
        bwd_dep = late[4]
        dh = _mm(dz, wl["in"], trans_b=True, dep=bwd_dep, name=f"d_hid_{l}")
        dx, dxb, dlng = _rms_bwd(xs, ln_g[l], dh, dx, name=f"rms_bwd_{l}")
        for key, val in (("ln_g", dlng), ("q_a", dgqa), ("kv_a", dgkva), ("q_n", dgq[:, :QK_DIM]),
                         ("k_n", dgk[:, :QK_DIM]), ("w_dw", dwdw[:CONV_K]), ("b_dw", dbias),
                         ("cln_g", dlg), ("cln_b", dlb)):
            small[key].append(val)
    grad_x = dx.reshape(nb, seq, d)
    for key in small:
        small[key] = jnp.stack(small[key][::-1])

    small_names = ("ln_g", "q_a", "kv_a", "q_n", "k_n", "b_dw", "cln_g", "cln_b", "w_dw")
    small_shapes = [small[k].shape for k in small_names]
    summed = _unpack_rows(_all_reduce_small(_pack_rows([small[k] for k in small_names]), name="reduce_small_grads"),
                          small_shapes)
    sg = dict(zip(small_names, summed))
    g_w_dw = lax.dynamic_slice_in_dim(sg["w_dw"], my_id * (cw // N_DEV), cw // N_DEV, axis=2)

    def adam_small(ws, ms, vs, gs, nm):
        shapes = [w.shape for w in ws]
        outs = _adam(_pack_rows(ws), _pack_rows(ms), _pack_rows(vs), [(_pack_rows(gs), None)], name=nm)
        return [_unpack_rows(o, shapes) for o in outs]

    big = [("w_in", w_in, m_w_in, v_w_in), ("w_q_up", w_q_up, m_w_q_up, v_w_q_up),
           ("w_kv_up", w_kv_up, m_w_kv_up, v_w_kv_up), ("w_out", w_out, m_w_out, v_w_out)]
    res, prev = {}, [None] * len(big)
    for l in reversed(range(depth)):
        own, lands = [], []
        for half, tag in zip(scatters[l], "ba"):
            sent, landed = _push_wait(half, dx, True, name=f"scatter_wait_{tag}_{l}")
            own += [lax.dynamic_index_in_dim(o, my_id, 0, keepdims=False) for o in sent]
            lands += landed
        for idx, ((nm, w, m, v), recv) in enumerate(zip(big, fill_own(lands, own))):
            rows, cols = w.shape[1], w.shape[2]
            flat = lambda a: a.reshape(depth * rows, cols)
            prev[idx] = _adam(flat(w), flat(m), flat(v), [(recv, s) for s in range(N_DEV)], layer=l, layers=depth,
                              prev=prev[idx], name=f"adam_{nm}_{l}")
    for idx, (nm, w, _, _) in enumerate(big):
        res[nm] = [o.reshape(w.shape) for o in prev[idx]]
    names_s = ["ln_g", "q_a_norm", "kv_a_norm", "q_norm", "k_norm", "w_dw", "b_dw", "conv_ln_g", "conv_ln_b"]
    ws = [ln_g, q_a_norm, kv_a_norm, q_norm, k_norm, w_dw, b_dw, conv_ln_g, conv_ln_b]
    ms = [m_ln_g, m_q_a_norm, m_kv_a_norm, m_q_norm, m_k_norm, m_w_dw, m_b_dw, m_conv_ln_g, m_conv_ln_b]
    vs = [v_ln_g, v_q_a_norm, v_kv_a_norm, v_q_norm, v_k_norm, v_w_dw, v_b_dw, v_conv_ln_g, v_conv_ln_b]
    gs = [sg["ln_g"].reshape(ln_g.shape), sg["q_a"].reshape(q_a_norm.shape), sg["kv_a"].reshape(kv_a_norm.shape),
          sg["q_n"].reshape(q_norm.shape), sg["k_n"].reshape(k_norm.shape), g_w_dw,
          sg["b_dw"].reshape(b_dw.shape), sg["cln_g"].reshape(conv_ln_g.shape), sg["cln_b"].reshape(conv_ln_b.shape)]
    outs_s = adam_small(ws, ms, vs, gs, "adam_small")
    for idx, nm in enumerate(names_s):
        res[nm] = [outs_s[k][idx] for k in range(4)]

    order = ["ln_g", "w_in", "q_a_norm", "w_q_up", "kv_a_norm", "w_kv_up", "q_norm", "k_norm", "w_dw", "b_dw",
             "conv_ln_g", "conv_ln_b", "w_out"]
    return (loss, grad_x, *[res[nm][0] for nm in order], *[res[nm][1] for nm in order],
            *[res[nm][2] for nm in order], *[res[nm][3] for nm in order])
```

```python
import functools
import math

import jax
import jax.numpy as jnp
from jax import lax
from jax.experimental import pallas as pl
from jax.experimental.pallas import tpu as pltpu

F32 = jnp.float32
BF16 = jnp.bfloat16
MESH = pl.DeviceIdType.MESH

N_DEV = 8
NOPE = 128
ROPE = 64
VDIM = 128
HEAD_PAD = 256
QK_DIM = NOPE + ROPE
CONV_K = 31
HALO = 32
EPS = 1e-6
ROPE_THETA = 10000.0
NEG = -1e30

ADAM_LR = 0.001
ADAM_B1 = 0.9
ADAM_B2 = 0.999
ADAM_EPS = 1e-08
ADAM_WD = 0.01
ADAM_STEP = 10

TOK_TILE = 256
CONV_TILE = 256
ATT_TQ = 512
HEAD_GROUP = 4
MM_TM = 1024
MM_TN = 512
MM_TK = 4096
MM_VMEM_BUDGET = 46 * 1024 * 1024
ADAM_BLOCK_ELEMS = 128 * 1024
LANE_CHUNK = 256
VMEM_LIMIT = 56 * 1024 * 1024


def _pcall(body, **kw):
    return pl.pallas_call(body, **kw)


def _tile(dim, pref, mult):
    t = min(pref, dim)
    t -= t % mult
    while t >= mult:
        if dim % t == 0:
            return t
        t -= mult
    return dim


def _params(sem):
    return pltpu.CompilerParams(dimension_semantics=sem, vmem_limit_bytes=VMEM_LIMIT)


def _sigmoid(v):
    return 1.0 / (1.0 + jnp.exp(-v))


def _dsilu(v, sg):
    return sg * (1.0 + v * (1.0 - sg))


def _mm_tiles(m, n, kdim, out_bytes, has_add):
    def need(tm, tn, tk):
        return (2 * 2 * (tm * tk + tk * tn) + 2 * tm * tn * out_bytes
                + tm * tn * 4 * ((kdim > tk) + 2 * has_add + 1))

    shapes = [(_tile(m, pm, 128), _tile(n, pn, 128))
              for pm, pn in ((MM_TM, MM_TN), (MM_TM // 2, MM_TN), (MM_TM // 2, MM_TN // 2), (MM_TM // 4, MM_TN // 2))]
    for tm, tn in shapes:
        if need(tm, tn, kdim) <= MM_VMEM_BUDGET:
            return tm, tn, kdim
    tk = _tile(kdim, MM_TK, 128)
    for tm, tn in shapes:
        if need(tm, tn, tk) <= MM_VMEM_BUDGET:
            break
    return tm, tn, tk


def _mm(a, b, *, name, trans_a=False, trans_b=False, add=None, out_dtype=F32, dep=None):
    assert not (trans_a and trans_b)
    if trans_a:
        kdim, m = a.shape
    else:
        m, kdim = a.shape
    n = b.shape[0] if trans_b else b.shape[1]
    assert b.shape[1 if trans_b else 0] == kdim
    has_add = add is not None
    tm, tn, tk = _mm_tiles(m, n, kdim, jnp.dtype(out_dtype).itemsize, has_add)
    nk = kdim // tk
    contract = (((0 if trans_a else 1,), (1 if trans_b else 0,)), ((), ()))

    def product(a_ref, b_ref):
        return lax.dot_general(a_ref[...], b_ref[...], contract, preferred_element_type=F32)

    def body(*refs):
        a_ref, b_ref = refs[:2]
        add_ref = refs[2] if has_add else None
        o_ref = refs[2 + has_add + (dep is not None)]

        def finish(r):
            if has_add:
                r = r + add_ref[...]
            o_ref[...] = r.astype(o_ref.dtype)

        if nk == 1:
            finish(product(a_ref, b_ref))
            return
        acc_ref = refs[-1]
        k = pl.program_id(2)

        @pl.when(k == 0)
        def _():
            acc_ref[...] = product(a_ref, b_ref)

        @pl.when((k > 0) & (k < nk - 1))
        def _():
            acc_ref[...] += product(a_ref, b_ref)

        @pl.when(k == nk - 1)
        def _():
            finish(acc_ref[...] + product(a_ref, b_ref))

    if trans_a:
        a_spec = pl.BlockSpec((tk, tm), lambda i, j, k: (k, i))
    else:
        a_spec = pl.BlockSpec((tm, tk), lambda i, j, k: (i, k))
    if trans_b:
        b_spec = pl.BlockSpec((tn, tk), lambda i, j, k: (j, k))
    else:
        b_spec = pl.BlockSpec((tk, tn), lambda i, j, k: (k, j))
    in_specs = [a_spec, b_spec]
    args = [a, b]
    if has_add:
        in_specs.append(pl.BlockSpec((tm, tn), lambda i, j, k: (i, j)))
        args.append(add)
    if dep is not None:
        in_specs.append(pl.BlockSpec(memory_space=pl.ANY))
        args.append(dep)
    return _pcall(
        body, name=name,
        grid=(m // tm, n // tn, nk),
        in_specs=in_specs,
        out_specs=pl.BlockSpec((tm, tn), lambda i, j, k: (i, j)),
        out_shape=jax.ShapeDtypeStruct((m, n), out_dtype),
        scratch_shapes=[pltpu.VMEM((tm, tn), F32)] if nk > 1 else [],
        compiler_params=_params(("parallel", "parallel", "arbitrary")),
    )(*args)


def _rms_fwd(x, g, *, name, dep=None):
    t, d = x.shape
    tt = _tile(t, TOK_TILE, 16)

    def body(x_ref, g_ref, *rest):
        h_ref = rest[-1]
        xv = x_ref[...]
        r = lax.rsqrt(jnp.mean(xv * xv, axis=-1, keepdims=True) + EPS)
        h_ref[...] = (xv * r * g_ref[...]).astype(BF16)

    deps = [] if dep is None else [dep]
    return _pcall(
        body, name=name, grid=(t // tt,),
        in_specs=[pl.BlockSpec((tt, d), lambda i: (i, 0)), pl.BlockSpec((1, d), lambda i: (0, 0))]
        + [pl.BlockSpec(memory_space=pl.ANY)] * len(deps),
        out_specs=pl.BlockSpec((tt, d), lambda i: (i, 0)),
        out_shape=jax.ShapeDtypeStruct((t, d), BF16),
        compiler_params=_params(("parallel",)),
    )(x, g.reshape(1, d), *deps)


def _rms_bwd(x, g, dh, dres, *, name):
    t, d = x.shape
    tt = _tile(t, TOK_TILE, 16)

    def body(x_ref, g_ref, dh_ref, dres_ref, dx_ref, dxb_ref, dg_ref):
        xv = x_ref[...]
        r = lax.rsqrt(jnp.mean(xv * xv, axis=-1, keepdims=True) + EPS)
        dy = dh_ref[...]
        dyg = dy * g_ref[...]
        dot = jnp.sum(dyg * xv, axis=-1, keepdims=True) * (1.0 / d)
        dx = dres_ref[...] + r * dyg - xv * (r * r * r) * dot
        dx_ref[...] = dx
        dxb_ref[...] = dx.astype(BF16)

        @pl.when(pl.program_id(0) == 0)
        def _():
            dg_ref[...] = jnp.zeros_like(dg_ref)

        dg_ref[...] += jnp.sum(dy * xv * r, axis=0, keepdims=True)

    row = pl.BlockSpec((tt, d), lambda i: (i, 0))
    vec = pl.BlockSpec((1, d), lambda i: (0, 0))
    return _pcall(
        body, name=name, grid=(t // tt,),
        in_specs=[row, vec, row, row],
        out_specs=[row, row, vec],
        out_shape=[jax.ShapeDtypeStruct((t, d), F32), jax.ShapeDtypeStruct((t, d), BF16),
                   jax.ShapeDtypeStruct((1, d), F32)],
        compiler_params=_params(("arbitrary",)),
    )(x, g.reshape(1, d), dh, dres)


def _lat_fwd(z, gq, gkv, lay, *, name):
    t = z.shape[0]
    ql, kvl = lay["QL"], lay["KVL"]
    tt = _tile(t, TOK_TILE, 16)

    def body(q_ref, kv_ref, gq_ref, gkv_ref, qn_ref, kvn_ref):
        for src, g_ref, dst in ((q_ref, gq_ref, qn_ref), (kv_ref, gkv_ref, kvn_ref)):
            v = src[...].astype(F32)
            r = lax.rsqrt(jnp.mean(v * v, axis=-1, keepdims=True) + EPS)
            dst[...] = (v * r * g_ref[...]).astype(BF16)

    return _pcall(
        body, name=name, grid=(t // tt,),
        in_specs=[pl.BlockSpec((tt, ql), lambda i: (i, lay["QC"] // ql)),
                  pl.BlockSpec((tt, kvl), lambda i: (i, lay["KVC"] // kvl)),
                  pl.BlockSpec((1, ql), lambda i: (0, 0)), pl.BlockSpec((1, kvl), lambda i: (0, 0))],
        out_specs=[pl.BlockSpec((tt, ql), lambda i: (i, 0)), pl.BlockSpec((tt, kvl), lambda i: (i, 0))],
        out_shape=[jax.ShapeDtypeStruct((t, ql), BF16), jax.ShapeDtypeStruct((t, kvl), BF16)],
        compiler_params=_params(("parallel",)),
    )(z, z, gq.reshape(1, ql), gkv.reshape(1, kvl))


def _lat_bwd(z, gq, gkv, dqn, dkvn, dpe, dz, lay, *, name):
    t = z.shape[0]
    ql, kvl = lay["QL"], lay["KVL"]
    tail = lay["NP"] - lay["QC"]
    assert lay["QC"] % tail == 0
    tt = _tile(t, TOK_TILE, 16)

    def body(q_ref, kv_ref, gq_ref, gkv_ref, dqn_ref, dkvn_ref, dpe_ref, dz_in, tail_ref, dgq_ref, dgkv_ref):
        del dz_in
        first = pl.program_id(0) == 0
        for src, g_ref, dy_ref, c0, dg_ref in ((q_ref, gq_ref, dqn_ref, 0, dgq_ref),
                                               (kv_ref, gkv_ref, dkvn_ref, ql, dgkv_ref)):
            v = src[...].astype(F32)
            n = v.shape[-1]
            r = lax.rsqrt(jnp.mean(v * v, axis=-1, keepdims=True) + EPS)
            dy = dy_ref[...]
            dyg = dy * g_ref[...]
            dot = jnp.sum(dyg * v, axis=-1, keepdims=True) * (1.0 / n)
            tail_ref[:, c0:c0 + n] = (r * dyg - v * (r * r * r) * dot).astype(BF16)

            @pl.when(first)
            def _():
                dg_ref[...] = jnp.zeros_like(dg_ref)

            dg_ref[...] += jnp.sum(dy * v * r, axis=0, keepdims=True)
        tail_ref[:, ql + kvl:ql + kvl + 128] = dpe_ref[...].astype(BF16)
        tail_ref[:, ql + kvl + 128:tail] = jnp.zeros((tt, tail - ql - kvl - 128), BF16)

    return _pcall(
        body, name=name, grid=(t // tt,),
        in_specs=[pl.BlockSpec((tt, ql), lambda i: (i, lay["QC"] // ql)),
                  pl.BlockSpec((tt, kvl), lambda i: (i, lay["KVC"] // kvl)),
                  pl.BlockSpec((1, ql), lambda i: (0, 0)), pl.BlockSpec((1, kvl), lambda i: (0, 0)),
                  pl.BlockSpec((tt, ql), lambda i: (i, 0)), pl.BlockSpec((tt, kvl), lambda i: (i, 0)),
                  pl.BlockSpec((tt, 128), lambda i: (i, 0)), pl.BlockSpec(memory_space=pl.ANY)],
        out_specs=[pl.BlockSpec((tt, tail), lambda i: (i, lay["QC"] // tail)),
                   pl.BlockSpec((1, ql), lambda i: (0, 0)), pl.BlockSpec((1, kvl), lambda i: (0, 0))],
        out_shape=[jax.ShapeDtypeStruct(dz.shape, BF16),
                   jax.ShapeDtypeStruct((1, ql), F32), jax.ShapeDtypeStruct((1, kvl), F32)],
        input_output_aliases={7: 0},
        compiler_params=_params(("arbitrary",)),
    )(z, z, gq.reshape(1, ql), gkv.reshape(1, kvl), dqn, dkvn, dpe, dz)


def _rope(r, c_tab, sa_tab, sb_tab):
    return r * c_tab + pltpu.roll(r, 96, 1) * sa_tab + pltpu.roll(r, 32, 1) * sb_tab


def _rope_t(d, c_tab, sa_tab, sb_tab):
    return d * c_tab + pltpu.roll(d * sa_tab, 32, 1) + pltpu.roll(d * sb_tab, 96, 1)


def _heads_fwd(q_raw, kv_raw, z, tabs, gq, gk, lay, *, name):
    t = z.shape[0]
    h = lay["H"]
    tt = _tile(t, TOK_TILE, 16)
    hg = _tile(h, HEAD_GROUP, 1)
    scale = 1.0 / math.sqrt(QK_DIM)

    def body(q_ref, kv_ref, pe_ref, c_ref, sa_ref, sb_ref, gq_ref, gk_ref, qh_ref, kh_ref, vh_ref):
        c_tab, sa_tab, sb_tab = c_ref[...], sa_ref[...], sb_ref[...]
        pe, gq_v, gk_v = pe_ref[...].astype(F32), gq_ref[...], gk_ref[...]
        ss_pe = jnp.sum(pe * pe, axis=-1, keepdims=True)
        for g in range(hg):
            q = q_ref[:, g * HEAD_PAD:(g + 1) * HEAD_PAD].astype(F32)
            r = lax.rsqrt(jnp.sum(q * q, axis=-1, keepdims=True) * (1.0 / QK_DIM) + EPS)
            qn = q * r * gq_v
            qh_ref[g] = (jnp.concatenate([qn[:, :NOPE], _rope(qn[:, NOPE:], c_tab, sa_tab, sb_tab)], axis=1)
                         * scale).astype(BF16)
            kv = kv_ref[:, g * HEAD_PAD:(g + 1) * HEAD_PAD].astype(F32)
            kn = kv[:, :NOPE]
            rk = lax.rsqrt((jnp.sum(kn * kn, axis=-1, keepdims=True) + ss_pe) * (1.0 / QK_DIM) + EPS)
            kh_ref[g] = jnp.concatenate(
                [kn * rk * gk_v[:, :NOPE], _rope(pe * rk * gk_v[:, NOPE:], c_tab, sa_tab, sb_tab)],
                axis=1).astype(BF16)
            vh_ref[g] = kv[:, NOPE:].astype(BF16)

    head = pl.BlockSpec((tt, hg * HEAD_PAD), lambda i, j: (i, j))
    tab = pl.BlockSpec((tt, 128), lambda i, j: (i, 0))
    gain = pl.BlockSpec((1, HEAD_PAD), lambda i, j: (0, 0))
    return _pcall(
        body, name=name, grid=(t // tt, h // hg),
        in_specs=[head, head, pl.BlockSpec((tt, 128), lambda i, j: (i, lay["KPE"] // 128)), tab, tab, tab, gain, gain],
        out_specs=[pl.BlockSpec((hg, tt, HEAD_PAD), lambda i, j: (j, i, 0)),
                   pl.BlockSpec((hg, tt, HEAD_PAD), lambda i, j: (j, i, 0)),
                   pl.BlockSpec((hg, tt, VDIM), lambda i, j: (j, i, 0))],
        out_shape=[jax.ShapeDtypeStruct((h, t, HEAD_PAD), BF16), jax.ShapeDtypeStruct((h, t, HEAD_PAD), BF16),
                   jax.ShapeDtypeStruct((h, t, VDIM), BF16)],
        compiler_params=_params(("parallel", "parallel")),
    )(q_raw, kv_raw, z, *tabs, gq, gk)


def _heads_bwd(q_raw, kv_raw, z, tabs, gq, gk, dqh, dkh, dvh, lay, *, name):
    t = z.shape[0]
    h = lay["H"]
    tt = _tile(t, TOK_TILE, 16)
    hg = _tile(h, HEAD_GROUP, 1)
    scale = 1.0 / math.sqrt(QK_DIM)

    def body(q_ref, kv_ref, pe_ref, c_ref, sa_ref, sb_ref, gq_ref, gk_ref, dqh_ref, dkh_ref, dvh_ref,
             dq_ref, dkv_ref, dpe_ref, dgq_ref, dgk_ref):
        i, j = pl.program_id(0), pl.program_id(1)
        c_tab, sa_tab, sb_tab = c_ref[...], sa_ref[...], sb_ref[...]

        @pl.when((i == 0) & (j == 0))
        def _():
            dgq_ref[...] = jnp.zeros_like(dgq_ref)
            dgk_ref[...] = jnp.zeros_like(dgk_ref)

        @pl.when(j == 0)
        def _():
            dpe_ref[...] = jnp.zeros_like(dpe_ref)

        def norm_bwd(v, g, dy):
            r = lax.rsqrt(jnp.sum(v * v, axis=-1, keepdims=True) * (1.0 / QK_DIM) + EPS)
            dyg = dy * g
            dot = jnp.sum(dyg * v, axis=-1, keepdims=True) * (1.0 / QK_DIM)
            return r * dyg - v * (r * r * r) * dot, jnp.sum(dy * v * r, axis=0, keepdims=True)

        pe, gq_v, gk_v = pe_ref[...].astype(F32), gq_ref[...], gk_ref[...]
        dpe, dgq, dgk = jnp.zeros_like(pe), jnp.zeros_like(gq_v), jnp.zeros_like(gk_v)
        for g in range(hg):
            cols = slice(g * HEAD_PAD, (g + 1) * HEAD_PAD)
            dqo = dqh_ref[g].astype(F32) * scale
            dy = jnp.concatenate([dqo[:, :NOPE], _rope_t(dqo[:, NOPE:], c_tab, sa_tab, sb_tab)], axis=1)
            dq, dg = norm_bwd(q_ref[:, cols].astype(F32), gq_v, dy)
            dq_ref[:, cols] = dq.astype(BF16)
            dgq = dgq + dg

            dko = dkh_ref[g].astype(F32)
            dy = jnp.concatenate([dko[:, :NOPE], _rope_t(dko[:, NOPE:], c_tab, sa_tab, sb_tab)], axis=1)
            kfull = jnp.concatenate([kv_ref[:, cols].astype(F32)[:, :NOPE], pe], axis=1)
            dk, dg = norm_bwd(kfull, gk_v, dy)
            dkv_ref[:, cols] = jnp.concatenate([dk[:, :NOPE].astype(BF16), dvh_ref[g]], axis=1)
            dpe = dpe + dk[:, NOPE:]
            dgk = dgk + dg
        dpe_ref[...] += dpe
        dgq_ref[...] += dgq
        dgk_ref[...] += dgk

    head = pl.BlockSpec((tt, hg * HEAD_PAD), lambda i, j: (i, j))
    tab = pl.BlockSpec((tt, 128), lambda i, j: (i, 0))
    gain = pl.BlockSpec((1, HEAD_PAD), lambda i, j: (0, 0))
    hm = pl.BlockSpec((hg, tt, HEAD_PAD), lambda i, j: (j, i, 0))
    return _pcall(
        body, name=name, grid=(t // tt, h // hg),
        in_specs=[head, head, pl.BlockSpec((tt, 128), lambda i, j: (i, lay["KPE"] // 128)), tab, tab, tab, gain, gain,
                  hm, hm, pl.BlockSpec((hg, tt, VDIM), lambda i, j: (j, i, 0))],
        out_specs=[head, head, tab, gain, gain],
        out_shape=[jax.ShapeDtypeStruct((t, h * HEAD_PAD), BF16), jax.ShapeDtypeStruct((t, h * HEAD_PAD), BF16),
                   jax.ShapeDtypeStruct((t, 128), F32),
                   jax.ShapeDtypeStruct((1, HEAD_PAD), F32), jax.ShapeDtypeStruct((1, HEAD_PAD), F32)],
        compiler_params=_params(("arbitrary", "arbitrary")),
    )(q_raw, kv_raw, z, *tabs, gq, gk, dqh, dkh, dvh)


def _lower_triangle(n):
    return lax.broadcasted_iota(jnp.int32, (n, n), 1) <= lax.broadcasted_iota(jnp.int32, (n, n), 0)


def _qk(q, k):
    return lax.dot_general(q, k, (((1,), (1,)), ((), ())), preferred_element_type=F32)


def _flash_fwd(qh, kh, vh, z, lay, nb, *, name):
    h, t, _ = qh.shape
    s = t // nb
    tq = _tile(s, ATT_TQ, 128)
    nq = s // tq
    att_w = h * VDIM
    gblk = lay["GATT"] // VDIM

    def body(q_ref, k_ref, v_ref, g_ref, att_ref, mix_ref, lse_ref):
        i = pl.program_id(2)
        tri = _lower_triangle(tq)
        for blk in range(nq):
            @pl.when(i == blk)
            def _():
                q = q_ref[...]
                pre = blk * tq
                sd = jnp.where(tri, _qk(q, k_ref[pre:pre + tq, :]), NEG)
                m = jnp.max(sd, axis=-1, keepdims=True)
                if pre:
                    sp = _qk(q, k_ref[0:pre, :])
                    m = jnp.maximum(m, jnp.max(sp, axis=-1, keepdims=True))
                pd = jnp.exp(sd - m)
                l = jnp.sum(pd, axis=-1, keepdims=True)
                acc = jnp.dot(pd.astype(BF16), v_ref[pre:pre + tq, :], preferred_element_type=F32)
                if pre:
                    pp = jnp.exp(sp - m)
                    l = l + jnp.sum(pp, axis=-1, keepdims=True)
                    acc = acc + jnp.dot(pp.astype(BF16), v_ref[0:pre, :], preferred_element_type=F32)
                o = acc / l
                att_ref[...] = o
                g = g_ref[...].astype(F32)
                mix_ref[...] = (o * (g * _sigmoid(g))).astype(BF16)
                lse_ref[...] = m + jnp.log(l)

    row = lambda hh, b, i: (b * nq + i, hh)
    seq = lambda hh, b, i: (hh, b, 0)
    return _pcall(
        body, name=name, grid=(h, nb, nq),
        in_specs=[pl.BlockSpec((None, tq, HEAD_PAD), lambda hh, b, i: (hh, b * nq + i, 0)),
                  pl.BlockSpec((None, s, HEAD_PAD), seq),
                  pl.BlockSpec((None, s, VDIM), seq),
                  pl.BlockSpec((tq, VDIM), lambda hh, b, i: (b * nq + i, gblk + hh))],
        out_specs=[pl.BlockSpec((tq, VDIM), row), pl.BlockSpec((tq, VDIM), row),
                   pl.BlockSpec((None, tq, 1), lambda hh, b, i: (hh, b * nq + i, 0))],
        out_shape=[jax.ShapeDtypeStruct((t, att_w), F32), jax.ShapeDtypeStruct((t, 2 * att_w), BF16),
                   jax.ShapeDtypeStruct((h, t, 1), F32)],
        compiler_params=_params(("parallel", "parallel", "parallel")),
    )(qh, kh, vh, z)


def _gate_bwd(dmix, att, z, dz, lay, *, name):
    t, att_w = att.shape
    h = att_w // VDIM
    tt = _tile(t, TOK_TILE, 16)
    gblk = lay["GATT"] // att_w

    def body(dm_ref, o_ref, g_ref, dz_in, do_ref, delta_ref, dg_ref):
        del dz_in
        dm, o, g = dm_ref[...], o_ref[...], g_ref[...].astype(F32)
        sg = _sigmoid(g)
        do = dm * (g * sg)
        do_ref[...] = do.astype(BF16)
        prod = do * o
        for hh in range(h):
            delta_ref[hh] = jnp.sum(prod[:, hh * VDIM:(hh + 1) * VDIM], axis=-1, keepdims=True)
        dg_ref[...] = (dm * o * _dsilu(g, sg)).astype(BF16)

    blk = pl.BlockSpec((tt, att_w), lambda i: (i, 0))
    gate = pl.BlockSpec((tt, att_w), lambda i: (i, gblk))
    return _pcall(
        body, name=name, grid=(t // tt,),
        in_specs=[blk, blk, gate, pl.BlockSpec(memory_space=pl.ANY)],
        out_specs=[blk, pl.BlockSpec((h, tt, 1), lambda i: (0, i, 0)), gate],
        out_shape=[jax.ShapeDtypeStruct((t, att_w), BF16), jax.ShapeDtypeStruct((h, t, 1), F32),
                   jax.ShapeDtypeStruct(dz.shape, BF16)],
        input_output_aliases={3: 2},
        compiler_params=_params(("parallel",)),
    )(dmix, att, z, dz)


def _flash_bwd(qh, kh, vh, do, lse, delta, nb, *, name):
    h, t, _ = qh.shape
    s = t // nb
    tk = _tile(s, ATT_TQ, 128)
    nk = s // tk
    tn_dims = (((0,), (0,)), ((), ()))

    def body(q_ref, k_ref, v_ref, do_ref, lse_ref, dl_ref, dq_out, dk_ref, dv_ref, dq_ref):
        j = pl.program_id(2)
        tri = _lower_triangle(tk)

        @pl.when(j == 0)
        def _():
            dq_ref[...] = jnp.zeros_like(dq_ref)

        def rows_against_block(r0, r1, masked):
            q, do_v = q_ref[r0:r1, :], do_ref[r0:r1, :]
            k = k_ref[...]
            sc = _qk(q, k)
            if masked:
                sc = jnp.where(tri, sc, NEG)
            p = jnp.exp(sc - lse_ref[r0:r1, :])
            dv = lax.dot_general(p.astype(BF16), do_v, tn_dims, preferred_element_type=F32)
            ds = (p * (_qk(do_v, v_ref[...]) - dl_ref[r0:r1, :])).astype(BF16)
            dq_ref[r0:r1, :] += jnp.dot(ds, k, preferred_element_type=F32)
            return lax.dot_general(ds, q, tn_dims, preferred_element_type=F32), dv

        for blk in range(nk):
            @pl.when(j == blk)
            def _():
                r0 = blk * tk
                dk, dv = rows_against_block(r0, r0 + tk, True)
                if r0 + tk < s:
                    dk2, dv2 = rows_against_block(r0 + tk, s, False)
                    dk, dv = dk + dk2, dv + dv2
                dk_ref[...] = dk.astype(BF16)
                dv_ref[...] = dv.astype(BF16)

        @pl.when(j == nk - 1)
        def _():
            dq_out[...] = dq_ref[...].astype(BF16)

    seq = lambda hh, b, j: (hh, b, 0)
    kv = lambda hh, b, j: (hh, b * nk + j, 0)
    return _pcall(
        body, name=name, grid=(h, nb, nk),
        in_specs=[pl.BlockSpec((None, s, HEAD_PAD), seq),
                  pl.BlockSpec((None, tk, HEAD_PAD), kv),
                  pl.BlockSpec((None, tk, VDIM), kv),
                  pl.BlockSpec((s, VDIM), lambda hh, b, j: (b, hh)),
                  pl.BlockSpec((None, s, 1), seq), pl.BlockSpec((None, s, 1), seq)],
        out_specs=[pl.BlockSpec((None, s, HEAD_PAD), seq), pl.BlockSpec((None, tk, HEAD_PAD), kv),
                   pl.BlockSpec((None, tk, VDIM), kv)],
        out_shape=[jax.ShapeDtypeStruct((h, t, HEAD_PAD), BF16), jax.ShapeDtypeStruct((h, t, HEAD_PAD), BF16),
                   jax.ShapeDtypeStruct((h, t, VDIM), BF16)],
        scratch_shapes=[pltpu.VMEM((s, HEAD_PAD), F32)],
        compiler_params=_params(("parallel", "parallel", "arbitrary")),
    )(qh, kh, vh, do, lse, delta)


SUBLANES = 8


def _stage_row_shifts(ext, sh, c0, lc, rows):
    for r in range(1, SUBLANES):
        sh[r - 1, 0:rows, :] = ext[r:r + rows, c0:c0 + lc]


def _row_window(ext, sh, c0, lc, off, n):
    r = off % SUBLANES
    if r == 0:
        return ext[off:off + n, c0:c0 + lc]
    return sh[r - 1, off - r:off - r + n, :]


def _conv_fwd(z, mix, w_dw, b_dw, ln_g, ln_b, lay, nb, *, name):
    t = z.shape[0]
    cw = lay["CW"]
    s = t // nb
    tt = _tile(s, CONV_TILE, HALO)
    ns = s // tt
    hb = tt // HALO
    lc = _tile(cw, LANE_CHUNK, 128)

    def body(a_ref, b_ref, ap_ref, bp_ref, gc_ref, w_ref, bias_ref, lg_ref, lb_ref, mix_in, mix_ref, u_ref, c_ref,
             ext, sh):
        del mix_in
        i = pl.program_id(1)
        u = a_ref[...].astype(F32) * _sigmoid(b_ref[...].astype(F32))
        u_ref[...] = u
        ext[0:HALO, :] = jnp.where(i > 0, ap_ref[...].astype(F32) * _sigmoid(bp_ref[...].astype(F32)), 0.0)
        ext[HALO:HALO + tt, :] = u
        for c0 in range(0, cw, lc):
            _stage_row_shifts(ext, sh, c0, lc, tt + HALO - SUBLANES)
            acc = jnp.zeros((tt, lc), F32) + bias_ref[:, c0:c0 + lc]
            for k in range(CONV_K):
                off = HALO - (CONV_K - 1) + k
                acc = acc + w_ref[k:k + 1, c0:c0 + lc] * _row_window(ext, sh, c0, lc, off, tt)
            c_ref[:, c0:c0 + lc] = acc
        c = c_ref[...]
        mu = jnp.mean(c, axis=-1, keepdims=True)
        xc = c - mu
        var = jnp.mean(xc * xc, axis=-1, keepdims=True)
        y = xc * lax.rsqrt(var + EPS) * lg_ref[...] + lb_ref[...]
        g = gc_ref[...].astype(F32)
        mix_ref[...] = (y * _sigmoid(y) * (g * _sigmoid(g))).astype(BF16)

    cur = lambda col: pl.BlockSpec((tt, cw), lambda b, i: (b * ns + i, col))
    prev = lambda col: pl.BlockSpec((HALO, cw), lambda b, i: (jnp.maximum((b * ns + i) * hb - 1, 0), col))
    vec = pl.BlockSpec((1, cw), lambda b, i: (0, 0))
    out_row = pl.BlockSpec((tt, cw), lambda b, i: (b * ns + i, 0))
    return _pcall(
        body, name=name, grid=(nb, ns),
        in_specs=[cur(lay["A"] // cw), cur(lay["B"] // cw), prev(lay["A"] // cw), prev(lay["B"] // cw),
                  cur(lay["GCONV"] // cw), pl.BlockSpec((HALO, cw), lambda b, i: (0, 0)), vec, vec, vec,
                  pl.BlockSpec(memory_space=pl.ANY)],
        out_specs=[pl.BlockSpec((tt, cw), lambda b, i: (b * ns + i, 1)), out_row, out_row],
        out_shape=[jax.ShapeDtypeStruct(mix.shape, BF16), jax.ShapeDtypeStruct((t, cw), F32),
                   jax.ShapeDtypeStruct((t, cw), F32)],
        scratch_shapes=[pltpu.VMEM((tt + HALO, cw), F32),
                        pltpu.VMEM((SUBLANES - 1, tt + HALO - SUBLANES, lc), F32)],
        input_output_aliases={9: 0},
        compiler_params=_params(("parallel", "parallel")),
    )(z, z, z, z, z, w_dw, b_dw.reshape(1, cw), ln_g.reshape(1, cw), ln_b.reshape(1, cw), mix)


def _conv_bwd_ln(c_pre, z, dmix, ln_g, ln_b, lay, *, name):
    t, cw = c_pre.shape
    tt = _tile(t, TOK_TILE, 16)

    def body(c_ref, gc_ref, dm_ref, lg_ref, lb_ref, dc_ref, dgc_ref, dlg_ref, dlb_ref, dbias_ref):
        @pl.when(pl.program_id(0) == 0)
        def _():
            dlg_ref[...] = jnp.zeros_like(dlg_ref)
            dlb_ref[...] = jnp.zeros_like(dlb_ref)
            dbias_ref[...] = jnp.zeros_like(dbias_ref)

        c = c_ref[...]
        mu = jnp.mean(c, axis=-1, keepdims=True)
        xc = c - mu
        rstd = lax.rsqrt(jnp.mean(xc * xc, axis=-1, keepdims=True) + EPS)
        xhat = xc * rstd
        y = xhat * lg_ref[...] + lb_ref[...]
        sy = _sigmoid(y)
        g = gc_ref[...].astype(F32)
        sg = _sigmoid(g)
        dm = dm_ref[...].astype(F32)
        dgc_ref[...] = (dm * (y * sy) * _dsilu(g, sg)).astype(BF16)
        dy = dm * (g * sg) * _dsilu(y, sy)
        dlb_ref[...] += jnp.sum(dy, axis=0, keepdims=True)
        dlg_ref[...] += jnp.sum(dy * xhat, axis=0, keepdims=True)
        dxh = dy * lg_ref[...]
        dc = rstd * (dxh - jnp.mean(dxh, axis=-1, keepdims=True)
                     - xhat * jnp.mean(dxh * xhat, axis=-1, keepdims=True))
        dc_ref[...] = dc
        dbias_ref[...] += jnp.sum(dc, axis=0, keepdims=True)

    row = pl.BlockSpec((tt, cw), lambda i: (i, 0))
    vec = pl.BlockSpec((1, cw), lambda i: (0, 0))
    return _pcall(
        body, name=name, grid=(t // tt,),
        in_specs=[row, pl.BlockSpec((tt, cw), lambda i: (i, lay["GCONV"] // cw)),
                  pl.BlockSpec((tt, cw), lambda i: (i, 1)), vec, vec],
        out_specs=[row, pl.BlockSpec((tt, cw), lambda i: (i, lay["GCONV"] // cw)), vec, vec, vec],
        out_shape=[jax.ShapeDtypeStruct((t, cw), F32), jax.ShapeDtypeStruct((t, lay["NP"]), BF16),
                   jax.ShapeDtypeStruct((1, cw), F32), jax.ShapeDtypeStruct((1, cw), F32),
                   jax.ShapeDtypeStruct((1, cw), F32)],
        compiler_params=_params(("arbitrary",)),
    )(c_pre, z, dmix, ln_g.reshape(1, cw), ln_b.reshape(1, cw))


def _conv_bwd_dw(dc, u, z, w_dw, dz, lay, nb, *, name):
    t, cw = dc.shape
    s = t // nb
    tt = _tile(s, CONV_TILE, HALO)
    ns = s // tt
    hb = tt // HALO
    lc = _tile(cw, LANE_CHUNK, 128)

    def body(dc_ref, dcn_ref, u_ref, up_ref, a_ref, b_ref, w_ref, dz_in, dab_ref, dw_ref, ext_dc, ext_u, du_ref,
             sh_dc, sh_u):
        del dz_in
        b_i, i = pl.program_id(0), pl.program_id(1)

        @pl.when((b_i == 0) & (i == 0))
        def _():
            dw_ref[...] = jnp.zeros_like(dw_ref)

        dc_v = dc_ref[...]
        ext_dc[0:tt, :] = dc_v
        ext_dc[tt:tt + HALO, :] = jnp.where(i < ns - 1, dcn_ref[...], 0.0)
        ext_u[0:HALO, :] = jnp.where(i > 0, up_ref[...], 0.0)
        ext_u[HALO:HALO + tt, :] = u_ref[...]
        for c0 in range(0, cw, lc):
            _stage_row_shifts(ext_dc, sh_dc, c0, lc, tt + HALO - SUBLANES)
            _stage_row_shifts(ext_u, sh_u, c0, lc, tt + HALO - SUBLANES)
            acc = jnp.zeros((tt, lc), F32)
            dcc = dc_v[:, c0:c0 + lc]
            for k in range(CONV_K):
                acc = acc + w_ref[k:k + 1, c0:c0 + lc] * _row_window(ext_dc, sh_dc, c0, lc, CONV_K - 1 - k, tt)
                off = HALO - (CONV_K - 1) + k
                dw_ref[k:k + 1, c0:c0 + lc] += jnp.sum(dcc * _row_window(ext_u, sh_u, c0, lc, off, tt),
                                                       axis=0, keepdims=True)
            du_ref[:, c0:c0 + lc] = acc
        du = du_ref[...]
        sb = _sigmoid(b_ref[...].astype(F32))
        dab_ref[:, 0:cw] = (du * sb).astype(BF16)
        dab_ref[:, cw:2 * cw] = (du * a_ref[...].astype(F32) * sb * (1.0 - sb)).astype(BF16)

    last = nb * ns * hb - 1
    row = pl.BlockSpec((tt, cw), lambda b, i: (b * ns + i, 0))
    return _pcall(
        body, name=name, grid=(nb, ns),
        in_specs=[row, pl.BlockSpec((HALO, cw), lambda b, i: (jnp.minimum((b * ns + i + 1) * hb, last), 0)),
                  row, pl.BlockSpec((HALO, cw), lambda b, i: (jnp.maximum((b * ns + i) * hb - 1, 0), 0)),
                  pl.BlockSpec((tt, cw), lambda b, i: (b * ns + i, lay["A"] // cw)),
                  pl.BlockSpec((tt, cw), lambda b, i: (b * ns + i, lay["B"] // cw)),
                  pl.BlockSpec((HALO, cw), lambda b, i: (0, 0)), pl.BlockSpec(memory_space=pl.ANY)],
        out_specs=[pl.BlockSpec((tt, 2 * cw), lambda b, i: (b * ns + i, 0)),
                   pl.BlockSpec((HALO, cw), lambda b, i: (0, 0))],
        out_shape=[jax.ShapeDtypeStruct(dz.shape, BF16), jax.ShapeDtypeStruct((HALO, cw), F32)],
        scratch_shapes=[pltpu.VMEM((tt + HALO, cw), F32), pltpu.VMEM((tt + HALO, cw), F32),
                        pltpu.VMEM((tt, cw), F32),
                        pltpu.VMEM((SUBLANES - 1, tt + HALO - SUBLANES, lc), F32),
                        pltpu.VMEM((SUBLANES - 1, tt + HALO - SUBLANES, lc), F32)],
        input_output_aliases={7: 0},
        compiler_params=_params(("arbitrary", "arbitrary")),
    )(dc, dc, u, u, z, z, w_dw, dz)


def _loss_head(y, target, *, name):
    t, d = y.shape
    tt = _tile(t, TOK_TILE, 16)

    def body(y_ref, t_ref, sse_ref, dy_ref, dyb_ref):
        @pl.when(pl.program_id(0) == 0)
        def _():
            sse_ref[...] = jnp.zeros_like(sse_ref)

        e = y_ref[...] - t_ref[...]
        sse_ref[...] += jnp.sum(e * e)
        dy = e * (1.0 / d)
        dy_ref[...] = dy
        dyb_ref[...] = dy.astype(BF16)

    row = pl.BlockSpec((tt, d), lambda i: (i, 0))
    return _pcall(
        body, name=name, grid=(t // tt,),
        in_specs=[row, row],
        out_specs=[pl.BlockSpec((8, 128), lambda i: (0, 0)), row, row],
        out_shape=[jax.ShapeDtypeStruct((8, 128), F32), jax.ShapeDtypeStruct((t, d), F32),
                   jax.ShapeDtypeStruct((t, d), BF16)],
        compiler_params=_params(("arbitrary",)),
    )(y, target)


def _adam(w, m, v, g_parts, *, name, layer=0, layers=1, prev=None):
    rows, cols = w.shape
    slab = rows // layers
    tr = _tile(slab, max(16, ADAM_BLOCK_ELEMS // cols), 16)
    blk0 = layer * (slab // tr)
    n = len(g_parts)
    n_prev = 0 if prev is None else 4

    def body(*refs):
        w_ref, m_ref, v_ref = refs[:3]
        g_refs = refs[3:3 + n]
        g_out, d_out, m_out, v_out = refs[3 + n + n_prev:]
        g = g_refs[0][...].astype(F32)
        for r in g_refs[1:]:
            g = g + r[...].astype(F32)
        g = g[:, :cols]
        m_new = ADAM_B1 * m_ref[...] + (1.0 - ADAM_B1) * g
        v_new = ADAM_B2 * v_ref[...] + (1.0 - ADAM_B2) * (g * g)
        m_hat = m_new / (1.0 - ADAM_B1 ** ADAM_STEP)
        v_hat = v_new / (1.0 - ADAM_B2 ** ADAM_STEP)
        g_out[...] = g
        d_out[...] = -ADAM_LR * (m_hat / (jnp.sqrt(v_hat) + ADAM_EPS) + ADAM_WD * w_ref[...])
        m_out[...] = m_new
        v_out[...] = v_new

    blk = pl.BlockSpec((tr, cols), lambda i: (blk0 + i, 0))
    g_specs, g_args = [], []
    for arr, lead in g_parts:
        g_args.append(arr)
        if lead is None:
            g_specs.append(pl.BlockSpec((tr, cols), lambda i: (i, 0)))
        else:
            g_specs.append(pl.BlockSpec((None, tr, arr.shape[2]), functools.partial(lambda i, p: (p, i, 0), p=lead)))
    out = jax.ShapeDtypeStruct((rows, cols), F32)
    return _pcall(
        body, name=name, grid=(slab // tr,),
        in_specs=[blk, blk, blk] + g_specs + [pl.BlockSpec(memory_space=pl.ANY)] * n_prev,
        out_specs=[blk, blk, blk, blk],
        out_shape=[out, out, out, out],
        input_output_aliases={3 + n + k: k for k in range(n_prev)},
        compiler_params=_params(("parallel",)),
    )(w, m, v, *g_args, *(prev or ()))


def _position():
    return lax.axis_index("x"), lax.axis_index("y"), lax.axis_index("c")


def _block_id(p):
    return 4 * p[0] + 2 * p[1] + p[2]


def _flip(p, mask):
    return tuple((1 - v) if (mask >> (2 - a)) & 1 else v for a, v in enumerate(p))


def _all_gather(xs, *, name):
    n = len(xs)

    def body(*refs):
        x_refs, o_refs = refs[:n], refs[n:2 * n]
        send_sems, recv_sems, local_sems = refs[2 * n:]
        x, y, c = _position()
        me, sibling = (x, y, c), (x, y, 1 - c)
        chips = [(1 - x, y), (x, 1 - y), (1 - x, 1 - y)]

        def copy(t, k, block, to, src=None):
            dst = o_refs[t].at[_block_id(block)]
            return pltpu.make_async_remote_copy(
                src_ref=dst if src is None else src, dst_ref=dst,
                send_sem=send_sems.at[t, k], recv_sem=recv_sems.at[t, k],
                device_id=to, device_id_type=MESH)

        mine = [pltpu.make_async_copy(x_refs[t], o_refs[t].at[_block_id(me)], local_sems.at[t]) for t in range(n)]
        for cp in mine:
            cp.start()
        started = []
        for t in range(n):
            first = [copy(t, 0, me, sibling, src=x_refs[t])]
            first += [copy(t, 1 + j, me, (*chip, c), src=x_refs[t]) for j, chip in enumerate(chips)]
            for cp in first:
                cp.start()
            started += first
        for j, chip in enumerate(chips):
            for t in range(n):
                copy(t, 1 + j, (*chip, c), me).wait_recv()
                fwd = copy(t, 4 + j, (*chip, c), sibling)
                fwd.start()
                started.append(fwd)
        for t in range(n):
            copy(t, 0, sibling, me).wait_recv()
            for j, chip in enumerate(chips):
                copy(t, 4 + j, (*chip, 1 - c), me).wait_recv()
        for cp in started:
            cp.wait_send()
        for cp in mine:
            cp.wait()

    any_spec = pl.BlockSpec(memory_space=pl.ANY)
    return _pcall(
        body, name=name,
        in_specs=[any_spec] * n, out_specs=[any_spec] * n,
        out_shape=[jax.ShapeDtypeStruct((N_DEV,) + a.shape, a.dtype) for a in xs],
        scratch_shapes=[pltpu.SemaphoreType.DMA((n, 7)), pltpu.SemaphoreType.DMA((n, 7)),
                        pltpu.SemaphoreType.DMA((n,))],
    )(*xs)


def _pushed_copy(x_ref, land_ref, send_sems, recv_sems, t, mask, me, chunked, at_receiver):
    peer = _flip(me, mask)
    src = x_ref.at[_block_id(peer)] if chunked else x_ref
    slot = _block_id(peer) if at_receiver else _block_id(me)
    k = (N_DEV - 1) * t + mask - 1
    return pltpu.make_async_remote_copy(
        src_ref=src, dst_ref=land_ref.at[slot], send_sem=send_sems.at[k], recv_sem=recv_sems.at[k],
        device_id=peer, device_id_type=MESH)


def _push_start(xs, chunked, *, name, after=None):
    n = len(xs)
    lands = [lax.empty(a.shape if chunked else (N_DEV,) + a.shape, a.dtype) for a in xs]

    n_after = 0 if after is None else 1

    def body(*refs):
        x_refs, land_refs = refs[:n], refs[n:2 * n]
        send_sems, recv_sems = refs[2 * n + n_after], refs[2 * n + n_after + 1]
        token = refs[4 * n + n_after + 2]
        me = _position()
        for t in range(n):
            for mask in range(1, N_DEV):
                _pushed_copy(x_refs[t], land_refs[t], send_sems, recv_sems, t, mask, me, chunked, False).start()
        token[...] = jnp.zeros_like(token)

    hbm = pl.BlockSpec(memory_space=pltpu.HBM)
    sem = pl.BlockSpec(memory_space=pltpu.SEMAPHORE)
    outs = _pcall(
        body, name=name,
        in_specs=[hbm] * (2 * n) + [pl.BlockSpec(memory_space=pl.ANY)] * n_after,
        out_specs=[sem, sem] + [hbm] * (2 * n) + [pl.BlockSpec(memory_space=pltpu.VMEM)],
        out_shape=[pltpu.SemaphoreType.DMA(((N_DEV - 1) * n,)), pltpu.SemaphoreType.DMA(((N_DEV - 1) * n,))]
        + [pltpu.HBM(a.shape, a.dtype) for a in xs] + [pltpu.HBM(a.shape, a.dtype) for a in lands]
        + [jax.ShapeDtypeStruct((8, 128), F32)],
        input_output_aliases={i: 2 + i for i in range(2 * n)},
        compiler_params=pltpu.CompilerParams(has_side_effects=pltpu.SideEffectType.DATAFLOW_SIDE_EFFECTING),
    )(*[pltpu.with_memory_space_constraint(a, pltpu.HBM) for a in list(xs) + lands], *([after] * n_after))
    return outs[0], outs[1], outs[2:2 + n], outs[2 + n:2 + 2 * n], outs[2 + 2 * n]


def _push_wait(handle, after, chunked, *, name):
    send_sems, recv_sems, xs, lands, _ = handle
    n = len(xs)

    def body(*refs):
        x_refs, land_refs = refs[:n], refs[n:2 * n]
        send_sems, recv_sems = refs[2 * n], refs[2 * n + 1]
        me = _position()
        for t in range(n):
            for mask in range(1, N_DEV):
                _pushed_copy(x_refs[t], land_refs[t], send_sems, recv_sems, t, mask, me, chunked, False).wait_send()
                _pushed_copy(x_refs[t], land_refs[t], send_sems, recv_sems, t, mask, me, chunked, True).wait_recv()

    hbm = pl.BlockSpec(memory_space=pltpu.HBM)
    sem = pl.BlockSpec(memory_space=pltpu.SEMAPHORE)
    outs = _pcall(
        body, name=name,
        in_specs=[hbm] * (2 * n) + [sem, sem, pl.BlockSpec(memory_space=pl.ANY)],
        out_specs=[hbm] * (2 * n),
        out_shape=[pltpu.HBM(a.shape, a.dtype) for a in list(xs) + list(lands)],
        input_output_aliases={i: i for i in range(2 * n)},
        compiler_params=pltpu.CompilerParams(has_side_effects=pltpu.SideEffectType.DATAFLOW_SIDE_EFFECTING),
    )(*xs, *lands, send_sems, recv_sems, after)
    return outs[:n], outs[n:]


def _all_reduce_small(pack, *, name):
    rows = pack.shape[0]

    def body(p_ref, o_ref, gath, send_sems, recv_sems):
        me = _position()
        my_id = _block_id(me)
        gath[my_id] = p_ref[...]
        sent = []
        for mask in range(1, N_DEV):
            peer = _flip(me, mask)
            cp = pltpu.make_async_remote_copy(
                src_ref=p_ref, dst_ref=gath.at[my_id], send_sem=send_sems.at[mask - 1],
                recv_sem=recv_sems.at[mask - 1], device_id=peer, device_id_type=MESH)
            cp.start()
            sent.append(cp)
        for mask in range(1, N_DEV):
            slot = gath.at[_block_id(_flip(me, mask))]
            pltpu.make_async_remote_copy(
                src_ref=slot, dst_ref=slot, send_sem=send_sems.at[mask - 1], recv_sem=recv_sems.at[mask - 1],
                device_id=me, device_id_type=MESH).wait_recv()
        for cp in sent:
            cp.wait_send()
        total = gath[0]
        for s in range(1, N_DEV):
            total = total + gath[s]
        o_ref[...] = total

    vm = pl.BlockSpec(memory_space=pltpu.VMEM)
    return _pcall(
        body, name=name,
        in_specs=[vm], out_specs=vm,
        out_shape=jax.ShapeDtypeStruct(pack.shape, F32),
        scratch_shapes=[pltpu.VMEM((N_DEV, rows, 128), F32), pltpu.SemaphoreType.DMA((7,)),
                        pltpu.SemaphoreType.DMA((7,))],
        compiler_params=pltpu.CompilerParams(vmem_limit_bytes=VMEM_LIMIT),
    )(pack)


def _layout(d, ql, kvl):
    cw = d // 2
    att = d // 2
    lay = {"D": d, "CW": cw, "ATT": att, "H": att // VDIM, "QL": ql, "KVL": kvl}
    lay["A"], lay["B"], lay["GATT"], lay["GCONV"] = 0, cw, 2 * cw, 2 * cw + att
    lay["QC"] = lay["GCONV"] + cw
    lay["KVC"] = lay["QC"] + ql
    lay["KPE"] = lay["KVC"] + kvl
    used = lay["KPE"] + 128
    tn = min(MM_TN, 1024)
    lay["NP"] = -(-used // tn) * tn
    assert att == cw and lay["QC"] % ql == 0 and lay["KVC"] % kvl == 0 and lay["KPE"] % 128 == 0
    lay["o_kv"], lay["o_pe"] = ql, ql + kvl
    lay["o_ga"] = lay["o_pe"] + ROPE
    lay["o_u"] = lay["o_ga"] + att
    lay["o_gc"] = lay["o_u"] + 2 * cw
    lay["IN_COLS"] = lay["o_gc"] + cw
    return lay


def _lane_pad(n):
    return -(-n // 128) * 128


def _assemble_w_in(g, shard, lay, *, name):
    _, d, padw = g.shape
    tr = _tile(d, 256, 16)
    sections = [(lay["A"], 2 * lay["CW"], lay["o_u"]), (lay["GATT"], lay["ATT"], lay["o_ga"]),
                (lay["GCONV"], lay["CW"], lay["o_gc"]), (lay["QC"], lay["QL"], 0),
                (lay["KVC"], lay["KVL"], lay["o_kv"]), (lay["KPE"], ROPE, lay["o_pe"])]

    def runs_of_tile(j):
        for start, width, orig in sections:
            if start <= j * 128 < start + width:
                todo, col, lane, out = min(128, start + width - j * 128), orig + j * 128 - start, 0, []
                while todo:
                    p, o = divmod(col, shard)
                    take = min(todo, shard - o)
                    first = (o // 128) * 128
                    win = 256 if first + 256 <= padw else 128
                    out.append((p, first, win, o - first, lane, take))
                    col, lane, todo = col + take, lane + take, todo - take
                return out
        return []

    def body(g_ref, o_ref):
        movers = {}

        def mover(win, off, lane, take):
            key = (win, off, lane, take)
            if key not in movers:
                row = lax.broadcasted_iota(jnp.int32, (win, 128), 0)
                col = lax.broadcasted_iota(jnp.int32, (win, 128), 1)
                hit = (row - off == col - lane) & (col >= lane) & (col < lane + take)
                movers[key] = jnp.where(hit, 1.0, 0.0).astype(BF16)
            return movers[key]

        for j in range(lay["NP"] // 128):
            tile = None
            for p, first, win, off, lane, take in runs_of_tile(j):
                part = jnp.dot(g_ref[p, :, first:first + win], mover(win, off, lane, take),
                               preferred_element_type=F32)
                tile = part if tile is None else tile + part
            if tile is None:
                tile = jnp.zeros((tr, 128), F32)
            o_ref[:, j * 128:(j + 1) * 128] = tile.astype(BF16)

    return _pcall(
        body, name=name, grid=(d // tr,),
        in_specs=[pl.BlockSpec((N_DEV, tr, padw), lambda i: (0, i, 0))],
        out_specs=pl.BlockSpec((tr, lay["NP"]), lambda i: (i, 0)),
        out_shape=jax.ShapeDtypeStruct((d, lay["NP"]), BF16),
        compiler_params=_params(("parallel",)),
    )(g)


def _split_dw_in(dwp, shard, lay, *, name):
    d = dwp.shape[0]
    padw = _lane_pad(shard)
    tr = _tile(d, 256, 16)
    sections = sorted([(lay["A"], 2 * lay["CW"], lay["o_u"]), (lay["GATT"], lay["ATT"], lay["o_ga"]),
                       (lay["GCONV"], lay["CW"], lay["o_gc"]), (lay["QC"], lay["QL"], 0),
                       (lay["KVC"], lay["KVL"], lay["o_kv"]), (lay["KPE"], ROPE, lay["o_pe"])], key=lambda s: s[2])

    def runs_of_tile(p, jt):
        lo, hi = p * shard + jt * 128, p * shard + min((jt + 1) * 128, shard)
        out = []
        for start, width, orig in sections:
            a, b = max(lo, orig), min(hi, orig + width)
            if a < b:
                src = start + a - orig
                first = (src // 128) * 128
                win = 256 if first + 256 <= lay["NP"] else 128
                out.append((first, win, src - first, a - lo, b - a))
        return out

    def body(w_ref, o_ref):
        movers = {}

        def mover(win, off, lane, take):
            key = (win, off, lane, take)
            if key not in movers:
                row = lax.broadcasted_iota(jnp.int32, (win, 128), 0)
                col = lax.broadcasted_iota(jnp.int32, (win, 128), 1)
                hit = (row - off == col - lane) & (col >= lane) & (col < lane + take)
                movers[key] = jnp.where(hit, 1.0, 0.0).astype(BF16)
            return movers[key]

        for p in range(N_DEV):
            for jt in range(padw // 128):
                tile = None
                for first, win, off, lane, take in runs_of_tile(p, jt):
                    part = jnp.dot(w_ref[:, first:first + win], mover(win, off, lane, take),
                                   preferred_element_type=F32)
                    tile = part if tile is None else tile + part
                if tile is None:
                    tile = jnp.zeros((tr, 128), F32)
                o_ref[p, :, jt * 128:(jt + 1) * 128] = tile.astype(BF16)

    return _pcall(
        body, name=name, grid=(d // tr,),
        in_specs=[pl.BlockSpec((tr, lay["NP"]), lambda i: (i, 0))],
        out_specs=pl.BlockSpec((N_DEV, tr, padw), lambda i: (0, i, 0)),
        out_shape=jax.ShapeDtypeStruct((N_DEV, d, padw), BF16),
        compiler_params=_params(("parallel",)),
    )(dwp)


def _ungather_cols(g):
    return jnp.transpose(g, (1, 0, 2)).reshape(g.shape[1], -1)


def _to_col_blocks(w):
    r, c = w.shape
    return jnp.transpose(w.reshape(r, N_DEV, c // N_DEV), (1, 0, 2))


def _pad_heads(w, h):
    r = w.shape[0]
    return jnp.pad(w.reshape(r, h, QK_DIM), ((0, 0), (0, 0), (0, HEAD_PAD - QK_DIM))).reshape(r, h * HEAD_PAD)


def _unpad_heads(w, h):
    r = w.shape[0]
    return w.reshape(r, h, HEAD_PAD)[:, :, :QK_DIM].reshape(r, h * QK_DIM)


def _rope_tabs(positions):
    half = ROPE // 2
    inv_freq = ROPE_THETA ** (-jnp.arange(half, dtype=F32) / half)
    ang = positions.astype(F32).reshape(-1)[:, None] * inv_freq
    cos, sin = jnp.cos(ang), jnp.sin(ang)
    zero = jnp.zeros_like(cos)
    return (jnp.concatenate([cos, cos, zero, zero], axis=1),
            jnp.concatenate([-sin, zero, zero, zero], axis=1),
            jnp.concatenate([zero, sin, zero, zero], axis=1))


def _pack_rows(vecs):
    rows = []
    for v in vecs:
        flat = v.reshape(-1)
        pad = (-flat.shape[0]) % 1024
        rows.append(jnp.pad(flat, (0, pad)).reshape(-1, 128))
    return jnp.concatenate(rows, axis=0)


def _unpack_rows(pack, shapes):
    out, r0 = [], 0
    for shp in shapes:
        size = math.prod(shp)
        nrows = -(-size // 1024) * 8
        out.append(pack[r0:r0 + nrows].reshape(-1)[:size].reshape(shp))
        r0 += nrows
    return out


def kernel(x, positions, ln_g, w_in, q_a_norm, w_q_up, kv_a_norm, w_kv_up, q_norm, k_norm, w_dw, b_dw, conv_ln_g, conv_ln_b, w_out, loss_target, m_ln_g, m_w_in, m_q_a_norm, m_w_q_up, m_kv_a_norm, m_w_kv_up, m_q_norm, m_k_norm, m_w_dw, m_b_dw, m_conv_ln_g, m_conv_ln_b, m_w_out, v_ln_g, v_w_in, v_q_a_norm, v_w_q_up, v_kv_a_norm, v_w_kv_up, v_q_norm, v_k_norm, v_w_dw, v_b_dw, v_conv_ln_g, v_conv_ln_b, v_w_out):
    nb, seq, d = x.shape
    depth = ln_g.shape[0]
    lay = _layout(d, q_a_norm.shape[1], kv_a_norm.shape[1])
    h, cw, ql, kvl = lay["H"], lay["CW"], lay["QL"], lay["KVL"]
    t = nb * seq
    my_id = _block_id(_position())

    shard_in = w_in.shape[2]

    def shards(l):
        padded = jnp.pad(w_in[l].astype(BF16), ((0, 0), (0, _lane_pad(shard_in) - shard_in)))
        return [padded, w_q_up[l].astype(BF16), w_kv_up[l].astype(BF16), w_out[l].astype(BF16)]

    def fill_own(lands, own):
        return [lax.dynamic_update_index_in_dim(land, blk, my_id, 0) for land, blk in zip(lands, own)]

    def layout_in(g_in, l):
        return {"in": _assemble_w_in(g_in, shard_in, lay, name=f"assemble_w_in_{l}")}

    def layout_rest(g_q, g_kv, g_out):
        return {"q": _pad_heads(_ungather_cols(g_q), h), "kv": _ungather_cols(g_kv), "out": g_out.reshape(2 * cw, d)}

    first = shards(0)
    g_in0, g_dw = _all_gather([first[0], w_dw], name="gather_w_in_0")
    gathers = {0: _push_start(first[1:], False, after=g_in0, name="gather_start_0")}
    for l in range(1, depth):
        gathers[l] = _push_start(shards(l), False, after=gathers[l - 1][4], name=f"gather_start_{l}")
    fwd_dep = gathers[depth - 1][4]
    weights = []

    tabs = _rope_tabs(positions)
    gq_pad = jnp.pad(q_norm, ((0, 0), (0, HEAD_PAD - QK_DIM)))
    gk_pad = jnp.pad(k_norm, ((0, 0), (0, HEAD_PAD - QK_DIM)))
    w_dw_all = jnp.transpose(g_dw, (1, 2, 0, 3)).reshape(depth, CONV_K, cw)
    w_dw_all = jnp.pad(w_dw_all, ((0, 0), (0, HALO - CONV_K), (0, 0)))

    saved = []
    xs = x.reshape(t, d)
    for l in range(depth):
        hid = _rms_fwd(xs, ln_g[l], dep=fwd_dep if l == 0 else None, name=f"rms_fwd_{l}")
        if l == 0:
            weights.append(layout_in(g_in0, 0))
        z = _mm(hid, weights[l]["in"], out_dtype=BF16, name=f"in_proj_{l}")
        if l == 0:
            own, lands = _push_wait(gathers[0], z, False, name="gather_wait_0")
            weights[0].update(layout_rest(*fill_own(lands, own)))
        wl = weights[l]
        qn, kvn = _lat_fwd(z, q_a_norm[l], kv_a_norm[l], lay, name=f"lat_fwd_{l}")
        q_raw = _mm(qn, wl["q"], out_dtype=BF16, name=f"q_up_{l}")
        kv_raw = _mm(kvn, wl["kv"], out_dtype=BF16, name=f"kv_up_{l}")
        qh, kh, vh = _heads_fwd(q_raw, kv_raw, z, tabs, gq_pad[l:l + 1], gk_pad[l:l + 1], lay, name=f"heads_fwd_{l}")
        att, mix, lse = _flash_fwd(qh, kh, vh, z, lay, nb, name=f"flash_fwd_{l}")
        mix, u, c_pre = _conv_fwd(z, mix, w_dw_all[l], b_dw[l], conv_ln_g[l], conv_ln_b[l], lay, nb,
                                  name=f"conv_fwd_{l}")
        x_next = _mm(mix, wl["out"], add=xs, name=f"out_proj_{l}")
        saved.append((xs, hid, z, qn, kvn, q_raw, kv_raw, qh, kh, vh, att, lse, mix, u, c_pre))
        xs = x_next
        if l + 1 < depth:
            own, lands = _push_wait(gathers[l + 1], xs, False, name=f"gather_wait_{l + 1}")
            g_in, *g_rest = fill_own(lands, own)
            weights.append({**layout_in(g_in, l + 1), **layout_rest(*g_rest)})

    sse, dx, dxb = _loss_head(xs, loss_target.reshape(t, d), name="loss_head")
    loss = lax.psum(sse[0, 0] * (0.5 / d), ("x", "y", "c"))

    small = {k: [] for k in ("ln_g", "q_a", "kv_a", "q_n", "k_n", "w_dw", "b_dw", "cln_g", "cln_b")}
    scatters, bwd_dep = {}, None
    for l in reversed(range(depth)):
        xs, hid, z, qn, kvn, q_raw, kv_raw, qh, kh, vh, att, lse, mix, u, c_pre = saved[l]
        wl = weights[l]
        dmix = _mm(dxb, wl["out"], trans_b=True, dep=bwd_dep, name=f"d_mix_{l}")
        dw_out = _mm(mix, dxb, trans_a=True, out_dtype=BF16, name=f"dw_out_{l}")
        dc, dz, dlg, dlb, dbias = _conv_bwd_ln(c_pre, z, dmix, conv_ln_g[l], conv_ln_b[l], lay,
                                               name=f"conv_bwd_ln_{l}")
        dz, dwdw = _conv_bwd_dw(dc, u, z, w_dw_all[l], dz, lay, nb, name=f"conv_bwd_dw_{l}")
        do, delta, dz = _gate_bwd(dmix, att, z, dz, lay, name=f"gate_bwd_{l}")
        dqh, dkh, dvh = _flash_bwd(qh, kh, vh, do, lse, delta, nb, name=f"flash_bwd_{l}")
        dq_raw, dkv_raw, dpe, dgq, dgk = _heads_bwd(q_raw, kv_raw, z, tabs, gq_pad[l:l + 1], gk_pad[l:l + 1],
                                                    dqh, dkh, dvh, lay, name=f"heads_bwd_{l}")
        dqn = _mm(dq_raw, wl["q"], trans_b=True, name=f"d_qn_{l}")
        dkvn = _mm(dkv_raw, wl["kv"], trans_b=True, name=f"d_kvn_{l}")
        dw_q = _mm(qn, dq_raw, trans_a=True, out_dtype=BF16, name=f"dw_q_{l}")
        dw_kv = _mm(kvn, dkv_raw, trans_a=True, out_dtype=BF16, name=f"dw_kv_{l}")
        early = _push_start(
            [_to_col_blocks(_unpad_heads(dw_q, h)), _to_col_blocks(dw_kv), dw_out.reshape(N_DEV, (2 * cw) // N_DEV, d)],
            True, name=f"scatter_start_a_{l}")
        dz, dgqa, dgkva = _lat_bwd(z, q_a_norm[l], kv_a_norm[l], dqn, dkvn, dpe, dz, lay, name=f"lat_bwd_{l}")
        dw_in = _mm(hid, dz, trans_a=True, out_dtype=BF16, dep=early[4], name=f"dw_in_{l}")
        late = _push_start([_split_dw_in(dw_in, shard_in, lay, name=f"split_dw_in_{l}")], True,
                           name=f"scatter_start_b_{l}")
        scatters[l] = (late, early)
        bwd_dep = late[4]
        dh = _mm(dz, wl["in"], trans_b=True, dep=bwd_dep, name=f"d_hid_{l}")
        dx, dxb, dlng = _rms_bwd(xs, ln_g[l], dh, dx, name=f"rms_bwd_{l}")
        for key, val in (("ln_g", dlng), ("q_a", dgqa), ("kv_a", dgkva), ("q_n", dgq[:, :QK_DIM]),
                         ("k_n", dgk[:, :QK_DIM]), ("w_dw", dwdw[:CONV_K]), ("b_dw", dbias),
                         ("cln_g", dlg), ("cln_b", dlb)):
            small[key].append(val)
    grad_x = dx.reshape(nb, seq, d)
    for key in small:
        small[key] = jnp.stack(small[key][::-1])

    small_names = ("ln_g", "q_a", "kv_a", "q_n", "k_n", "b_dw", "cln_g", "cln_b", "w_dw")
    small_shapes = [small[k].shape for k in small_names]
    summed = _unpack_rows(_all_reduce_small(_pack_rows([small[k] for k in small_names]), name="reduce_small_grads"),
                          small_shapes)
    sg = dict(zip(small_names, summed))
    g_w_dw = lax.dynamic_slice_in_dim(sg["w_dw"], my_id * (cw // N_DEV), cw // N_DEV, axis=2)

    def adam_small(ws, ms, vs, gs, nm):
        shapes = [w.shape for w in ws]
        outs = _adam(_pack_rows(ws), _pack_rows(ms), _pack_rows(vs), [(_pack_rows(gs), None)], name=nm)
        return [_unpack_rows(o, shapes) for o in outs]

    big = [("w_in", w_in, m_w_in, v_w_in), ("w_q_up", w_q_up, m_w_q_up, v_w_q_up),
           ("w_kv_up", w_kv_up, m_w_kv_up, v_w_kv_up), ("w_out", w_out, m_w_out, v_w_out)]
    res, prev = {}, [None] * len(big)
    for l in reversed(range(depth)):
        own, lands = [], []
        for half, tag in zip(scatters[l], "ba"):
            sent, landed = _push_wait(half, dx, True, name=f"scatter_wait_{tag}_{l}")
            own += [lax.dynamic_index_in_dim(o, my_id, 0, keepdims=False) for o in sent]
            lands += landed
        for idx, ((nm, w, m, v), recv) in enumerate(zip(big, fill_own(lands, own))):
            rows, cols = w.shape[1], w.shape[2]
            flat = lambda a: a.reshape(depth * rows, cols)
            prev[idx] = _adam(flat(w), flat(m), flat(v), [(recv, s) for s in range(N_DEV)], layer=l, layers=depth,
                              prev=prev[idx], name=f"adam_{nm}_{l}")
    for idx, (nm, w, _, _) in enumerate(big):
        res[nm] = [o.reshape(w.shape) for o in prev[idx]]
    names_s = ["ln_g", "q_a_norm", "kv_a_norm", "q_norm", "k_norm", "w_dw", "b_dw", "conv_ln_g", "conv_ln_b"]
    ws = [ln_g, q_a_norm, kv_a_norm, q_norm, k_norm, w_dw, b_dw, conv_ln_g, conv_ln_b]
    ms = [m_ln_g, m_q_a_norm, m_kv_a_norm, m_q_norm, m_k_norm, m_w_dw, m_b_dw, m_conv_ln_g, m_conv_ln_b]
    vs = [v_ln_g, v_q_a_norm, v_kv_a_norm, v_q_norm, v_k_norm, v_w_dw, v_b_dw, v_conv_ln_g, v_conv_ln_b]
    gs = [sg["ln_g"].reshape(ln_g.shape), sg["q_a"].reshape(q_a_norm.shape), sg["kv_a"].reshape(kv_a_norm.shape),
          sg["q_n"].reshape(q_norm.shape), sg["k_n"].reshape(k_norm.shape), g_w_dw,
          sg["b_dw"].reshape(b_dw.shape), sg["cln_g"].reshape(conv_ln_g.shape), sg["cln_b"].reshape(conv_ln_b.shape)]
    outs_s = adam_small(ws, ms, vs, gs, "adam_small")
    for idx, nm in enumerate(names_s):
        res[nm] = [outs_s[k][idx] for k in range(4)]

    order = ["ln_g", "w_in", "q_a_norm", "w_q_up", "kv_a_norm", "w_kv_up", "q_norm", "k_norm", "w_dw", "b_dw",
             "conv_ln_g", "conv_ln_b", "w_out"]
    return (loss, grad_x, *[res[nm][0] for nm in order], *[res[nm][1] for nm in order],
            *[res[nm][2] for nm in order], *[res[nm][3] for nm in order])
```

```python
import functools
import math

import jax
import jax.numpy as jnp
from jax import lax
from jax.experimental import pallas as pl
from jax.experimental.pallas import tpu as pltpu

F32 = jnp.float32
BF16 = jnp.bfloat16
MESH = pl.DeviceIdType.MESH

N_DEV = 8
NOPE = 128
ROPE = 64
VDIM = 128
HEAD_PAD = 256
QK_DIM = NOPE + ROPE
CONV_K = 31
HALO = 32
EPS = 1e-6
ROPE_THETA = 10000.0
NEG = -1e30

ADAM_LR = 0.001
ADAM_B1 = 0.9
ADAM_B2 = 0.999
ADAM_EPS = 1e-08
ADAM_WD = 0.01
ADAM_STEP = 10

TOK_TILE = 256
CONV_TILE = 256
ATT_TQ = 512
HEAD_GROUP = 4
MM_TM = 1024
MM_TN = 512
MM_TK = 4096
MM_VMEM_BUDGET = 46 * 1024 * 1024
ADAM_BLOCK_ELEMS = 128 * 1024
LANE_CHUNK = 256
VMEM_LIMIT = 56 * 1024 * 1024


def _pcall(body, **kw):
    return pl.pallas_call(body, **kw)


def _tile(dim, pref, mult):
    t = min(pref, dim)
    t -= t % mult
    while t >= mult:
        if dim % t == 0:
            return t
        t -= mult
    return dim


def _params(sem):
    return pltpu.CompilerParams(dimension_semantics=sem, vmem_limit_bytes=VMEM_LIMIT)


def _sigmoid(v):
    return 1.0 / (1.0 + jnp.exp(-v))


def _dsilu(v, sg):
    return sg * (1.0 + v * (1.0 - sg))


def _mm_tiles(m, n, kdim, out_bytes, has_add):
    def need(tm, tn, tk):
        return (2 * 2 * (tm * tk + tk * tn) + 2 * tm * tn * out_bytes
                + tm * tn * 4 * ((kdim > tk) + 2 * has_add + 1))

    shapes = [(_tile(m, pm, 128), _tile(n, pn, 128))
              for pm, pn in ((MM_TM, MM_TN), (MM_TM // 2, MM_TN), (MM_TM // 2, MM_TN // 2), (MM_TM // 4, MM_TN // 2))]
    for tm, tn in shapes:
        if need(tm, tn, kdim) <= MM_VMEM_BUDGET:
            return tm, tn, kdim
    tk = _tile(kdim, MM_TK, 128)
    for tm, tn in shapes:
        if need(tm, tn, tk) <= MM_VMEM_BUDGET:
            break
    return tm, tn, tk


def _mm(a, b, *, name, trans_a=False, trans_b=False, add=None, out_dtype=F32, dep=None):
    assert not (trans_a and trans_b)
    if trans_a:
        kdim, m = a.shape
    else:
        m, kdim = a.shape
    n = b.shape[0] if trans_b else b.shape[1]
    assert b.shape[1 if trans_b else 0] == kdim
    has_add = add is not None
    tm, tn, tk = _mm_tiles(m, n, kdim, jnp.dtype(out_dtype).itemsize, has_add)
    nk = kdim // tk
    contract = (((0 if trans_a else 1,), (1 if trans_b else 0,)), ((), ()))

    def product(a_ref, b_ref):
        return lax.dot_general(a_ref[...], b_ref[...], contract, preferred_element_type=F32)

    def body(*refs):
        a_ref, b_ref = refs[:2]
        add_ref = refs[2] if has_add else None
        o_ref = refs[2 + has_add + (dep is not None)]

        def finish(r):
            if has_add:
                r = r + add_ref[...]
            o_ref[...] = r.astype(o_ref.dtype)

        if nk == 1:
            finish(product(a_ref, b_ref))
            return
        acc_ref = refs[-1]
        k = pl.program_id(2)

        @pl.when(k == 0)
        def _():
            acc_ref[...] = product(a_ref, b_ref)

        @pl.when((k > 0) & (k < nk - 1))
        def _():
            acc_ref[...] += product(a_ref, b_ref)

        @pl.when(k == nk - 1)
        def _():
            finish(acc_ref[...] + product(a_ref, b_ref))

    if trans_a:
        a_spec = pl.BlockSpec((tk, tm), lambda i, j, k: (k, i))
    else:
        a_spec = pl.BlockSpec((tm, tk), lambda i, j, k: (i, k))
    if trans_b:
        b_spec = pl.BlockSpec((tn, tk), lambda i, j, k: (j, k))
    else:
        b_spec = pl.BlockSpec((tk, tn), lambda i, j, k: (k, j))
    in_specs = [a_spec, b_spec]
    args = [a, b]
    if has_add:
        in_specs.append(pl.BlockSpec((tm, tn), lambda i, j, k: (i, j)))
        args.append(add)
    if dep is not None:
        in_specs.append(pl.BlockSpec(memory_space=pl.ANY))
        args.append(dep)
    return _pcall(
        body, name=name,
        grid=(m // tm, n // tn, nk),
        in_specs=in_specs,
        out_specs=pl.BlockSpec((tm, tn), lambda i, j, k: (i, j)),
        out_shape=jax.ShapeDtypeStruct((m, n), out_dtype),
        scratch_shapes=[pltpu.VMEM((tm, tn), F32)] if nk > 1 else [],
        compiler_params=_params(("parallel", "parallel", "arbitrary")),
    )(*args)


def _rms_fwd(x, g, *, name, dep=None):
    t, d = x.shape
    tt = _tile(t, TOK_TILE, 16)

    def body(x_ref, g_ref, *rest):
        h_ref = rest[-1]
        xv = x_ref[...]
        r = lax.rsqrt(jnp.mean(xv * xv, axis=-1, keepdims=True) + EPS)
        h_ref[...] = (xv * r * g_ref[...]).astype(BF16)

    deps = [] if dep is None else [dep]
    return _pcall(
        body, name=name, grid=(t // tt,),
        in_specs=[pl.BlockSpec((tt, d), lambda i: (i, 0)), pl.BlockSpec((1, d), lambda i: (0, 0))]
        + [pl.BlockSpec(memory_space=pl.ANY)] * len(deps),
        out_specs=pl.BlockSpec((tt, d), lambda i: (i, 0)),
        out_shape=jax.ShapeDtypeStruct((t, d), BF16),
        compiler_params=_params(("parallel",)),
    )(x, g.reshape(1, d), *deps)


def _rms_bwd(x, g, dh, dres, *, name):
    t, d = x.shape
    tt = _tile(t, TOK_TILE, 16)

    def body(x_ref, g_ref, dh_ref, dres_ref, dx_ref, dxb_ref, dg_ref):
        xv = x_ref[...]
        r = lax.rsqrt(jnp.mean(xv * xv, axis=-1, keepdims=True) + EPS)
        dy = dh_ref[...]
        dyg = dy * g_ref[...]
        dot = jnp.sum(dyg * xv, axis=-1, keepdims=True) * (1.0 / d)
        dx = dres_ref[...] + r * dyg - xv * (r * r * r) * dot
        dx_ref[...] = dx
        dxb_ref[...] = dx.astype(BF16)

        @pl.when(pl.program_id(0) == 0)
        def _():
            dg_ref[...] = jnp.zeros_like(dg_ref)

        dg_ref[...] += jnp.sum(dy * xv * r, axis=0, keepdims=True)

    row = pl.BlockSpec((tt, d), lambda i: (i, 0))
    vec = pl.BlockSpec((1, d), lambda i: (0, 0))
    return _pcall(
        body, name=name, grid=(t // tt,),
        in_specs=[row, vec, row, row],
        out_specs=[row, row, vec],
        out_shape=[jax.ShapeDtypeStruct((t, d), F32), jax.ShapeDtypeStruct((t, d), BF16),
                   jax.ShapeDtypeStruct((1, d), F32)],
        compiler_params=_params(("arbitrary",)),
    )(x, g.reshape(1, d), dh, dres)


def _lat_fwd(z, gq, gkv, lay, *, name):
    t = z.shape[0]
    ql, kvl = lay["QL"], lay["KVL"]
    tt = _tile(t, TOK_TILE, 16)

    def body(q_ref, kv_ref, gq_ref, gkv_ref, qn_ref, kvn_ref):
        for src, g_ref, dst in ((q_ref, gq_ref, qn_ref), (kv_ref, gkv_ref, kvn_ref)):
            v = src[...].astype(F32)
            r = lax.rsqrt(jnp.mean(v * v, axis=-1, keepdims=True) + EPS)
            dst[...] = (v * r * g_ref[...]).astype(BF16)

    return _pcall(
        body, name=name, grid=(t // tt,),
        in_specs=[pl.BlockSpec((tt, ql), lambda i: (i, lay["QC"] // ql)),
                  pl.BlockSpec((tt, kvl), lambda i: (i, lay["KVC"] // kvl)),
                  pl.BlockSpec((1, ql), lambda i: (0, 0)), pl.BlockSpec((1, kvl), lambda i: (0, 0))],
        out_specs=[pl.BlockSpec((tt, ql), lambda i: (i, 0)), pl.BlockSpec((tt, kvl), lambda i: (i, 0))],
        out_shape=[jax.ShapeDtypeStruct((t, ql), BF16), jax.ShapeDtypeStruct((t, kvl), BF16)],
        compiler_params=_params(("parallel",)),
    )(z, z, gq.reshape(1, ql), gkv.reshape(1, kvl))


def _lat_bwd(z, gq, gkv, dqn, dkvn, dpe, dz, lay, *, name):
    t = z.shape[0]
    ql, kvl = lay["QL"], lay["KVL"]
    tail = lay["NP"] - lay["QC"]
    assert lay["QC"] % tail == 0
    tt = _tile(t, TOK_TILE, 16)

    def body(q_ref, kv_ref, gq_ref, gkv_ref, dqn_ref, dkvn_ref, dpe_ref, dz_in, tail_ref, dgq_ref, dgkv_ref):
        del dz_in
        first = pl.program_id(0) == 0
        for src, g_ref, dy_ref, c0, dg_ref in ((q_ref, gq_ref, dqn_ref, 0, dgq_ref),
                                               (kv_ref, gkv_ref, dkvn_ref, ql, dgkv_ref)):
            v = src[...].astype(F32)
            n = v.shape[-1]
            r = lax.rsqrt(jnp.mean(v * v, axis=-1, keepdims=True) + EPS)
            dy = dy_ref[...]
            dyg = dy * g_ref[...]
            dot = jnp.sum(dyg * v, axis=-1, keepdims=True) * (1.0 / n)
            tail_ref[:, c0:c0 + n] = (r * dyg - v * (r * r * r) * dot).astype(BF16)

            @pl.when(first)
            def _():
                dg_ref[...] = jnp.zeros_like(dg_ref)

            dg_ref[...] += jnp.sum(dy * v * r, axis=0, keepdims=True)
        tail_ref[:, ql + kvl:ql + kvl + 128] = dpe_ref[...].astype(BF16)
        tail_ref[:, ql + kvl + 128:tail] = jnp.zeros((tt, tail - ql - kvl - 128), BF16)

    return _pcall(
        body, name=name, grid=(t // tt,),
        in_specs=[pl.BlockSpec((tt, ql), lambda i: (i, lay["QC"] // ql)),
                  pl.BlockSpec((tt, kvl), lambda i: (i, lay["KVC"] // kvl)),
                  pl.BlockSpec((1, ql), lambda i: (0, 0)), pl.BlockSpec((1, kvl), lambda i: (0, 0)),
                  pl.BlockSpec((tt, ql), lambda i: (i, 0)), pl.BlockSpec((tt, kvl), lambda i: (i, 0)),
                  pl.BlockSpec((tt, 128), lambda i: (i, 0)), pl.BlockSpec(memory_space=pl.ANY)],
        out_specs=[pl.BlockSpec((tt, tail), lambda i: (i, lay["QC"] // tail)),
                   pl.BlockSpec((1, ql), lambda i: (0, 0)), pl.BlockSpec((1, kvl), lambda i: (0, 0))],
        out_shape=[jax.ShapeDtypeStruct(dz.shape, BF16),
                   jax.ShapeDtypeStruct((1, ql), F32), jax.ShapeDtypeStruct((1, kvl), F32)],
        input_output_aliases={7: 0},
        compiler_params=_params(("arbitrary",)),
    )(z, z, gq.reshape(1, ql), gkv.reshape(1, kvl), dqn, dkvn, dpe, dz)


def _rope(r, c_tab, sa_tab, sb_tab):
    return r * c_tab + pltpu.roll(r, 96, 1) * sa_tab + pltpu.roll(r, 32, 1) * sb_tab


def _rope_t(d, c_tab, sa_tab, sb_tab):
    return d * c_tab + pltpu.roll(d * sa_tab, 32, 1) + pltpu.roll(d * sb_tab, 96, 1)


def _heads_fwd(q_raw, kv_raw, z, tabs, gq, gk, lay, *, name):
    t = z.shape[0]
    h = lay["H"]
    tt = _tile(t, TOK_TILE, 16)
    hg = _tile(h, HEAD_GROUP, 1)
    scale = 1.0 / math.sqrt(QK_DIM)

    def body(q_ref, kv_ref, pe_ref, c_ref, sa_ref, sb_ref, gq_ref, gk_ref, qh_ref, kh_ref, vh_ref):
        c_tab, sa_tab, sb_tab = c_ref[...], sa_ref[...], sb_ref[...]
        pe, gq_v, gk_v = pe_ref[...].astype(F32), gq_ref[...], gk_ref[...]
        ss_pe = jnp.sum(pe * pe, axis=-1, keepdims=True)
        for g in range(hg):
            q = q_ref[:, g * HEAD_PAD:(g + 1) * HEAD_PAD].astype(F32)
            r = lax.rsqrt(jnp.sum(q * q, axis=-1, keepdims=True) * (1.0 / QK_DIM) + EPS)
            qn = q * r * gq_v
            qh_ref[g] = (jnp.concatenate([qn[:, :NOPE], _rope(qn[:, NOPE:], c_tab, sa_tab, sb_tab)], axis=1)
                         * scale).astype(BF16)
            kv = kv_ref[:, g * HEAD_PAD:(g + 1) * HEAD_PAD].astype(F32)
            kn = kv[:, :NOPE]
            rk = lax.rsqrt((jnp.sum(kn * kn, axis=-1, keepdims=True) + ss_pe) * (1.0 / QK_DIM) + EPS)
            kh_ref[g] = jnp.concatenate(
                [kn * rk * gk_v[:, :NOPE], _rope(pe * rk * gk_v[:, NOPE:], c_tab, sa_tab, sb_tab)],
                axis=1).astype(BF16)
            vh_ref[g] = kv[:, NOPE:].astype(BF16)

    head = pl.BlockSpec((tt, hg * HEAD_PAD), lambda i, j: (i, j))
    tab = pl.BlockSpec((tt, 128), lambda i, j: (i, 0))
    gain = pl.BlockSpec((1, HEAD_PAD), lambda i, j: (0, 0))
    return _pcall(
        body, name=name, grid=(t // tt, h // hg),
        in_specs=[head, head, pl.BlockSpec((tt, 128), lambda i, j: (i, lay["KPE"] // 128)), tab, tab, tab, gain, gain],
        out_specs=[pl.BlockSpec((hg, tt, HEAD_PAD), lambda i, j: (j, i, 0)),
                   pl.BlockSpec((hg, tt, HEAD_PAD), lambda i, j: (j, i, 0)),
                   pl.BlockSpec((hg, tt, VDIM), lambda i, j: (j, i, 0))],
        out_shape=[jax.ShapeDtypeStruct((h, t, HEAD_PAD), BF16), jax.ShapeDtypeStruct((h, t, HEAD_PAD), BF16),
                   jax.ShapeDtypeStruct((h, t, VDIM), BF16)],
        compiler_params=_params(("parallel", "parallel")),
    )(q_raw, kv_raw, z, *tabs, gq, gk)


def _heads_bwd(q_raw, kv_raw, z, tabs, gq, gk, dqh, dkh, dvh, lay, *, name):
    t = z.shape[0]
    h = lay["H"]
    tt = _tile(t, TOK_TILE, 16)
    hg = _tile(h, HEAD_GROUP, 1)
    scale = 1.0 / math.sqrt(QK_DIM)

    def body(q_ref, kv_ref, pe_ref, c_ref, sa_ref, sb_ref, gq_ref, gk_ref, dqh_ref, dkh_ref, dvh_ref,
             dq_ref, dkv_ref, dpe_ref, dgq_ref, dgk_ref):
        i, j = pl.program_id(0), pl.program_id(1)
        c_tab, sa_tab, sb_tab = c_ref[...], sa_ref[...], sb_ref[...]

        @pl.when((i == 0) & (j == 0))
        def _():
            dgq_ref[...] = jnp.zeros_like(dgq_ref)
            dgk_ref[...] = jnp.zeros_like(dgk_ref)

        @pl.when(j == 0)
        def _():
            dpe_ref[...] = jnp.zeros_like(dpe_ref)

        def norm_bwd(v, g, dy):
            r = lax.rsqrt(jnp.sum(v * v, axis=-1, keepdims=True) * (1.0 / QK_DIM) + EPS)
            dyg = dy * g
            dot = jnp.sum(dyg * v, axis=-1, keepdims=True) * (1.0 / QK_DIM)
            return r * dyg - v * (r * r * r) * dot, jnp.sum(dy * v * r, axis=0, keepdims=True)

        pe, gq_v, gk_v = pe_ref[...].astype(F32), gq_ref[...], gk_ref[...]
        dpe, dgq, dgk = jnp.zeros_like(pe), jnp.zeros_like(gq_v), jnp.zeros_like(gk_v)
        for g in range(hg):
            cols = slice(g * HEAD_PAD, (g + 1) * HEAD_PAD)
            dqo = dqh_ref[g].astype(F32) * scale
            dy = jnp.concatenate([dqo[:, :NOPE], _rope_t(dqo[:, NOPE:], c_tab, sa_tab, sb_tab)], axis=1)
            dq, dg = norm_bwd(q_ref[:, cols].astype(F32), gq_v, dy)
            dq_ref[:, cols] = dq.astype(BF16)
            dgq = dgq + dg

            dko = dkh_ref[g].astype(F32)
            dy = jnp.concatenate([dko[:, :NOPE], _rope_t(dko[:, NOPE:], c_tab, sa_tab, sb_tab)], axis=1)
            kfull = jnp.concatenate([kv_ref[:, cols].astype(F32)[:, :NOPE], pe], axis=1)
            dk, dg = norm_bwd(kfull, gk_v, dy)
            dkv_ref[:, cols] = jnp.concatenate([dk[:, :NOPE].astype(BF16), dvh_ref[g]], axis=1)
            dpe = dpe + dk[:, NOPE:]
            dgk = dgk + dg
        dpe_ref[...] += dpe
        dgq_ref[...] += dgq
        dgk_ref[...] += dgk

    head = pl.BlockSpec((tt, hg * HEAD_PAD), lambda i, j: (i, j))
    tab = pl.BlockSpec((tt, 128), lambda i, j: (i, 0))
    gain = pl.BlockSpec((1, HEAD_PAD), lambda i, j: (0, 0))
    hm = pl.BlockSpec((hg, tt, HEAD_PAD), lambda i, j: (j, i, 0))
    return _pcall(
        body, name=name, grid=(t // tt, h // hg),
        in_specs=[head, head, pl.BlockSpec((tt, 128), lambda i, j: (i, lay["KPE"] // 128)), tab, tab, tab, gain, gain,
                  hm, hm, pl.BlockSpec((hg, tt, VDIM), lambda i, j: (j, i, 0))],
        out_specs=[head, head, tab, gain, gain],
        out_shape=[jax.ShapeDtypeStruct((t, h * HEAD_PAD), BF16), jax.ShapeDtypeStruct((t, h * HEAD_PAD), BF16),
                   jax.ShapeDtypeStruct((t, 128), F32),
                   jax.ShapeDtypeStruct((1, HEAD_PAD), F32), jax.ShapeDtypeStruct((1, HEAD_PAD), F32)],
        compiler_params=_params(("arbitrary", "arbitrary")),
    )(q_raw, kv_raw, z, *tabs, gq, gk, dqh, dkh, dvh)


def _lower_triangle(n):
    return lax.broadcasted_iota(jnp.int32, (n, n), 1) <= lax.broadcasted_iota(jnp.int32, (n, n), 0)


def _qk(q, k):
    return lax.dot_general(q, k, (((1,), (1,)), ((), ())), preferred_element_type=F32)


def _flash_fwd(qh, kh, vh, z, lay, nb, *, name):
    h, t, _ = qh.shape
    s = t // nb
    tq = _tile(s, ATT_TQ, 128)
    nq = s // tq
    att_w = h * VDIM
    gblk = lay["GATT"] // VDIM

    def body(q_ref, k_ref, v_ref, g_ref, att_ref, mix_ref, lse_ref):
        i = pl.program_id(2)
        tri = _lower_triangle(tq)
        for blk in range(nq):
            @pl.when(i == blk)
            def _():
                q = q_ref[...]
                pre = blk * tq
                sd = jnp.where(tri, _qk(q, k_ref[pre:pre + tq, :]), NEG)
                m = jnp.max(sd, axis=-1, keepdims=True)
                if pre:
                    sp = _qk(q, k_ref[0:pre, :])
                    m = jnp.maximum(m, jnp.max(sp, axis=-1, keepdims=True))
                pd = jnp.exp(sd - m)
                l = jnp.sum(pd, axis=-1, keepdims=True)
                acc = jnp.dot(pd.astype(BF16), v_ref[pre:pre + tq, :], preferred_element_type=F32)
                if pre:
                    pp = jnp.exp(sp - m)
                    l = l + jnp.sum(pp, axis=-1, keepdims=True)
                    acc = acc + jnp.dot(pp.astype(BF16), v_ref[0:pre, :], preferred_element_type=F32)
                o = acc / l
                att_ref[...] = o
                g = g_ref[...].astype(F32)
                mix_ref[...] = (o * (g * _sigmoid(g))).astype(BF16)
                lse_ref[...] = m + jnp.log(l)

    row = lambda hh, b, i: (b * nq + i, hh)
    seq = lambda hh, b, i: (hh, b, 0)
    return _pcall(
        body, name=name, grid=(h, nb, nq),
        in_specs=[pl.BlockSpec((None, tq, HEAD_PAD), lambda hh, b, i: (hh, b * nq + i, 0)),
                  pl.BlockSpec((None, s, HEAD_PAD), seq),
                  pl.BlockSpec((None, s, VDIM), seq),
                  pl.BlockSpec((tq, VDIM), lambda hh, b, i: (b * nq + i, gblk + hh))],
        out_specs=[pl.BlockSpec((tq, VDIM), row), pl.BlockSpec((tq, VDIM), row),
                   pl.BlockSpec((None, tq, 1), lambda hh, b, i: (hh, b * nq + i, 0))],
        out_shape=[jax.ShapeDtypeStruct((t, att_w), F32), jax.ShapeDtypeStruct((t, 2 * att_w), BF16),
                   jax.ShapeDtypeStruct((h, t, 1), F32)],
        compiler_params=_params(("parallel", "parallel", "parallel")),
    )(qh, kh, vh, z)


def _gate_bwd(dmix, att, z, dz, lay, *, name):
    t, att_w = att.shape
    h = att_w // VDIM
    tt = _tile(t, TOK_TILE, 16)
    gblk = lay["GATT"] // att_w

    def body(dm_ref, o_ref, g_ref, dz_in, do_ref, delta_ref, dg_ref):
        del dz_in
        dm, o, g = dm_ref[...], o_ref[...], g_ref[...].astype(F32)
        sg = _sigmoid(g)
        do = dm * (g * sg)
        do_ref[...] = do.astype(BF16)
        prod = do * o
        for hh in range(h):
            delta_ref[hh] = jnp.sum(prod[:, hh * VDIM:(hh + 1) * VDIM], axis=-1, keepdims=True)
        dg_ref[...] = (dm * o * _dsilu(g, sg)).astype(BF16)

    blk = pl.BlockSpec((tt, att_w), lambda i: (i, 0))
    gate = pl.BlockSpec((tt, att_w), lambda i: (i, gblk))
    return _pcall(
        body, name=name, grid=(t // tt,),
        in_specs=[blk, blk, gate, pl.BlockSpec(memory_space=pl.ANY)],
        out_specs=[blk, pl.BlockSpec((h, tt, 1), lambda i: (0, i, 0)), gate],
        out_shape=[jax.ShapeDtypeStruct((t, att_w), BF16), jax.ShapeDtypeStruct((h, t, 1), F32),
                   jax.ShapeDtypeStruct(dz.shape, BF16)],
        input_output_aliases={3: 2},
        compiler_params=_params(("parallel",)),
    )(dmix, att, z, dz)


def _flash_bwd(qh, kh, vh, do, lse, delta, nb, *, name):
    h, t, _ = qh.shape
    s = t // nb
    tk = _tile(s, ATT_TQ, 128)
    nk = s // tk
    tn_dims = (((0,), (0,)), ((), ()))

    def body(q_ref, k_ref, v_ref, do_ref, lse_ref, dl_ref, dq_out, dk_ref, dv_ref, dq_ref):
        j = pl.program_id(2)
        tri = _lower_triangle(tk)

        @pl.when(j == 0)
        def _():
            dq_ref[...] = jnp.zeros_like(dq_ref)

        def rows_against_block(r0, r1, masked):
            q, do_v = q_ref[r0:r1, :], do_ref[r0:r1, :]
            k = k_ref[...]
            sc = _qk(q, k)
            if masked:
                sc = jnp.where(tri, sc, NEG)
            p = jnp.exp(sc - lse_ref[r0:r1, :])
            dv = lax.dot_general(p.astype(BF16), do_v, tn_dims, preferred_element_type=F32)
            ds = (p * (_qk(do_v, v_ref[...]) - dl_ref[r0:r1, :])).astype(BF16)
            dq_ref[r0:r1, :] += jnp.dot(ds, k, preferred_element_type=F32)
            return lax.dot_general(ds, q, tn_dims, preferred_element_type=F32), dv

        for blk in range(nk):
            @pl.when(j == blk)
            def _():
                r0 = blk * tk
                dk, dv = rows_against_block(r0, r0 + tk, True)
                if r0 + tk < s:
                    dk2, dv2 = rows_against_block(r0 + tk, s, False)
                    dk, dv = dk + dk2, dv + dv2
                dk_ref[...] = dk.astype(BF16)
                dv_ref[...] = dv.astype(BF16)

        @pl.when(j == nk - 1)
        def _():
            dq_out[...] = dq_ref[...].astype(BF16)

    seq = lambda hh, b, j: (hh, b, 0)
    kv = lambda hh, b, j: (hh, b * nk + j, 0)
    return _pcall(
        body, name=name, grid=(h, nb, nk),
        in_specs=[pl.BlockSpec((None, s, HEAD_PAD), seq),
                  pl.BlockSpec((None, tk, HEAD_PAD), kv),
                  pl.BlockSpec((None, tk, VDIM), kv),
                  pl.BlockSpec((s, VDIM), lambda hh, b, j: (b, hh)),
                  pl.BlockSpec((None, s, 1), seq), pl.BlockSpec((None, s, 1), seq)],
        out_specs=[pl.BlockSpec((None, s, HEAD_PAD), seq), pl.BlockSpec((None, tk, HEAD_PAD), kv),
                   pl.BlockSpec((None, tk, VDIM), kv)],
        out_shape=[jax.ShapeDtypeStruct((h, t, HEAD_PAD), BF16), jax.ShapeDtypeStruct((h, t, HEAD_PAD), BF16),
                   jax.ShapeDtypeStruct((h, t, VDIM), BF16)],
        scratch_shapes=[pltpu.VMEM((s, HEAD_PAD), F32)],
        compiler_params=_params(("parallel", "parallel", "arbitrary")),
    )(qh, kh, vh, do, lse, delta)


SUBLANES = 8


def _stage_row_shifts(ext, sh, c0, lc, rows):
    for r in range(1, SUBLANES):
        sh[r - 1, 0:rows, :] = ext[r:r + rows, c0:c0 + lc]


def _row_window(ext, sh, c0, lc, off, n):
    r = off % SUBLANES
    if r == 0:
        return ext[off:off + n, c0:c0 + lc]
    return sh[r - 1, off - r:off - r + n, :]


def _conv_fwd(z, mix, w_dw, b_dw, ln_g, ln_b, lay, nb, *, name):
    t = z.shape[0]
    cw = lay["CW"]
    s = t // nb
    tt = _tile(s, CONV_TILE, HALO)
    ns = s // tt
    hb = tt // HALO
    lc = _tile(cw, LANE_CHUNK, 128)

    def body(a_ref, b_ref, ap_ref, bp_ref, gc_ref, w_ref, bias_ref, lg_ref, lb_ref, mix_in, mix_ref, u_ref, c_ref,
             ext, sh):
        del mix_in
        i = pl.program_id(1)
        u = a_ref[...].astype(F32) * _sigmoid(b_ref[...].astype(F32))
        u_ref[...] = u
        ext[0:HALO, :] = jnp.where(i > 0, ap_ref[...].astype(F32) * _sigmoid(bp_ref[...].astype(F32)), 0.0)
        ext[HALO:HALO + tt, :] = u
        for c0 in range(0, cw, lc):
            _stage_row_shifts(ext, sh, c0, lc, tt + HALO - SUBLANES)
            acc = jnp.zeros((tt, lc), F32) + bias_ref[:, c0:c0 + lc]
            for k in range(CONV_K):
                off = HALO - (CONV_K - 1) + k
                acc = acc + w_ref[k:k + 1, c0:c0 + lc] * _row_window(ext, sh, c0, lc, off, tt)
            c_ref[:, c0:c0 + lc] = acc
        c = c_ref[...]
        mu = jnp.mean(c, axis=-1, keepdims=True)
        xc = c - mu
        var = jnp.mean(xc * xc, axis=-1, keepdims=True)
        y = xc * lax.rsqrt(var + EPS) * lg_ref[...] + lb_ref[...]
        g = gc_ref[...].astype(F32)
        mix_ref[...] = (y * _sigmoid(y) * (g * _sigmoid(g))).astype(BF16)

    cur = lambda col: pl.BlockSpec((tt, cw), lambda b, i: (b * ns + i, col))
    prev = lambda col: pl.BlockSpec((HALO, cw), lambda b, i: (jnp.maximum((b * ns + i) * hb - 1, 0), col))
    vec = pl.BlockSpec((1, cw), lambda b, i: (0, 0))
    out_row = pl.BlockSpec((tt, cw), lambda b, i: (b * ns + i, 0))
    return _pcall(
        body, name=name, grid=(nb, ns),
        in_specs=[cur(lay["A"] // cw), cur(lay["B"] // cw), prev(lay["A"] // cw), prev(lay["B"] // cw),
                  cur(lay["GCONV"] // cw), pl.BlockSpec((HALO, cw), lambda b, i: (0, 0)), vec, vec, vec,
                  pl.BlockSpec(memory_space=pl.ANY)],
        out_specs=[pl.BlockSpec((tt, cw), lambda b, i: (b * ns + i, 1)), out_row, out_row],
        out_shape=[jax.ShapeDtypeStruct(mix.shape, BF16), jax.ShapeDtypeStruct((t, cw), F32),
                   jax.ShapeDtypeStruct((t, cw), F32)],
        scratch_shapes=[pltpu.VMEM((tt + HALO, cw), F32),
                        pltpu.VMEM((SUBLANES - 1, tt + HALO - SUBLANES, lc), F32)],
        input_output_aliases={9: 0},
        compiler_params=_params(("parallel", "parallel")),
    )(z, z, z, z, z, w_dw, b_dw.reshape(1, cw), ln_g.reshape(1, cw), ln_b.reshape(1, cw), mix)


def _conv_bwd_ln(c_pre, z, dmix, ln_g, ln_b, lay, *, name):
    t, cw = c_pre.shape
    tt = _tile(t, TOK_TILE, 16)

    def body(c_ref, gc_ref, dm_ref, lg_ref, lb_ref, dc_ref, dgc_ref, dlg_ref, dlb_ref, dbias_ref):
        @pl.when(pl.program_id(0) == 0)
        def _():
            dlg_ref[...] = jnp.zeros_like(dlg_ref)
            dlb_ref[...] = jnp.zeros_like(dlb_ref)
            dbias_ref[...] = jnp.zeros_like(dbias_ref)

        c = c_ref[...]
        mu = jnp.mean(c, axis=-1, keepdims=True)
        xc = c - mu
        rstd = lax.rsqrt(jnp.mean(xc * xc, axis=-1, keepdims=True) + EPS)
        xhat = xc * rstd
        y = xhat * lg_ref[...] + lb_ref[...]
        sy = _sigmoid(y)
        g = gc_ref[...].astype(F32)
        sg = _sigmoid(g)
        dm = dm_ref[...].astype(F32)
        dgc_ref[...] = (dm * (y * sy) * _dsilu(g, sg)).astype(BF16)
        dy = dm * (g * sg) * _dsilu(y, sy)
        dlb_ref[...] += jnp.sum(dy, axis=0, keepdims=True)
        dlg_ref[...] += jnp.sum(dy * xhat, axis=0, keepdims=True)
        dxh = dy * lg_ref[...]
        dc = rstd * (dxh - jnp.mean(dxh, axis=-1, keepdims=True)
                     - xhat * jnp.mean(dxh * xhat, axis=-1, keepdims=True))
        dc_ref[...] = dc
        dbias_ref[...] += jnp.sum(dc, axis=0, keepdims=True)

    row = pl.BlockSpec((tt, cw), lambda i: (i, 0))
    vec = pl.BlockSpec((1, cw), lambda i: (0, 0))
    return _pcall(
        body, name=name, grid=(t // tt,),
        in_specs=[row, pl.BlockSpec((tt, cw), lambda i: (i, lay["GCONV"] // cw)),
                  pl.BlockSpec((tt, cw), lambda i: (i, 1)), vec, vec],
        out_specs=[row, pl.BlockSpec((tt, cw), lambda i: (i, lay["GCONV"] // cw)), vec, vec, vec],
        out_shape=[jax.ShapeDtypeStruct((t, cw), F32), jax.ShapeDtypeStruct((t, lay["NP"]), BF16),
                   jax.ShapeDtypeStruct((1, cw), F32), jax.ShapeDtypeStruct((1, cw), F32),
                   jax.ShapeDtypeStruct((1, cw), F32)],
        compiler_params=_params(("arbitrary",)),
    )(c_pre, z, dmix, ln_g.reshape(1, cw), ln_b.reshape(1, cw))


def _conv_bwd_dw(dc, u, z, w_dw, dz, lay, nb, *, name):
    t, cw = dc.shape
    s = t // nb
    tt = _tile(s, CONV_TILE, HALO)
    ns = s // tt
    hb = tt // HALO
    lc = _tile(cw, LANE_CHUNK, 128)

    def body(dc_ref, dcn_ref, u_ref, up_ref, a_ref, b_ref, w_ref, dz_in, dab_ref, dw_ref, ext_dc, ext_u, du_ref,
             sh_dc, sh_u):
        del dz_in
        b_i, i = pl.program_id(0), pl.program_id(1)

        @pl.when((b_i == 0) & (i == 0))
        def _():
            dw_ref[...] = jnp.zeros_like(dw_ref)

        dc_v = dc_ref[...]
        ext_dc[0:tt, :] = dc_v
        ext_dc[tt:tt + HALO, :] = jnp.where(i < ns - 1, dcn_ref[...], 0.0)
        ext_u[0:HALO, :] = jnp.where(i > 0, up_ref[...], 0.0)
        ext_u[HALO:HALO + tt, :] = u_ref[...]
        for c0 in range(0, cw, lc):
            _stage_row_shifts(ext_dc, sh_dc, c0, lc, tt + HALO - SUBLANES)
            _stage_row_shifts(ext_u, sh_u, c0, lc, tt + HALO - SUBLANES)
            acc = jnp.zeros((tt, lc), F32)
            dcc = dc_v[:, c0:c0 + lc]
            for k in range(CONV_K):
                acc = acc + w_ref[k:k + 1, c0:c0 + lc] * _row_window(ext_dc, sh_dc, c0, lc, CONV_K - 1 - k, tt)
                off = HALO - (CONV_K - 1) + k
                dw_ref[k:k + 1, c0:c0 + lc] += jnp.sum(dcc * _row_window(ext_u, sh_u, c0, lc, off, tt),
                                                       axis=0, keepdims=True)
            du_ref[:, c0:c0 + lc] = acc
        du = du_ref[...]
        sb = _sigmoid(b_ref[...].astype(F32))
        dab_ref[:, 0:cw] = (du * sb).astype(BF16)
        dab_ref[:, cw:2 * cw] = (du * a_ref[...].astype(F32) * sb * (1.0 - sb)).astype(BF16)

    last = nb * ns * hb - 1
    row = pl.BlockSpec((tt, cw), lambda b, i: (b * ns + i, 0))
    return _pcall(
        body, name=name, grid=(nb, ns),
        in_specs=[row, pl.BlockSpec((HALO, cw), lambda b, i: (jnp.minimum((b * ns + i + 1) * hb, last), 0)),
                  row, pl.BlockSpec((HALO, cw), lambda b, i: (jnp.maximum((b * ns + i) * hb - 1, 0), 0)),
                  pl.BlockSpec((tt, cw), lambda b, i: (b * ns + i, lay["A"] // cw)),
                  pl.BlockSpec((tt, cw), lambda b, i: (b * ns + i, lay["B"] // cw)),
                  pl.BlockSpec((HALO, cw), lambda b, i: (0, 0)), pl.BlockSpec(memory_space=pl.ANY)],
        out_specs=[pl.BlockSpec((tt, 2 * cw), lambda b, i: (b * ns + i, 0)),
                   pl.BlockSpec((HALO, cw), lambda b, i: (0, 0))],
        out_shape=[jax.ShapeDtypeStruct(dz.shape, BF16), jax.ShapeDtypeStruct((HALO, cw), F32)],
        scratch_shapes=[pltpu.VMEM((tt + HALO, cw), F32), pltpu.VMEM((tt + HALO, cw), F32),
                        pltpu.VMEM((tt, cw), F32),
                        pltpu.VMEM((SUBLANES - 1, tt + HALO - SUBLANES, lc), F32),
                        pltpu.VMEM((SUBLANES - 1, tt + HALO - SUBLANES, lc), F32)],
        input_output_aliases={7: 0},
        compiler_params=_params(("arbitrary", "arbitrary")),
    )(dc, dc, u, u, z, z, w_dw, dz)


def _loss_head(y, target, *, name):
    t, d = y.shape
    tt = _tile(t, TOK_TILE, 16)

    def body(y_ref, t_ref, sse_ref, dy_ref, dyb_ref):
        @pl.when(pl.program_id(0) == 0)
        def _():
            sse_ref[...] = jnp.zeros_like(sse_ref)

        e = y_ref[...] - t_ref[...]
        sse_ref[...] += jnp.sum(e * e)
        dy = e * (1.0 / d)
        dy_ref[...] = dy
        dyb_ref[...] = dy.astype(BF16)

    row = pl.BlockSpec((tt, d), lambda i: (i, 0))
    return _pcall(
        body, name=name, grid=(t // tt,),
        in_specs=[row, row],
        out_specs=[pl.BlockSpec((8, 128), lambda i: (0, 0)), row, row],
        out_shape=[jax.ShapeDtypeStruct((8, 128), F32), jax.ShapeDtypeStruct((t, d), F32),
                   jax.ShapeDtypeStruct((t, d), BF16)],
        compiler_params=_params(("arbitrary",)),
    )(y, target)


def _adam(w, m, v, g_parts, *, name, layer=0, layers=1, prev=None):
    rows, cols = w.shape
    slab = rows // layers
    tr = _tile(slab, max(16, ADAM_BLOCK_ELEMS // cols), 16)
    blk0 = layer * (slab // tr)
    n = len(g_parts)
    n_prev = 0 if prev is None else 4

    def body(*refs):
        w_ref, m_ref, v_ref = refs[:3]
        g_refs = refs[3:3 + n]
        g_out, d_out, m_out, v_out = refs[3 + n + n_prev:]
        g = g_refs[0][...].astype(F32)
        for r in g_refs[1:]:
            g = g + r[...].astype(F32)
        g = g[:, :cols]
        m_new = ADAM_B1 * m_ref[...] + (1.0 - ADAM_B1) * g
        v_new = ADAM_B2 * v_ref[...] + (1.0 - ADAM_B2) * (g * g)
        m_hat = m_new / (1.0 - ADAM_B1 ** ADAM_STEP)
        v_hat = v_new / (1.0 - ADAM_B2 ** ADAM_STEP)
        g_out[...] = g
        d_out[...] = -ADAM_LR * (m_hat / (jnp.sqrt(v_hat) + ADAM_EPS) + ADAM_WD * w_ref[...])
        m_out[...] = m_new
        v_out[...] = v_new

    blk = pl.BlockSpec((tr, cols), lambda i: (blk0 + i, 0))
    g_specs, g_args = [], []
    for arr, lead in g_parts:
        g_args.append(arr)
        if lead is None:
            g_specs.append(pl.BlockSpec((tr, cols), lambda i: (i, 0)))
        else:
            g_specs.append(pl.BlockSpec((None, tr, arr.shape[2]), functools.partial(lambda i, p: (p, i, 0), p=lead)))
    out = jax.ShapeDtypeStruct((rows, cols), F32)
    return _pcall(
        body, name=name, grid=(slab // tr,),
        in_specs=[blk, blk, blk] + g_specs + [pl.BlockSpec(memory_space=pl.ANY)] * n_prev,
        out_specs=[blk, blk, blk, blk],
        out_shape=[out, out, out, out],
        input_output_aliases={3 + n + k: k for k in range(n_prev)},
        compiler_params=_params(("parallel",)),
    )(w, m, v, *g_args, *(prev or ()))


def _position():
    return lax.axis_index("x"), lax.axis_index("y"), lax.axis_index("c")


def _block_id(p):
    return 4 * p[0] + 2 * p[1] + p[2]


def _flip(p, mask):
    return tuple((1 - v) if (mask >> (2 - a)) & 1 else v for a, v in enumerate(p))


def _all_gather(xs, *, name):
    n = len(xs)

    def body(*refs):
        x_refs, o_refs = refs[:n], refs[n:2 * n]
        send_sems, recv_sems, local_sems = refs[2 * n:]
        x, y, c = _position()
        me, sibling = (x, y, c), (x, y, 1 - c)
        chips = [(1 - x, y), (x, 1 - y), (1 - x, 1 - y)]

        def copy(t, k, block, to, src=None):
            dst = o_refs[t].at[_block_id(block)]
            return pltpu.make_async_remote_copy(
                src_ref=dst if src is None else src, dst_ref=dst,
                send_sem=send_sems.at[t, k], recv_sem=recv_sems.at[t, k],
                device_id=to, device_id_type=MESH)

        mine = [pltpu.make_async_copy(x_refs[t], o_refs[t].at[_block_id(me)], local_sems.at[t]) for t in range(n)]
        for cp in mine:
            cp.start()
        started = []
        for t in range(n):
            first = [copy(t, 0, me, sibling, src=x_refs[t])]
            first += [copy(t, 1 + j, me, (*chip, c), src=x_refs[t]) for j, chip in enumerate(chips)]
            for cp in first:
                cp.start()
            started += first
        for j, chip in enumerate(chips):
            for t in range(n):
                copy(t, 1 + j, (*chip, c), me).wait_recv()
                fwd = copy(t, 4 + j, (*chip, c), sibling)
                fwd.start()
                started.append(fwd)
        for t in range(n):
            copy(t, 0, sibling, me).wait_recv()
            for j, chip in enumerate(chips):
                copy(t, 4 + j, (*chip, 1 - c), me).wait_recv()
        for cp in started:
            cp.wait_send()
        for cp in mine:
            cp.wait()

    any_spec = pl.BlockSpec(memory_space=pl.ANY)
    return _pcall(
        body, name=name,
        in_specs=[any_spec] * n, out_specs=[any_spec] * n,
        out_shape=[jax.ShapeDtypeStruct((N_DEV,) + a.shape, a.dtype) for a in xs],
        scratch_shapes=[pltpu.SemaphoreType.DMA((n, 7)), pltpu.SemaphoreType.DMA((n, 7)),
                        pltpu.SemaphoreType.DMA((n,))],
    )(*xs)


def _pushed_copy(x_ref, land_ref, send_sems, recv_sems, t, mask, me, chunked, at_receiver):
    peer = _flip(me, mask)
    src = x_ref.at[_block_id(peer)] if chunked else x_ref
    slot = _block_id(peer) if at_receiver else _block_id(me)
    k = (N_DEV - 1) * t + mask - 1
    return pltpu.make_async_remote_copy(
        src_ref=src, dst_ref=land_ref.at[slot], send_sem=send_sems.at[k], recv_sem=recv_sems.at[k],
        device_id=peer, device_id_type=MESH)


ALL_PEERS = tuple(range(1, N_DEV))
NEAR_PEERS = (1, 2, 4, 6)
FAR_RELAY = (2, 4, 6)


def _push_start(xs, chunked, *, name, after=None, masks=ALL_PEERS):
    n = len(xs)
    lands = [lax.empty(a.shape if chunked else (N_DEV,) + a.shape, a.dtype) for a in xs]

    n_after = 0 if after is None else 1

    def body(*refs):
        x_refs, land_refs = refs[:n], refs[n:2 * n]
        send_sems, recv_sems = refs[2 * n + n_after], refs[2 * n + n_after + 1]
        token = refs[4 * n + n_after + 2]
        me = _position()
        for t in range(n):
            for mask in masks:
                _pushed_copy(x_refs[t], land_refs[t], send_sems, recv_sems, t, mask, me, chunked, False).start()
        token[...] = jnp.zeros_like(token)

    hbm = pl.BlockSpec(memory_space=pltpu.HBM)
    sem = pl.BlockSpec(memory_space=pltpu.SEMAPHORE)
    outs = _pcall(
        body, name=name,
        in_specs=[hbm] * (2 * n) + [pl.BlockSpec(memory_space=pl.ANY)] * n_after,
        out_specs=[sem, sem] + [hbm] * (2 * n) + [pl.BlockSpec(memory_space=pltpu.VMEM)],
        out_shape=[pltpu.SemaphoreType.DMA(((N_DEV - 1) * n,)), pltpu.SemaphoreType.DMA(((N_DEV - 1) * n,))]
        + [pltpu.HBM(a.shape, a.dtype) for a in xs] + [pltpu.HBM(a.shape, a.dtype) for a in lands]
        + [jax.ShapeDtypeStruct((8, 128), F32)],
        input_output_aliases={i: 2 + i for i in range(2 * n)},
        compiler_params=pltpu.CompilerParams(has_side_effects=pltpu.SideEffectType.DATAFLOW_SIDE_EFFECTING),
    )(*[pltpu.with_memory_space_constraint(a, pltpu.HBM) for a in list(xs) + lands], *([after] * n_after))
    return outs[0], outs[1], outs[2:2 + n], outs[2 + n:2 + 2 * n], outs[2 + 2 * n]


def _push_wait(handle, after, chunked, *, name, masks=ALL_PEERS):
    send_sems, recv_sems, xs, lands, _ = handle
    n = len(xs)

    def body(*refs):
        x_refs, land_refs = refs[:n], refs[n:2 * n]
        send_sems, recv_sems = refs[2 * n], refs[2 * n + 1]
        me = _position()
        for t in range(n):
            for mask in masks:
                _pushed_copy(x_refs[t], land_refs[t], send_sems, recv_sems, t, mask, me, chunked, False).wait_send()
                _pushed_copy(x_refs[t], land_refs[t], send_sems, recv_sems, t, mask, me, chunked, True).wait_recv()

    hbm = pl.BlockSpec(memory_space=pltpu.HBM)
    sem = pl.BlockSpec(memory_space=pltpu.SEMAPHORE)
    outs = _pcall(
        body, name=name,
        in_specs=[hbm] * (2 * n) + [sem, sem, pl.BlockSpec(memory_space=pl.ANY)],
        out_specs=[hbm] * (2 * n),
        out_shape=[pltpu.HBM(a.shape, a.dtype) for a in list(xs) + list(lands)],
        input_output_aliases={i: i for i in range(2 * n)},
        compiler_params=pltpu.CompilerParams(has_side_effects=pltpu.SideEffectType.DATAFLOW_SIDE_EFFECTING),
    )(*xs, *lands, send_sems, recv_sems, after)
    return outs[:n], outs[n:]


def _relayed_copy(land_ref, send_sems, recv_sems, t, j, me, at_receiver):
    source = _flip(_flip(me, 1), FAR_RELAY[j]) if at_receiver else _flip(me, FAR_RELAY[j])
    slot = land_ref.at[_block_id(source)]
    k = len(FAR_RELAY) * t + j
    return pltpu.make_async_remote_copy(
        src_ref=slot, dst_ref=slot, send_sem=send_sems.at[k], recv_sem=recv_sems.at[k],
        device_id=_flip(me, 1), device_id_type=MESH)


def _relay_start(lands, *, name, after):
    n = len(lands)

    def body(*refs):
        land_refs = refs[:n]
        send_sems, recv_sems = refs[n + 1], refs[n + 2]
        token = refs[2 * n + 3]
        me = _position()
        for t in range(n):
            for j in range(len(FAR_RELAY)):
                _relayed_copy(land_refs[t], send_sems, recv_sems, t, j, me, False).start()
        token[...] = jnp.zeros_like(token)

    hbm = pl.BlockSpec(memory_space=pltpu.HBM)
    sem = pl.BlockSpec(memory_space=pltpu.SEMAPHORE)
    outs = _pcall(
        body, name=name,
        in_specs=[hbm] * n + [pl.BlockSpec(memory_space=pl.ANY)],
        out_specs=[sem, sem] + [hbm] * n + [pl.BlockSpec(memory_space=pltpu.VMEM)],
        out_shape=[pltpu.SemaphoreType.DMA((len(FAR_RELAY) * n,)), pltpu.SemaphoreType.DMA((len(FAR_RELAY) * n,))]
        + [pltpu.HBM(a.shape, a.dtype) for a in lands] + [jax.ShapeDtypeStruct((8, 128), F32)],
        input_output_aliases={i: 2 + i for i in range(n)},
        compiler_params=pltpu.CompilerParams(has_side_effects=pltpu.SideEffectType.DATAFLOW_SIDE_EFFECTING),
    )(*lands, after)
    return outs[0], outs[1], outs[2:2 + n], outs[2 + n]


def _relay_wait(handle, after, *, name):
    send_sems, recv_sems, lands, _ = handle
    n = len(lands)

    def body(*refs):
        land_refs = refs[:n]
        send_sems, recv_sems = refs[n], refs[n + 1]
        me = _position()
        for t in range(n):
            for j in range(len(FAR_RELAY)):
                _relayed_copy(land_refs[t], send_sems, recv_sems, t, j, me, False).wait_send()
                _relayed_copy(land_refs[t], send_sems, recv_sems, t, j, me, True).wait_recv()

    hbm = pl.BlockSpec(memory_space=pltpu.HBM)
    sem = pl.BlockSpec(memory_space=pltpu.SEMAPHORE)
    return _pcall(
        body, name=name,
        in_specs=[hbm] * n + [sem, sem, pl.BlockSpec(memory_space=pl.ANY)],
        out_specs=[hbm] * n,
        out_shape=[pltpu.HBM(a.shape, a.dtype) for a in lands],
        input_output_aliases={i: i for i in range(n)},
        compiler_params=pltpu.CompilerParams(has_side_effects=pltpu.SideEffectType.DATAFLOW_SIDE_EFFECTING),
    )(*lands, send_sems, recv_sems, after)


def _all_reduce_small(pack, *, name):
    rows = pack.shape[0]

    def body(p_ref, o_ref, gath, send_sems, recv_sems):
        me = _position()
        my_id = _block_id(me)
        gath[my_id] = p_ref[...]
        sent = []
        for mask in range(1, N_DEV):
            peer = _flip(me, mask)
            cp = pltpu.make_async_remote_copy(
                src_ref=p_ref, dst_ref=gath.at[my_id], send_sem=send_sems.at[mask - 1],
                recv_sem=recv_sems.at[mask - 1], device_id=peer, device_id_type=MESH)
            cp.start()
            sent.append(cp)
        for mask in range(1, N_DEV):
            slot = gath.at[_block_id(_flip(me, mask))]
            pltpu.make_async_remote_copy(
                src_ref=slot, dst_ref=slot, send_sem=send_sems.at[mask - 1], recv_sem=recv_sems.at[mask - 1],
                device_id=me, device_id_type=MESH).wait_recv()
        for cp in sent:
            cp.wait_send()
        total = gath[0]
        for s in range(1, N_DEV):
            total = total + gath[s]
        o_ref[...] = total

    vm = pl.BlockSpec(memory_space=pltpu.VMEM)
    return _pcall(
        body, name=name,
        in_specs=[vm], out_specs=vm,
        out_shape=jax.ShapeDtypeStruct(pack.shape, F32),
        scratch_shapes=[pltpu.VMEM((N_DEV, rows, 128), F32), pltpu.SemaphoreType.DMA((7,)),
                        pltpu.SemaphoreType.DMA((7,))],
        compiler_params=pltpu.CompilerParams(vmem_limit_bytes=VMEM_LIMIT),
    )(pack)


def _layout(d, ql, kvl):
    cw = d // 2
    att = d // 2
    lay = {"D": d, "CW": cw, "ATT": att, "H": att // VDIM, "QL": ql, "KVL": kvl}
    lay["A"], lay["B"], lay["GATT"], lay["GCONV"] = 0, cw, 2 * cw, 2 * cw + att
    lay["QC"] = lay["GCONV"] + cw
    lay["KVC"] = lay["QC"] + ql
    lay["KPE"] = lay["KVC"] + kvl
    used = lay["KPE"] + 128
    tn = min(MM_TN, 1024)
    lay["NP"] = -(-used // tn) * tn
    assert att == cw and lay["QC"] % ql == 0 and lay["KVC"] % kvl == 0 and lay["KPE"] % 128 == 0
    lay["o_kv"], lay["o_pe"] = ql, ql + kvl
    lay["o_ga"] = lay["o_pe"] + ROPE
    lay["o_u"] = lay["o_ga"] + att
    lay["o_gc"] = lay["o_u"] + 2 * cw
    lay["IN_COLS"] = lay["o_gc"] + cw
    return lay


def _lane_pad(n):
    return -(-n // 128) * 128


def _assemble_w_in(g, shard, lay, *, name):
    _, d, padw = g.shape
    tr = _tile(d, 256, 16)
    sections = [(lay["A"], 2 * lay["CW"], lay["o_u"]), (lay["GATT"], lay["ATT"], lay["o_ga"]),
                (lay["GCONV"], lay["CW"], lay["o_gc"]), (lay["QC"], lay["QL"], 0),
                (lay["KVC"], lay["KVL"], lay["o_kv"]), (lay["KPE"], ROPE, lay["o_pe"])]

    def runs_of_tile(j):
        for start, width, orig in sections:
            if start <= j * 128 < start + width:
                todo, col, lane, out = min(128, start + width - j * 128), orig + j * 128 - start, 0, []
                while todo:
                    p, o = divmod(col, shard)
                    take = min(todo, shard - o)
                    first = (o // 128) * 128
                    win = 256 if first + 256 <= padw else 128
                    out.append((p, first, win, o - first, lane, take))
                    col, lane, todo = col + take, lane + take, todo - take
                return out
        return []

    def body(g_ref, o_ref):
        movers = {}

        def mover(win, off, lane, take):
            key = (win, off, lane, take)
            if key not in movers:
                row = lax.broadcasted_iota(jnp.int32, (win, 128), 0)
                col = lax.broadcasted_iota(jnp.int32, (win, 128), 1)
                hit = (row - off == col - lane) & (col >= lane) & (col < lane + take)
                movers[key] = jnp.where(hit, 1.0, 0.0).astype(BF16)
            return movers[key]

        for j in range(lay["NP"] // 128):
            tile = None
            for p, first, win, off, lane, take in runs_of_tile(j):
                part = jnp.dot(g_ref[p, :, first:first + win], mover(win, off, lane, take),
                               preferred_element_type=F32)
                tile = part if tile is None else tile + part
            if tile is None:
                tile = jnp.zeros((tr, 128), F32)
            o_ref[:, j * 128:(j + 1) * 128] = tile.astype(BF16)

    return _pcall(
        body, name=name, grid=(d // tr,),
        in_specs=[pl.BlockSpec((N_DEV, tr, padw), lambda i: (0, i, 0))],
        out_specs=pl.BlockSpec((tr, lay["NP"]), lambda i: (i, 0)),
        out_shape=jax.ShapeDtypeStruct((d, lay["NP"]), BF16),
        compiler_params=_params(("parallel",)),
    )(g)


def _split_dw_in(dwp, shard, lay, *, name):
    d = dwp.shape[0]
    padw = _lane_pad(shard)
    tr = _tile(d, 256, 16)
    sections = sorted([(lay["A"], 2 * lay["CW"], lay["o_u"]), (lay["GATT"], lay["ATT"], lay["o_ga"]),
                       (lay["GCONV"], lay["CW"], lay["o_gc"]), (lay["QC"], lay["QL"], 0),
                       (lay["KVC"], lay["KVL"], lay["o_kv"]), (lay["KPE"], ROPE, lay["o_pe"])], key=lambda s: s[2])

    def runs_of_tile(p, jt):
        lo, hi = p * shard + jt * 128, p * shard + min((jt + 1) * 128, shard)
        out = []
        for start, width, orig in sections:
            a, b = max(lo, orig), min(hi, orig + width)
            if a < b:
                src = start + a - orig
                first = (src // 128) * 128
                win = 256 if first + 256 <= lay["NP"] else 128
                out.append((first, win, src - first, a - lo, b - a))
        return out

    def body(w_ref, o_ref):
        movers = {}

        def mover(win, off, lane, take):
            key = (win, off, lane, take)
            if key not in movers:
                row = lax.broadcasted_iota(jnp.int32, (win, 128), 0)
                col = lax.broadcasted_iota(jnp.int32, (win, 128), 1)
                hit = (row - off == col - lane) & (col >= lane) & (col < lane + take)
                movers[key] = jnp.where(hit, 1.0, 0.0).astype(BF16)
            return movers[key]

        for p in range(N_DEV):
            for jt in range(padw // 128):
                tile = None
                for first, win, off, lane, take in runs_of_tile(p, jt):
                    part = jnp.dot(w_ref[:, first:first + win], mover(win, off, lane, take),
                                   preferred_element_type=F32)
                    tile = part if tile is None else tile + part
                if tile is None:
                    tile = jnp.zeros((tr, 128), F32)
                o_ref[p, :, jt * 128:(jt + 1) * 128] = tile.astype(BF16)

    return _pcall(
        body, name=name, grid=(d // tr,),
        in_specs=[pl.BlockSpec((tr, lay["NP"]), lambda i: (i, 0))],
        out_specs=pl.BlockSpec((N_DEV, tr, padw), lambda i: (0, i, 0)),
        out_shape=jax.ShapeDtypeStruct((N_DEV, d, padw), BF16),
        compiler_params=_params(("parallel",)),
    )(dwp)


def _ungather_cols(g):
    return jnp.transpose(g, (1, 0, 2)).reshape(g.shape[1], -1)


def _to_col_blocks(w):
    r, c = w.shape
    return jnp.transpose(w.reshape(r, N_DEV, c // N_DEV), (1, 0, 2))


def _pad_heads(w, h):
    r = w.shape[0]
    return jnp.pad(w.reshape(r, h, QK_DIM), ((0, 0), (0, 0), (0, HEAD_PAD - QK_DIM))).reshape(r, h * HEAD_PAD)


def _unpad_heads(w, h):
    r = w.shape[0]
    return w.reshape(r, h, HEAD_PAD)[:, :, :QK_DIM].reshape(r, h * QK_DIM)


def _rope_tabs(positions):
    half = ROPE // 2
    inv_freq = ROPE_THETA ** (-jnp.arange(half, dtype=F32) / half)
    ang = positions.astype(F32).reshape(-1)[:, None] * inv_freq
    cos, sin = jnp.cos(ang), jnp.sin(ang)
    zero = jnp.zeros_like(cos)
    return (jnp.concatenate([cos, cos, zero, zero], axis=1),
            jnp.concatenate([-sin, zero, zero, zero], axis=1),
            jnp.concatenate([zero, sin, zero, zero], axis=1))


def _pack_rows(vecs):
    rows = []
    for v in vecs:
        flat = v.reshape(-1)
        pad = (-flat.shape[0]) % 1024
        rows.append(jnp.pad(flat, (0, pad)).reshape(-1, 128))
    return jnp.concatenate(rows, axis=0)


def _unpack_rows(pack, shapes):
    out, r0 = [], 0
    for shp in shapes:
        size = math.prod(shp)
        nrows = -(-size // 1024) * 8
        out.append(pack[r0:r0 + nrows].reshape(-1)[:size].reshape(shp))
        r0 += nrows
    return out


def kernel(x, positions, ln_g, w_in, q_a_norm, w_q_up, kv_a_norm, w_kv_up, q_norm, k_norm, w_dw, b_dw, conv_ln_g, conv_ln_b, w_out, loss_target, m_ln_g, m_w_in, m_q_a_norm, m_w_q_up, m_kv_a_norm, m_w_kv_up, m_q_norm, m_k_norm, m_w_dw, m_b_dw, m_conv_ln_g, m_conv_ln_b, m_w_out, v_ln_g, v_w_in, v_q_a_norm, v_w_q_up, v_kv_a_norm, v_w_kv_up, v_q_norm, v_k_norm, v_w_dw, v_b_dw, v_conv_ln_g, v_conv_ln_b, v_w_out):
    nb, seq, d = x.shape
    depth = ln_g.shape[0]
    lay = _layout(d, q_a_norm.shape[1], kv_a_norm.shape[1])
    h, cw, ql, kvl = lay["H"], lay["CW"], lay["QL"], lay["KVL"]
    t = nb * seq
    my_id = _block_id(_position())

    shard_in = w_in.shape[2]

    def shards(l):
        padded = jnp.pad(w_in[l].astype(BF16), ((0, 0), (0, _lane_pad(shard_in) - shard_in)))
        return [padded, w_q_up[l].astype(BF16), w_kv_up[l].astype(BF16), w_out[l].astype(BF16)]

    def fill_own(lands, own):
        return [lax.dynamic_update_index_in_dim(land, blk, my_id, 0) for land, blk in zip(lands, own)]

    def layout_in(g_in, l):
        return {"in": _assemble_w_in(g_in, shard_in, lay, name=f"assemble_w_in_{l}")}

    def layout_rest(g_q, g_kv, g_out):
        return {"q": _pad_heads(_ungather_cols(g_q), h), "kv": _ungather_cols(g_kv), "out": g_out.reshape(2 * cw, d)}

    first = shards(0)
    g_in0, g_dw = _all_gather([first[0], w_dw], name="gather_w_in_0")
    gathers = {0: _push_start(first[1:], False, after=g_in0, name="gather_start_0")}
    for l in range(1, depth):
        gathers[l] = _push_start(shards(l), False, after=gathers[l - 1][4], masks=NEAR_PEERS,
                                 name=f"gather_start_{l}")
    fwd_dep = gathers[depth - 1][4]
    weights = []

    tabs = _rope_tabs(positions)
    gq_pad = jnp.pad(q_norm, ((0, 0), (0, HEAD_PAD - QK_DIM)))
    gk_pad = jnp.pad(k_norm, ((0, 0), (0, HEAD_PAD - QK_DIM)))
    w_dw_all = jnp.transpose(g_dw, (1, 2, 0, 3)).reshape(depth, CONV_K, cw)
    w_dw_all = jnp.pad(w_dw_all, ((0, 0), (0, HALO - CONV_K), (0, 0)))

    saved = []
    xs = x.reshape(t, d)
    for l in range(depth):
        hid = _rms_fwd(xs, ln_g[l], dep=fwd_dep if l == 0 else None, name=f"rms_fwd_{l}")
        if l == 0:
            weights.append(layout_in(g_in0, 0))
        z = _mm(hid, weights[l]["in"], out_dtype=BF16, name=f"in_proj_{l}")
        if l == 0:
            own, lands = _push_wait(gathers[0], z, False, name="gather_wait_0")
            weights[0].update(layout_rest(*fill_own(lands, own)))
        wl = weights[l]
        qn, kvn = _lat_fwd(z, q_a_norm[l], kv_a_norm[l], lay, name=f"lat_fwd_{l}")
        q_raw = _mm(qn, wl["q"], out_dtype=BF16, name=f"q_up_{l}")
        kv_raw = _mm(kvn, wl["kv"], out_dtype=BF16, name=f"kv_up_{l}")
        qh, kh, vh = _heads_fwd(q_raw, kv_raw, z, tabs, gq_pad[l:l + 1], gk_pad[l:l + 1], lay, name=f"heads_fwd_{l}")
        att, mix, lse = _flash_fwd(qh, kh, vh, z, lay, nb, name=f"flash_fwd_{l}")
        relay = None
        if l + 1 < depth:
            own, lands = _push_wait(gathers[l + 1], att, False, masks=NEAR_PEERS, name=f"gather_wait_{l + 1}")
            relay = _relay_start(lands, after=att, name=f"relay_start_{l + 1}")
        mix, u, c_pre = _conv_fwd(z, mix, w_dw_all[l], b_dw[l], conv_ln_g[l], conv_ln_b[l], lay, nb,
                                  name=f"conv_fwd_{l}")
        x_next = _mm(mix, wl["out"], add=xs, dep=None if relay is None else relay[3], name=f"out_proj_{l}")
        saved.append((xs, hid, z, qn, kvn, q_raw, kv_raw, qh, kh, vh, att, lse, mix, u, c_pre))
        xs = x_next
        if relay is not None:
            g_in, *g_rest = fill_own(_relay_wait(relay, xs, name=f"relay_wait_{l + 1}"), own)
            weights.append({**layout_in(g_in, l + 1), **layout_rest(*g_rest)})

    sse, dx, dxb = _loss_head(xs, loss_target.reshape(t, d), name="loss_head")
    loss = lax.psum(sse[0, 0] * (0.5 / d), ("x", "y", "c"))

    small = {k: [] for k in ("ln_g", "q_a", "kv_a", "q_n", "k_n", "w_dw", "b_dw", "cln_g", "cln_b")}
    scatters, bwd_dep = {}, None
    for l in reversed(range(depth)):
        xs, hid, z, qn, kvn, q_raw, kv_raw, qh, kh, vh, att, lse, mix, u, c_pre = saved[l]
        wl = weights[l]
        dmix = _mm(dxb, wl["out"], trans_b=True, dep=bwd_dep, name=f"d_mix_{l}")
        dw_out = _mm(mix, dxb, trans_a=True, out_dtype=BF16, name=f"dw_out_{l}")
        dc, dz, dlg, dlb, dbias = _conv_bwd_ln(c_pre, z, dmix, conv_ln_g[l], conv_ln_b[l], lay,
                                               name=f"conv_bwd_ln_{l}")
        dz, dwdw = _conv_bwd_dw(dc, u, z, w_dw_all[l], dz, lay, nb, name=f"conv_bwd_dw_{l}")
        do, delta, dz = _gate_bwd(dmix, att, z, dz, lay, name=f"gate_bwd_{l}")
        dqh, dkh, dvh = _flash_bwd(qh, kh, vh, do, lse, delta, nb, name=f"flash_bwd_{l}")
        dq_raw, dkv_raw, dpe, dgq, dgk = _heads_bwd(q_raw, kv_raw, z, tabs, gq_pad[l:l + 1], gk_pad[l:l + 1],
                                                    dqh, dkh, dvh, lay, name=f"heads_bwd_{l}")
        dqn = _mm(dq_raw, wl["q"], trans_b=True, name=f"d_qn_{l}")
        dkvn = _mm(dkv_raw, wl["kv"], trans_b=True, name=f"d_kvn_{l}")
        dw_q = _mm(qn, dq_raw, trans_a=True, out_dtype=BF16, name=f"dw_q_{l}")
        dw_kv = _mm(kvn, dkv_raw, trans_a=True, out_dtype=BF16, name=f"dw_kv_{l}")
        early = _push_start(
            [_to_col_blocks(_unpad_heads(dw_q, h)), _to_col_blocks(dw_kv), dw_out.reshape(N_DEV, (2 * cw) // N_DEV, d)],
            True, name=f"scatter_start_a_{l}")
        dz, dgqa, dgkva = _lat_bwd(z, q_a_norm[l], kv_a_norm[l], dqn, dkvn, dpe, dz, lay, name=f"lat_bwd_{l}")
        dw_in = _mm(hid, dz, trans_a=True, out_dtype=BF16, dep=early[4], name=f"dw_in_{l}")
        late = _push_start([_split_dw_in(dw_in, shard_in, lay, name=f"split_dw_in_{l}")], True,
                           name=f"scatter_start_b_{l}")
        scatters[l] = (late, early)
        bwd_dep = late[4]
        dh = _mm(dz, wl["in"], trans_b=True, dep=bwd_dep, name=f"d_hid_{l}")
        dx, dxb, dlng = _rms_bwd(xs, ln_g[l], dh, dx, name=f"rms_bwd_{l}")
        for key, val in (("ln_g", dlng), ("q_a", dgqa), ("kv_a", dgkva), ("q_n", dgq[:, :QK_DIM]),
                         ("k_n", dgk[:, :QK_DIM]), ("w_dw", dwdw[:CONV_K]), ("b_dw", dbias),
                         ("cln_g", dlg), ("cln_b", dlb)):
            small[key].append(val)
    grad_x = dx.reshape(nb, seq, d)
    for key in small:
        small[key] = jnp.stack(small[key][::-1])

    small_names = ("ln_g", "q_a", "kv_a", "q_n", "k_n", "b_dw", "cln_g", "cln_b", "w_dw")
    small_shapes = [small[k].shape for k in small_names]
    summed = _unpack_rows(_all_reduce_small(_pack_rows([small[k] for k in small_names]), name="reduce_small_grads"),
                          small_shapes)
    sg = dict(zip(small_names, summed))
    g_w_dw = lax.dynamic_slice_in_dim(sg["w_dw"], my_id * (cw // N_DEV), cw // N_DEV, axis=2)

    def adam_small(ws, ms, vs, gs, nm):
        shapes = [w.shape for w in ws]
        outs = _adam(_pack_rows(ws), _pack_rows(ms), _pack_rows(vs), [(_pack_rows(gs), None)], name=nm)
        return [_unpack_rows(o, shapes) for o in outs]

    big = [("w_in", w_in, m_w_in, v_w_in), ("w_q_up", w_q_up, m_w_q_up, v_w_q_up),
           ("w_kv_up", w_kv_up, m_w_kv_up, v_w_kv_up), ("w_out", w_out, m_w_out, v_w_out)]
    res, prev = {}, [None] * len(big)
    for l in reversed(range(depth)):
        own, lands = [], []
        for half, tag in zip(scatters[l], "ba"):
            sent, landed = _push_wait(half, dx, True, name=f"scatter_wait_{tag}_{l}")
            own += [lax.dynamic_index_in_dim(o, my_id, 0, keepdims=False) for o in sent]
            lands += landed
        for idx, ((nm, w, m, v), recv) in enumerate(zip(big, fill_own(lands, own))):
            rows, cols = w.shape[1], w.shape[2]
            flat = lambda a: a.reshape(depth * rows, cols)
            prev[idx] = _adam(flat(w), flat(m), flat(v), [(recv, s) for s in range(N_DEV)], layer=l, layers=depth,
                              prev=prev[idx], name=f"adam_{nm}_{l}")
    for idx, (nm, w, _, _) in enumerate(big):
        res[nm] = [o.reshape(w.shape) for o in prev[idx]]
    names_s = ["ln_g", "q_a_norm", "kv_a_norm", "q_norm", "k_norm", "w_dw", "b_dw", "conv_ln_g", "conv_ln_b"]
    ws = [ln_g, q_a_norm, kv_a_norm, q_norm, k_norm, w_dw, b_dw, conv_ln_g, conv_ln_b]
    ms = [m_ln_g, m_q_a_norm, m_kv_a_norm, m_q_norm, m_k_norm, m_w_dw, m_b_dw, m_conv_ln_g, m_conv_ln_b]
    vs = [v_ln_g, v_q_a_norm, v_kv_a_norm, v_q_norm, v_k_norm, v_w_dw, v_b_dw, v_conv_ln_g, v_conv_ln_b]
    gs = [sg["ln_g"].reshape(ln_g.shape), sg["q_a"].reshape(q_a_norm.shape), sg["kv_a"].reshape(kv_a_norm.shape),
          sg["q_n"].reshape(q_norm.shape), sg["k_n"].reshape(k_norm.shape), g_w_dw,
          sg["b_dw"].reshape(b_dw.shape), sg["cln_g"].reshape(conv_ln_g.shape), sg["cln_b"].reshape(conv_ln_b.shape)]
    outs_s = adam_small(ws, ms, vs, gs, "adam_small")
    for idx, nm in enumerate(names_s):
        res[nm] = [outs_s[k][idx] for k in range(4)]

    order = ["ln_g", "w_in", "q_a_norm", "w_q_up", "kv_a_norm", "w_kv_up", "q_norm", "k_norm", "w_dw", "b_dw",
             "conv_ln_g", "conv_ln_b", "w_out"]
    return (loss, grad_x, *[res[nm][0] for nm in order], *[res[nm][1] for nm in order],
            *[res[nm][2] for nm in order], *[res[nm][3] for nm in order])
```

```python
import functools
import math

import jax
import jax.numpy as jnp
from jax import lax
from jax.experimental import pallas as pl
from jax.experimental.pallas import tpu as pltpu

F32 = jnp.float32
BF16 = jnp.bfloat16
MESH = pl.DeviceIdType.MESH

N_DEV = 8
NOPE = 128
ROPE = 64
VDIM = 128
HEAD_PAD = 256
QK_DIM = NOPE + ROPE
CONV_K = 31
HALO = 32
EPS = 1e-6
ROPE_THETA = 10000.0
NEG = -1e30

ADAM_LR = 0.001
ADAM_B1 = 0.9
ADAM_B2 = 0.999
ADAM_EPS = 1e-08
ADAM_WD = 0.01
ADAM_STEP = 10

TOK_TILE = 256
CONV_TILE = 256
ATT_TQ = 512
HEAD_GROUP = 4
MM_TM = 1024
MM_TN = 1024
MM_TK = 4096
MM_VMEM_BUDGET = 46 * 1024 * 1024
ADAM_BLOCK_ELEMS = 128 * 1024
LANE_CHUNK = 256
VMEM_LIMIT = 56 * 1024 * 1024


def _pcall(body, **kw):
    return pl.pallas_call(body, **kw)


def _tile(dim, pref, mult):
    t = min(pref, dim)
    t -= t % mult
    while t >= mult:
        if dim % t == 0:
            return t
        t -= mult
    return dim


def _params(sem):
    return pltpu.CompilerParams(dimension_semantics=sem, vmem_limit_bytes=VMEM_LIMIT)


def _sigmoid(v):
    return 1.0 / (1.0 + jnp.exp(-v))


def _dsilu(v, sg):
    return sg * (1.0 + v * (1.0 - sg))


def _mm_tiles(m, n, kdim, out_bytes, has_add):
    def need(tm, tn, tk):
        return (2 * 2 * (tm * tk + tk * tn) + 2 * tm * tn * out_bytes
                + tm * tn * 4 * ((kdim > tk) + 2 * has_add + 1))

    shapes = [(_tile(m, pm, 128), _tile(n, pn, 128))
              for pm, pn in ((MM_TM, MM_TN), (MM_TM // 2, MM_TN), (MM_TM // 2, MM_TN // 2), (MM_TM // 4, MM_TN // 2))]
    for tm, tn in shapes:
        if need(tm, tn, kdim) <= MM_VMEM_BUDGET:
            return tm, tn, kdim
    tk = _tile(kdim, MM_TK, 128)
    for tm, tn in shapes:
        if need(tm, tn, tk) <= MM_VMEM_BUDGET:
            break
    return tm, tn, tk


def _mm(a, b, *, name, trans_a=False, trans_b=False, add=None, out_dtype=F32, dep=None):
    assert not (trans_a and trans_b)
    if trans_a:
        kdim, m = a.shape
    else:
        m, kdim = a.shape
    n = b.shape[0] if trans_b else b.shape[1]
    assert b.shape[1 if trans_b else 0] == kdim
    has_add = add is not None
    tm, tn, tk = _mm_tiles(m, n, kdim, jnp.dtype(out_dtype).itemsize, has_add)
    nk = kdim // tk
    contract = (((0 if trans_a else 1,), (1 if trans_b else 0,)), ((), ()))

    def product(a_ref, b_ref):
        return lax.dot_general(a_ref[...], b_ref[...], contract, preferred_element_type=F32)

    def body(*refs):
        a_ref, b_ref = refs[:2]
        add_ref = refs[2] if has_add else None
        o_ref = refs[2 + has_add + (dep is not None)]

        def finish(r):
            if has_add:
                r = r + add_ref[...]
            o_ref[...] = r.astype(o_ref.dtype)

        if nk == 1:
            finish(product(a_ref, b_ref))
            return
        acc_ref = refs[-1]
        k = pl.program_id(2)

        @pl.when(k == 0)
        def _():
            acc_ref[...] = product(a_ref, b_ref)

        @pl.when((k > 0) & (k < nk - 1))
        def _():
            acc_ref[...] += product(a_ref, b_ref)

        @pl.when(k == nk - 1)
        def _():
            finish(acc_ref[...] + product(a_ref, b_ref))

    if trans_a:
        a_spec = pl.BlockSpec((tk, tm), lambda i, j, k: (k, i))
    else:
        a_spec = pl.BlockSpec((tm, tk), lambda i, j, k: (i, k))
    if trans_b:
        b_spec = pl.BlockSpec((tn, tk), lambda i, j, k: (j, k))
    else:
        b_spec = pl.BlockSpec((tk, tn), lambda i, j, k: (k, j))
    in_specs = [a_spec, b_spec]
    args = [a, b]
    if has_add:
        in_specs.append(pl.BlockSpec((tm, tn), lambda i, j, k: (i, j)))
        args.append(add)
    if dep is not None:
        in_specs.append(pl.BlockSpec(memory_space=pl.ANY))
        args.append(dep)
    return _pcall(
        body, name=name,
        grid=(m // tm, n // tn, nk),
        in_specs=in_specs,
        out_specs=pl.BlockSpec((tm, tn), lambda i, j, k: (i, j)),
        out_shape=jax.ShapeDtypeStruct((m, n), out_dtype),
        scratch_shapes=[pltpu.VMEM((tm, tn), F32)] if nk > 1 else [],
        compiler_params=_params(("parallel", "parallel", "arbitrary")),
    )(*args)


def _rms_fwd(x, g, *, name, dep=None):
    t, d = x.shape
    tt = _tile(t, TOK_TILE, 16)

    def body(x_ref, g_ref, *rest):
        h_ref = rest[-1]
        xv = x_ref[...]
        r = lax.rsqrt(jnp.mean(xv * xv, axis=-1, keepdims=True) + EPS)
        h_ref[...] = (xv * r * g_ref[...]).astype(BF16)

    deps = [] if dep is None else [dep]
    return _pcall(
        body, name=name, grid=(t // tt,),
        in_specs=[pl.BlockSpec((tt, d), lambda i: (i, 0)), pl.BlockSpec((1, d), lambda i: (0, 0))]
        + [pl.BlockSpec(memory_space=pl.ANY)] * len(deps),
        out_specs=pl.BlockSpec((tt, d), lambda i: (i, 0)),
        out_shape=jax.ShapeDtypeStruct((t, d), BF16),
        compiler_params=_params(("parallel",)),
    )(x, g.reshape(1, d), *deps)


def _rms_bwd(x, g, dh, dres, *, name):
    t, d = x.shape
    tt = _tile(t, TOK_TILE, 16)

    def body(x_ref, g_ref, dh_ref, dres_ref, dx_ref, dxb_ref, dg_ref):
        xv = x_ref[...]
        r = lax.rsqrt(jnp.mean(xv * xv, axis=-1, keepdims=True) + EPS)
        dy = dh_ref[...]
        dyg = dy * g_ref[...]
        dot = jnp.sum(dyg * xv, axis=-1, keepdims=True) * (1.0 / d)
        dx = dres_ref[...] + r * dyg - xv * (r * r * r) * dot
        dx_ref[...] = dx
        dxb_ref[...] = dx.astype(BF16)

        @pl.when(pl.program_id(0) == 0)
        def _():
            dg_ref[...] = jnp.zeros_like(dg_ref)

        dg_ref[...] += jnp.sum(dy * xv * r, axis=0, keepdims=True)

    row = pl.BlockSpec((tt, d), lambda i: (i, 0))
    vec = pl.BlockSpec((1, d), lambda i: (0, 0))
    return _pcall(
        body, name=name, grid=(t // tt,),
        in_specs=[row, vec, row, row],
        out_specs=[row, row, vec],
        out_shape=[jax.ShapeDtypeStruct((t, d), F32), jax.ShapeDtypeStruct((t, d), BF16),
                   jax.ShapeDtypeStruct((1, d), F32)],
        compiler_params=_params(("arbitrary",)),
    )(x, g.reshape(1, d), dh, dres)


def _lat_fwd(z, gq, gkv, lay, *, name):
    t = z.shape[0]
    ql, kvl = lay["QL"], lay["KVL"]
    tt = _tile(t, TOK_TILE, 16)

    def body(q_ref, kv_ref, gq_ref, gkv_ref, qn_ref, kvn_ref):
        for src, g_ref, dst in ((q_ref, gq_ref, qn_ref), (kv_ref, gkv_ref, kvn_ref)):
            v = src[...].astype(F32)
            r = lax.rsqrt(jnp.mean(v * v, axis=-1, keepdims=True) + EPS)
            dst[...] = (v * r * g_ref[...]).astype(BF16)

    return _pcall(
        body, name=name, grid=(t // tt,),
        in_specs=[pl.BlockSpec((tt, ql), lambda i: (i, lay["QC"] // ql)),
                  pl.BlockSpec((tt, kvl), lambda i: (i, lay["KVC"] // kvl)),
                  pl.BlockSpec((1, ql), lambda i: (0, 0)), pl.BlockSpec((1, kvl), lambda i: (0, 0))],
        out_specs=[pl.BlockSpec((tt, ql), lambda i: (i, 0)), pl.BlockSpec((tt, kvl), lambda i: (i, 0))],
        out_shape=[jax.ShapeDtypeStruct((t, ql), BF16), jax.ShapeDtypeStruct((t, kvl), BF16)],
        compiler_params=_params(("parallel",)),
    )(z, z, gq.reshape(1, ql), gkv.reshape(1, kvl))


def _lat_bwd(z, gq, gkv, dqn, dkvn, dpe, dz, lay, *, name):
    t = z.shape[0]
    ql, kvl = lay["QL"], lay["KVL"]
    tail = lay["NP"] - lay["QC"]
    assert lay["QC"] % tail == 0
    tt = _tile(t, TOK_TILE, 16)

    def body(q_ref, kv_ref, gq_ref, gkv_ref, dqn_ref, dkvn_ref, dpe_ref, dz_in, tail_ref, dgq_ref, dgkv_ref):
        del dz_in
        first = pl.program_id(0) == 0
        for src, g_ref, dy_ref, c0, dg_ref in ((q_ref, gq_ref, dqn_ref, 0, dgq_ref),
                                               (kv_ref, gkv_ref, dkvn_ref, ql, dgkv_ref)):
            v = src[...].astype(F32)
            n = v.shape[-1]
            r = lax.rsqrt(jnp.mean(v * v, axis=-1, keepdims=True) + EPS)
            dy = dy_ref[...]
            dyg = dy * g_ref[...]
            dot = jnp.sum(dyg * v, axis=-1, keepdims=True) * (1.0 / n)
            tail_ref[:, c0:c0 + n] = (r * dyg - v * (r * r * r) * dot).astype(BF16)

            @pl.when(first)
            def _():
                dg_ref[...] = jnp.zeros_like(dg_ref)

            dg_ref[...] += jnp.sum(dy * v * r, axis=0, keepdims=True)
        tail_ref[:, ql + kvl:ql + kvl + 128] = dpe_ref[...].astype(BF16)
        tail_ref[:, ql + kvl + 128:tail] = jnp.zeros((tt, tail - ql - kvl - 128), BF16)

    return _pcall(
        body, name=name, grid=(t // tt,),
        in_specs=[pl.BlockSpec((tt, ql), lambda i: (i, lay["QC"] // ql)),
                  pl.BlockSpec((tt, kvl), lambda i: (i, lay["KVC"] // kvl)),
                  pl.BlockSpec((1, ql), lambda i: (0, 0)), pl.BlockSpec((1, kvl), lambda i: (0, 0)),
                  pl.BlockSpec((tt, ql), lambda i: (i, 0)), pl.BlockSpec((tt, kvl), lambda i: (i, 0)),
                  pl.BlockSpec((tt, 128), lambda i: (i, 0)), pl.BlockSpec(memory_space=pl.ANY)],
        out_specs=[pl.BlockSpec((tt, tail), lambda i: (i, lay["QC"] // tail)),
                   pl.BlockSpec((1, ql), lambda i: (0, 0)), pl.BlockSpec((1, kvl), lambda i: (0, 0))],
        out_shape=[jax.ShapeDtypeStruct(dz.shape, BF16),
                   jax.ShapeDtypeStruct((1, ql), F32), jax.ShapeDtypeStruct((1, kvl), F32)],
        input_output_aliases={7: 0},
        compiler_params=_params(("arbitrary",)),
    )(z, z, gq.reshape(1, ql), gkv.reshape(1, kvl), dqn, dkvn, dpe, dz)


def _rope(r, c_tab, sa_tab, sb_tab):
    return r * c_tab + pltpu.roll(r, 96, 1) * sa_tab + pltpu.roll(r, 32, 1) * sb_tab


def _rope_t(d, c_tab, sa_tab, sb_tab):
    return d * c_tab + pltpu.roll(d * sa_tab, 32, 1) + pltpu.roll(d * sb_tab, 96, 1)


def _heads_fwd(q_raw, kv_raw, z, tabs, gq, gk, lay, *, name):
    t = z.shape[0]
    h = lay["H"]
    tt = _tile(t, TOK_TILE, 16)
    hg = _tile(h, HEAD_GROUP, 1)
    scale = 1.0 / math.sqrt(QK_DIM)

    def body(q_ref, kv_ref, pe_ref, c_ref, sa_ref, sb_ref, gq_ref, gk_ref, qh_ref, kh_ref, vh_ref):
        c_tab, sa_tab, sb_tab = c_ref[...], sa_ref[...], sb_ref[...]
        pe, gq_v, gk_v = pe_ref[...].astype(F32), gq_ref[...], gk_ref[...]
        ss_pe = jnp.sum(pe * pe, axis=-1, keepdims=True)
        for g in range(hg):
            q = q_ref[:, g * HEAD_PAD:(g + 1) * HEAD_PAD].astype(F32)
            r = lax.rsqrt(jnp.sum(q * q, axis=-1, keepdims=True) * (1.0 / QK_DIM) + EPS)
            qn = q * r * gq_v
            qh_ref[g] = (jnp.concatenate([qn[:, :NOPE], _rope(qn[:, NOPE:], c_tab, sa_tab, sb_tab)], axis=1)
                         * scale).astype(BF16)
            kv = kv_ref[:, g * HEAD_PAD:(g + 1) * HEAD_PAD].astype(F32)
            kn = kv[:, :NOPE]
            rk = lax.rsqrt((jnp.sum(kn * kn, axis=-1, keepdims=True) + ss_pe) * (1.0 / QK_DIM) + EPS)
            kh_ref[g] = jnp.concatenate(
                [kn * rk * gk_v[:, :NOPE], _rope(pe * rk * gk_v[:, NOPE:], c_tab, sa_tab, sb_tab)],
                axis=1).astype(BF16)
            vh_ref[g] = kv[:, NOPE:].astype(BF16)

    head = pl.BlockSpec((tt, hg * HEAD_PAD), lambda i, j: (i, j))
    tab = pl.BlockSpec((tt, 128), lambda i, j: (i, 0))
    gain = pl.BlockSpec((1, HEAD_PAD), lambda i, j: (0, 0))
    return _pcall(
        body, name=name, grid=(t // tt, h // hg),
        in_specs=[head, head, pl.BlockSpec((tt, 128), lambda i, j: (i, lay["KPE"] // 128)), tab, tab, tab, gain, gain],
        out_specs=[pl.BlockSpec((hg, tt, HEAD_PAD), lambda i, j: (j, i, 0)),
                   pl.BlockSpec((hg, tt, HEAD_PAD), lambda i, j: (j, i, 0)),
                   pl.BlockSpec((hg, tt, VDIM), lambda i, j: (j, i, 0))],
        out_shape=[jax.ShapeDtypeStruct((h, t, HEAD_PAD), BF16), jax.ShapeDtypeStruct((h, t, HEAD_PAD), BF16),
                   jax.ShapeDtypeStruct((h, t, VDIM), BF16)],
        compiler_params=_params(("parallel", "parallel")),
    )(q_raw, kv_raw, z, *tabs, gq, gk)


def _heads_bwd(q_raw, kv_raw, z, tabs, gq, gk, dqh, dkh, dvh, lay, *, name):
    t = z.shape[0]
    h = lay["H"]
    tt = _tile(t, TOK_TILE, 16)
    hg = _tile(h, HEAD_GROUP, 1)
    scale = 1.0 / math.sqrt(QK_DIM)

    def body(q_ref, kv_ref, pe_ref, c_ref, sa_ref, sb_ref, gq_ref, gk_ref, dqh_ref, dkh_ref, dvh_ref,
             dq_ref, dkv_ref, dpe_ref, dgq_ref, dgk_ref):
        i, j = pl.program_id(0), pl.program_id(1)
        c_tab, sa_tab, sb_tab = c_ref[...], sa_ref[...], sb_ref[...]

        @pl.when((i == 0) & (j == 0))
        def _():
            dgq_ref[...] = jnp.zeros_like(dgq_ref)
            dgk_ref[...] = jnp.zeros_like(dgk_ref)

        @pl.when(j == 0)
        def _():
            dpe_ref[...] = jnp.zeros_like(dpe_ref)

        def norm_bwd(v, g, dy):
            r = lax.rsqrt(jnp.sum(v * v, axis=-1, keepdims=True) * (1.0 / QK_DIM) + EPS)
            dyg = dy * g
            dot = jnp.sum(dyg * v, axis=-1, keepdims=True) * (1.0 / QK_DIM)
            return r * dyg - v * (r * r * r) * dot, jnp.sum(dy * v * r, axis=0, keepdims=True)

        pe, gq_v, gk_v = pe_ref[...].astype(F32), gq_ref[...], gk_ref[...]
        dpe, dgq, dgk = jnp.zeros_like(pe), jnp.zeros_like(gq_v), jnp.zeros_like(gk_v)
        for g in range(hg):
            cols = slice(g * HEAD_PAD, (g + 1) * HEAD_PAD)
            dqo = dqh_ref[g].astype(F32) * scale
            dy = jnp.concatenate([dqo[:, :NOPE], _rope_t(dqo[:, NOPE:], c_tab, sa_tab, sb_tab)], axis=1)
            dq, dg = norm_bwd(q_ref[:, cols].astype(F32), gq_v, dy)
            dq_ref[:, cols] = dq.astype(BF16)
            dgq = dgq + dg

            dko = dkh_ref[g].astype(F32)
            dy = jnp.concatenate([dko[:, :NOPE], _rope_t(dko[:, NOPE:], c_tab, sa_tab, sb_tab)], axis=1)
            kfull = jnp.concatenate([kv_ref[:, cols].astype(F32)[:, :NOPE], pe], axis=1)
            dk, dg = norm_bwd(kfull, gk_v, dy)
            dkv_ref[:, cols] = jnp.concatenate([dk[:, :NOPE].astype(BF16), dvh_ref[g]], axis=1)
            dpe = dpe + dk[:, NOPE:]
            dgk = dgk + dg
        dpe_ref[...] += dpe
        dgq_ref[...] += dgq
        dgk_ref[...] += dgk

    head = pl.BlockSpec((tt, hg * HEAD_PAD), lambda i, j: (i, j))
    tab = pl.BlockSpec((tt, 128), lambda i, j: (i, 0))
    gain = pl.BlockSpec((1, HEAD_PAD), lambda i, j: (0, 0))
    hm = pl.BlockSpec((hg, tt, HEAD_PAD), lambda i, j: (j, i, 0))
    return _pcall(
        body, name=name, grid=(t // tt, h // hg),
        in_specs=[head, head, pl.BlockSpec((tt, 128), lambda i, j: (i, lay["KPE"] // 128)), tab, tab, tab, gain, gain,
                  hm, hm, pl.BlockSpec((hg, tt, VDIM), lambda i, j: (j, i, 0))],
        out_specs=[head, head, tab, gain, gain],
        out_shape=[jax.ShapeDtypeStruct((t, h * HEAD_PAD), BF16), jax.ShapeDtypeStruct((t, h * HEAD_PAD), BF16),
                   jax.ShapeDtypeStruct((t, 128), F32),
                   jax.ShapeDtypeStruct((1, HEAD_PAD), F32), jax.ShapeDtypeStruct((1, HEAD_PAD), F32)],
        compiler_params=_params(("arbitrary", "arbitrary")),
    )(q_raw, kv_raw, z, *tabs, gq, gk, dqh, dkh, dvh)


def _lower_triangle(n):
    return lax.broadcasted_iota(jnp.int32, (n, n), 1) <= lax.broadcasted_iota(jnp.int32, (n, n), 0)


def _qk(q, k):
    return lax.dot_general(q, k, (((1,), (1,)), ((), ())), preferred_element_type=F32)


def _flash_fwd(qh, kh, vh, z, lay, nb, *, name):
    h, t, _ = qh.shape
    s = t // nb
    tq = _tile(s, ATT_TQ, 128)
    nq = s // tq
    att_w = h * VDIM
    gblk = lay["GATT"] // VDIM

    def body(q_ref, k_ref, v_ref, g_ref, att_ref, mix_ref, lse_ref):
        i = pl.program_id(2)
        tri = _lower_triangle(tq)
        for blk in range(nq):
            @pl.when(i == blk)
            def _():
                q = q_ref[...]
                pre = blk * tq
                sd = jnp.where(tri, _qk(q, k_ref[pre:pre + tq, :]), NEG)
                m = jnp.max(sd, axis=-1, keepdims=True)
                if pre:
                    sp = _qk(q, k_ref[0:pre, :])
                    m = jnp.maximum(m, jnp.max(sp, axis=-1, keepdims=True))
                pd = jnp.exp(sd - m)
                l = jnp.sum(pd, axis=-1, keepdims=True)
                acc = jnp.dot(pd.astype(BF16), v_ref[pre:pre + tq, :], preferred_element_type=F32)
                if pre:
                    pp = jnp.exp(sp - m)
                    l = l + jnp.sum(pp, axis=-1, keepdims=True)
                    acc = acc + jnp.dot(pp.astype(BF16), v_ref[0:pre, :], preferred_element_type=F32)
                o = acc / l
                att_ref[...] = o
                g = g_ref[...].astype(F32)
                mix_ref[...] = (o * (g * _sigmoid(g))).astype(BF16)
                lse_ref[...] = m + jnp.log(l)

    row = lambda hh, b, i: (b * nq + i, hh)
    seq = lambda hh, b, i: (hh, b, 0)
    return _pcall(
        body, name=name, grid=(h, nb, nq),
        in_specs=[pl.BlockSpec((None, tq, HEAD_PAD), lambda hh, b, i: (hh, b * nq + i, 0)),
                  pl.BlockSpec((None, s, HEAD_PAD), seq),
                  pl.BlockSpec((None, s, VDIM), seq),
                  pl.BlockSpec((tq, VDIM), lambda hh, b, i: (b * nq + i, gblk + hh))],
        out_specs=[pl.BlockSpec((tq, VDIM), row), pl.BlockSpec((tq, VDIM), row),
                   pl.BlockSpec((None, tq, 1), lambda hh, b, i: (hh, b * nq + i, 0))],
        out_shape=[jax.ShapeDtypeStruct((t, att_w), F32), jax.ShapeDtypeStruct((t, 2 * att_w), BF16),
                   jax.ShapeDtypeStruct((h, t, 1), F32)],
        compiler_params=_params(("parallel", "parallel", "parallel")),
    )(qh, kh, vh, z)


def _gate_bwd(dmix, att, z, dz, lay, *, name):
    t, att_w = att.shape
    h = att_w // VDIM
    tt = _tile(t, TOK_TILE, 16)
    gblk = lay["GATT"] // att_w

    def body(dm_ref, o_ref, g_ref, dz_in, do_ref, delta_ref, dg_ref):
        del dz_in
        dm, o, g = dm_ref[...], o_ref[...], g_ref[...].astype(F32)
        sg = _sigmoid(g)
        do = dm * (g * sg)
        do_ref[...] = do.astype(BF16)
        prod = do * o
        for hh in range(h):
            delta_ref[hh] = jnp.sum(prod[:, hh * VDIM:(hh + 1) * VDIM], axis=-1, keepdims=True)
        dg_ref[...] = (dm * o * _dsilu(g, sg)).astype(BF16)

    blk = pl.BlockSpec((tt, att_w), lambda i: (i, 0))
    gate = pl.BlockSpec((tt, att_w), lambda i: (i, gblk))
    return _pcall(
        body, name=name, grid=(t // tt,),
        in_specs=[blk, blk, gate, pl.BlockSpec(memory_space=pl.ANY)],
        out_specs=[blk, pl.BlockSpec((h, tt, 1), lambda i: (0, i, 0)), gate],
        out_shape=[jax.ShapeDtypeStruct((t, att_w), BF16), jax.ShapeDtypeStruct((h, t, 1), F32),
                   jax.ShapeDtypeStruct(dz.shape, BF16)],
        input_output_aliases={3: 2},
        compiler_params=_params(("parallel",)),
    )(dmix, att, z, dz)


def _flash_bwd(qh, kh, vh, do, lse, delta, nb, *, name):
    h, t, _ = qh.shape
    s = t // nb
    tk = _tile(s, ATT_TQ, 128)
    nk = s // tk
    tn_dims = (((0,), (0,)), ((), ()))

    def body(q_ref, k_ref, v_ref, do_ref, lse_ref, dl_ref, dq_out, dk_ref, dv_ref, dq_ref):
        j = pl.program_id(2)
        tri = _lower_triangle(tk)

        @pl.when(j == 0)
        def _():
            dq_ref[...] = jnp.zeros_like(dq_ref)

        def rows_against_block(r0, r1, masked):
            q, do_v = q_ref[r0:r1, :], do_ref[r0:r1, :]
            k = k_ref[...]
            sc = _qk(q, k)
            if masked:
                sc = jnp.where(tri, sc, NEG)
            p = jnp.exp(sc - lse_ref[r0:r1, :])
            dv = lax.dot_general(p.astype(BF16), do_v, tn_dims, preferred_element_type=F32)
            ds = (p * (_qk(do_v, v_ref[...]) - dl_ref[r0:r1, :])).astype(BF16)
            dq_ref[r0:r1, :] += jnp.dot(ds, k, preferred_element_type=F32)
            return lax.dot_general(ds, q, tn_dims, preferred_element_type=F32), dv

        for blk in range(nk):
            @pl.when(j == blk)
            def _():
                r0 = blk * tk
                dk, dv = rows_against_block(r0, r0 + tk, True)
                if r0 + tk < s:
                    dk2, dv2 = rows_against_block(r0 + tk, s, False)
                    dk, dv = dk + dk2, dv + dv2
                dk_ref[...] = dk.astype(BF16)
                dv_ref[...] = dv.astype(BF16)

        @pl.when(j == nk - 1)
        def _():
            dq_out[...] = dq_ref[...].astype(BF16)

    seq = lambda hh, b, j: (hh, b, 0)
    kv = lambda hh, b, j: (hh, b * nk + j, 0)
    return _pcall(
        body, name=name, grid=(h, nb, nk),
        in_specs=[pl.BlockSpec((None, s, HEAD_PAD), seq),
                  pl.BlockSpec((None, tk, HEAD_PAD), kv),
                  pl.BlockSpec((None, tk, VDIM), kv),
                  pl.BlockSpec((s, VDIM), lambda hh, b, j: (b, hh)),
                  pl.BlockSpec((None, s, 1), seq), pl.BlockSpec((None, s, 1), seq)],
        out_specs=[pl.BlockSpec((None, s, HEAD_PAD), seq), pl.BlockSpec((None, tk, HEAD_PAD), kv),
                   pl.BlockSpec((None, tk, VDIM), kv)],
        out_shape=[jax.ShapeDtypeStruct((h, t, HEAD_PAD), BF16), jax.ShapeDtypeStruct((h, t, HEAD_PAD), BF16),
                   jax.ShapeDtypeStruct((h, t, VDIM), BF16)],
        scratch_shapes=[pltpu.VMEM((s, HEAD_PAD), F32)],
        compiler_params=_params(("parallel", "parallel", "arbitrary")),
    )(qh, kh, vh, do, lse, delta)


SUBLANES = 8


def _stage_row_shifts(ext, sh, c0, lc, rows):
    for r in range(1, SUBLANES):
        sh[r - 1, 0:rows, :] = ext[r:r + rows, c0:c0 + lc]


def _row_window(ext, sh, c0, lc, off, n):
    r = off % SUBLANES
    if r == 0:
        return ext[off:off + n, c0:c0 + lc]
    return sh[r - 1, off - r:off - r + n, :]


def _conv_fwd(z, mix, w_dw, b_dw, ln_g, ln_b, lay, nb, *, name):
    t = z.shape[0]
    cw = lay["CW"]
    s = t // nb
    tt = _tile(s, CONV_TILE, HALO)
    ns = s // tt
    hb = tt // HALO
    lc = _tile(cw, LANE_CHUNK, 128)

    def body(a_ref, b_ref, ap_ref, bp_ref, gc_ref, w_ref, bias_ref, lg_ref, lb_ref, mix_in, mix_ref, u_ref, c_ref,
             ext, sh):
        del mix_in
        i = pl.program_id(1)
        u = a_ref[...].astype(F32) * _sigmoid(b_ref[...].astype(F32))
        u_ref[...] = u
        ext[0:HALO, :] = jnp.where(i > 0, ap_ref[...].astype(F32) * _sigmoid(bp_ref[...].astype(F32)), 0.0)
        ext[HALO:HALO + tt, :] = u
        for c0 in range(0, cw, lc):
            _stage_row_shifts(ext, sh, c0, lc, tt + HALO - SUBLANES)
            acc = jnp.zeros((tt, lc), F32) + bias_ref[:, c0:c0 + lc]
            for k in range(CONV_K):
                off = HALO - (CONV_K - 1) + k
                acc = acc + w_ref[k:k + 1, c0:c0 + lc] * _row_window(ext, sh, c0, lc, off, tt)
            c_ref[:, c0:c0 + lc] = acc
        c = c_ref[...]
        mu = jnp.mean(c, axis=-1, keepdims=True)
        xc = c - mu
        var = jnp.mean(xc * xc, axis=-1, keepdims=True)
        y = xc * lax.rsqrt(var + EPS) * lg_ref[...] + lb_ref[...]
        g = gc_ref[...].astype(F32)
        mix_ref[...] = (y * _sigmoid(y) * (g * _sigmoid(g))).astype(BF16)

    cur = lambda col: pl.BlockSpec((tt, cw), lambda b, i: (b * ns + i, col))
    prev = lambda col: pl.BlockSpec((HALO, cw), lambda b, i: (jnp.maximum((b * ns + i) * hb - 1, 0), col))
    vec = pl.BlockSpec((1, cw), lambda b, i: (0, 0))
    out_row = pl.BlockSpec((tt, cw), lambda b, i: (b * ns + i, 0))
    return _pcall(
        body, name=name, grid=(nb, ns),
        in_specs=[cur(lay["A"] // cw), cur(lay["B"] // cw), prev(lay["A"] // cw), prev(lay["B"] // cw),
                  cur(lay["GCONV"] // cw), pl.BlockSpec((HALO, cw), lambda b, i: (0, 0)), vec, vec, vec,
                  pl.BlockSpec(memory_space=pl.ANY)],
        out_specs=[pl.BlockSpec((tt, cw), lambda b, i: (b * ns + i, 1)), out_row, out_row],
        out_shape=[jax.ShapeDtypeStruct(mix.shape, BF16), jax.ShapeDtypeStruct((t, cw), F32),
                   jax.ShapeDtypeStruct((t, cw), F32)],
        scratch_shapes=[pltpu.VMEM((tt + HALO, cw), F32),
                        pltpu.VMEM((SUBLANES - 1, tt + HALO - SUBLANES, lc), F32)],
        input_output_aliases={9: 0},
        compiler_params=_params(("parallel", "parallel")),
    )(z, z, z, z, z, w_dw, b_dw.reshape(1, cw), ln_g.reshape(1, cw), ln_b.reshape(1, cw), mix)


def _conv_bwd_ln(c_pre, z, dmix, ln_g, ln_b, lay, *, name):
    t, cw = c_pre.shape
    tt = _tile(t, TOK_TILE, 16)

    def body(c_ref, gc_ref, dm_ref, lg_ref, lb_ref, dc_ref, dgc_ref, dlg_ref, dlb_ref, dbias_ref):
        @pl.when(pl.program_id(0) == 0)
        def _():
            dlg_ref[...] = jnp.zeros_like(dlg_ref)
            dlb_ref[...] = jnp.zeros_like(dlb_ref)
            dbias_ref[...] = jnp.zeros_like(dbias_ref)

        c = c_ref[...]
        mu = jnp.mean(c, axis=-1, keepdims=True)
        xc = c - mu
        rstd = lax.rsqrt(jnp.mean(xc * xc, axis=-1, keepdims=True) + EPS)
        xhat = xc * rstd
        y = xhat * lg_ref[...] + lb_ref[...]
        sy = _sigmoid(y)
        g = gc_ref[...].astype(F32)
        sg = _sigmoid(g)
        dm = dm_ref[...].astype(F32)
        dgc_ref[...] = (dm * (y * sy) * _dsilu(g, sg)).astype(BF16)
        dy = dm * (g * sg) * _dsilu(y, sy)
        dlb_ref[...] += jnp.sum(dy, axis=0, keepdims=True)
        dlg_ref[...] += jnp.sum(dy * xhat, axis=0, keepdims=True)
        dxh = dy * lg_ref[...]
        dc = rstd * (dxh - jnp.mean(dxh, axis=-1, keepdims=True)
                     - xhat * jnp.mean(dxh * xhat, axis=-1, keepdims=True))
        dc_ref[...] = dc
        dbias_ref[...] += jnp.sum(dc, axis=0, keepdims=True)

    row = pl.BlockSpec((tt, cw), lambda i: (i, 0))
    vec = pl.BlockSpec((1, cw), lambda i: (0, 0))
    return _pcall(
        body, name=name, grid=(t // tt,),
        in_specs=[row, pl.BlockSpec((tt, cw), lambda i: (i, lay["GCONV"] // cw)),
                  pl.BlockSpec((tt, cw), lambda i: (i, 1)), vec, vec],
        out_specs=[row, pl.BlockSpec((tt, cw), lambda i: (i, lay["GCONV"] // cw)), vec, vec, vec],
        out_shape=[jax.ShapeDtypeStruct((t, cw), F32), jax.ShapeDtypeStruct((t, lay["NP"]), BF16),
                   jax.ShapeDtypeStruct((1, cw), F32), jax.ShapeDtypeStruct((1, cw), F32),
                   jax.ShapeDtypeStruct((1, cw), F32)],
        compiler_params=_params(("arbitrary",)),
    )(c_pre, z, dmix, ln_g.reshape(1, cw), ln_b.reshape(1, cw))


def _conv_bwd_dw(dc, u, z, w_dw, dz, lay, nb, *, name):
    t, cw = dc.shape
    s = t // nb
    tt = _tile(s, CONV_TILE, HALO)
    ns = s // tt
    hb = tt // HALO
    lc = _tile(cw, LANE_CHUNK, 128)

    def body(dc_ref, dcn_ref, u_ref, up_ref, a_ref, b_ref, w_ref, dz_in, dab_ref, dw_ref, ext_dc, ext_u, du_ref,
             sh_dc, sh_u):
        del dz_in
        b_i, i = pl.program_id(0), pl.program_id(1)

        @pl.when((b_i == 0) & (i == 0))
        def _():
            dw_ref[...] = jnp.zeros_like(dw_ref)

        dc_v = dc_ref[...]
        ext_dc[0:tt, :] = dc_v
        ext_dc[tt:tt + HALO, :] = jnp.where(i < ns - 1, dcn_ref[...], 0.0)
        ext_u[0:HALO, :] = jnp.where(i > 0, up_ref[...], 0.0)
        ext_u[HALO:HALO + tt, :] = u_ref[...]
        for c0 in range(0, cw, lc):
            _stage_row_shifts(ext_dc, sh_dc, c0, lc, tt + HALO - SUBLANES)
            _stage_row_shifts(ext_u, sh_u, c0, lc, tt + HALO - SUBLANES)
            acc = jnp.zeros((tt, lc), F32)
            dcc = dc_v[:, c0:c0 + lc]
            for k in range(CONV_K):
                acc = acc + w_ref[k:k + 1, c0:c0 + lc] * _row_window(ext_dc, sh_dc, c0, lc, CONV_K - 1 - k, tt)
                off = HALO - (CONV_K - 1) + k
                dw_ref[k:k + 1, c0:c0 + lc] += jnp.sum(dcc * _row_window(ext_u, sh_u, c0, lc, off, tt),
                                                       axis=0, keepdims=True)
            du_ref[:, c0:c0 + lc] = acc
        du = du_ref[...]
        sb = _sigmoid(b_ref[...].astype(F32))
        dab_ref[:, 0:cw] = (du * sb).astype(BF16)
        dab_ref[:, cw:2 * cw] = (du * a_ref[...].astype(F32) * sb * (1.0 - sb)).astype(BF16)

    last = nb * ns * hb - 1
    row = pl.BlockSpec((tt, cw), lambda b, i: (b * ns + i, 0))
    return _pcall(
        body, name=name, grid=(nb, ns),
        in_specs=[row, pl.BlockSpec((HALO, cw), lambda b, i: (jnp.minimum((b * ns + i + 1) * hb, last), 0)),
                  row, pl.BlockSpec((HALO, cw), lambda b, i: (jnp.maximum((b * ns + i) * hb - 1, 0), 0)),
                  pl.BlockSpec((tt, cw), lambda b, i: (b * ns + i, lay["A"] // cw)),
                  pl.BlockSpec((tt, cw), lambda b, i: (b * ns + i, lay["B"] // cw)),
                  pl.BlockSpec((HALO, cw), lambda b, i: (0, 0)), pl.BlockSpec(memory_space=pl.ANY)],
        out_specs=[pl.BlockSpec((tt, 2 * cw), lambda b, i: (b * ns + i, 0)),
                   pl.BlockSpec((HALO, cw), lambda b, i: (0, 0))],
        out_shape=[jax.ShapeDtypeStruct(dz.shape, BF16), jax.ShapeDtypeStruct((HALO, cw), F32)],
        scratch_shapes=[pltpu.VMEM((tt + HALO, cw), F32), pltpu.VMEM((tt + HALO, cw), F32),
                        pltpu.VMEM((tt, cw), F32),
                        pltpu.VMEM((SUBLANES - 1, tt + HALO - SUBLANES, lc), F32),
                        pltpu.VMEM((SUBLANES - 1, tt + HALO - SUBLANES, lc), F32)],
        input_output_aliases={7: 0},
        compiler_params=_params(("arbitrary", "arbitrary")),
    )(dc, dc, u, u, z, z, w_dw, dz)


def _loss_head(y, target, *, name):
    t, d = y.shape
    tt = _tile(t, TOK_TILE, 16)

    def body(y_ref, t_ref, sse_ref, dy_ref, dyb_ref):
        @pl.when(pl.program_id(0) == 0)
        def _():
            sse_ref[...] = jnp.zeros_like(sse_ref)

        e = y_ref[...] - t_ref[...]
        sse_ref[...] += jnp.sum(e * e)
        dy = e * (1.0 / d)
        dy_ref[...] = dy
        dyb_ref[...] = dy.astype(BF16)

    row = pl.BlockSpec((tt, d), lambda i: (i, 0))
    return _pcall(
        body, name=name, grid=(t // tt,),
        in_specs=[row, row],
        out_specs=[pl.BlockSpec((8, 128), lambda i: (0, 0)), row, row],
        out_shape=[jax.ShapeDtypeStruct((8, 128), F32), jax.ShapeDtypeStruct((t, d), F32),
                   jax.ShapeDtypeStruct((t, d), BF16)],
        compiler_params=_params(("arbitrary",)),
    )(y, target)


def _adam(w, m, v, g_parts, *, name, layer=0, layers=1, prev=None):
    rows, cols = w.shape
    slab = rows // layers
    tr = _tile(slab, max(16, ADAM_BLOCK_ELEMS // cols), 16)
    blk0 = layer * (slab // tr)
    n = len(g_parts)
    n_prev = 0 if prev is None else 4

    def body(*refs):
        w_ref, m_ref, v_ref = refs[:3]
        g_refs = refs[3:3 + n]
        g_out, d_out, m_out, v_out = refs[3 + n + n_prev:]
        g = g_refs[0][...].astype(F32)
        for r in g_refs[1:]:
            g = g + r[...].astype(F32)
        g = g[:, :cols]
        m_new = ADAM_B1 * m_ref[...] + (1.0 - ADAM_B1) * g
        v_new = ADAM_B2 * v_ref[...] + (1.0 - ADAM_B2) * (g * g)
        m_hat = m_new / (1.0 - ADAM_B1 ** ADAM_STEP)
        v_hat = v_new / (1.0 - ADAM_B2 ** ADAM_STEP)
        g_out[...] = g
        d_out[...] = -ADAM_LR * (m_hat / (jnp.sqrt(v_hat) + ADAM_EPS) + ADAM_WD * w_ref[...])
        m_out[...] = m_new
        v_out[...] = v_new

    blk = pl.BlockSpec((tr, cols), lambda i: (blk0 + i, 0))
    g_specs, g_args = [], []
    for arr, lead in g_parts:
        g_args.append(arr)
        if lead is None:
            g_specs.append(pl.BlockSpec((tr, cols), lambda i: (i, 0)))
        else:
            g_specs.append(pl.BlockSpec((None, tr, arr.shape[2]), functools.partial(lambda i, p: (p, i, 0), p=lead)))
    out = jax.ShapeDtypeStruct((rows, cols), F32)
    return _pcall(
        body, name=name, grid=(slab // tr,),
        in_specs=[blk, blk, blk] + g_specs + [pl.BlockSpec(memory_space=pl.ANY)] * n_prev,
        out_specs=[blk, blk, blk, blk],
        out_shape=[out, out, out, out],
        input_output_aliases={3 + n + k: k for k in range(n_prev)},
        compiler_params=_params(("parallel",)),
    )(w, m, v, *g_args, *(prev or ()))


def _position():
    return lax.axis_index("x"), lax.axis_index("y"), lax.axis_index("c")


def _block_id(p):
    return 4 * p[0] + 2 * p[1] + p[2]


def _flip(p, mask):
    return tuple((1 - v) if (mask >> (2 - a)) & 1 else v for a, v in enumerate(p))


def _all_gather(xs, *, name):
    n = len(xs)

    def body(*refs):
        x_refs, o_refs = refs[:n], refs[n:2 * n]
        send_sems, recv_sems, local_sems = refs[2 * n:]
        x, y, c = _position()
        me, sibling = (x, y, c), (x, y, 1 - c)
        chips = [(1 - x, y), (x, 1 - y), (1 - x, 1 - y)]

        def copy(t, k, block, to, src=None):
            dst = o_refs[t].at[_block_id(block)]
            return pltpu.make_async_remote_copy(
                src_ref=dst if src is None else src, dst_ref=dst,
                send_sem=send_sems.at[t, k], recv_sem=recv_sems.at[t, k],
                device_id=to, device_id_type=MESH)

        mine = [pltpu.make_async_copy(x_refs[t], o_refs[t].at[_block_id(me)], local_sems.at[t]) for t in range(n)]
        for cp in mine:
            cp.start()
        started = []
        for t in range(n):
            first = [copy(t, 0, me, sibling, src=x_refs[t])]
            first += [copy(t, 1 + j, me, (*chip, c), src=x_refs[t]) for j, chip in enumerate(chips)]
            for cp in first:
                cp.start()
            started += first
        for j, chip in enumerate(chips):
            for t in range(n):
                copy(t, 1 + j, (*chip, c), me).wait_recv()
                fwd = copy(t, 4 + j, (*chip, c), sibling)
                fwd.start()
                started.append(fwd)
        for t in range(n):
            copy(t, 0, sibling, me).wait_recv()
            for j, chip in enumerate(chips):
                copy(t, 4 + j, (*chip, 1 - c), me).wait_recv()
        for cp in started:
            cp.wait_send()
        for cp in mine:
            cp.wait()

    any_spec = pl.BlockSpec(memory_space=pl.ANY)
    return _pcall(
        body, name=name,
        in_specs=[any_spec] * n, out_specs=[any_spec] * n,
        out_shape=[jax.ShapeDtypeStruct((N_DEV,) + a.shape, a.dtype) for a in xs],
        scratch_shapes=[pltpu.SemaphoreType.DMA((n, 7)), pltpu.SemaphoreType.DMA((n, 7)),
                        pltpu.SemaphoreType.DMA((n,))],
    )(*xs)


def _pushed_copy(x_ref, land_ref, send_sems, recv_sems, t, mask, me, chunked, at_receiver):
    peer = _flip(me, mask)
    src = x_ref.at[_block_id(peer)] if chunked else x_ref
    slot = _block_id(peer) if at_receiver else _block_id(me)
    k = (N_DEV - 1) * t + mask - 1
    return pltpu.make_async_remote_copy(
        src_ref=src, dst_ref=land_ref.at[slot], send_sem=send_sems.at[k], recv_sem=recv_sems.at[k],
        device_id=peer, device_id_type=MESH)


ALL_PEERS = tuple(range(1, N_DEV))
NEAR_PEERS = (1, 2, 4, 6)
FAR_RELAY = (2, 4, 6)


def _push_start(xs, chunked, *, name, after=None, masks=ALL_PEERS):
    n = len(xs)
    lands = [lax.empty(a.shape if chunked else (N_DEV,) + a.shape, a.dtype) for a in xs]

    n_after = 0 if after is None else 1

    def body(*refs):
        x_refs, land_refs = refs[:n], refs[n:2 * n]
        send_sems, recv_sems = refs[2 * n + n_after], refs[2 * n + n_after + 1]
        token = refs[4 * n + n_after + 2]
        me = _position()
        for t in range(n):
            for mask in masks:
                _pushed_copy(x_refs[t], land_refs[t], send_sems, recv_sems, t, mask, me, chunked, False).start()
        token[...] = jnp.zeros_like(token)

    hbm = pl.BlockSpec(memory_space=pltpu.HBM)
    sem = pl.BlockSpec(memory_space=pltpu.SEMAPHORE)
    outs = _pcall(
        body, name=name,
        in_specs=[hbm] * (2 * n) + [pl.BlockSpec(memory_space=pl.ANY)] * n_after,
        out_specs=[sem, sem] + [hbm] * (2 * n) + [pl.BlockSpec(memory_space=pltpu.VMEM)],
        out_shape=[pltpu.SemaphoreType.DMA(((N_DEV - 1) * n,)), pltpu.SemaphoreType.DMA(((N_DEV - 1) * n,))]
        + [pltpu.HBM(a.shape, a.dtype) for a in xs] + [pltpu.HBM(a.shape, a.dtype) for a in lands]
        + [jax.ShapeDtypeStruct((8, 128), F32)],
        input_output_aliases={i: 2 + i for i in range(2 * n)},
        compiler_params=pltpu.CompilerParams(has_side_effects=pltpu.SideEffectType.DATAFLOW_SIDE_EFFECTING),
    )(*[pltpu.with_memory_space_constraint(a, pltpu.HBM) for a in list(xs) + lands], *([after] * n_after))
    return outs[0], outs[1], outs[2:2 + n], outs[2 + n:2 + 2 * n], outs[2 + 2 * n]


def _push_wait(handle, after, chunked, *, name, masks=ALL_PEERS):
    send_sems, recv_sems, xs, lands, _ = handle
    n = len(xs)

    def body(*refs):
        x_refs, land_refs = refs[:n], refs[n:2 * n]
        send_sems, recv_sems = refs[2 * n], refs[2 * n + 1]
        me = _position()
        for t in range(n):
            for mask in masks:
                _pushed_copy(x_refs[t], land_refs[t], send_sems, recv_sems, t, mask, me, chunked, False).wait_send()
                _pushed_copy(x_refs[t], land_refs[t], send_sems, recv_sems, t, mask, me, chunked, True).wait_recv()

    hbm = pl.BlockSpec(memory_space=pltpu.HBM)
    sem = pl.BlockSpec(memory_space=pltpu.SEMAPHORE)
    outs = _pcall(
        body, name=name,
        in_specs=[hbm] * (2 * n) + [sem, sem, pl.BlockSpec(memory_space=pl.ANY)],
        out_specs=[hbm] * (2 * n),
        out_shape=[pltpu.HBM(a.shape, a.dtype) for a in list(xs) + list(lands)],
        input_output_aliases={i: i for i in range(2 * n)},
        compiler_params=pltpu.CompilerParams(has_side_effects=pltpu.SideEffectType.DATAFLOW_SIDE_EFFECTING),
    )(*xs, *lands, send_sems, recv_sems, after)
    return outs[:n], outs[n:]


def _relayed_copy(land_ref, send_sems, recv_sems, t, j, me, at_receiver):
    source = _flip(_flip(me, 1), FAR_RELAY[j]) if at_receiver else _flip(me, FAR_RELAY[j])
    slot = land_ref.at[_block_id(source)]
    k = len(FAR_RELAY) * t + j
    return pltpu.make_async_remote_copy(
        src_ref=slot, dst_ref=slot, send_sem=send_sems.at[k], recv_sem=recv_sems.at[k],
        device_id=_flip(me, 1), device_id_type=MESH)


def _relay_start(lands, *, name, after):
    n = len(lands)

    def body(*refs):
        land_refs = refs[:n]
        send_sems, recv_sems = refs[n + 1], refs[n + 2]
        token = refs[2 * n + 3]
        me = _position()
        for t in range(n):
            for j in range(len(FAR_RELAY)):
                _relayed_copy(land_refs[t], send_sems, recv_sems, t, j, me, False).start()
        token[...] = jnp.zeros_like(token)

    hbm = pl.BlockSpec(memory_space=pltpu.HBM)
    sem = pl.BlockSpec(memory_space=pltpu.SEMAPHORE)
    outs = _pcall(
        body, name=name,
        in_specs=[hbm] * n + [pl.BlockSpec(memory_space=pl.ANY)],
        out_specs=[sem, sem] + [hbm] * n + [pl.BlockSpec(memory_space=pltpu.VMEM)],
        out_shape=[pltpu.SemaphoreType.DMA((len(FAR_RELAY) * n,)), pltpu.SemaphoreType.DMA((len(FAR_RELAY) * n,))]
        + [pltpu.HBM(a.shape, a.dtype) for a in lands] + [jax.ShapeDtypeStruct((8, 128), F32)],
        input_output_aliases={i: 2 + i for i in range(n)},
        compiler_params=pltpu.CompilerParams(has_side_effects=pltpu.SideEffectType.DATAFLOW_SIDE_EFFECTING),
    )(*lands, after)
    return outs[0], outs[1], outs[2:2 + n], outs[2 + n]


def _relay_wait(handle, after, *, name):
    send_sems, recv_sems, lands, _ = handle
    n = len(lands)

    def body(*refs):
        land_refs = refs[:n]
        send_sems, recv_sems = refs[n], refs[n + 1]
        me = _position()
        for t in range(n):
            for j in range(len(FAR_RELAY)):
                _relayed_copy(land_refs[t], send_sems, recv_sems, t, j, me, False).wait_send()
                _relayed_copy(land_refs[t], send_sems, recv_sems, t, j, me, True).wait_recv()

    hbm = pl.BlockSpec(memory_space=pltpu.HBM)
    sem = pl.BlockSpec(memory_space=pltpu.SEMAPHORE)
    return _pcall(
        body, name=name,
        in_specs=[hbm] * n + [sem, sem, pl.BlockSpec(memory_space=pl.ANY)],
        out_specs=[hbm] * n,
        out_shape=[pltpu.HBM(a.shape, a.dtype) for a in lands],
        input_output_aliases={i: i for i in range(n)},
        compiler_params=pltpu.CompilerParams(has_side_effects=pltpu.SideEffectType.DATAFLOW_SIDE_EFFECTING),
    )(*lands, send_sems, recv_sems, after)


def _all_reduce_small(pack, *, name):
    rows = pack.shape[0]

    def body(p_ref, o_ref, gath, send_sems, recv_sems):
        me = _position()
        my_id = _block_id(me)
        gath[my_id] = p_ref[...]
        sent = []
        for mask in range(1, N_DEV):
            peer = _flip(me, mask)
            cp = pltpu.make_async_remote_copy(
                src_ref=p_ref, dst_ref=gath.at[my_id], send_sem=send_sems.at[mask - 1],
                recv_sem=recv_sems.at[mask - 1], device_id=peer, device_id_type=MESH)
            cp.start()
            sent.append(cp)
        for mask in range(1, N_DEV):
            slot = gath.at[_block_id(_flip(me, mask))]
            pltpu.make_async_remote_copy(
                src_ref=slot, dst_ref=slot, send_sem=send_sems.at[mask - 1], recv_sem=recv_sems.at[mask - 1],
                device_id=me, device_id_type=MESH).wait_recv()
        for cp in sent:
            cp.wait_send()
        total = gath[0]
        for s in range(1, N_DEV):
            total = total + gath[s]
        o_ref[...] = total

    vm = pl.BlockSpec(memory_space=pltpu.VMEM)
    return _pcall(
        body, name=name,
        in_specs=[vm], out_specs=vm,
        out_shape=jax.ShapeDtypeStruct(pack.shape, F32),
        scratch_shapes=[pltpu.VMEM((N_DEV, rows, 128), F32), pltpu.SemaphoreType.DMA((7,)),
                        pltpu.SemaphoreType.DMA((7,))],
        compiler_params=pltpu.CompilerParams(vmem_limit_bytes=VMEM_LIMIT),
    )(pack)


def _layout(d, ql, kvl):
    cw = d // 2
    att = d // 2
    lay = {"D": d, "CW": cw, "ATT": att, "H": att // VDIM, "QL": ql, "KVL": kvl}
    lay["A"], lay["B"], lay["GATT"], lay["GCONV"] = 0, cw, 2 * cw, 2 * cw + att
    lay["QC"] = lay["GCONV"] + cw
    lay["KVC"] = lay["QC"] + ql
    lay["KPE"] = lay["KVC"] + kvl
    used = lay["KPE"] + 128
    tn = min(MM_TN, 1024)
    lay["NP"] = -(-used // tn) * tn
    assert att == cw and lay["QC"] % ql == 0 and lay["KVC"] % kvl == 0 and lay["KPE"] % 128 == 0
    lay["o_kv"], lay["o_pe"] = ql, ql + kvl
    lay["o_ga"] = lay["o_pe"] + ROPE
    lay["o_u"] = lay["o_ga"] + att
    lay["o_gc"] = lay["o_u"] + 2 * cw
    lay["IN_COLS"] = lay["o_gc"] + cw
    return lay


def _lane_pad(n):
    return -(-n // 128) * 128


def _assemble_w_in(g, shard, lay, *, name):
    _, d, padw = g.shape
    tr = _tile(d, 256, 16)
    sections = [(lay["A"], 2 * lay["CW"], lay["o_u"]), (lay["GATT"], lay["ATT"], lay["o_ga"]),
                (lay["GCONV"], lay["CW"], lay["o_gc"]), (lay["QC"], lay["QL"], 0),
                (lay["KVC"], lay["KVL"], lay["o_kv"]), (lay["KPE"], ROPE, lay["o_pe"])]

    def runs_of_tile(j):
        for start, width, orig in sections:
            if start <= j * 128 < start + width:
                todo, col, lane, out = min(128, start + width - j * 128), orig + j * 128 - start, 0, []
                while todo:
                    p, o = divmod(col, shard)
                    take = min(todo, shard - o)
                    first = (o // 128) * 128
                    win = 256 if first + 256 <= padw else 128
                    out.append((p, first, win, o - first, lane, take))
                    col, lane, todo = col + take, lane + take, todo - take
                return out
        return []

    def body(g_ref, o_ref):
        movers = {}

        def mover(win, off, lane, take):
            key = (win, off, lane, take)
            if key not in movers:
                row = lax.broadcasted_iota(jnp.int32, (win, 128), 0)
                col = lax.broadcasted_iota(jnp.int32, (win, 128), 1)
                hit = (row - off == col - lane) & (col >= lane) & (col < lane + take)
                movers[key] = jnp.where(hit, 1.0, 0.0).astype(BF16)
            return movers[key]

        for j in range(lay["NP"] // 128):
            tile = None
            for p, first, win, off, lane, take in runs_of_tile(j):
                part = jnp.dot(g_ref[p, :, first:first + win], mover(win, off, lane, take),
                               preferred_element_type=F32)
                tile = part if tile is None else tile + part
            if tile is None:
                tile = jnp.zeros((tr, 128), F32)
            o_ref[:, j * 128:(j + 1) * 128] = tile.astype(BF16)

    return _pcall(
        body, name=name, grid=(d // tr,),
        in_specs=[pl.BlockSpec((N_DEV, tr, padw), lambda i: (0, i, 0))],
        out_specs=pl.BlockSpec((tr, lay["NP"]), lambda i: (i, 0)),
        out_shape=jax.ShapeDtypeStruct((d, lay["NP"]), BF16),
        compiler_params=_params(("parallel",)),
    )(g)


def _split_dw_in(dwp, shard, lay, *, name):
    d = dwp.shape[0]
    padw = _lane_pad(shard)
    tr = _tile(d, 256, 16)
    sections = sorted([(lay["A"], 2 * lay["CW"], lay["o_u"]), (lay["GATT"], lay["ATT"], lay["o_ga"]),
                       (lay["GCONV"], lay["CW"], lay["o_gc"]), (lay["QC"], lay["QL"], 0),
                       (lay["KVC"], lay["KVL"], lay["o_kv"]), (lay["KPE"], ROPE, lay["o_pe"])], key=lambda s: s[2])

    def runs_of_tile(p, jt):
        lo, hi = p * shard + jt * 128, p * shard + min((jt + 1) * 128, shard)
        out = []
        for start, width, orig in sections:
            a, b = max(lo, orig), min(hi, orig + width)
            if a < b:
                src = start + a - orig
                first = (src // 128) * 128
                win = 256 if first + 256 <= lay["NP"] else 128
                out.append((first, win, src - first, a - lo, b - a))
        return out

    def body(w_ref, o_ref):
        movers = {}

        def mover(win, off, lane, take):
            key = (win, off, lane, take)
            if key not in movers:
                row = lax.broadcasted_iota(jnp.int32, (win, 128), 0)
                col = lax.broadcasted_iota(jnp.int32, (win, 128), 1)
                hit = (row - off == col - lane) & (col >= lane) & (col < lane + take)
                movers[key] = jnp.where(hit, 1.0, 0.0).astype(BF16)
            return movers[key]

        for p in range(N_DEV):
            for jt in range(padw // 128):
                tile = None
                for first, win, off, lane, take in runs_of_tile(p, jt):
                    part = jnp.dot(w_ref[:, first:first + win], mover(win, off, lane, take),
                                   preferred_element_type=F32)
                    tile = part if tile is None else tile + part
                if tile is None:
                    tile = jnp.zeros((tr, 128), F32)
                o_ref[p, :, jt * 128:(jt + 1) * 128] = tile.astype(BF16)

    return _pcall(
        body, name=name, grid=(d // tr,),
        in_specs=[pl.BlockSpec((tr, lay["NP"]), lambda i: (i, 0))],
        out_specs=pl.BlockSpec((N_DEV, tr, padw), lambda i: (0, i, 0)),
        out_shape=jax.ShapeDtypeStruct((N_DEV, d, padw), BF16),
        compiler_params=_params(("parallel",)),
    )(dwp)


def _ungather_cols(g):
    return jnp.transpose(g, (1, 0, 2)).reshape(g.shape[1], -1)


def _to_col_blocks(w):
    r, c = w.shape
    return jnp.transpose(w.reshape(r, N_DEV, c // N_DEV), (1, 0, 2))


def _pad_heads(w, h):
    r = w.shape[0]
    return jnp.pad(w.reshape(r, h, QK_DIM), ((0, 0), (0, 0), (0, HEAD_PAD - QK_DIM))).reshape(r, h * HEAD_PAD)


def _unpad_heads(w, h):
    r = w.shape[0]
    return w.reshape(r, h, HEAD_PAD)[:, :, :QK_DIM].reshape(r, h * QK_DIM)


def _rope_tabs(positions):
    half = ROPE // 2
    inv_freq = ROPE_THETA ** (-jnp.arange(half, dtype=F32) / half)
    ang = positions.astype(F32).reshape(-1)[:, None] * inv_freq
    cos, sin = jnp.cos(ang), jnp.sin(ang)
    zero = jnp.zeros_like(cos)
    return (jnp.concatenate([cos, cos, zero, zero], axis=1),
            jnp.concatenate([-sin, zero, zero, zero], axis=1),
            jnp.concatenate([zero, sin, zero, zero], axis=1))


def _pack_rows(vecs):
    rows = []
    for v in vecs:
        flat = v.reshape(-1)
        pad = (-flat.shape[0]) % 1024
        rows.append(jnp.pad(flat, (0, pad)).reshape(-1, 128))
    return jnp.concatenate(rows, axis=0)


def _unpack_rows(pack, shapes):
    out, r0 = [], 0
    for shp in shapes:
        size = math.prod(shp)
        nrows = -(-size // 1024) * 8
        out.append(pack[r0:r0 + nrows].reshape(-1)[:size].reshape(shp))
        r0 += nrows
    return out


def kernel(x, positions, ln_g, w_in, q_a_norm, w_q_up, kv_a_norm, w_kv_up, q_norm, k_norm, w_dw, b_dw, conv_ln_g, conv_ln_b, w_out, loss_target, m_ln_g, m_w_in, m_q_a_norm, m_w_q_up, m_kv_a_norm, m_w_kv_up, m_q_norm, m_k_norm, m_w_dw, m_b_dw, m_conv_ln_g, m_conv_ln_b, m_w_out, v_ln_g, v_w_in, v_q_a_norm, v_w_q_up, v_kv_a_norm, v_w_kv_up, v_q_norm, v_k_norm, v_w_dw, v_b_dw, v_conv_ln_g, v_conv_ln_b, v_w_out):
    nb, seq, d = x.shape
    depth = ln_g.shape[0]
    lay = _layout(d, q_a_norm.shape[1], kv_a_norm.shape[1])
    h, cw, ql, kvl = lay["H"], lay["CW"], lay["QL"], lay["KVL"]
    t = nb * seq
    my_id = _block_id(_position())

    shard_in = w_in.shape[2]

    def shards(l):
        padded = jnp.pad(w_in[l].astype(BF16), ((0, 0), (0, _lane_pad(shard_in) - shard_in)))
        return [padded, w_q_up[l].astype(BF16), w_kv_up[l].astype(BF16), w_out[l].astype(BF16)]

    def fill_own(lands, own):
        return [lax.dynamic_update_index_in_dim(land, blk, my_id, 0) for land, blk in zip(lands, own)]

    def layout_in(g_in, l):
        return {"in": _assemble_w_in(g_in, shard_in, lay, name=f"assemble_w_in_{l}")}

    def layout_rest(g_q, g_kv, g_out):
        return {"q": _pad_heads(_ungather_cols(g_q), h), "kv": _ungather_cols(g_kv), "out": g_out.reshape(2 * cw, d)}

    first = shards(0)
    g_in0, g_dw = _all_gather([first[0], w_dw], name="gather_w_in_0")
    gathers = {0: _push_start(first[1:], False, after=g_in0, name="gather_start_0")}
    for l in range(1, depth):
        gathers[l] = _push_start(shards(l), False, after=gathers[l - 1][4], masks=NEAR_PEERS,
                                 name=f"gather_start_{l}")
    fwd_dep = gathers[depth - 1][4]
    weights = []

    tabs = _rope_tabs(positions)
    gq_pad = jnp.pad(q_norm, ((0, 0), (0, HEAD_PAD - QK_DIM)))
    gk_pad = jnp.pad(k_norm, ((0, 0), (0, HEAD_PAD - QK_DIM)))
    w_dw_all = jnp.transpose(g_dw, (1, 2, 0, 3)).reshape(depth, CONV_K, cw)
    w_dw_all = jnp.pad(w_dw_all, ((0, 0), (0, HALO - CONV_K), (0, 0)))

    saved = []
    xs = x.reshape(t, d)
    for l in range(depth):
        hid = _rms_fwd(xs, ln_g[l], dep=fwd_dep if l == 0 else None, name=f"rms_fwd_{l}")
        if l == 0:
            weights.append(layout_in(g_in0, 0))
        z = _mm(hid, weights[l]["in"], out_dtype=BF16, name=f"in_proj_{l}")
        if l == 0:
            own, lands = _push_wait(gathers[0], z, False, name="gather_wait_0")
            weights[0].update(layout_rest(*fill_own(lands, own)))
        wl = weights[l]
        qn, kvn = _lat_fwd(z, q_a_norm[l], kv_a_norm[l], lay, name=f"lat_fwd_{l}")
        q_raw = _mm(qn, wl["q"], out_dtype=BF16, name=f"q_up_{l}")
        kv_raw = _mm(kvn, wl["kv"], out_dtype=BF16, name=f"kv_up_{l}")
        qh, kh, vh = _heads_fwd(q_raw, kv_raw, z, tabs, gq_pad[l:l + 1], gk_pad[l:l + 1], lay, name=f"heads_fwd_{l}")
        att, mix, lse = _flash_fwd(qh, kh, vh, z, lay, nb, name=f"flash_fwd_{l}")
        relay = None
        if l + 1 < depth:
            own, lands = _push_wait(gathers[l + 1], att, False, masks=NEAR_PEERS, name=f"gather_wait_{l + 1}")
            relay = _relay_start(lands, after=att, name=f"relay_start_{l + 1}")
        mix, u, c_pre = _conv_fwd(z, mix, w_dw_all[l], b_dw[l], conv_ln_g[l], conv_ln_b[l], lay, nb,
                                  name=f"conv_fwd_{l}")
        x_next = _mm(mix, wl["out"], add=xs, dep=None if relay is None else relay[3], name=f"out_proj_{l}")
        saved.append((xs, hid, z, qn, kvn, q_raw, kv_raw, qh, kh, vh, att, lse, mix, u, c_pre))
        xs = x_next
        if relay is not None:
            g_in, *g_rest = fill_own(_relay_wait(relay, xs, name=f"relay_wait_{l + 1}"), own)
            weights.append({**layout_in(g_in, l + 1), **layout_rest(*g_rest)})

    sse, dx, dxb = _loss_head(xs, loss_target.reshape(t, d), name="loss_head")
    loss = lax.psum(sse[0, 0] * (0.5 / d), ("x", "y", "c"))

    small = {k: [] for k in ("ln_g", "q_a", "kv_a", "q_n", "k_n", "w_dw", "b_dw", "cln_g", "cln_b")}
    scatters, bwd_dep = {}, None
    for l in reversed(range(depth)):
        xs, hid, z, qn, kvn, q_raw, kv_raw, qh, kh, vh, att, lse, mix, u, c_pre = saved[l]
        wl = weights[l]
        dmix = _mm(dxb, wl["out"], trans_b=True, dep=bwd_dep, name=f"d_mix_{l}")
        dw_out = _mm(mix, dxb, trans_a=True, out_dtype=BF16, name=f"dw_out_{l}")
        dc, dz, dlg, dlb, dbias = _conv_bwd_ln(c_pre, z, dmix, conv_ln_g[l], conv_ln_b[l], lay,
                                               name=f"conv_bwd_ln_{l}")
        dz, dwdw = _conv_bwd_dw(dc, u, z, w_dw_all[l], dz, lay, nb, name=f"conv_bwd_dw_{l}")
        do, delta, dz = _gate_bwd(dmix, att, z, dz, lay, name=f"gate_bwd_{l}")
        dqh, dkh, dvh = _flash_bwd(qh, kh, vh, do, lse, delta, nb, name=f"flash_bwd_{l}")
        dq_raw, dkv_raw, dpe, dgq, dgk = _heads_bwd(q_raw, kv_raw, z, tabs, gq_pad[l:l + 1], gk_pad[l:l + 1],
                                                    dqh, dkh, dvh, lay, name=f"heads_bwd_{l}")
        dqn = _mm(dq_raw, wl["q"], trans_b=True, name=f"d_qn_{l}")
        dkvn = _mm(dkv_raw, wl["kv"], trans_b=True, name=f"d_kvn_{l}")
        dw_q = _mm(qn, dq_raw, trans_a=True, out_dtype=BF16, name=f"dw_q_{l}")
        dw_kv = _mm(kvn, dkv_raw, trans_a=True, out_dtype=BF16, name=f"dw_kv_{l}")
        early = _push_start(
            [_to_col_blocks(_unpad_heads(dw_q, h)), _to_col_blocks(dw_kv), dw_out.reshape(N_DEV, (2 * cw) // N_DEV, d)],
            True, name=f"scatter_start_a_{l}")
        dz, dgqa, dgkva = _lat_bwd(z, q_a_norm[l], kv_a_norm[l], dqn, dkvn, dpe, dz, lay, name=f"lat_bwd_{l}")
        dw_in = _mm(hid, dz, trans_a=True, out_dtype=BF16, dep=early[4], name=f"dw_in_{l}")
        late = _push_start([_split_dw_in(dw_in, shard_in, lay, name=f"split_dw_in_{l}")], True,
                           name=f"scatter_start_b_{l}")
        scatters[l] = (late, early)
        bwd_dep = late[4]
        dh = _mm(dz, wl["in"], trans_b=True, dep=bwd_dep, name=f"d_hid_{l}")
        dx, dxb, dlng = _rms_bwd(xs, ln_g[l], dh, dx, name=f"rms_bwd_{l}")
        for key, val in (("ln_g", dlng), ("q_a", dgqa), ("kv_a", dgkva), ("q_n", dgq[:, :QK_DIM]),
                         ("k_n", dgk[:, :QK_DIM]), ("w_dw", dwdw[:CONV_K]), ("b_dw", dbias),
                         ("cln_g", dlg), ("cln_b", dlb)):
            small[key].append(val)
    grad_x = dx.reshape(nb, seq, d)
    for key in small:
        small[key] = jnp.stack(small[key][::-1])

    small_names = ("ln_g", "q_a", "kv_a", "q_n", "k_n", "b_dw", "cln_g", "cln_b", "w_dw")
    small_shapes = [small[k].shape for k in small_names]
    summed = _unpack_rows(_all_reduce_small(_pack_rows([small[k] for k in small_names]), name="reduce_small_grads"),
                          small_shapes)
    sg = dict(zip(small_names, summed))
    g_w_dw = lax.dynamic_slice_in_dim(sg["w_dw"], my_id * (cw // N_DEV), cw // N_DEV, axis=2)

    def adam_small(ws, ms, vs, gs, nm):
        shapes = [w.shape for w in ws]
        outs = _adam(_pack_rows(ws), _pack_rows(ms), _pack_rows(vs), [(_pack_rows(gs), None)], name=nm)
        return [_unpack_rows(o, shapes) for o in outs]

    big = [("w_in", w_in, m_w_in, v_w_in), ("w_q_up", w_q_up, m_w_q_up, v_w_q_up),
           ("w_kv_up", w_kv_up, m_w_kv_up, v_w_kv_up), ("w_out", w_out, m_w_out, v_w_out)]
    res, prev = {}, [None] * len(big)
    for l in reversed(range(depth)):
        own, lands = [], []
        for half, tag in zip(scatters[l], "ba"):
            sent, landed = _push_wait(half, dx, True, name=f"scatter_wait_{tag}_{l}")
            own += [lax.dynamic_index_in_dim(o, my_id, 0, keepdims=False) for o in sent]
            lands += landed
        for idx, ((nm, w, m, v), recv) in enumerate(zip(big, fill_own(lands, own))):
            rows, cols = w.shape[1], w.shape[2]
            flat = lambda a: a.reshape(depth * rows, cols)
            prev[idx] = _adam(flat(w), flat(m), flat(v), [(recv, s) for s in range(N_DEV)], layer=l, layers=depth,
                              prev=prev[idx], name=f"adam_{nm}_{l}")
    for idx, (nm, w, _, _) in enumerate(big):
        res[nm] = [o.reshape(w.shape) for o in prev[idx]]
    names_s = ["ln_g", "q_a_norm", "kv_a_norm", "q_norm", "k_norm", "w_dw", "b_dw", "conv_ln_g", "conv_ln_b"]
    ws = [ln_g, q_a_norm, kv_a_norm, q_norm, k_norm, w_dw, b_dw, conv_ln_g, conv_ln_b]
    ms = [m_ln_g, m_q_a_norm, m_kv_a_norm, m_q_norm, m_k_norm, m_w_dw, m_b_dw, m_conv_ln_g, m_conv_ln_b]
    vs = [v_ln_g, v_q_a_norm, v_kv_a_norm, v_q_norm, v_k_norm, v_w_dw, v_b_dw, v_conv_ln_g, v_conv_ln_b]
    gs = [sg["ln_g"].reshape(ln_g.shape), sg["q_a"].reshape(q_a_norm.shape), sg["kv_a"].reshape(kv_a_norm.shape),
          sg["q_n"].reshape(q_norm.shape), sg["k_n"].reshape(k_norm.shape), g_w_dw,
          sg["b_dw"].reshape(b_dw.shape), sg["cln_g"].reshape(conv_ln_g.shape), sg["cln_b"].reshape(conv_ln_b.shape)]
    outs_s = adam_small(ws, ms, vs, gs, "adam_small")
    for idx, nm in enumerate(names_s):
        res[nm] = [outs_s[k][idx] for k in range(4)]

    order = ["ln_g", "w_in", "q_a_norm", "w_q_up", "kv_a_norm", "w_kv_up", "q_norm", "k_norm", "w_dw", "b_dw",
             "conv_ln_g", "conv_ln_b", "w_out"]
    return (loss, grad_x, *[res[nm][0] for nm in order], *[res[nm][1] for nm in order],
            *[res[nm][2] for nm in order], *[res[nm][3] for nm in order])
```
